```python
import math
import jax, jax.numpy as jnp
from jax import lax
import numpy as np

D_MODEL = 1024
BATCH = 2
SEQ = 8192
DEPTH = 1
DEC_BATCH = 8
DEC_SEQ = 4096
PAST_LEN = 128

D_MIX = D_MODEL
RET_HEADS = 8
RET_DK = 64
RET_DV = 64
RET_WIDTH = RET_HEADS * RET_DV
CHUNK = 128
MLA_HEADS = 8
MLA_NOPE = 64
MLA_ROPE = 32
MLA_V = 64
MLA_WIDTH = MLA_HEADS * MLA_V
Q_LORA = 512
KV_LORA = 256
Q_BLOCK = 128
ROPE_BASE = 10000.0
MEM_LEN = 256
X_HEADS = 4
X_HEAD_DIM = D_MODEL // X_HEADS
N_EXPERTS = 32
TOP_K = 4
D_EXPERT = D_MODEL
SWIGLU_LIMIT = 7.0
SWIGLU_ALPHA = 1.702
MOE_BLOCK = 512
EPS = 1e-6

IN_SPLITS = (RET_HEADS * RET_DK, RET_HEADS * RET_DK, RET_WIDTH, RET_WIDTH, Q_LORA, KV_LORA, MLA_ROPE)
D_IN = sum(IN_SPLITS)

kernel_name = 'hybrid_retention_mla_moe_encoder'


def rms_norm(x, g):
    xf = x.astype(jnp.float32)
    y = xf * lax.rsqrt(jnp.mean(xf * xf, axis=-1, keepdims=True) + EPS)
    return (y * g.astype(jnp.float32)).astype(x.dtype)


def rope(x, pos):
    d = x.shape[-1]
    half = d // 2
    inv = ROPE_BASE ** (-jnp.arange(half, dtype=jnp.float32) / half)
    ang = pos.astype(jnp.float32)[:, None] * inv[None, :]
    cos = jnp.cos(ang)[None, :, None, :]
    sin = jnp.sin(ang)[None, :, None, :]
    x1 = x[..., :half].astype(jnp.float32)
    x2 = x[..., half:].astype(jnp.float32)
    return jnp.concatenate([x1 * cos - x2 * sin, x2 * cos + x1 * sin], axis=-1).astype(x.dtype)


def retention_direction(q, k, v, log_gamma, strict):
    t = jnp.arange(CHUNK, dtype=jnp.float32)
    diff = t[:, None] - t[None, :]
    mask = (diff > 0) if strict else (diff >= 0)
    lg = log_gamma[:, None, None]
    d_intra = jnp.where(mask[None], jnp.exp(jnp.where(mask[None], diff[None], 0.0) * lg), 0.0).astype(q.dtype)
    scores = jnp.einsum('bhncd,bhnsd->bhncs', q, k) * d_intra[None, :, None]
    inner = jnp.einsum('bhncs,bhnse->bhnce', scores, v)
    k_dec = jnp.exp((CHUNK - 1 - t)[None, :] * log_gamma[:, None]).astype(q.dtype)
    kv = jnp.einsum('bhnsd,hs,bhnse->nbhde', k, k_dec, v)
    chunk_dec = jnp.exp(CHUNK * log_gamma).astype(q.dtype)[None, :, None, None]

    def step(s, kv_n):
        return chunk_dec * s + kv_n, s

    _, s_prev = lax.scan(step, jnp.zeros(kv.shape[1:], kv.dtype), kv)
    q_dec = jnp.exp((t + 1)[None, :] * log_gamma[:, None]).astype(q.dtype)
    cross = jnp.einsum('bhncd,hc,nbhde->bhnce', q, q_dec, s_prev)
    return inner + cross


def dense_attention_blocked(q, k, v, scale):
    B, S, H, dq = q.shape
    nb = S // Q_BLOCK
    qb = q.reshape(B, nb, Q_BLOCK, H, dq).transpose(1, 0, 2, 3, 4)

    def one(qblk):
        s = jnp.einsum('bqhd,bkhd->bhqk', qblk, k).astype(jnp.float32) * scale
        p = jax.nn.softmax(s, axis=-1).astype(v.dtype)
        return jnp.einsum('bhqk,bkhd->bqhd', p, v)

    o = lax.map(one, qb)
    return o.transpose(1, 0, 2, 3, 4).reshape(B, S, H, v.shape[-1])


def hybrid_mixer(xn, w_in, ret_decay_fwd, ret_decay_bwd, ret_gn, q_a_norm, w_uq, kv_a_norm, w_ukv, w_mix_out):
    B, S, _ = xn.shape
    N = S // CHUNK
    pos = jnp.arange(S)
    proj = xn @ w_in
    bounds = np.cumsum(IN_SPLITS)[:-1].tolist()
    rq, rk, rv, rg, cq, ckv, kr = jnp.split(proj, bounds, axis=-1)

    rq = rope(rq.reshape(B, S, RET_HEADS, RET_DK), pos)
    rk = rope(rk.reshape(B, S, RET_HEADS, RET_DK), pos) * (RET_DK ** -0.5)
    rv = rv.reshape(B, S, RET_HEADS, RET_DV)

    def to_chunks(a):
        return a.reshape(B, N, CHUNK, RET_HEADS, a.shape[-1]).transpose(0, 3, 1, 2, 4)

    def flip(a):
        return a[:, :, ::-1, ::-1]

    qc, kc, vc = to_chunks(rq), to_chunks(rk), to_chunks(rv)
    lg_f = jnp.log1p(-jnp.exp2(ret_decay_fwd.astype(jnp.float32)))
    lg_b = jnp.log1p(-jnp.exp2(ret_decay_bwd.astype(jnp.float32)))
    y = retention_direction(qc, kc, vc, lg_f, False) + flip(retention_direction(flip(qc), flip(kc), flip(vc), lg_b, True))
    y = y.transpose(0, 2, 3, 1, 4).reshape(B, S, RET_HEADS, RET_DV).astype(jnp.float32)
    mu = jnp.mean(y, axis=-1, keepdims=True)
    var = jnp.mean(jnp.square(y - mu), axis=-1, keepdims=True)
    yn = ((y - mu) * lax.rsqrt(var + EPS)).reshape(B, S, RET_WIDTH) * ret_gn.astype(jnp.float32)
    ret_out = (yn * jax.nn.silu(rg.astype(jnp.float32))).astype(xn.dtype)

    qm = (rms_norm(cq, q_a_norm) @ w_uq).reshape(B, S, MLA_HEADS, MLA_NOPE + MLA_ROPE)
    qm = jnp.concatenate([qm[..., :MLA_NOPE], rope(qm[..., MLA_NOPE:], pos)], axis=-1)
    kvm = (rms_norm(ckv, kv_a_norm) @ w_ukv).reshape(B, S, MLA_HEADS, MLA_NOPE + MLA_V)
    k_pe = rope(kr.reshape(B, S, 1, MLA_ROPE), pos)
    km = jnp.concatenate([kvm[..., :MLA_NOPE], jnp.broadcast_to(k_pe, (B, S, MLA_HEADS, MLA_ROPE))], axis=-1)
    vm = kvm[..., MLA_NOPE:]
    mla_out = dense_attention_blocked(qm, km, vm, (MLA_NOPE + MLA_ROPE) ** -0.5).reshape(B, S, MLA_WIDTH)

    return jnp.concatenate([ret_out, mla_out], axis=-1) @ w_mix_out


def memory_cross_attention(hn, mem, mem_norm, w_xq, w_xkv, w_xo):
    B, S, _ = hn.shape
    M = mem.shape[1]
    q = (hn @ w_xq).reshape(B, S, X_HEADS, X_HEAD_DIM)
    kv = rms_norm(mem, mem_norm) @ w_xkv
    k = kv[..., :D_MODEL].reshape(B, M, X_HEADS, X_HEAD_DIM)
    v = kv[..., D_MODEL:].reshape(B, M, X_HEADS, X_HEAD_DIM)
    s = jnp.einsum('bqhd,bkhd->bhqk', q, k).astype(jnp.float32) * (X_HEAD_DIM ** -0.5)
    p = jax.nn.softmax(s, axis=-1).astype(v.dtype)
    o = jnp.einsum('bhqk,bkhd->bqhd', p, v).reshape(B, S, D_MODEL)
    return o @ w_xo


def moe_ffn(xn, router_w, router_b, w_gu, b_gu, w_down, b_down):
    B, S, D = xn.shape
    T = B * S
    xt = xn.reshape(T, D)
    logits = (xt @ router_w).astype(jnp.float32) + router_b.astype(jnp.float32)
    top_val, top_idx = lax.top_k(logits, TOP_K)
    gates = jax.nn.softmax(top_val, axis=-1)
    A = T * TOP_K
    flat_e = top_idx.reshape(A)
    flat_tok = jnp.arange(A, dtype=jnp.int32) // TOP_K
    flat_gate = gates.reshape(A)
    order = jnp.argsort(flat_e)
    sorted_e = flat_e[order]
    counts = jnp.bincount(flat_e, length=N_EXPERTS)
    padded = (counts + MOE_BLOCK - 1) // MOE_BLOCK * MOE_BLOCK
    start = jnp.cumsum(counts) - counts
    pend = jnp.cumsum(padded)
    pstart = pend - padded
    dest = pstart[sorted_e] + jnp.arange(A, dtype=jnp.int32) - start[sorted_e]
    n_blocks = -(-(A + N_EXPERTS * (MOE_BLOCK - 1)) // MOE_BLOCK)
    P = n_blocks * MOE_BLOCK
    buf_tok = jnp.zeros((P,), jnp.int32).at[dest].set(flat_tok[order])
    buf_gate = jnp.zeros((P,), jnp.float32).at[dest].set(flat_gate[order])
    block_e = jnp.minimum(jnp.searchsorted(pend, jnp.arange(n_blocks, dtype=pend.dtype) * MOE_BLOCK, side='right'), N_EXPERTS - 1)

    def step(y, blk):
        tok, g, e = blk
        xblk = xt[tok]
        gu = xblk @ w_gu[e] + b_gu[e]
        gate = jnp.minimum(gu[:, :D_EXPERT], SWIGLU_LIMIT)
        up = jnp.clip(gu[:, D_EXPERT:], -SWIGLU_LIMIT, SWIGLU_LIMIT)
        h = (up + 1.0) * (gate * jax.nn.sigmoid(SWIGLU_ALPHA * gate))
        out = h @ w_down[e] + b_down[e]
        return y.at[tok].add(out.astype(jnp.float32) * g[:, None]), None

    y, _ = lax.scan(step, jnp.zeros((T, D), jnp.float32),
                    (buf_tok.reshape(n_blocks, MOE_BLOCK), buf_gate.reshape(n_blocks, MOE_BLOCK), block_e))
    return y.astype(xn.dtype).reshape(B, S, D)


def setup_inputs(seed: int = 0) -> dict:
    key = jax.random.key(seed)
    ks = jax.random.split(key, 32)
    L = DEPTH
    f32 = jnp.float32

    def nrm(k, shape, scale):
        return jax.random.normal(k, shape, f32) * scale

    def gain(k, shape):
        return 1.0 + 0.05 * jax.random.normal(k, shape, f32)

    head_scale = -5.0 - jnp.arange(RET_HEADS, dtype=f32)
    return {
        'x_prompt': nrm(ks[0], (BATCH, SEQ, D_MODEL), 1.0),
        'x_sample': nrm(ks[1], (DEC_BATCH, DEC_SEQ, D_MODEL), 1.0),
        'mem_prompt': nrm(ks[2], (BATCH, MEM_LEN, D_MODEL), 1.0),
        'mem_sample': nrm(ks[3], (DEC_BATCH, MEM_LEN, D_MODEL), 1.0),
        'norm_mix': gain(ks[4], (L, D_MODEL)),
        'w_in': nrm(ks[5], (L, D_MODEL, D_IN), D_MODEL ** -0.5),
        'ret_decay_fwd': head_scale[None] + 0.1 * jax.random.normal(ks[6], (L, RET_HEADS), f32),
        'ret_decay_bwd': head_scale[None] + 0.1 * jax.random.normal(ks[7], (L, RET_HEADS), f32),
        'ret_gn': gain(ks[8], (L, RET_WIDTH)),
        'q_a_norm': gain(ks[9], (L, Q_LORA)),
        'w_uq': nrm(ks[10], (L, Q_LORA, MLA_HEADS * (MLA_NOPE + MLA_ROPE)), Q_LORA ** -0.5),
        'kv_a_norm': gain(ks[11], (L, KV_LORA)),
        'w_ukv': nrm(ks[12], (L, KV_LORA, MLA_HEADS * (MLA_NOPE + MLA_V)), KV_LORA ** -0.5),
        'w_mix_out': nrm(ks[13], (L, D_MIX, D_MODEL), D_MIX ** -0.5),
        'norm_cross': gain(ks[14], (L, D_MODEL)),
        'norm_mem': gain(ks[15], (L, D_MODEL)),
        'w_xq': nrm(ks[16], (L, D_MODEL, D_MODEL), D_MODEL ** -0.5),
        'w_xkv': nrm(ks[17], (L, D_MODEL, 2 * D_MODEL), D_MODEL ** -0.5),
        'w_xo': nrm(ks[18], (L, D_MODEL, D_MODEL), D_MODEL ** -0.5),
        'norm_ffn': gain(ks[19], (L, D_MODEL)),
        'router_w': nrm(ks[20], (L, D_MODEL, N_EXPERTS), D_MODEL ** -0.5),
        'router_b': nrm(ks[21], (L, N_EXPERTS), 0.01),
        'w_gu': nrm(ks[22], (L, N_EXPERTS, D_MODEL, 2 * D_EXPERT), D_MODEL ** -0.5),
        'b_gu': nrm(ks[23], (L, N_EXPERTS, 2 * D_EXPERT), 0.02),
        'w_down': nrm(ks[24], (L, N_EXPERTS, D_EXPERT, D_MODEL), D_EXPERT ** -0.5),
        'b_down': nrm(ks[25], (L, N_EXPERTS, D_MODEL), 0.02),
        'norm_final': gain(ks[26], (D_MODEL,)),
    }


def reference(x_prompt, x_sample, mem_prompt, mem_sample, norm_mix, w_in, ret_decay_fwd, ret_decay_bwd, ret_gn,
              q_a_norm, w_uq, kv_a_norm, w_ukv, w_mix_out, norm_cross, norm_mem, w_xq, w_xkv, w_xo,
              norm_ffn, router_w, router_b, w_gu, b_gu, w_down, b_down, norm_final):
    def trunk(x, mem):
        for l in range(DEPTH):
            x = x + hybrid_mixer(rms_norm(x, norm_mix[l]), w_in[l], ret_decay_fwd[l], ret_decay_bwd[l], ret_gn[l],
                                 q_a_norm[l], w_uq[l], kv_a_norm[l], w_ukv[l], w_mix_out[l])
            x = x + memory_cross_attention(rms_norm(x, norm_cross[l]), mem, norm_mem[l], w_xq[l], w_xkv[l], w_xo[l])
            x = x + moe_ffn(rms_norm(x, norm_ffn[l]), router_w[l], router_b[l], w_gu[l], b_gu[l], w_down[l], b_down[l])
        return rms_norm(x, norm_final)

    y_prompt = trunk(x_prompt, mem_prompt)
    y_sample = trunk(x_sample, mem_sample)
    return (y_prompt, y_sample)
```

```python
import functools
import math

import jax
import jax.numpy as jnp
import numpy as np
from jax import lax
from jax.experimental import pallas as pl
from jax.experimental.pallas import tpu as pltpu

D_MODEL = 1024
RET_HEADS = 8
RET_DK = 64
RET_WIDTH = 512
CHUNK = 128
MLA_HEADS = 8
MLA_NOPE = 64
MLA_ROPE = 32
MLA_V = 64
MLA_WIDTH = 512
Q_LORA = 512
KV_LORA = 256
ROPE_BASE = 10000.0
X_HEADS = 4
X_HEAD_DIM = 256
N_EXPERTS = 32
TOP_K = 4
SWIGLU_LIMIT = 7.0
SWIGLU_ALPHA = 1.702
MOE_BLOCK = 512
EPS = 1e-6

LANES = 128
VMEM_LIMIT = 56 * 1024 * 1024

TOKEN_TILE = 512
ATTN_TQ = 256
ATTN_TK = 512
COMBINE_TILE = 256
D_IN_PAD = 4 * 512 + Q_LORA + KV_LORA + LANES
NEG_BIG = -1e30

F32 = jnp.float32
BF16 = jnp.bfloat16


def _params(sem, vmem=VMEM_LIMIT):
    return pltpu.CompilerParams(dimension_semantics=sem, vmem_limit_bytes=vmem)


def _const_spec(shape):
    nd = len(shape)
    return pl.BlockSpec(shape, lambda *_: (0,) * nd, pipeline_mode=pl.Buffered(1))


def _rms(x, g):
    ms = jnp.mean(x * x, axis=-1, keepdims=True)
    return x * lax.rsqrt(ms + EPS) * g


def _dot(a, b):
    return jnp.dot(a, b, preferred_element_type=F32)


def _dot_nt(a, b):
    return lax.dot_general(a, b, (((1,), (1,)), ((), ())), preferred_element_type=F32)


def _dot_tn(a, b):
    return lax.dot_general(a, b, (((0,), (0,)), ((), ())), preferred_element_type=F32)


def _split_dot(y, a):
    hi = y.astype(BF16)
    lo = (y - hi.astype(F32)).astype(BF16)
    return _dot(hi, a) + _dot(lo, a)


def _mem_kv_kernel(mem_ref, g_ref, w_ref, o_ref):
    mn = _rms(mem_ref[...], g_ref[...])
    o_ref[...] = _dot(mn.astype(BF16), w_ref[...]).astype(BF16)


def _mem_kv(mem, g, w):
    rows, mem_len = mem.shape[0], 256
    return pl.pallas_call(
        _mem_kv_kernel,
        out_shape=jax.ShapeDtypeStruct((rows, 2 * D_MODEL), BF16),
        grid=(rows // mem_len,),
        in_specs=[pl.BlockSpec((mem_len, D_MODEL), lambda i: (i, 0)),
                  _const_spec((1, D_MODEL)),
                  _const_spec((D_MODEL, 2 * D_MODEL))],
        out_specs=pl.BlockSpec((mem_len, 2 * D_MODEL), lambda i: (i, 0)),
        compiler_params=_params(("parallel",)),
        name="mem_kv",
    )(mem, g, w)


def _rope_slab(x, c, ss, first, shift_up, shift_down):
    swap = jnp.where(first, pltpu.roll(x, shift_up, 1), pltpu.roll(x, shift_down, 1))
    return x * c + swap * ss


def _in_proj_kernel(xp_ref, xs_ref, g_ref, win_ref, qan_ref, wuq_ref, kvan_ref, wuk_ref, wuv_ref,
                    cr_ref, sr_ref, cm_ref, sm_ref,
                    rq_ref, rk_ref, rv_ref, rg_ref, qm_ref, km_ref, vm_ref, *, n_prompt_tiles):
    i = pl.program_id(0)
    x = jnp.where(i < n_prompt_tiles, xp_ref[...], xs_ref[...])
    xn = _rms(x, g_ref[...])
    proj = _dot(xn.astype(BF16), win_ref[...])

    lane = lax.broadcasted_iota(jnp.int32, (1, LANES), 1)
    ret_first = (lane % RET_DK) < (RET_DK // 2)
    mla_first = (lane >= MLA_NOPE) & (lane < MLA_NOPE + MLA_ROPE // 2)
    cr, sr, cm, sm = cr_ref[...], sr_ref[...], cm_ref[...], sm_ref[...]
    half_r, half_m = RET_DK // 2, MLA_ROPE // 2

    for s in range(RET_WIDTH // LANES):
        lo = s * LANES
        q = _rope_slab(proj[:, lo:lo + LANES], cr, sr, ret_first, LANES - half_r, half_r)
        rq_ref[:, lo:lo + LANES] = q.astype(BF16)
        k = _rope_slab(proj[:, 512 + lo:512 + lo + LANES], cr, sr, ret_first, LANES - half_r, half_r)
        rk_ref[:, lo:lo + LANES] = (k * (RET_DK ** -0.5)).astype(BF16)
    rv_ref[...] = proj[:, 1024:1536].astype(BF16)
    rg_ref[...] = proj[:, 1536:2048].astype(BF16)

    cq = _rms(proj[:, 2048:2048 + Q_LORA], qan_ref[...])
    qm = _dot(cq.astype(BF16), wuq_ref[...])
    ckv = _rms(proj[:, 2560:2560 + KV_LORA], kvan_ref[...]).astype(BF16)
    kn = _dot(ckv, wuk_ref[...])
    vm_ref[...] = _dot(ckv, wuv_ref[...]).astype(BF16)
    kr = proj[:, 2816:2816 + LANES]
    kpe = _rope_slab(kr, cm, sm, mla_first, LANES - half_m, half_m)
    q_scale = (MLA_NOPE + MLA_ROPE) ** -0.5
    for h in range(MLA_HEADS):
        lo = h * LANES
        qh = _rope_slab(qm[:, lo:lo + LANES], cm, sm, mla_first, LANES - half_m, half_m)
        qm_ref[:, lo:lo + LANES] = (qh * q_scale).astype(BF16)
        km_ref[:, lo:lo + LANES] = (kn[:, lo:lo + LANES] + kpe).astype(BF16)


def _in_proj(xp, xs, g, w_in_pad, qan, wuq_pad, kvan, wuk_pad, wuv, tabs, sp, ss):
    tm = TOKEN_TILE
    tp, ts = xp.shape[0], xs.shape[0]
    npt, nst = tp // tm, ts // tm
    t = tp + ts
    tiles_p, tiles_s = sp // tm, ss // tm

    def xp_map(i):
        return (jnp.minimum(i, npt - 1), 0)

    def xs_map(i):
        return (jnp.maximum(i - npt, 0), 0)

    def tab_map(i):
        return (jnp.where(i < npt, i % tiles_p, (i - npt) % tiles_s), 0)

    tok = lambda w: pl.BlockSpec((tm, w), lambda i: (i, 0))
    widths = (512, 512, 512, 512, 1024, 1024, 512)
    return pl.pallas_call(
        functools.partial(_in_proj_kernel, n_prompt_tiles=npt),
        out_shape=[jax.ShapeDtypeStruct((t, w), BF16) for w in widths],
        grid=(npt + nst,),
        in_specs=[pl.BlockSpec((tm, D_MODEL), xp_map), pl.BlockSpec((tm, D_MODEL), xs_map),
                  _const_spec((1, D_MODEL)), _const_spec((D_MODEL, D_IN_PAD)),
                  _const_spec((1, Q_LORA)), _const_spec((Q_LORA, MLA_HEADS * LANES)),
                  _const_spec((1, KV_LORA)), _const_spec((KV_LORA, MLA_HEADS * LANES)),
                  _const_spec((KV_LORA, MLA_WIDTH))]
                 + [pl.BlockSpec((tm, LANES), tab_map)] * 4,
        out_specs=[tok(w) for w in widths],
        compiler_params=_params(("parallel",)),
        name="in_proj",
    )(xp, xs, g, w_in_pad, qan, wuq_pad, kvan, wuk_pad, wuv, *tabs)


def _retention_kernel(lgf_ref, lgb_ref, q_ref, k_ref, v_ref, g_ref, gn_ref, o_ref, sf_scr, *, n_chunks):
    hp = pl.program_id(1)
    c = CHUNK
    lane = lax.broadcasted_iota(jnp.int32, (1, c), 1)
    row = lax.broadcasted_iota(jnp.int32, (c, 1), 0)
    lane_h0 = lane < RET_DK
    row_h0 = row < RET_DK
    lgf0, lgf1 = lgf_ref[2 * hp], lgf_ref[2 * hp + 1]
    lgb0, lgb1 = lgb_ref[2 * hp], lgb_ref[2 * hp + 1]
    lgf_lane = jnp.where(lane_h0, lgf0, lgf1)
    lgb_lane = jnp.where(lane_h0, lgb0, lgb1)
    t = row.astype(F32)
    q_dec_f = jnp.exp((t + 1.0) * lgf_lane)
    q_dec_b = jnp.exp((c - t) * lgb_lane)
    k_dec_f = jnp.exp((c - 1.0 - t) * lgf_lane)
    k_dec_b = jnp.exp(t * lgb_lane)
    chunk_dec_f = jnp.exp(c * jnp.where(row_h0, lgf0, lgf1))
    chunk_dec_b = jnp.exp(c * jnp.where(row_h0, lgb0, lgb1))
    same_head = row_h0 == lane_h0
    diff = t - lane.astype(F32)
    d_intra = []
    for lgf, lgb in ((lgf0, lgb0), (lgf1, lgb1)):
        fwd = jnp.where(diff >= 0, jnp.exp(jnp.where(diff >= 0, diff, 0.0) * lgf), 0.0)
        bwd = jnp.where(diff < 0, jnp.exp(jnp.where(diff < 0, -diff, 0.0) * lgb), 0.0)
        d_intra.append(fwd + bwd)
    avg = jnp.where(same_head, 1.0 / RET_DK, 0.0).astype(BF16)
    gn = gn_ref[...]
    head_masks = (lane_h0, jnp.logical_not(lane_h0))

    def chunk(n):
        return pl.ds(pl.multiple_of(n * c, c), c)

    def fwd_state(n, s_f):
        sf_scr[n] = s_f.astype(BF16)
        kd = (k_ref[chunk(n), :].astype(F32) * k_dec_f).astype(BF16)
        kv = _dot_tn(kd, v_ref[chunk(n), :])
        return chunk_dec_f * s_f + jnp.where(same_head, kv, 0.0)

    lax.fori_loop(0, n_chunks, fwd_state, jnp.zeros((c, c), F32))

    def out_chunk(j, s_b):
        n = n_chunks - 1 - j
        q = q_ref[chunk(n), :]
        k = k_ref[chunk(n), :]
        v = v_ref[chunk(n), :]
        qf = q.astype(F32)
        y = _dot((qf * q_dec_f).astype(BF16), sf_scr[n]) + _dot((qf * q_dec_b).astype(BF16), s_b.astype(BF16))
        for h in range(2):
            qh = jnp.where(head_masks[h], q, jnp.zeros_like(q))
            p = (_dot_nt(qh, k) * d_intra[h]).astype(BF16)
            y = y + jnp.where(head_masks[h], _dot(p, v), 0.0)
        mu = _split_dot(y, avg)
        d = y - mu
        var = _split_dot(d * d, avg)
        yn = d * lax.rsqrt(var + EPS) * gn
        gate = g_ref[chunk(n), :].astype(F32)
        o_ref[chunk(n), :] = (yn * (gate * jax.nn.sigmoid(gate))).astype(BF16)
        kd = (k.astype(F32) * k_dec_b).astype(BF16)
        return chunk_dec_b * s_b + jnp.where(same_head, _dot_tn(kd, v), 0.0)

    lax.fori_loop(0, n_chunks, out_chunk, jnp.zeros((c, c), F32))


def _retention(lgf, lgb, rq, rk, rv, rg, gn, seq_len, n_seq, row_block0):
    n_chunks = seq_len // CHUNK
    hp_count = RET_WIDTH // LANES
    blk = pl.BlockSpec((seq_len, LANES), lambda b, hp, *_: (row_block0 + b, hp))
    return pl.pallas_call(
        functools.partial(_retention_kernel, n_chunks=n_chunks),
        out_shape=jax.ShapeDtypeStruct((n_seq * seq_len, RET_WIDTH), BF16),
        grid_spec=pltpu.PrefetchScalarGridSpec(
            num_scalar_prefetch=2,
            grid=(n_seq, hp_count),
            in_specs=[blk, blk, blk, blk, pl.BlockSpec((1, LANES), lambda b, hp, *_: (0, hp))],
            out_specs=pl.BlockSpec((seq_len, LANES), lambda b, hp, *_: (b, hp)),
            scratch_shapes=[pltpu.VMEM((n_chunks, CHUNK, CHUNK), BF16)]),
        compiler_params=_params(("parallel", "parallel")),
        name="retention",
    )(lgf, lgb, rq, rk, rv, rg, gn)


def _mla_attn_kernel(q_ref, k_ref, v_ref, o_ref, *, tk, n_kv):
    tq = q_ref.shape[0]
    lane = lax.broadcasted_iota(jnp.int32, (1, LANES), 1)
    outs = []
    for h in range(2):
        q = q_ref[:, h * LANES:(h + 1) * LANES]

        def body(j, carry, h=h, q=q):
            m, l, acc = carry
            rows = pl.ds(pl.multiple_of(j * tk, tk), tk)
            s = _dot_nt(q, k_ref[rows, h * LANES:(h + 1) * LANES])
            m_new = jnp.maximum(m, jnp.max(s, axis=-1, keepdims=True))
            alpha = jnp.exp(m - m_new)
            p = jnp.exp(s - m_new)
            l = alpha * l + jnp.sum(p, axis=-1, keepdims=True)
            acc = alpha * acc + _dot(p.astype(BF16), v_ref[rows, :])
            return m_new, l, acc

        init = (jnp.full((tq, 1), NEG_BIG, F32), jnp.zeros((tq, 1), F32), jnp.zeros((tq, LANES), F32))
        _, l, acc = lax.fori_loop(0, n_kv, body, init)
        outs.append(acc / l)
    o_ref[...] = jnp.where(lane < MLA_V, outs[0], outs[1]).astype(BF16)


def _mla_attn(qm, km, vm, seq_len, n_seq, row_block0):
    tq, tk = ATTN_TQ, min(ATTN_TK, seq_len)
    hp_count = MLA_HEADS // 2
    nq = seq_len // tq
    return pl.pallas_call(
        functools.partial(_mla_attn_kernel, tk=tk, n_kv=seq_len // tk),
        out_shape=jax.ShapeDtypeStruct((n_seq * seq_len, MLA_WIDTH), BF16),
        grid=(n_seq, hp_count, nq),
        in_specs=[pl.BlockSpec((tq, 2 * LANES), lambda b, hp, i: ((row_block0 + b) * nq + i, hp)),
                  pl.BlockSpec((seq_len, 2 * LANES), lambda b, hp, i: (row_block0 + b, hp)),
                  pl.BlockSpec((seq_len, LANES), lambda b, hp, i: (row_block0 + b, hp))],
        out_specs=pl.BlockSpec((tq, LANES), lambda b, hp, i: (b * nq + i, hp)),
        compiler_params=_params(("parallel", "parallel", "parallel")),
        name="mla_attn",
    )(qm, km, vm)


def _mix_cross_kernel(xp_ref, xs_ref, rp_ref, rs_ref, mp_ref, ms_ref, wmix_ref, gx_ref, wxq_ref, kv_ref, wxo_ref,
                      gf_ref, rw_ref, rb_ref,
                      h_ref, xn_ref, idx_ref, gate_ref, rank_ref, cnt_ref, carry_scr, *, n_prompt_tiles):
    i = pl.program_id(0)
    tm = h_ref.shape[0]
    is_p = i < n_prompt_tiles
    x = jnp.where(is_p, xp_ref[...], xs_ref[...])
    ret = jnp.where(is_p, rp_ref[...], rs_ref[...])
    mla = jnp.where(is_p, mp_ref[...], ms_ref[...])
    h1 = x + _dot(ret, wmix_ref[0:RET_WIDTH, :]) + _dot(mla, wmix_ref[RET_WIDTH:, :])

    hn = _rms(h1, gx_ref[...]).astype(BF16)
    q = (_dot(hn, wxq_ref[...]) * (X_HEAD_DIM ** -0.5)).astype(BF16)
    heads = []
    for h in range(X_HEADS):
        lo = h * X_HEAD_DIM
        s = _dot_nt(q[:, lo:lo + X_HEAD_DIM], kv_ref[0, :, lo:lo + X_HEAD_DIM])
        e = jnp.exp(s - jnp.max(s, axis=-1, keepdims=True))
        p = (e / jnp.sum(e, axis=-1, keepdims=True)).astype(BF16)
        heads.append(_dot(p, kv_ref[0, :, D_MODEL + lo:D_MODEL + lo + X_HEAD_DIM]).astype(BF16))
    h2 = h1 + _dot(jnp.concatenate(heads, axis=-1), wxo_ref[...])
    h_ref[...] = h2

    xn = _rms(h2, gf_ref[...])
    xn_ref[...] = xn
    logits = _dot(xn.astype(BF16), rw_ref[...]) + rb_ref[...]
    lane = lax.broadcasted_iota(jnp.int32, (tm, LANES), 1)
    lane_f = lane.astype(F32)
    work = logits
    vals, idxs, sels = [], [], []
    for _ in range(TOP_K):
        m = jnp.max(work, axis=-1, keepdims=True)
        idx = jnp.min(jnp.where(work == m, lane_f, float(LANES)), axis=-1, keepdims=True)
        sel = lane_f == idx
        work = jnp.where(sel, -jnp.inf, work)
        vals.append(m)
        idxs.append(idx)
        sels.append(sel)
    exps = [jnp.exp(v - vals[0]) for v in vals]
    denom = exps[0] + exps[1] + exps[2] + exps[3]
    onehot = jnp.where(sels[0] | sels[1] | sels[2] | sels[3], 1.0, 0.0)

    @pl.when(i == 0)
    def _():
        carry_scr[...] = jnp.zeros_like(carry_scr)

    r_iota = lax.broadcasted_iota(jnp.int32, (tm, tm), 0)
    c_iota = lax.broadcasted_iota(jnp.int32, (tm, tm), 1)
    lower = jnp.where(c_iota < r_iota, 1.0, 0.0).astype(BF16)
    before = _dot(lower, onehot.astype(BF16)) + carry_scr[...]
    carry = carry_scr[...] + jnp.sum(onehot, axis=0, keepdims=True)
    carry_scr[...] = carry
    cnt_ref[...] = carry.astype(jnp.int32)

    idx_out = jnp.zeros((tm, LANES), F32)
    gate_out = jnp.zeros((tm, LANES), F32)
    rank_out = jnp.zeros((tm, LANES), F32)
    for k in range(TOP_K):
        rank_k = jnp.sum(jnp.where(sels[k], before, 0.0), axis=-1, keepdims=True)
        idx_out = jnp.where(lane == k, idxs[k], idx_out)
        gate_out = jnp.where(lane == k, exps[k] / denom, gate_out)
        rank_out = jnp.where(lane == k, rank_k, rank_out)
    idx_ref[...] = idx_out.astype(jnp.int32)
    gate_ref[...] = gate_out
    rank_ref[...] = rank_out.astype(jnp.int32)


def _mix_cross(xp, xs, ret_p, ret_s, mla_p, mla_s, wmix, gx, wxq, kvmem, wxo, gf, rw_pad, rb_pad, sp, ss):
    tm = TOKEN_TILE
    tp, ts = xp.shape[0], xs.shape[0]
    npt, nst = tp // tm, ts // tm
    t = tp + ts
    n_seq_p = tp // sp
    mem_len = kvmem.shape[1]

    def p_map(i):
        return (jnp.minimum(i, npt - 1), 0)

    def s_map(i):
        return (jnp.maximum(i - npt, 0), 0)

    def kv_map(i):
        return (jnp.where(i < npt, i // (sp // tm), n_seq_p + (i - npt) // (ss // tm)), 0, 0)

    tok = lambda w: pl.BlockSpec((tm, w), lambda i: (i, 0))
    return pl.pallas_call(
        functools.partial(_mix_cross_kernel, n_prompt_tiles=npt),
        out_shape=[jax.ShapeDtypeStruct((t, D_MODEL), F32), jax.ShapeDtypeStruct((t, D_MODEL), F32),
                   jax.ShapeDtypeStruct((t, LANES), jnp.int32), jax.ShapeDtypeStruct((t, LANES), F32),
                   jax.ShapeDtypeStruct((t, LANES), jnp.int32), jax.ShapeDtypeStruct((1, LANES), jnp.int32)],
        grid=(npt + nst,),
        in_specs=[pl.BlockSpec((tm, D_MODEL), p_map), pl.BlockSpec((tm, D_MODEL), s_map),
                  pl.BlockSpec((tm, RET_WIDTH), p_map), pl.BlockSpec((tm, RET_WIDTH), s_map),
                  pl.BlockSpec((tm, MLA_WIDTH), p_map), pl.BlockSpec((tm, MLA_WIDTH), s_map),
                  _const_spec((D_MODEL, D_MODEL)), _const_spec((1, D_MODEL)), _const_spec((D_MODEL, D_MODEL)),
                  pl.BlockSpec((1, mem_len, 2 * D_MODEL), kv_map),
                  _const_spec((D_MODEL, D_MODEL)), _const_spec((1, D_MODEL)),
                  _const_spec((D_MODEL, LANES)), _const_spec((1, LANES))],
        out_specs=[tok(D_MODEL), tok(D_MODEL), tok(LANES), tok(LANES), tok(LANES),
                   pl.BlockSpec((1, LANES), lambda i: (0, 0))],
        scratch_shapes=[pltpu.VMEM((1, LANES), F32)],
        compiler_params=_params(("arbitrary",)),
        name="mix_cross_router",
    )(xp, xs, ret_p, ret_s, mla_p, mla_s, wmix, gx, wxq, kvmem, wxo, gf, rw_pad, rb_pad)


def _row_copy(src, src_row, dst, dst_row, sem):
    return pltpu.make_async_copy(src.at[pl.ds(src_row, 1)], dst.at[pl.ds(dst_row, 1)], sem)


def _dispatch_kernel(dest_ref, x_ref, xs_hbm, sem):
    tm = x_ref.shape[0]

    def start(r, _):
        for k in range(TOP_K):
            _row_copy(x_ref, r, xs_hbm, dest_ref[r * TOP_K + k], sem).start()
        return 0

    lax.fori_loop(0, tm, start, 0)

    def wait(r, _):
        for k in range(TOP_K):
            _row_copy(x_ref, r, xs_hbm, dest_ref[r * TOP_K + k], sem).wait()
        return 0

    lax.fori_loop(0, tm, wait, 0)


def _dispatch(dest_flat, xn, n_rows):
    tm = TOKEN_TILE
    t = xn.shape[0]
    return pl.pallas_call(
        _dispatch_kernel,
        out_shape=jax.ShapeDtypeStruct((n_rows, D_MODEL), F32),
        grid=(t // tm,),
        in_specs=[pl.BlockSpec((tm * TOP_K,), lambda i: (i,), memory_space=pltpu.SMEM),
                  pl.BlockSpec((tm, D_MODEL), lambda i: (i, 0))],
        out_specs=pl.BlockSpec(memory_space=pl.ANY),
        scratch_shapes=[pltpu.SemaphoreType.DMA(())],
        compiler_params=_params(("arbitrary",)),
        name="moe_dispatch",
    )(dest_flat, xn)


def _experts_kernel(be_ref, bi_ref, na_ref, x_ref, wgu_ref, bgu_ref, wd_ref, bd_ref, o_ref):
    j = pl.program_id(0)

    @pl.when(j < na_ref[0])
    def _():
        x = x_ref[...].astype(BF16)
        gu = _dot(x, wgu_ref[0]) + bgu_ref[0]
        gate = jnp.minimum(gu[:, :D_MODEL], SWIGLU_LIMIT)
        up = jnp.clip(gu[:, D_MODEL:], -SWIGLU_LIMIT, SWIGLU_LIMIT)
        hid = (up + 1.0) * (gate * jax.nn.sigmoid(SWIGLU_ALPHA * gate))
        o_ref[...] = _dot(hid.astype(BF16), wd_ref[0]) + bd_ref[0]


def _experts(block_e, block_i, n_active, xs, wgu, bgu, wd, bd):
    n_blocks = xs.shape[0] // MOE_BLOCK
    row_map = lambda j, be, bi, na: (bi[j], 0)
    e_map = lambda j, be, bi, na: (be[j], 0, 0)
    return pl.pallas_call(
        _experts_kernel,
        out_shape=jax.ShapeDtypeStruct(xs.shape, F32),
        grid_spec=pltpu.PrefetchScalarGridSpec(
            num_scalar_prefetch=3,
            grid=(n_blocks,),
            in_specs=[pl.BlockSpec((MOE_BLOCK, D_MODEL), row_map),
                      pl.BlockSpec((1, D_MODEL, 2 * D_MODEL), e_map),
                      pl.BlockSpec((1, 1, 2 * D_MODEL), e_map),
                      pl.BlockSpec((1, D_MODEL, D_MODEL), e_map),
                      pl.BlockSpec((1, 1, D_MODEL), e_map)],
            out_specs=pl.BlockSpec((MOE_BLOCK, D_MODEL), row_map)),
        compiler_params=_params(("arbitrary",)),
        name="moe_experts",
    )(block_e, block_i, n_active, xs, wgu, bgu, wd, bd)


def _combine_kernel(dest_ref, gate_ref, h_ref, g_ref, ys_hbm, op_ref, os_ref, buf, sem, *, n_prompt_tiles):
    i = pl.program_id(0)
    tm = h_ref.shape[0]

    def start(r, _):
        for k in range(TOP_K):
            _row_copy(ys_hbm, dest_ref[r * TOP_K + k], buf.at[k], r, sem).start()
        return 0

    lax.fori_loop(0, tm, start, 0)

    def wait(r, _):
        for k in range(TOP_K):
            _row_copy(ys_hbm, dest_ref[r * TOP_K + k], buf.at[k], r, sem).wait()
        return 0

    lax.fori_loop(0, tm, wait, 0)

    gates = gate_ref[...]
    y = jnp.zeros((tm, D_MODEL), F32)
    for k in range(TOP_K):
        y = y + buf[k] * gates[:, k:k + 1]
    out = _rms(h_ref[...] + y, g_ref[...])

    @pl.when(i < n_prompt_tiles)
    def _():
        op_ref[...] = out

    @pl.when(i >= n_prompt_tiles)
    def _():
        os_ref[...] = out


def _combine(dest_flat, gates, h2, g_final, ys, tp, ts):
    tm = COMBINE_TILE
    npt, nst = tp // tm, ts // tm
    return pl.pallas_call(
        functools.partial(_combine_kernel, n_prompt_tiles=npt),
        out_shape=[jax.ShapeDtypeStruct((tp, D_MODEL), F32), jax.ShapeDtypeStruct((ts, D_MODEL), F32)],
        grid=(npt + nst,),
        in_specs=[pl.BlockSpec((tm * TOP_K,), lambda i: (i,), memory_space=pltpu.SMEM),
                  pl.BlockSpec((tm, LANES), lambda i: (i, 0)),
                  pl.BlockSpec((tm, D_MODEL), lambda i: (i, 0)),
                  _const_spec((1, D_MODEL)),
                  pl.BlockSpec(memory_space=pl.ANY)],
        out_specs=[pl.BlockSpec((tm, D_MODEL), lambda i: (jnp.minimum(i, npt - 1), 0)),
                   pl.BlockSpec((tm, D_MODEL), lambda i: (jnp.maximum(i - npt, 0), 0))],
        scratch_shapes=[pltpu.VMEM((TOP_K, tm, D_MODEL), F32), pltpu.SemaphoreType.DMA(())],
        compiler_params=_params(("arbitrary",)),
        name="moe_combine_norm",
    )(dest_flat, gates, h2, g_final, ys)


def _rope_tables(seq_len):
    pos = jnp.arange(seq_len, dtype=F32)[:, None]
    lane = np.arange(LANES)

    def table(half, lane_freq, first, active):
        inv = ROPE_BASE ** (-jnp.arange(half, dtype=F32) / half)
        ang = pos * inv[None, :]
        cos = jnp.cos(ang)[:, lane_freq]
        sin = jnp.sin(ang)[:, lane_freq]
        c = jnp.where(active[None, :], cos, 1.0)
        s = jnp.where(active[None, :], jnp.where(first[None, :], -sin, sin), 0.0)
        return c, s

    half_r = RET_DK // 2
    cr, sr = table(half_r, lane % half_r, (lane % RET_DK) < half_r, np.ones(LANES, bool))
    half_m = MLA_ROPE // 2
    rel = lane - MLA_NOPE
    active = (rel >= 0) & (rel < MLA_ROPE)
    cm, sm = table(half_m, np.where(active, rel % half_m, 0), active & (rel < half_m), active)
    return cr, sr, cm, sm


def kernel(x_prompt, x_sample, mem_prompt, mem_sample, norm_mix, w_in, ret_decay_fwd, ret_decay_bwd, ret_gn, q_a_norm, w_uq, kv_a_norm, w_ukv, w_mix_out, norm_cross, norm_mem, w_xq, w_xkv, w_xo, norm_ffn, router_w, router_b, w_gu, b_gu, w_down, b_down, norm_final):
    assert norm_mix.shape[0] == 1, "single layer"
    bp, sp, d = x_prompt.shape
    bs, ss, _ = x_sample.shape
    tp, ts = bp * sp, bs * ss
    t = tp + ts
    assert d == D_MODEL and sp % TOKEN_TILE == 0 and ss % TOKEN_TILE == 0 and tp % ss == 0 and sp >= ss
    mem_len = mem_prompt.shape[1]

    w_in0 = w_in[0]
    w_in_pad = jnp.zeros((D_MODEL, D_IN_PAD), F32)
    w_in_pad = w_in_pad.at[:, :2816].set(w_in0[:, :2816])
    w_in_pad = w_in_pad.at[:, 2816 + MLA_NOPE:2816 + MLA_NOPE + MLA_ROPE].set(w_in0[:, 2816:]).astype(BF16)
    wuq_pad = jnp.pad(w_uq[0].reshape(Q_LORA, MLA_HEADS, MLA_NOPE + MLA_ROPE),
                      ((0, 0), (0, 0), (0, LANES - MLA_NOPE - MLA_ROPE))).reshape(Q_LORA, MLA_HEADS * LANES).astype(BF16)
    wukv = w_ukv[0].reshape(KV_LORA, MLA_HEADS, MLA_NOPE + MLA_V)
    wuk_pad = jnp.pad(wukv[:, :, :MLA_NOPE], ((0, 0), (0, 0), (0, LANES - MLA_NOPE))).reshape(KV_LORA, MLA_HEADS * LANES).astype(BF16)
    wuv = wukv[:, :, MLA_NOPE:].reshape(KV_LORA, MLA_WIDTH).astype(BF16)
    rw_pad = jnp.pad(router_w[0], ((0, 0), (0, LANES - N_EXPERTS))).astype(BF16)
    rb_pad = jnp.pad(router_b[0].astype(F32), (0, LANES - N_EXPERTS), constant_values=NEG_BIG)[None, :]
    lgf = jnp.log1p(-jnp.exp2(ret_decay_fwd[0].astype(F32)))
    lgb = jnp.log1p(-jnp.exp2(ret_decay_bwd[0].astype(F32)))
    tabs = _rope_tables(sp)

    xp = x_prompt.reshape(tp, D_MODEL)
    xs = x_sample.reshape(ts, D_MODEL)
    mem = jnp.concatenate([mem_prompt.reshape(-1, D_MODEL), mem_sample.reshape(-1, D_MODEL)], axis=0)

    kvmem = _mem_kv(mem, norm_mem[0][None, :], w_xkv[0].astype(BF16)).reshape(bp + bs, mem_len, 2 * D_MODEL)

    rq, rk, rv, rg, qm, km, vm = _in_proj(xp, xs, norm_mix[0][None, :], w_in_pad, q_a_norm[0][None, :], wuq_pad,
                                          kv_a_norm[0][None, :], wuk_pad, wuv, tabs, sp, ss)

    gn = ret_gn[0][None, :]
    ret_p = _retention(lgf, lgb, rq, rk, rv, rg, gn, sp, bp, 0)
    ret_s = _retention(lgf, lgb, rq, rk, rv, rg, gn, ss, bs, tp // ss)
    mla_p = _mla_attn(qm, km, vm, sp, bp, 0)
    mla_s = _mla_attn(qm, km, vm, ss, bs, tp // ss)

    h2, xn, idx, gates, rank, counts = _mix_cross(
        xp, xs, ret_p, ret_s, mla_p, mla_s, w_mix_out[0].astype(BF16), norm_cross[0][None, :], w_xq[0].astype(BF16),
        kvmem, w_xo[0].astype(BF16), norm_ffn[0][None, :], rw_pad, rb_pad, sp, ss)

    counts = counts[0, :N_EXPERTS]
    padded = (counts + MOE_BLOCK - 1) // MOE_BLOCK * MOE_BLOCK
    pend = jnp.cumsum(padded)
    pstart = pend - padded
    idx4 = idx[:, :TOP_K]
    dest = (jnp.sum(jnp.where(idx4[:, :, None] == jnp.arange(N_EXPERTS)[None, None, :], pstart[None, None, :], 0), axis=-1)
            + rank[:, :TOP_K]).astype(jnp.int32).reshape(t * TOP_K)
    n_blocks = -(-(t * TOP_K + N_EXPERTS * (MOE_BLOCK - 1)) // MOE_BLOCK)
    blk = jnp.arange(n_blocks, dtype=jnp.int32)
    n_active = (pend[-1] // MOE_BLOCK).astype(jnp.int32)
    block_i = jnp.minimum(blk, n_active - 1)
    block_e = jnp.minimum(jnp.sum((block_i[:, None] * MOE_BLOCK >= pend[None, :]).astype(jnp.int32), axis=-1),
                          N_EXPERTS - 1).astype(jnp.int32)

    xs_sorted = _dispatch(dest, xn, n_blocks * MOE_BLOCK)
    ys = _experts(block_e, block_i, n_active[None], xs_sorted, w_gu[0].astype(BF16), b_gu[0][:, None, :],
                  w_down[0].astype(BF16), b_down[0][:, None, :])
    out_p, out_s = _combine(dest, gates, h2, norm_final[None, :], ys, tp, ts)
    return out_p.reshape(bp, sp, D_MODEL), out_s.reshape(bs, ss, D_MODEL)
```

```python
import functools
import math

import jax
import jax.numpy as jnp
import numpy as np
from jax import lax
from jax.experimental import pallas as pl
from jax.experimental.pallas import tpu as pltpu

D_MODEL = 1024
RET_HEADS = 8
RET_DK = 64
RET_WIDTH = 512
CHUNK = 128
MLA_HEADS = 8
MLA_NOPE = 64
MLA_ROPE = 32
MLA_V = 64
MLA_WIDTH = 512
Q_LORA = 512
KV_LORA = 256
ROPE_BASE = 10000.0
X_HEADS = 4
X_HEAD_DIM = 256
N_EXPERTS = 32
TOP_K = 4
SWIGLU_LIMIT = 7.0
SWIGLU_ALPHA = 1.702
MOE_BLOCK = 512
EPS = 1e-6

LANES = 128
VMEM_LIMIT = 56 * 1024 * 1024

TOKEN_TILE = 512
ATTN_TQ = 1024
ATTN_TK = 1024
COMBINE_TILE = 256
D_IN_PAD = 4 * 512 + Q_LORA + KV_LORA + LANES
NEG_BIG = -1e30

F32 = jnp.float32
BF16 = jnp.bfloat16


def _params(sem, vmem=VMEM_LIMIT):
    return pltpu.CompilerParams(dimension_semantics=sem, vmem_limit_bytes=vmem)


def _const_spec(shape):
    nd = len(shape)
    return pl.BlockSpec(shape, lambda *_: (0,) * nd, pipeline_mode=pl.Buffered(1))


def _rms(x, g):
    ms = jnp.mean(x * x, axis=-1, keepdims=True)
    return x * lax.rsqrt(ms + EPS) * g


def _dot(a, b):
    return jnp.dot(a, b, preferred_element_type=F32)


def _dot_nt(a, b):
    return lax.dot_general(a, b, (((1,), (1,)), ((), ())), preferred_element_type=F32)


def _dot_tn(a, b):
    return lax.dot_general(a, b, (((0,), (0,)), ((), ())), preferred_element_type=F32)


def _split_dot(y, a):
    hi = y.astype(BF16)
    lo = (y - hi.astype(F32)).astype(BF16)
    return _dot(hi, a) + _dot(lo, a)


def _mem_kv_kernel(mem_ref, g_ref, w_ref, o_ref):
    mn = _rms(mem_ref[...], g_ref[...])
    o_ref[...] = _dot(mn.astype(BF16), w_ref[...]).astype(BF16)


def _mem_kv(mem, g, w):
    rows, mem_len = mem.shape[0], 256
    return pl.pallas_call(
        _mem_kv_kernel,
        out_shape=jax.ShapeDtypeStruct((rows, 2 * D_MODEL), BF16),
        grid=(rows // mem_len,),
        in_specs=[pl.BlockSpec((mem_len, D_MODEL), lambda i: (i, 0)),
                  _const_spec((1, D_MODEL)),
                  _const_spec((D_MODEL, 2 * D_MODEL))],
        out_specs=pl.BlockSpec((mem_len, 2 * D_MODEL), lambda i: (i, 0)),
        compiler_params=_params(("parallel",)),
        name="mem_kv",
    )(mem, g, w)


def _rope_slab(x, c, ss, first, shift_up, shift_down):
    swap = jnp.where(first, pltpu.roll(x, shift_up, 1), pltpu.roll(x, shift_down, 1))
    return x * c + swap * ss


def _in_proj_kernel(xp_ref, xs_ref, g_ref, win_ref, qan_ref, wuq_ref, kvan_ref, wuk_ref, wuv_ref,
                    cr_ref, sr_ref, cm_ref, sm_ref,
                    rq_ref, rk_ref, rv_ref, rg_ref, qm_ref, km_ref, vm_ref, *, n_prompt_tiles):
    i = pl.program_id(0)
    x = jnp.where(i < n_prompt_tiles, xp_ref[...], xs_ref[...])
    xn = _rms(x, g_ref[...])
    proj = _dot(xn.astype(BF16), win_ref[...])

    lane = lax.broadcasted_iota(jnp.int32, (1, LANES), 1)
    ret_first = (lane % RET_DK) < (RET_DK // 2)
    mla_first = (lane >= MLA_NOPE) & (lane < MLA_NOPE + MLA_ROPE // 2)
    cr, sr, cm, sm = cr_ref[...], sr_ref[...], cm_ref[...], sm_ref[...]
    half_r, half_m = RET_DK // 2, MLA_ROPE // 2

    for s in range(RET_WIDTH // LANES):
        lo = s * LANES
        q = _rope_slab(proj[:, lo:lo + LANES], cr, sr, ret_first, LANES - half_r, half_r)
        rq_ref[:, lo:lo + LANES] = q.astype(BF16)
        k = _rope_slab(proj[:, 512 + lo:512 + lo + LANES], cr, sr, ret_first, LANES - half_r, half_r)
        rk_ref[:, lo:lo + LANES] = (k * (RET_DK ** -0.5)).astype(BF16)
    rv_ref[...] = proj[:, 1024:1536].astype(BF16)
    rg_ref[...] = proj[:, 1536:2048].astype(BF16)

    cq = _rms(proj[:, 2048:2048 + Q_LORA], qan_ref[...])
    qm = _dot(cq.astype(BF16), wuq_ref[...])
    ckv = _rms(proj[:, 2560:2560 + KV_LORA], kvan_ref[...]).astype(BF16)
    kn = _dot(ckv, wuk_ref[...])
    vm_ref[...] = _dot(ckv, wuv_ref[...]).astype(BF16)
    kr = proj[:, 2816:2816 + LANES]
    kpe = _rope_slab(kr, cm, sm, mla_first, LANES - half_m, half_m)
    q_scale = (MLA_NOPE + MLA_ROPE) ** -0.5 * math.log2(math.e)
    for h in range(MLA_HEADS):
        lo = h * LANES
        qh = _rope_slab(qm[:, lo:lo + LANES], cm, sm, mla_first, LANES - half_m, half_m)
        qm_ref[:, lo:lo + LANES] = (qh * q_scale).astype(BF16)
        km_ref[:, lo:lo + LANES] = (kn[:, lo:lo + LANES] + kpe).astype(BF16)


def _in_proj(xp, xs, g, w_in_pad, qan, wuq_pad, kvan, wuk_pad, wuv, tabs, sp, ss):
    tm = TOKEN_TILE
    tp, ts = xp.shape[0], xs.shape[0]
    npt, nst = tp // tm, ts // tm
    t = tp + ts
    tiles_p, tiles_s = sp // tm, ss // tm

    def xp_map(i):
        return (jnp.minimum(i, npt - 1), 0)

    def xs_map(i):
        return (jnp.maximum(i - npt, 0), 0)

    def tab_map(i):
        return (jnp.where(i < npt, i % tiles_p, (i - npt) % tiles_s), 0)

    tok = lambda w: pl.BlockSpec((tm, w), lambda i: (i, 0))
    widths = (512, 512, 512, 512, 1024, 1024, 512)
    return pl.pallas_call(
        functools.partial(_in_proj_kernel, n_prompt_tiles=npt),
        out_shape=[jax.ShapeDtypeStruct((t, w), BF16) for w in widths],
        grid=(npt + nst,),
        in_specs=[pl.BlockSpec((tm, D_MODEL), xp_map), pl.BlockSpec((tm, D_MODEL), xs_map),
                  _const_spec((1, D_MODEL)), _const_spec((D_MODEL, D_IN_PAD)),
                  _const_spec((1, Q_LORA)), _const_spec((Q_LORA, MLA_HEADS * LANES)),
                  _const_spec((1, KV_LORA)), _const_spec((KV_LORA, MLA_HEADS * LANES)),
                  _const_spec((KV_LORA, MLA_WIDTH))]
                 + [pl.BlockSpec((tm, LANES), tab_map)] * 4,
        out_specs=[tok(w) for w in widths],
        compiler_params=_params(("parallel",)),
        name="in_proj",
    )(xp, xs, g, w_in_pad, qan, wuq_pad, kvan, wuk_pad, wuv, *tabs)


def _retention_kernel(lgf_ref, lgb_ref, q_ref, k_ref, v_ref, g_ref, gn_ref, o_ref, sf_scr, *, n_chunks):
    hp = pl.program_id(1)
    c = CHUNK
    lane = lax.broadcasted_iota(jnp.int32, (1, c), 1)
    row = lax.broadcasted_iota(jnp.int32, (c, 1), 0)
    lane_h0 = lane < RET_DK
    row_h0 = row < RET_DK
    lgf0, lgf1 = lgf_ref[2 * hp], lgf_ref[2 * hp + 1]
    lgb0, lgb1 = lgb_ref[2 * hp], lgb_ref[2 * hp + 1]
    lgf_lane = jnp.where(lane_h0, lgf0, lgf1)
    lgb_lane = jnp.where(lane_h0, lgb0, lgb1)
    t = row.astype(F32)
    q_dec_f = jnp.exp((t + 1.0) * lgf_lane)
    q_dec_b = jnp.exp((c - t) * lgb_lane)
    k_dec_f = jnp.exp((c - 1.0 - t) * lgf_lane)
    k_dec_b = jnp.exp(t * lgb_lane)
    chunk_dec_f = jnp.exp(c * jnp.where(row_h0, lgf0, lgf1))
    chunk_dec_b = jnp.exp(c * jnp.where(row_h0, lgb0, lgb1))
    same_head = row_h0 == lane_h0
    diff = t - lane.astype(F32)
    d_intra = []
    for lgf, lgb in ((lgf0, lgb0), (lgf1, lgb1)):
        fwd = jnp.where(diff >= 0, jnp.exp(jnp.where(diff >= 0, diff, 0.0) * lgf), 0.0)
        bwd = jnp.where(diff < 0, jnp.exp(jnp.where(diff < 0, -diff, 0.0) * lgb), 0.0)
        d_intra.append(fwd + bwd)
    avg = jnp.where(same_head, 1.0 / RET_DK, 0.0).astype(BF16)
    gn = gn_ref[...]
    head_masks = (lane_h0, jnp.logical_not(lane_h0))

    def chunk(n):
        return pl.ds(pl.multiple_of(n * c, c), c)

    def fwd_state(n, s_f):
        sf_scr[n] = s_f.astype(BF16)
        kd = (k_ref[chunk(n), :].astype(F32) * k_dec_f).astype(BF16)
        kv = _dot_tn(kd, v_ref[chunk(n), :])
        return chunk_dec_f * s_f + jnp.where(same_head, kv, 0.0)

    lax.fori_loop(0, n_chunks, fwd_state, jnp.zeros((c, c), F32))

    def out_chunk(j, s_b):
        n = n_chunks - 1 - j
        q = q_ref[chunk(n), :]
        k = k_ref[chunk(n), :]
        v = v_ref[chunk(n), :]
        qf = q.astype(F32)
        y = _dot((qf * q_dec_f).astype(BF16), sf_scr[n]) + _dot((qf * q_dec_b).astype(BF16), s_b.astype(BF16))
        for h in range(2):
            qh = jnp.where(head_masks[h], q, jnp.zeros_like(q))
            p = (_dot_nt(qh, k) * d_intra[h]).astype(BF16)
            y = y + jnp.where(head_masks[h], _dot(p, v), 0.0)
        mu = _split_dot(y, avg)
        d = y - mu
        var = _split_dot(d * d, avg)
        yn = d * lax.rsqrt(var + EPS) * gn
        gate = g_ref[chunk(n), :].astype(F32)
        o_ref[chunk(n), :] = (yn * (gate * jax.nn.sigmoid(gate))).astype(BF16)
        kd = (k.astype(F32) * k_dec_b).astype(BF16)
        return chunk_dec_b * s_b + jnp.where(same_head, _dot_tn(kd, v), 0.0)

    lax.fori_loop(0, n_chunks, out_chunk, jnp.zeros((c, c), F32))


def _retention(lgf, lgb, rq, rk, rv, rg, gn, seq_len, n_seq, row_block0):
    n_chunks = seq_len // CHUNK
    hp_count = RET_WIDTH // LANES
    blk = pl.BlockSpec((seq_len, LANES), lambda b, hp, *_: (row_block0 + b, hp))
    return pl.pallas_call(
        functools.partial(_retention_kernel, n_chunks=n_chunks),
        out_shape=jax.ShapeDtypeStruct((n_seq * seq_len, RET_WIDTH), BF16),
        grid_spec=pltpu.PrefetchScalarGridSpec(
            num_scalar_prefetch=2,
            grid=(n_seq, hp_count),
            in_specs=[blk, blk, blk, blk, pl.BlockSpec((1, LANES), lambda b, hp, *_: (0, hp))],
            out_specs=pl.BlockSpec((seq_len, LANES), lambda b, hp, *_: (b, hp)),
            scratch_shapes=[pltpu.VMEM((n_chunks, CHUNK, CHUNK), BF16)]),
        compiler_params=_params(("parallel", "parallel")),
        name="retention",
    )(lgf, lgb, rq, rk, rv, rg, gn)


def _mla_attn_kernel(q_ref, k_ref, v_ref, o_ref, *, tk, n_kv):
    tq = q_ref.shape[0]
    lane = lax.broadcasted_iota(jnp.int32, (1, LANES), 1)
    qs = (q_ref[:, :LANES], q_ref[:, LANES:])

    def body(j, carry):
        rows = pl.ds(pl.multiple_of(j * tk, tk), tk)
        v = v_ref[rows, :]
        new = []
        for h in range(2):
            m, l, acc = carry[h]
            s = _dot_nt(qs[h], k_ref[rows, h * LANES:(h + 1) * LANES])
            m_new = jnp.maximum(m, jnp.max(s, axis=-1, keepdims=True))
            alpha = jnp.exp2(m - m_new)
            p = jnp.exp2(s - m_new)
            l = alpha * l + jnp.sum(p, axis=-1, keepdims=True)
            acc = alpha * acc + _dot(p.astype(BF16), v)
            new.append((m_new, l, acc))
        return tuple(new)

    init = (jnp.full((tq, 1), NEG_BIG, F32), jnp.zeros((tq, 1), F32), jnp.zeros((tq, LANES), F32))
    (_, l0, acc0), (_, l1, acc1) = lax.fori_loop(0, n_kv, body, (init, init), unroll=min(2, n_kv))
    o_ref[...] = jnp.where(lane < MLA_V, acc0 / l0, acc1 / l1).astype(BF16)


def _mla_attn(qm, km, vm, seq_len, n_seq, row_block0):
    tq, tk = min(ATTN_TQ, seq_len), min(ATTN_TK, seq_len)
    hp_count = MLA_HEADS // 2
    nq = seq_len // tq
    return pl.pallas_call(
        functools.partial(_mla_attn_kernel, tk=tk, n_kv=seq_len // tk),
        out_shape=jax.ShapeDtypeStruct((n_seq * seq_len, MLA_WIDTH), BF16),
        grid=(n_seq, hp_count, nq),
        in_specs=[pl.BlockSpec((tq, 2 * LANES), lambda b, hp, i: ((row_block0 + b) * nq + i, hp)),
                  pl.BlockSpec((seq_len, 2 * LANES), lambda b, hp, i: (row_block0 + b, hp)),
                  pl.BlockSpec((seq_len, LANES), lambda b, hp, i: (row_block0 + b, hp))],
        out_specs=pl.BlockSpec((tq, LANES), lambda b, hp, i: (b * nq + i, hp)),
        compiler_params=_params(("parallel", "parallel", "parallel")),
        name="mla_attn",
    )(qm, km, vm)


def _mix_cross_kernel(xp_ref, xs_ref, rp_ref, rs_ref, mp_ref, ms_ref, wmix_ref, gx_ref, wxq_ref, kv_ref, wxo_ref,
                      gf_ref, rw_ref, rb_ref,
                      h_ref, xn_ref, idx_ref, gate_ref, rank_ref, cnt_ref, carry_scr, *, n_prompt_tiles):
    i = pl.program_id(0)
    tm = h_ref.shape[0]
    is_p = i < n_prompt_tiles
    x = jnp.where(is_p, xp_ref[...], xs_ref[...])
    ret = jnp.where(is_p, rp_ref[...], rs_ref[...])
    mla = jnp.where(is_p, mp_ref[...], ms_ref[...])
    h1 = x + _dot(ret, wmix_ref[0:RET_WIDTH, :]) + _dot(mla, wmix_ref[RET_WIDTH:, :])

    hn = _rms(h1, gx_ref[...]).astype(BF16)
    q = (_dot(hn, wxq_ref[...]) * (X_HEAD_DIM ** -0.5)).astype(BF16)
    heads = []
    for h in range(X_HEADS):
        lo = h * X_HEAD_DIM
        s = _dot_nt(q[:, lo:lo + X_HEAD_DIM], kv_ref[0, :, lo:lo + X_HEAD_DIM])
        e = jnp.exp(s - jnp.max(s, axis=-1, keepdims=True))
        p = (e / jnp.sum(e, axis=-1, keepdims=True)).astype(BF16)
        heads.append(_dot(p, kv_ref[0, :, D_MODEL + lo:D_MODEL + lo + X_HEAD_DIM]).astype(BF16))
    h2 = h1 + _dot(jnp.concatenate(heads, axis=-1), wxo_ref[...])
    h_ref[...] = h2

    xn = _rms(h2, gf_ref[...])
    xn_ref[...] = xn
    logits = _dot(xn.astype(BF16), rw_ref[...]) + rb_ref[...]
    lane = lax.broadcasted_iota(jnp.int32, (tm, LANES), 1)
    lane_f = lane.astype(F32)
    work = logits
    vals, idxs, sels = [], [], []
    for _ in range(TOP_K):
        m = jnp.max(work, axis=-1, keepdims=True)
        idx = jnp.min(jnp.where(work == m, lane_f, float(LANES)), axis=-1, keepdims=True)
        sel = lane_f == idx
        work = jnp.where(sel, -jnp.inf, work)
        vals.append(m)
        idxs.append(idx)
        sels.append(sel)
    exps = [jnp.exp(v - vals[0]) for v in vals]
    denom = exps[0] + exps[1] + exps[2] + exps[3]
    onehot = jnp.where(sels[0] | sels[1] | sels[2] | sels[3], 1.0, 0.0)

    @pl.when(i == 0)
    def _():
        carry_scr[...] = jnp.zeros_like(carry_scr)

    r_iota = lax.broadcasted_iota(jnp.int32, (tm, tm), 0)
    c_iota = lax.broadcasted_iota(jnp.int32, (tm, tm), 1)
    lower = jnp.where(c_iota < r_iota, 1.0, 0.0).astype(BF16)
    before = _dot(lower, onehot.astype(BF16)) + carry_scr[...]
    carry = carry_scr[...] + jnp.sum(onehot, axis=0, keepdims=True)
    carry_scr[...] = carry
    cnt_ref[...] = carry.astype(jnp.int32)

    idx_out = jnp.zeros((tm, LANES), F32)
    gate_out = jnp.zeros((tm, LANES), F32)
    rank_out = jnp.zeros((tm, LANES), F32)
    for k in range(TOP_K):
        rank_k = jnp.sum(jnp.where(sels[k], before, 0.0), axis=-1, keepdims=True)
        idx_out = jnp.where(lane == k, idxs[k], idx_out)
        gate_out = jnp.where(lane == k, exps[k] / denom, gate_out)
        rank_out = jnp.where(lane == k, rank_k, rank_out)
    idx_ref[...] = idx_out.astype(jnp.int32)
    gate_ref[...] = gate_out
    rank_ref[...] = rank_out.astype(jnp.int32)


def _mix_cross(xp, xs, ret_p, ret_s, mla_p, mla_s, wmix, gx, wxq, kvmem, wxo, gf, rw_pad, rb_pad, sp, ss):
    tm = TOKEN_TILE
    tp, ts = xp.shape[0], xs.shape[0]
    npt, nst = tp // tm, ts // tm
    t = tp + ts
    n_seq_p = tp // sp
    mem_len = kvmem.shape[1]

    def p_map(i):
        return (jnp.minimum(i, npt - 1), 0)

    def s_map(i):
        return (jnp.maximum(i - npt, 0), 0)

    def kv_map(i):
        return (jnp.where(i < npt, i // (sp // tm), n_seq_p + (i - npt) // (ss // tm)), 0, 0)

    tok = lambda w: pl.BlockSpec((tm, w), lambda i: (i, 0))
    return pl.pallas_call(
        functools.partial(_mix_cross_kernel, n_prompt_tiles=npt),
        out_shape=[jax.ShapeDtypeStruct((t, D_MODEL), F32), jax.ShapeDtypeStruct((t, D_MODEL), F32),
                   jax.ShapeDtypeStruct((t, LANES), jnp.int32), jax.ShapeDtypeStruct((t, LANES), F32),
                   jax.ShapeDtypeStruct((t, LANES), jnp.int32), jax.ShapeDtypeStruct((1, LANES), jnp.int32)],
        grid=(npt + nst,),
        in_specs=[pl.BlockSpec((tm, D_MODEL), p_map), pl.BlockSpec((tm, D_MODEL), s_map),
                  pl.BlockSpec((tm, RET_WIDTH), p_map), pl.BlockSpec((tm, RET_WIDTH), s_map),
                  pl.BlockSpec((tm, MLA_WIDTH), p_map), pl.BlockSpec((tm, MLA_WIDTH), s_map),
                  _const_spec((D_MODEL, D_MODEL)), _const_spec((1, D_MODEL)), _const_spec((D_MODEL, D_MODEL)),
                  pl.BlockSpec((1, mem_len, 2 * D_MODEL), kv_map),
                  _const_spec((D_MODEL, D_MODEL)), _const_spec((1, D_MODEL)),
                  _const_spec((D_MODEL, LANES)), _const_spec((1, LANES))],
        out_specs=[tok(D_MODEL), tok(D_MODEL), tok(LANES), tok(LANES), tok(LANES),
                   pl.BlockSpec((1, LANES), lambda i: (0, 0))],
        scratch_shapes=[pltpu.VMEM((1, LANES), F32)],
        compiler_params=_params(("arbitrary",)),
        name="mix_cross_router",
    )(xp, xs, ret_p, ret_s, mla_p, mla_s, wmix, gx, wxq, kvmem, wxo, gf, rw_pad, rb_pad)


def _row_copy(src, src_row, dst, dst_row, sem):
    return pltpu.make_async_copy(src.at[pl.ds(src_row, 1)], dst.at[pl.ds(dst_row, 1)], sem)


def _dispatch_kernel(dest_ref, x_ref, xs_hbm, sem):
    tm = x_ref.shape[0]

    def start(r, _):
        for k in range(TOP_K):
            _row_copy(x_ref, r, xs_hbm, dest_ref[r * TOP_K + k], sem).start()
        return 0

    lax.fori_loop(0, tm, start, 0)

    def wait(r, _):
        for k in range(TOP_K):
            _row_copy(x_ref, r, xs_hbm, dest_ref[r * TOP_K + k], sem).wait()
        return 0

    lax.fori_loop(0, tm, wait, 0)


def _dispatch(dest_flat, xn, n_rows):
    tm = TOKEN_TILE
    t = xn.shape[0]
    return pl.pallas_call(
        _dispatch_kernel,
        out_shape=jax.ShapeDtypeStruct((n_rows, D_MODEL), F32),
        grid=(t // tm,),
        in_specs=[pl.BlockSpec((tm * TOP_K,), lambda i: (i,), memory_space=pltpu.SMEM),
                  pl.BlockSpec((tm, D_MODEL), lambda i: (i, 0))],
        out_specs=pl.BlockSpec(memory_space=pl.ANY),
        scratch_shapes=[pltpu.SemaphoreType.DMA(())],
        compiler_params=_params(("arbitrary",)),
        name="moe_dispatch",
    )(dest_flat, xn)


def _experts_kernel(be_ref, bi_ref, na_ref, x_ref, wgu_ref, bgu_ref, wd_ref, bd_ref, o_ref):
    j = pl.program_id(0)

    @pl.when(j < na_ref[0])
    def _():
        x = x_ref[...].astype(BF16)
        gu = _dot(x, wgu_ref[0]) + bgu_ref[0]
        gate = jnp.minimum(gu[:, :D_MODEL], SWIGLU_LIMIT)
        up = jnp.clip(gu[:, D_MODEL:], -SWIGLU_LIMIT, SWIGLU_LIMIT)
        hid = (up + 1.0) * (gate * jax.nn.sigmoid(SWIGLU_ALPHA * gate))
        o_ref[...] = _dot(hid.astype(BF16), wd_ref[0]) + bd_ref[0]


def _experts(block_e, block_i, n_active, xs, wgu, bgu, wd, bd):
    n_blocks = xs.shape[0] // MOE_BLOCK
    row_map = lambda j, be, bi, na: (bi[j], 0)
    e_map = lambda j, be, bi, na: (be[j], 0, 0)
    return pl.pallas_call(
        _experts_kernel,
        out_shape=jax.ShapeDtypeStruct(xs.shape, F32),
        grid_spec=pltpu.PrefetchScalarGridSpec(
            num_scalar_prefetch=3,
            grid=(n_blocks,),
            in_specs=[pl.BlockSpec((MOE_BLOCK, D_MODEL), row_map),
                      pl.BlockSpec((1, D_MODEL, 2 * D_MODEL), e_map),
                      pl.BlockSpec((1, 1, 2 * D_MODEL), e_map),
                      pl.BlockSpec((1, D_MODEL, D_MODEL), e_map),
                      pl.BlockSpec((1, 1, D_MODEL), e_map)],
            out_specs=pl.BlockSpec((MOE_BLOCK, D_MODEL), row_map)),
        compiler_params=_params(("arbitrary",)),
        name="moe_experts",
    )(block_e, block_i, n_active, xs, wgu, bgu, wd, bd)


def _combine_kernel(dest_ref, gate_ref, h_ref, g_ref, ys_hbm, op_ref, os_ref, buf, sem, *, n_prompt_tiles):
    i = pl.program_id(0)
    tm = h_ref.shape[0]

    def start(r, _):
        for k in range(TOP_K):
            _row_copy(ys_hbm, dest_ref[r * TOP_K + k], buf.at[k], r, sem).start()
        return 0

    lax.fori_loop(0, tm, start, 0)

    def wait(r, _):
        for k in range(TOP_K):
            _row_copy(ys_hbm, dest_ref[r * TOP_K + k], buf.at[k], r, sem).wait()
        return 0

    lax.fori_loop(0, tm, wait, 0)

    gates = gate_ref[...]
    y = jnp.zeros((tm, D_MODEL), F32)
    for k in range(TOP_K):
        y = y + buf[k] * gates[:, k:k + 1]
    out = _rms(h_ref[...] + y, g_ref[...])

    @pl.when(i < n_prompt_tiles)
    def _():
        op_ref[...] = out

    @pl.when(i >= n_prompt_tiles)
    def _():
        os_ref[...] = out


def _combine(dest_flat, gates, h2, g_final, ys, tp, ts):
    tm = COMBINE_TILE
    npt, nst = tp // tm, ts // tm
    return pl.pallas_call(
        functools.partial(_combine_kernel, n_prompt_tiles=npt),
        out_shape=[jax.ShapeDtypeStruct((tp, D_MODEL), F32), jax.ShapeDtypeStruct((ts, D_MODEL), F32)],
        grid=(npt + nst,),
        in_specs=[pl.BlockSpec((tm * TOP_K,), lambda i: (i,), memory_space=pltpu.SMEM),
                  pl.BlockSpec((tm, LANES), lambda i: (i, 0)),
                  pl.BlockSpec((tm, D_MODEL), lambda i: (i, 0)),
                  _const_spec((1, D_MODEL)),
                  pl.BlockSpec(memory_space=pl.ANY)],
        out_specs=[pl.BlockSpec((tm, D_MODEL), lambda i: (jnp.minimum(i, npt - 1), 0)),
                   pl.BlockSpec((tm, D_MODEL), lambda i: (jnp.maximum(i - npt, 0), 0))],
        scratch_shapes=[pltpu.VMEM((TOP_K, tm, D_MODEL), F32), pltpu.SemaphoreType.DMA(())],
        compiler_params=_params(("arbitrary",)),
        name="moe_combine_norm",
    )(dest_flat, gates, h2, g_final, ys)


def _rope_tables(seq_len):
    pos = jnp.arange(seq_len, dtype=F32)[:, None]
    lane = np.arange(LANES)

    def table(half, lane_freq, first, active):
        inv = ROPE_BASE ** (-jnp.arange(half, dtype=F32) / half)
        ang = pos * inv[None, :]
        cos = jnp.cos(ang)[:, lane_freq]
        sin = jnp.sin(ang)[:, lane_freq]
        c = jnp.where(active[None, :], cos, 1.0)
        s = jnp.where(active[None, :], jnp.where(first[None, :], -sin, sin), 0.0)
        return c, s

    half_r = RET_DK // 2
    cr, sr = table(half_r, lane % half_r, (lane % RET_DK) < half_r, np.ones(LANES, bool))
    half_m = MLA_ROPE // 2
    rel = lane - MLA_NOPE
    active = (rel >= 0) & (rel < MLA_ROPE)
    cm, sm = table(half_m, np.where(active, rel % half_m, 0), active & (rel < half_m), active)
    return cr, sr, cm, sm


def kernel(x_prompt, x_sample, mem_prompt, mem_sample, norm_mix, w_in, ret_decay_fwd, ret_decay_bwd, ret_gn, q_a_norm, w_uq, kv_a_norm, w_ukv, w_mix_out, norm_cross, norm_mem, w_xq, w_xkv, w_xo, norm_ffn, router_w, router_b, w_gu, b_gu, w_down, b_down, norm_final):
    assert norm_mix.shape[0] == 1, "single layer"
    bp, sp, d = x_prompt.shape
    bs, ss, _ = x_sample.shape
    tp, ts = bp * sp, bs * ss
    t = tp + ts
    assert d == D_MODEL and sp % TOKEN_TILE == 0 and ss % TOKEN_TILE == 0 and tp % ss == 0 and sp >= ss
    mem_len = mem_prompt.shape[1]

    w_in0 = w_in[0]
    w_in_pad = jnp.zeros((D_MODEL, D_IN_PAD), F32)
    w_in_pad = w_in_pad.at[:, :2816].set(w_in0[:, :2816])
    w_in_pad = w_in_pad.at[:, 2816 + MLA_NOPE:2816 + MLA_NOPE + MLA_ROPE].set(w_in0[:, 2816:]).astype(BF16)
    wuq_pad = jnp.pad(w_uq[0].reshape(Q_LORA, MLA_HEADS, MLA_NOPE + MLA_ROPE),
                      ((0, 0), (0, 0), (0, LANES - MLA_NOPE - MLA_ROPE))).reshape(Q_LORA, MLA_HEADS * LANES).astype(BF16)
    wukv = w_ukv[0].reshape(KV_LORA, MLA_HEADS, MLA_NOPE + MLA_V)
    wuk_pad = jnp.pad(wukv[:, :, :MLA_NOPE], ((0, 0), (0, 0), (0, LANES - MLA_NOPE))).reshape(KV_LORA, MLA_HEADS * LANES).astype(BF16)
    wuv = wukv[:, :, MLA_NOPE:].reshape(KV_LORA, MLA_WIDTH).astype(BF16)
    rw_pad = jnp.pad(router_w[0], ((0, 0), (0, LANES - N_EXPERTS))).astype(BF16)
    rb_pad = jnp.pad(router_b[0].astype(F32), (0, LANES - N_EXPERTS), constant_values=NEG_BIG)[None, :]
    lgf = jnp.log1p(-jnp.exp2(ret_decay_fwd[0].astype(F32)))
    lgb = jnp.log1p(-jnp.exp2(ret_decay_bwd[0].astype(F32)))
    tabs = _rope_tables(sp)

    xp = x_prompt.reshape(tp, D_MODEL)
    xs = x_sample.reshape(ts, D_MODEL)
    mem = jnp.concatenate([mem_prompt.reshape(-1, D_MODEL), mem_sample.reshape(-1, D_MODEL)], axis=0)

    kvmem = _mem_kv(mem, norm_mem[0][None, :], w_xkv[0].astype(BF16)).reshape(bp + bs, mem_len, 2 * D_MODEL)

    rq, rk, rv, rg, qm, km, vm = _in_proj(xp, xs, norm_mix[0][None, :], w_in_pad, q_a_norm[0][None, :], wuq_pad,
                                          kv_a_norm[0][None, :], wuk_pad, wuv, tabs, sp, ss)

    gn = ret_gn[0][None, :]
    ret_p = _retention(lgf, lgb, rq, rk, rv, rg, gn, sp, bp, 0)
    ret_s = _retention(lgf, lgb, rq, rk, rv, rg, gn, ss, bs, tp // ss)
    mla_p = _mla_attn(qm, km, vm, sp, bp, 0)
    mla_s = _mla_attn(qm, km, vm, ss, bs, tp // ss)

    h2, xn, idx, gates, rank, counts = _mix_cross(
        xp, xs, ret_p, ret_s, mla_p, mla_s, w_mix_out[0].astype(BF16), norm_cross[0][None, :], w_xq[0].astype(BF16),
        kvmem, w_xo[0].astype(BF16), norm_ffn[0][None, :], rw_pad, rb_pad, sp, ss)

    counts = counts[0, :N_EXPERTS]
    padded = (counts + MOE_BLOCK - 1) // MOE_BLOCK * MOE_BLOCK
    pend = jnp.cumsum(padded)
    pstart = pend - padded
    idx4 = idx[:, :TOP_K]
    dest = (jnp.sum(jnp.where(idx4[:, :, None] == jnp.arange(N_EXPERTS)[None, None, :], pstart[None, None, :], 0), axis=-1)
            + rank[:, :TOP_K]).astype(jnp.int32).reshape(t * TOP_K)
    n_blocks = -(-(t * TOP_K + N_EXPERTS * (MOE_BLOCK - 1)) // MOE_BLOCK)
    blk = jnp.arange(n_blocks, dtype=jnp.int32)
    n_active = (pend[-1] // MOE_BLOCK).astype(jnp.int32)
    block_i = jnp.minimum(blk, n_active - 1)
    block_e = jnp.minimum(jnp.sum((block_i[:, None] * MOE_BLOCK >= pend[None, :]).astype(jnp.int32), axis=-1),
                          N_EXPERTS - 1).astype(jnp.int32)

    xs_sorted = _dispatch(dest, xn, n_blocks * MOE_BLOCK)
    ys = _experts(block_e, block_i, n_active[None], xs_sorted, w_gu[0].astype(BF16), b_gu[0][:, None, :],
                  w_down[0].astype(BF16), b_down[0][:, None, :])
    out_p, out_s = _combine(dest, gates, h2, norm_final[None, :], ys, tp, ts)
    return out_p.reshape(bp, sp, D_MODEL), out_s.reshape(bs, ss, D_MODEL)
```

```python
import functools
import math

import jax
import jax.numpy as jnp
import numpy as np
from jax import lax
from jax.experimental import pallas as pl
from jax.experimental.pallas import tpu as pltpu

D_MODEL = 1024
RET_HEADS = 8
RET_DK = 64
RET_WIDTH = 512
CHUNK = 128
MLA_HEADS = 8
MLA_NOPE = 64
MLA_ROPE = 32
MLA_V = 64
MLA_WIDTH = 512
Q_LORA = 512
KV_LORA = 256
ROPE_BASE = 10000.0
X_HEADS = 4
X_HEAD_DIM = 256
N_EXPERTS = 32
TOP_K = 4
SWIGLU_LIMIT = 7.0
SWIGLU_ALPHA = 1.702
MOE_BLOCK = 512
EPS = 1e-6

LANES = 128
VMEM_LIMIT = 56 * 1024 * 1024

TOKEN_TILE = 512
ATTN_TQ = 1024
ATTN_TK = 1024
ROW_CHUNK = 8
LOCAL_STEP = 256
LOCAL_ROWS = -(-(TOKEN_TILE * 4 + 32 * (ROW_CHUNK - 1)) // LOCAL_STEP) * LOCAL_STEP
D_IN_PAD = 4 * 512 + Q_LORA + KV_LORA + LANES
NEG_BIG = -1e30

F32 = jnp.float32
BF16 = jnp.bfloat16


def _params(sem, vmem=VMEM_LIMIT):
    return pltpu.CompilerParams(dimension_semantics=sem, vmem_limit_bytes=vmem)


def _const_spec(shape):
    nd = len(shape)
    return pl.BlockSpec(shape, lambda *_: (0,) * nd, pipeline_mode=pl.Buffered(1))


def _rms(x, g):
    ms = jnp.mean(x * x, axis=-1, keepdims=True)
    return x * lax.rsqrt(ms + EPS) * g


def _dot(a, b):
    return jnp.dot(a, b, preferred_element_type=F32)


def _dot_nt(a, b):
    return lax.dot_general(a, b, (((1,), (1,)), ((), ())), preferred_element_type=F32)


def _dot_tn(a, b):
    return lax.dot_general(a, b, (((0,), (0,)), ((), ())), preferred_element_type=F32)


def _split_dot(y, a):
    hi = y.astype(BF16)
    lo = (y - hi.astype(F32)).astype(BF16)
    return _dot(hi, a) + _dot(lo, a)


def _mem_kv_kernel(mem_ref, g_ref, w_ref, o_ref):
    mn = _rms(mem_ref[...], g_ref[...])
    o_ref[...] = _dot(mn.astype(BF16), w_ref[...]).astype(BF16)


def _mem_kv(mem, g, w):
    rows, mem_len = mem.shape[0], 256
    return pl.pallas_call(
        _mem_kv_kernel,
        out_shape=jax.ShapeDtypeStruct((rows, 2 * D_MODEL), BF16),
        grid=(rows // mem_len,),
        in_specs=[pl.BlockSpec((mem_len, D_MODEL), lambda i: (i, 0)),
                  _const_spec((1, D_MODEL)),
                  _const_spec((D_MODEL, 2 * D_MODEL))],
        out_specs=pl.BlockSpec((mem_len, 2 * D_MODEL), lambda i: (i, 0)),
        compiler_params=_params(("arbitrary",)),
        name="mem_kv",
    )(mem, g, w)


def _rope_slab(x, c, ss, first, shift_up, shift_down):
    swap = jnp.where(first, pltpu.roll(x, shift_up, 1), pltpu.roll(x, shift_down, 1))
    return x * c + swap * ss


def _in_proj_kernel(xp_ref, xs_ref, g_ref, win_ref, qan_ref, wuq_ref, kvan_ref, wuk_ref, wuv_ref,
                    cr_ref, sr_ref, cm_ref, sm_ref,
                    rq_ref, rk_ref, rv_ref, rg_ref, qm_ref, km_ref, vm_ref, *, n_prompt_tiles):
    i = pl.program_id(0)
    x = jnp.where(i < n_prompt_tiles, xp_ref[...], xs_ref[...])
    xn = _rms(x, g_ref[...])
    proj = _dot(xn.astype(BF16), win_ref[...])

    lane = lax.broadcasted_iota(jnp.int32, (1, LANES), 1)
    ret_first = (lane % RET_DK) < (RET_DK // 2)
    mla_first = (lane >= MLA_NOPE) & (lane < MLA_NOPE + MLA_ROPE // 2)
    cr, sr, cm, sm = cr_ref[...], sr_ref[...], cm_ref[...], sm_ref[...]
    half_r, half_m = RET_DK // 2, MLA_ROPE // 2

    for s in range(RET_WIDTH // LANES):
        lo = s * LANES
        q = _rope_slab(proj[:, lo:lo + LANES], cr, sr, ret_first, LANES - half_r, half_r)
        rq_ref[:, lo:lo + LANES] = q.astype(BF16)
        k = _rope_slab(proj[:, 512 + lo:512 + lo + LANES], cr, sr, ret_first, LANES - half_r, half_r)
        rk_ref[:, lo:lo + LANES] = (k * (RET_DK ** -0.5)).astype(BF16)
    rv_ref[...] = proj[:, 1024:1536].astype(BF16)
    rg_ref[...] = proj[:, 1536:2048].astype(BF16)

    cq = _rms(proj[:, 2048:2048 + Q_LORA], qan_ref[...])
    qm = _dot(cq.astype(BF16), wuq_ref[...])
    ckv = _rms(proj[:, 2560:2560 + KV_LORA], kvan_ref[...]).astype(BF16)
    kn = _dot(ckv, wuk_ref[...])
    vm_ref[...] = _dot(ckv, wuv_ref[...]).astype(BF16)
    kr = proj[:, 2816:2816 + LANES]
    kpe = _rope_slab(kr, cm, sm, mla_first, LANES - half_m, half_m)
    q_scale = (MLA_NOPE + MLA_ROPE) ** -0.5 * math.log2(math.e)
    for h in range(MLA_HEADS):
        lo = h * LANES
        qh = _rope_slab(qm[:, lo:lo + LANES], cm, sm, mla_first, LANES - half_m, half_m)
        qm_ref[:, lo:lo + LANES] = (qh * q_scale).astype(BF16)
        km_ref[:, lo:lo + LANES] = (kn[:, lo:lo + LANES] + kpe).astype(BF16)


def _in_proj(xp, xs, g, w_in_pad, qan, wuq_pad, kvan, wuk_pad, wuv, tabs, sp, ss):
    tm = TOKEN_TILE
    tp, ts = xp.shape[0], xs.shape[0]
    npt, nst = tp // tm, ts // tm
    t = tp + ts
    tiles_p, tiles_s = sp // tm, ss // tm

    def xp_map(i):
        return (jnp.minimum(i, npt - 1), 0)

    def xs_map(i):
        return (jnp.maximum(i - npt, 0), 0)

    def tab_map(i):
        return (jnp.where(i < npt, i % tiles_p, (i - npt) % tiles_s), 0)

    tok = lambda w: pl.BlockSpec((tm, w), lambda i: (i, 0))
    widths = (512, 512, 512, 512, 1024, 1024, 512)
    return pl.pallas_call(
        functools.partial(_in_proj_kernel, n_prompt_tiles=npt),
        out_shape=[jax.ShapeDtypeStruct((t, w), BF16) for w in widths],
        grid=(npt + nst,),
        in_specs=[pl.BlockSpec((tm, D_MODEL), xp_map), pl.BlockSpec((tm, D_MODEL), xs_map),
                  _const_spec((1, D_MODEL)), _const_spec((D_MODEL, D_IN_PAD)),
                  _const_spec((1, Q_LORA)), _const_spec((Q_LORA, MLA_HEADS * LANES)),
                  _const_spec((1, KV_LORA)), _const_spec((KV_LORA, MLA_HEADS * LANES)),
                  _const_spec((KV_LORA, MLA_WIDTH))]
                 + [pl.BlockSpec((tm, LANES), tab_map)] * 4,
        out_specs=[tok(w) for w in widths],
        compiler_params=_params(("arbitrary",)),
        name="in_proj",
    )(xp, xs, g, w_in_pad, qan, wuq_pad, kvan, wuk_pad, wuv, *tabs)


def _retention_kernel(lgf_ref, lgb_ref, q_ref, k_ref, v_ref, g_ref, gn_ref, o_ref, sf_scr, *, n_chunks):
    hp = pl.program_id(1)
    c = CHUNK
    lane = lax.broadcasted_iota(jnp.int32, (1, c), 1)
    row = lax.broadcasted_iota(jnp.int32, (c, 1), 0)
    lane_h0 = lane < RET_DK
    row_h0 = row < RET_DK
    lgf0, lgf1 = lgf_ref[2 * hp], lgf_ref[2 * hp + 1]
    lgb0, lgb1 = lgb_ref[2 * hp], lgb_ref[2 * hp + 1]
    lgf_lane = jnp.where(lane_h0, lgf0, lgf1)
    lgb_lane = jnp.where(lane_h0, lgb0, lgb1)
    t = row.astype(F32)
    q_dec_f = jnp.exp((t + 1.0) * lgf_lane)
    q_dec_b = jnp.exp((c - t) * lgb_lane)
    k_dec_f = jnp.exp((c - 1.0 - t) * lgf_lane)
    k_dec_b = jnp.exp(t * lgb_lane)
    chunk_dec_f = jnp.exp(c * jnp.where(row_h0, lgf0, lgf1))
    chunk_dec_b = jnp.exp(c * jnp.where(row_h0, lgb0, lgb1))
    same_head = row_h0 == lane_h0
    diff = t - lane.astype(F32)
    d_intra = []
    for lgf, lgb in ((lgf0, lgb0), (lgf1, lgb1)):
        fwd = jnp.where(diff >= 0, jnp.exp(jnp.where(diff >= 0, diff, 0.0) * lgf), 0.0)
        bwd = jnp.where(diff < 0, jnp.exp(jnp.where(diff < 0, -diff, 0.0) * lgb), 0.0)
        d_intra.append(fwd + bwd)
    avg = jnp.where(same_head, 1.0 / RET_DK, 0.0).astype(BF16)
    gn = gn_ref[...]
    head_masks = (lane_h0, jnp.logical_not(lane_h0))

    def chunk(n):
        return pl.ds(pl.multiple_of(n * c, c), c)

    def fwd_state(n, s_f):
        sf_scr[n] = s_f.astype(BF16)
        kd = (k_ref[chunk(n), :].astype(F32) * k_dec_f).astype(BF16)
        kv = _dot_tn(kd, v_ref[chunk(n), :])
        return chunk_dec_f * s_f + jnp.where(same_head, kv, 0.0)

    lax.fori_loop(0, n_chunks, fwd_state, jnp.zeros((c, c), F32))

    def out_chunk(j, s_b):
        n = n_chunks - 1 - j
        q = q_ref[chunk(n), :]
        k = k_ref[chunk(n), :]
        v = v_ref[chunk(n), :]
        qf = q.astype(F32)
        y = _dot((qf * q_dec_f).astype(BF16), sf_scr[n]) + _dot((qf * q_dec_b).astype(BF16), s_b.astype(BF16))
        for h in range(2):
            qh = jnp.where(head_masks[h], q, jnp.zeros_like(q))
            p = (_dot_nt(qh, k) * d_intra[h]).astype(BF16)
            y = y + jnp.where(head_masks[h], _dot(p, v), 0.0)
        mu = _split_dot(y, avg)
        d = y - mu
        var = _split_dot(d * d, avg)
        yn = d * lax.rsqrt(var + EPS) * gn
        gate = g_ref[chunk(n), :].astype(F32)
        o_ref[chunk(n), :] = (yn * (gate * jax.nn.sigmoid(gate))).astype(BF16)
        kd = (k.astype(F32) * k_dec_b).astype(BF16)
        return chunk_dec_b * s_b + jnp.where(same_head, _dot_tn(kd, v), 0.0)

    lax.fori_loop(0, n_chunks, out_chunk, jnp.zeros((c, c), F32))


def _retention(lgf, lgb, rq, rk, rv, rg, gn, seq_len, n_seq, row_block0):
    n_chunks = seq_len // CHUNK
    hp_count = RET_WIDTH // LANES
    blk = pl.BlockSpec((seq_len, LANES), lambda b, hp, *_: (row_block0 + b, hp))
    return pl.pallas_call(
        functools.partial(_retention_kernel, n_chunks=n_chunks),
        out_shape=jax.ShapeDtypeStruct((n_seq * seq_len, RET_WIDTH), BF16),
        grid_spec=pltpu.PrefetchScalarGridSpec(
            num_scalar_prefetch=2,
            grid=(n_seq, hp_count),
            in_specs=[blk, blk, blk, blk, pl.BlockSpec((1, LANES), lambda b, hp, *_: (0, hp))],
            out_specs=pl.BlockSpec((seq_len, LANES), lambda b, hp, *_: (b, hp)),
            scratch_shapes=[pltpu.VMEM((n_chunks, CHUNK, CHUNK), BF16)]),
        compiler_params=_params(("arbitrary", "arbitrary")),
        name="retention",
    )(lgf, lgb, rq, rk, rv, rg, gn)


def _mla_attn_kernel(q_ref, k_ref, v_ref, o_ref, *, tk, n_kv):
    tq = q_ref.shape[0]
    lane = lax.broadcasted_iota(jnp.int32, (1, LANES), 1)
    qs = (q_ref[:, :LANES], q_ref[:, LANES:])

    def body(j, carry):
        rows = pl.ds(pl.multiple_of(j * tk, tk), tk)
        v = v_ref[rows, :]
        new = []
        for h in range(2):
            m, l, acc = carry[h]
            s = _dot_nt(qs[h], k_ref[rows, h * LANES:(h + 1) * LANES])
            m_new = jnp.maximum(m, jnp.max(s, axis=-1, keepdims=True))
            alpha = jnp.exp2(m - m_new)
            p = jnp.exp2(s - m_new)
            l = alpha * l + jnp.sum(p, axis=-1, keepdims=True)
            acc = alpha * acc + _dot(p.astype(BF16), v)
            new.append((m_new, l, acc))
        return tuple(new)

    init = (jnp.full((tq, 1), NEG_BIG, F32), jnp.zeros((tq, 1), F32), jnp.zeros((tq, LANES), F32))
    (_, l0, acc0), (_, l1, acc1) = lax.fori_loop(0, n_kv, body, (init, init), unroll=min(2, n_kv))
    o_ref[...] = jnp.where(lane < MLA_V, acc0 / l0, acc1 / l1).astype(BF16)


def _mla_attn(qm, km, vm, seq_len, n_seq, row_block0):
    tq, tk = min(ATTN_TQ, seq_len), min(ATTN_TK, seq_len)
    hp_count = MLA_HEADS // 2
    nq = seq_len // tq
    return pl.pallas_call(
        functools.partial(_mla_attn_kernel, tk=tk, n_kv=seq_len // tk),
        out_shape=jax.ShapeDtypeStruct((n_seq * seq_len, MLA_WIDTH), BF16),
        grid=(n_seq, hp_count, nq),
        in_specs=[pl.BlockSpec((tq, 2 * LANES), lambda b, hp, i: ((row_block0 + b) * nq + i, hp)),
                  pl.BlockSpec((seq_len, 2 * LANES), lambda b, hp, i: (row_block0 + b, hp)),
                  pl.BlockSpec((seq_len, LANES), lambda b, hp, i: (row_block0 + b, hp))],
        out_specs=pl.BlockSpec((tq, LANES), lambda b, hp, i: (b * nq + i, hp)),
        compiler_params=_params(("arbitrary", "arbitrary", "arbitrary")),
        name="mla_attn",
    )(qm, km, vm)


def _mix_cross_kernel(xp_ref, xs_ref, rp_ref, rs_ref, mp_ref, ms_ref, wmix_ref, gx_ref, wxq_ref, kv_ref, wxo_ref,
                      gf_ref, rw_ref, rb_ref,
                      h_ref, xn_ref, gate_ref, slot_ref, stat_ref, cnt_ref, carry_scr, *, n_prompt_tiles):
    i = pl.program_id(0)
    tm = h_ref.shape[0]
    is_p = i < n_prompt_tiles
    x = jnp.where(is_p, xp_ref[...], xs_ref[...])
    ret = jnp.where(is_p, rp_ref[...], rs_ref[...])
    mla = jnp.where(is_p, mp_ref[...], ms_ref[...])
    h1 = x + _dot(ret, wmix_ref[0:RET_WIDTH, :]) + _dot(mla, wmix_ref[RET_WIDTH:, :])

    hn = _rms(h1, gx_ref[...]).astype(BF16)
    q = (_dot(hn, wxq_ref[...]) * (X_HEAD_DIM ** -0.5)).astype(BF16)
    heads = []
    for h in range(X_HEADS):
        lo = h * X_HEAD_DIM
        s = _dot_nt(q[:, lo:lo + X_HEAD_DIM], kv_ref[0, :, lo:lo + X_HEAD_DIM])
        e = jnp.exp(s - jnp.max(s, axis=-1, keepdims=True))
        p = (e / jnp.sum(e, axis=-1, keepdims=True)).astype(BF16)
        heads.append(_dot(p, kv_ref[0, :, D_MODEL + lo:D_MODEL + lo + X_HEAD_DIM]).astype(BF16))
    h2 = h1 + _dot(jnp.concatenate(heads, axis=-1), wxo_ref[...])
    h_ref[...] = h2

    xn = _rms(h2, gf_ref[...]).astype(BF16)
    xn_ref[...] = xn
    logits = _dot(xn, rw_ref[...]) + rb_ref[...]
    lane = lax.broadcasted_iota(jnp.int32, (tm, LANES), 1)
    lane_f = lane.astype(F32)
    work = logits
    vals, idxs, sels = [], [], []
    for _ in range(TOP_K):
        m = jnp.max(work, axis=-1, keepdims=True)
        idx = jnp.min(jnp.where(work == m, lane_f, float(LANES)), axis=-1, keepdims=True)
        sel = lane_f == idx
        work = jnp.where(sel, -jnp.inf, work)
        vals.append(m)
        idxs.append(idx)
        sels.append(sel)
    exps = [jnp.exp(v - vals[0]) for v in vals]
    denom = exps[0] + exps[1] + exps[2] + exps[3]
    onehot = jnp.where(sels[0] | sels[1] | sels[2] | sels[3], 1.0, 0.0)

    @pl.when(i == 0)
    def _():
        carry_scr[...] = jnp.zeros_like(carry_scr)

    r_iota = lax.broadcasted_iota(jnp.int32, (tm, tm), 0)
    c_iota = lax.broadcasted_iota(jnp.int32, (tm, tm), 1)
    lower = jnp.where(c_iota < r_iota, 1.0, 0.0).astype(BF16)
    before = _dot(lower, onehot.astype(BF16))
    tile_cnt = jnp.sum(onehot, axis=0, keepdims=True)
    group_rows = jnp.floor((tile_cnt + (ROW_CHUNK - 1)) * (1.0 / ROW_CHUNK)) * ROW_CHUNK
    carry_before = carry_scr[...]
    carry = carry_before + group_rows
    carry_scr[...] = carry
    cnt_ref[...] = carry.astype(jnp.int32)
    e_row = lax.broadcasted_iota(jnp.int32, (LANES, LANES), 0)
    e_col = lax.broadcasted_iota(jnp.int32, (LANES, LANES), 1)
    earlier = jnp.where(e_row < e_col, 1.0, 0.0).astype(BF16)
    group_start = _dot(jnp.broadcast_to(group_rows, (8, LANES)).astype(BF16), earlier)[0:1, :]
    row8 = lax.broadcasted_iota(jnp.int32, (8, LANES), 0)
    stats = jnp.where(row8 == 0, tile_cnt, jnp.where(row8 == 1, carry_before, jnp.where(row8 == 2, group_start, 0.0)))
    stat_ref[0] = stats.astype(jnp.int32)

    place = before + group_start
    gate_out = jnp.zeros((tm, LANES), F32)
    slot_out = jnp.zeros((tm, LANES), F32)
    for k in range(TOP_K):
        slot_k = jnp.sum(jnp.where(sels[k], place, 0.0), axis=-1, keepdims=True)
        gate_out = jnp.where(lane == k, exps[k] / denom, gate_out)
        slot_out = jnp.where(lane == k, slot_k, slot_out)
    gate_ref[...] = gate_out
    slot_ref[...] = slot_out.astype(jnp.int32)


def _mix_cross(xp, xs, ret_p, ret_s, mla_p, mla_s, wmix, gx, wxq, kvmem, wxo, gf, rw_pad, rb_pad, sp, ss):
    tm = TOKEN_TILE
    tp, ts = xp.shape[0], xs.shape[0]
    npt, nst = tp // tm, ts // tm
    t = tp + ts
    n_seq_p = tp // sp
    mem_len = kvmem.shape[1]

    def p_map(i):
        return (jnp.minimum(i, npt - 1), 0)

    def s_map(i):
        return (jnp.maximum(i - npt, 0), 0)

    def kv_map(i):
        return (jnp.where(i < npt, i // (sp // tm), n_seq_p + (i - npt) // (ss // tm)), 0, 0)

    tok = lambda w: pl.BlockSpec((tm, w), lambda i: (i, 0))
    return pl.pallas_call(
        functools.partial(_mix_cross_kernel, n_prompt_tiles=npt),
        out_shape=[jax.ShapeDtypeStruct((t, D_MODEL), F32), jax.ShapeDtypeStruct((t, D_MODEL), BF16),
                   jax.ShapeDtypeStruct((t, LANES), F32), jax.ShapeDtypeStruct((t, LANES), jnp.int32),
                   jax.ShapeDtypeStruct((npt + nst, 8, LANES), jnp.int32), jax.ShapeDtypeStruct((1, LANES), jnp.int32)],
        grid=(npt + nst,),
        in_specs=[pl.BlockSpec((tm, D_MODEL), p_map), pl.BlockSpec((tm, D_MODEL), s_map),
                  pl.BlockSpec((tm, RET_WIDTH), p_map), pl.BlockSpec((tm, RET_WIDTH), s_map),
                  pl.BlockSpec((tm, MLA_WIDTH), p_map), pl.BlockSpec((tm, MLA_WIDTH), s_map),
                  _const_spec((D_MODEL, D_MODEL)), _const_spec((1, D_MODEL)), _const_spec((D_MODEL, D_MODEL)),
                  pl.BlockSpec((1, mem_len, 2 * D_MODEL), kv_map),
                  _const_spec((D_MODEL, D_MODEL)), _const_spec((1, D_MODEL)),
                  _const_spec((D_MODEL, LANES)), _const_spec((1, LANES))],
        out_specs=[tok(D_MODEL), tok(D_MODEL), tok(LANES), tok(LANES),
                   pl.BlockSpec((1, 8, LANES), lambda i: (i, 0, 0)),
                   pl.BlockSpec((1, LANES), lambda i: (0, 0))],
        scratch_shapes=[pltpu.VMEM((1, LANES), F32)],
        compiler_params=_params(("arbitrary",)),
        name="mix_cross_router",
    )(xp, xs, ret_p, ret_s, mla_p, mla_s, wmix, gx, wxq, kvmem, wxo, gf, rw_pad, rb_pad)


def _to_rows(a):
    eye = (lax.broadcasted_iota(jnp.int32, (LANES, LANES), 0)
           == lax.broadcasted_iota(jnp.int32, (LANES, LANES), 1)).astype(BF16)
    hi = a.astype(BF16)
    r1 = a - hi.astype(F32)
    mid = r1.astype(BF16)
    lo = (r1 - mid.astype(F32)).astype(BF16)
    return _dot_nt(eye, hi) + _dot_nt(eye, mid) + _dot_nt(eye, lo)


def _chunk_copies(tile, nch_ref, lstart_ref, gstart_ref, local, hbm, sem, to_hbm, act):
    def per_expert(e, _):
        base = tile * N_EXPERTS + e
        ls = lstart_ref[base]
        gs = gstart_ref[base]

        def per_chunk(c, _):
            l_rows = local.at[pl.ds(pl.multiple_of(ls + c * ROW_CHUNK, ROW_CHUNK), ROW_CHUNK)]
            g_rows = hbm.at[pl.ds(pl.multiple_of(gs + c * ROW_CHUNK, ROW_CHUNK), ROW_CHUNK)]
            act(pltpu.make_async_copy(l_rows, g_rows, sem) if to_hbm else pltpu.make_async_copy(g_rows, l_rows, sem))
            return 0

        lax.fori_loop(0, nch_ref[base], per_chunk, 0)
        return 0

    lax.fori_loop(0, N_EXPERTS, per_expert, 0)


def _start(cp):
    cp.start()


def _wait(cp):
    cp.wait()


def _dispatch_kernel(nch_ref, lstart_ref, gstart_ref, zstart_ref, znch_ref,
                     x_ref, slot_ref, xs_hbm, buf, zero_scr, sem, zsem):
    i = pl.program_id(0)
    last = pl.num_programs(0) - 1
    tm = x_ref.shape[0]
    b = i % 2
    slot_rows = _to_rows(slot_ref[...].astype(F32))
    x = x_ref[...]

    def build(s, _):
        r0 = pl.multiple_of(s * LOCAL_STEP, LOCAL_STEP)
        rid = (lax.broadcasted_iota(jnp.int32, (LOCAL_STEP, tm), 0) + r0).astype(F32)
        hit = rid == slot_rows[0:1, :]
        for k in range(1, TOP_K):
            hit = hit | (rid == slot_rows[k:k + 1, :])
        buf[b, pl.ds(r0, LOCAL_STEP), :] = _dot(jnp.where(hit, 1.0, 0.0).astype(BF16), x)
        return 0

    lax.fori_loop(0, LOCAL_ROWS // LOCAL_STEP, build, 0)

    _chunk_copies(i, nch_ref, lstart_ref, gstart_ref, buf.at[b], xs_hbm, sem.at[b], True, _start)

    @pl.when(i > 0)
    def _():
        _chunk_copies(i - 1, nch_ref, lstart_ref, gstart_ref, buf.at[1 - b], xs_hbm, sem.at[1 - b], True, _wait)

    @pl.when(i == last)
    def _():
        _chunk_copies(i, nch_ref, lstart_ref, gstart_ref, buf.at[b], xs_hbm, sem.at[b], True, _wait)
        zero_scr[...] = jnp.zeros_like(zero_scr)

        def fill(act):
            def per_expert(e, _):
                def per_chunk(c, _):
                    row = pl.multiple_of(zstart_ref[e] + c * ROW_CHUNK, ROW_CHUNK)
                    act(pltpu.make_async_copy(zero_scr, xs_hbm.at[pl.ds(row, ROW_CHUNK)], zsem))
                    return 0

                lax.fori_loop(0, znch_ref[e], per_chunk, 0)
                return 0

            lax.fori_loop(0, N_EXPERTS, per_expert, 0)

        fill(_start)
        fill(_wait)


def _dispatch(tables, ztables, xn, slot, n_rows):
    tm = TOKEN_TILE
    t = xn.shape[0]
    return pl.pallas_call(
        _dispatch_kernel,
        out_shape=jax.ShapeDtypeStruct((n_rows, D_MODEL), F32),
        grid_spec=pltpu.PrefetchScalarGridSpec(
            num_scalar_prefetch=5,
            grid=(t // tm,),
            in_specs=[pl.BlockSpec((tm, D_MODEL), lambda i, *_: (i, 0)),
                      pl.BlockSpec((tm, LANES), lambda i, *_: (i, 0))],
            out_specs=pl.BlockSpec(memory_space=pl.ANY),
            scratch_shapes=[pltpu.VMEM((2, LOCAL_ROWS, D_MODEL), F32), pltpu.VMEM((ROW_CHUNK, D_MODEL), F32),
                            pltpu.SemaphoreType.DMA((2,)), pltpu.SemaphoreType.DMA(())]),
        compiler_params=_params(("arbitrary",)),
        name="moe_dispatch",
    )(*tables, *ztables, xn, slot)


def _experts_kernel(be_ref, bi_ref, na_ref, x_ref, wgu_ref, bgu_ref, wd_ref, bd_ref, o_ref):
    j = pl.program_id(0)

    @pl.when(j < na_ref[0])
    def _():
        x = x_ref[...].astype(BF16)
        gu = _dot(x, wgu_ref[0]) + bgu_ref[0]
        gate = jnp.minimum(gu[:, :D_MODEL], SWIGLU_LIMIT)
        up = jnp.clip(gu[:, D_MODEL:], -SWIGLU_LIMIT, SWIGLU_LIMIT)
        hid = (up + 1.0) * (gate * jax.nn.sigmoid(SWIGLU_ALPHA * gate))
        o_ref[...] = _dot(hid.astype(BF16), wd_ref[0]) + bd_ref[0]


def _experts(block_e, block_i, n_active, xs, wgu, bgu, wd, bd):
    n_blocks = xs.shape[0] // MOE_BLOCK
    row_map = lambda j, be, bi, na: (bi[j], 0)
    e_map = lambda j, be, bi, na: (be[j], 0, 0)
    return pl.pallas_call(
        _experts_kernel,
        out_shape=jax.ShapeDtypeStruct(xs.shape, F32),
        grid_spec=pltpu.PrefetchScalarGridSpec(
            num_scalar_prefetch=3,
            grid=(n_blocks,),
            in_specs=[pl.BlockSpec((MOE_BLOCK, D_MODEL), row_map),
                      pl.BlockSpec((1, D_MODEL, 2 * D_MODEL), e_map),
                      pl.BlockSpec((1, 1, 2 * D_MODEL), e_map),
                      pl.BlockSpec((1, D_MODEL, D_MODEL), e_map),
                      pl.BlockSpec((1, 1, D_MODEL), e_map)],
            out_specs=pl.BlockSpec((MOE_BLOCK, D_MODEL), row_map)),
        compiler_params=_params(("arbitrary",)),
        name="moe_experts",
    )(block_e, block_i, n_active, xs, wgu, bgu, wd, bd)


def _combine_kernel(nch_ref, lstart_ref, gstart_ref, slot_ref, gate_ref, h_ref, g_ref, ys_hbm, op_ref, os_ref,
                    buf, sem, *, n_prompt_tiles):
    i = pl.program_id(0)
    n = pl.num_programs(0)
    tm = h_ref.shape[0]
    b = i % 2

    @pl.when(i == 0)
    def _():
        buf[...] = jnp.zeros_like(buf)
        _chunk_copies(0, nch_ref, lstart_ref, gstart_ref, buf.at[0], ys_hbm, sem.at[0], False, _start)

    @pl.when(i + 1 < n)
    def _():
        _chunk_copies(i + 1, nch_ref, lstart_ref, gstart_ref, buf.at[1 - b], ys_hbm, sem.at[1 - b], False, _start)

    _chunk_copies(i, nch_ref, lstart_ref, gstart_ref, buf.at[b], ys_hbm, sem.at[b], False, _wait)

    slot = slot_ref[...].astype(F32)
    slot_rows = _to_rows(slot)
    gate_rows = _to_rows(gate_ref[...])

    def step(s, y):
        r0 = pl.multiple_of(s * LOCAL_STEP, LOCAL_STEP)
        cid = (lax.broadcasted_iota(jnp.int32, (tm, LOCAL_STEP), 1) + r0).astype(F32)
        rid = (lax.broadcasted_iota(jnp.int32, (LOCAL_STEP, tm), 0) + r0).astype(F32)
        take = cid == slot[:, 0:1]
        row_gate = jnp.where(rid == slot_rows[0:1, :], gate_rows[0:1, :], 0.0)
        for k in range(1, TOP_K):
            take = take | (cid == slot[:, k:k + 1])
            row_gate = row_gate + jnp.where(rid == slot_rows[k:k + 1, :], gate_rows[k:k + 1, :], 0.0)
        g_col = jnp.sum(row_gate, axis=-1, keepdims=True)
        rows = buf[b, pl.ds(r0, LOCAL_STEP), :]
        scaled = jnp.where(g_col != 0.0, rows * g_col, 0.0).astype(BF16)
        return y + _dot(jnp.where(take, 1.0, 0.0).astype(BF16), scaled)

    y = lax.fori_loop(0, LOCAL_ROWS // LOCAL_STEP, step, jnp.zeros((tm, D_MODEL), F32))
    out = _rms(h_ref[...] + y, g_ref[...])

    @pl.when(i < n_prompt_tiles)
    def _():
        op_ref[...] = out

    @pl.when(i >= n_prompt_tiles)
    def _():
        os_ref[...] = out


def _combine(tables, slot, gates, h2, g_final, ys, tp, ts):
    tm = TOKEN_TILE
    npt, nst = tp // tm, ts // tm
    tok = lambda w: pl.BlockSpec((tm, w), lambda i, *_: (i, 0))
    return pl.pallas_call(
        functools.partial(_combine_kernel, n_prompt_tiles=npt),
        out_shape=[jax.ShapeDtypeStruct((tp, D_MODEL), F32), jax.ShapeDtypeStruct((ts, D_MODEL), F32)],
        grid_spec=pltpu.PrefetchScalarGridSpec(
            num_scalar_prefetch=3,
            grid=(npt + nst,),
            in_specs=[tok(LANES), tok(LANES), tok(D_MODEL),
                      pl.BlockSpec((1, D_MODEL), lambda i, *_: (0, 0)),
                      pl.BlockSpec(memory_space=pl.ANY)],
            out_specs=[pl.BlockSpec((tm, D_MODEL), lambda i, *_: (jnp.minimum(i, npt - 1), 0)),
                       pl.BlockSpec((tm, D_MODEL), lambda i, *_: (jnp.maximum(i - npt, 0), 0))],
            scratch_shapes=[pltpu.VMEM((2, LOCAL_ROWS, D_MODEL), F32), pltpu.SemaphoreType.DMA((2,))]),
        compiler_params=_params(("arbitrary",)),
        name="moe_combine_norm",
    )(*tables, slot, gates, h2, g_final, ys)


def _rope_tables(seq_len):
    pos = jnp.arange(seq_len, dtype=F32)[:, None]
    lane = np.arange(LANES)

    def table(half, lane_freq, first, active):
        inv = ROPE_BASE ** (-jnp.arange(half, dtype=F32) / half)
        ang = pos * inv[None, :]
        cos = jnp.cos(ang)[:, lane_freq]
        sin = jnp.sin(ang)[:, lane_freq]
        c = jnp.where(active[None, :], cos, 1.0)
        s = jnp.where(active[None, :], jnp.where(first[None, :], -sin, sin), 0.0)
        return c, s

    half_r = RET_DK // 2
    cr, sr = table(half_r, lane % half_r, (lane % RET_DK) < half_r, np.ones(LANES, bool))
    half_m = MLA_ROPE // 2
    rel = lane - MLA_NOPE
    active = (rel >= 0) & (rel < MLA_ROPE)
    cm, sm = table(half_m, np.where(active, rel % half_m, 0), active & (rel < half_m), active)
    return cr, sr, cm, sm


def kernel(x_prompt, x_sample, mem_prompt, mem_sample, norm_mix, w_in, ret_decay_fwd, ret_decay_bwd, ret_gn, q_a_norm, w_uq, kv_a_norm, w_ukv, w_mix_out, norm_cross, norm_mem, w_xq, w_xkv, w_xo, norm_ffn, router_w, router_b, w_gu, b_gu, w_down, b_down, norm_final):
    assert norm_mix.shape[0] == 1, "single layer"
    bp, sp, d = x_prompt.shape
    bs, ss, _ = x_sample.shape
    tp, ts = bp * sp, bs * ss
    t = tp + ts
    assert d == D_MODEL and sp % TOKEN_TILE == 0 and ss % TOKEN_TILE == 0 and tp % ss == 0 and sp >= ss
    mem_len = mem_prompt.shape[1]

    w_in0 = w_in[0]
    w_in_pad = jnp.zeros((D_MODEL, D_IN_PAD), F32)
    w_in_pad = w_in_pad.at[:, :2816].set(w_in0[:, :2816])
    w_in_pad = w_in_pad.at[:, 2816 + MLA_NOPE:2816 + MLA_NOPE + MLA_ROPE].set(w_in0[:, 2816:]).astype(BF16)
    wuq_pad = jnp.pad(w_uq[0].reshape(Q_LORA, MLA_HEADS, MLA_NOPE + MLA_ROPE),
                      ((0, 0), (0, 0), (0, LANES - MLA_NOPE - MLA_ROPE))).reshape(Q_LORA, MLA_HEADS * LANES).astype(BF16)
    wukv = w_ukv[0].reshape(KV_LORA, MLA_HEADS, MLA_NOPE + MLA_V)
    wuk_pad = jnp.pad(wukv[:, :, :MLA_NOPE], ((0, 0), (0, 0), (0, LANES - MLA_NOPE))).reshape(KV_LORA, MLA_HEADS * LANES).astype(BF16)
    wuv = wukv[:, :, MLA_NOPE:].reshape(KV_LORA, MLA_WIDTH).astype(BF16)
    rw_pad = jnp.pad(router_w[0], ((0, 0), (0, LANES - N_EXPERTS))).astype(BF16)
    rb_pad = jnp.pad(router_b[0].astype(F32), (0, LANES - N_EXPERTS), constant_values=NEG_BIG)[None, :]
    lgf = jnp.log1p(-jnp.exp2(ret_decay_fwd[0].astype(F32)))
    lgb = jnp.log1p(-jnp.exp2(ret_decay_bwd[0].astype(F32)))
    tabs = _rope_tables(sp)

    xp = x_prompt.reshape(tp, D_MODEL)
    xs = x_sample.reshape(ts, D_MODEL)
    mem = jnp.concatenate([mem_prompt.reshape(-1, D_MODEL), mem_sample.reshape(-1, D_MODEL)], axis=0)

    kvmem = _mem_kv(mem, norm_mem[0][None, :], w_xkv[0].astype(BF16)).reshape(bp + bs, mem_len, 2 * D_MODEL)

    rq, rk, rv, rg, qm, km, vm = _in_proj(xp, xs, norm_mix[0][None, :], w_in_pad, q_a_norm[0][None, :], wuq_pad,
                                          kv_a_norm[0][None, :], wuk_pad, wuv, tabs, sp, ss)

    gn = ret_gn[0][None, :]
    ret_p = _retention(lgf, lgb, rq, rk, rv, rg, gn, sp, bp, 0)
    ret_s = _retention(lgf, lgb, rq, rk, rv, rg, gn, ss, bs, tp // ss)
    mla_p = _mla_attn(qm, km, vm, sp, bp, 0)
    mla_s = _mla_attn(qm, km, vm, ss, bs, tp // ss)

    h2, xn, gates, slot, stats, counts = _mix_cross(
        xp, xs, ret_p, ret_s, mla_p, mla_s, w_mix_out[0].astype(BF16), norm_cross[0][None, :], w_xq[0].astype(BF16),
        kvmem, w_xo[0].astype(BF16), norm_ffn[0][None, :], rw_pad, rb_pad, sp, ss)

    used = counts[0, :N_EXPERTS]
    padded = (used + MOE_BLOCK - 1) // MOE_BLOCK * MOE_BLOCK
    pend = jnp.cumsum(padded)
    pstart = pend - padded
    tile_cnt, tile_before, tile_lstart = (stats[:, r, :N_EXPERTS] for r in range(3))
    n_tiles = stats.shape[0]
    tables = (((tile_cnt + ROW_CHUNK - 1) // ROW_CHUNK).reshape(-1),
              tile_lstart.reshape(-1),
              (pstart[None, :] + tile_before).reshape(-1))
    ztables = (pstart + used, (padded - used) // ROW_CHUNK)
    n_blocks = -(-(t * TOP_K + N_EXPERTS * (n_tiles * (ROW_CHUNK - 1) + MOE_BLOCK - 1)) // MOE_BLOCK)
    blk = jnp.arange(n_blocks, dtype=jnp.int32)
    n_active = (pend[-1] // MOE_BLOCK).astype(jnp.int32)
    block_i = jnp.minimum(blk, n_active - 1)
    block_e = jnp.minimum(jnp.sum((block_i[:, None] * MOE_BLOCK >= pend[None, :]).astype(jnp.int32), axis=-1),
                          N_EXPERTS - 1).astype(jnp.int32)

    xs_sorted = _dispatch(tables, ztables, xn, slot, n_blocks * MOE_BLOCK)
    ys = _experts(block_e, block_i, n_active[None], xs_sorted, w_gu[0].astype(BF16), b_gu[0][:, None, :],
                  w_down[0].astype(BF16), b_down[0][:, None, :])
    out_p, out_s = _combine(tables, slot, gates, h2, norm_final[None, :], ys, tp, ts)
    return out_p.reshape(bp, sp, D_MODEL), out_s.reshape(bs, ss, D_MODEL)
```

```python
import functools
import math

import jax
import jax.numpy as jnp
import numpy as np
from jax import lax
from jax.experimental import pallas as pl
from jax.experimental.pallas import tpu as pltpu

D_MODEL = 1024
RET_HEADS = 8
RET_DK = 64
RET_WIDTH = 512
CHUNK = 128
MLA_HEADS = 8
MLA_NOPE = 64
MLA_ROPE = 32
MLA_V = 64
MLA_WIDTH = 512
Q_LORA = 512
KV_LORA = 256
ROPE_BASE = 10000.0
X_HEADS = 4
X_HEAD_DIM = 256
N_EXPERTS = 32
TOP_K = 4
SWIGLU_LIMIT = 7.0
SWIGLU_ALPHA = 1.702
MOE_BLOCK = 512
EPS = 1e-6

LANES = 128
VMEM_LIMIT = 56 * 1024 * 1024

TOKEN_TILE = 512
ATTN_TQ = 1024
ATTN_TK = 1024
ROW_CHUNK = 8
GROUP_BITS = (TOKEN_TILE // ROW_CHUNK).bit_length()
LOCAL_STEP = 256
LOCAL_ROWS = -(-(TOKEN_TILE * 4 + 32 * (ROW_CHUNK - 1)) // LOCAL_STEP) * LOCAL_STEP
D_IN_PAD = 4 * 512 + Q_LORA + KV_LORA + LANES
NEG_BIG = -1e30

F32 = jnp.float32
BF16 = jnp.bfloat16


def _params(sem, vmem=VMEM_LIMIT):
    return pltpu.CompilerParams(dimension_semantics=sem, vmem_limit_bytes=vmem)


def _const_spec(shape):
    nd = len(shape)
    return pl.BlockSpec(shape, lambda *_: (0,) * nd, pipeline_mode=pl.Buffered(1))


def _rms(x, g):
    ms = jnp.mean(x * x, axis=-1, keepdims=True)
    return x * lax.rsqrt(ms + EPS) * g


def _dot(a, b):
    return jnp.dot(a, b, preferred_element_type=F32)


def _dot_nt(a, b):
    return lax.dot_general(a, b, (((1,), (1,)), ((), ())), preferred_element_type=F32)


def _dot_tn(a, b):
    return lax.dot_general(a, b, (((0,), (0,)), ((), ())), preferred_element_type=F32)


def _split_dot(y, a):
    hi = y.astype(BF16)
    lo = (y - hi.astype(F32)).astype(BF16)
    return _dot(hi, a) + _dot(lo, a)


def _mem_kv_kernel(mem_ref, g_ref, w_ref, o_ref):
    mn = _rms(mem_ref[...], g_ref[...])
    o_ref[...] = _dot(mn.astype(BF16), w_ref[...]).astype(BF16)


def _mem_kv(mem, g, w):
    rows, mem_len = mem.shape[0], 256
    return pl.pallas_call(
        _mem_kv_kernel,
        out_shape=jax.ShapeDtypeStruct((rows, 2 * D_MODEL), BF16),
        grid=(rows // mem_len,),
        in_specs=[pl.BlockSpec((mem_len, D_MODEL), lambda i: (i, 0)),
                  _const_spec((1, D_MODEL)),
                  _const_spec((D_MODEL, 2 * D_MODEL))],
        out_specs=pl.BlockSpec((mem_len, 2 * D_MODEL), lambda i: (i, 0)),
        compiler_params=_params(("arbitrary",)),
        name="mem_kv",
    )(mem, g, w)


def _rope_slab(x, c, ss, first, shift_up, shift_down):
    swap = jnp.where(first, pltpu.roll(x, shift_up, 1), pltpu.roll(x, shift_down, 1))
    return x * c + swap * ss


def _in_proj_kernel(xp_ref, xs_ref, g_ref, win_ref, qan_ref, wuq_ref, kvan_ref, wuk_ref, wuv_ref,
                    cr_ref, sr_ref, cm_ref, sm_ref,
                    rq_ref, rk_ref, rv_ref, rg_ref, qm_ref, km_ref, vm_ref, *, n_prompt_tiles):
    i = pl.program_id(0)
    x = jnp.where(i < n_prompt_tiles, xp_ref[...], xs_ref[...])
    xn = _rms(x, g_ref[...])
    proj = _dot(xn.astype(BF16), win_ref[...])

    lane = lax.broadcasted_iota(jnp.int32, (1, LANES), 1)
    ret_first = (lane % RET_DK) < (RET_DK // 2)
    mla_first = (lane >= MLA_NOPE) & (lane < MLA_NOPE + MLA_ROPE // 2)
    cr, sr, cm, sm = cr_ref[...], sr_ref[...], cm_ref[...], sm_ref[...]
    half_r, half_m = RET_DK // 2, MLA_ROPE // 2

    for s in range(RET_WIDTH // LANES):
        lo = s * LANES
        q = _rope_slab(proj[:, lo:lo + LANES], cr, sr, ret_first, LANES - half_r, half_r)
        rq_ref[:, lo:lo + LANES] = q.astype(BF16)
        k = _rope_slab(proj[:, 512 + lo:512 + lo + LANES], cr, sr, ret_first, LANES - half_r, half_r)
        rk_ref[:, lo:lo + LANES] = (k * (RET_DK ** -0.5)).astype(BF16)
    rv_ref[...] = proj[:, 1024:1536].astype(BF16)
    rg_ref[...] = proj[:, 1536:2048].astype(BF16)

    cq = _rms(proj[:, 2048:2048 + Q_LORA], qan_ref[...])
    qm = _dot(cq.astype(BF16), wuq_ref[...])
    ckv = _rms(proj[:, 2560:2560 + KV_LORA], kvan_ref[...]).astype(BF16)
    kn = _dot(ckv, wuk_ref[...])
    vm_ref[...] = _dot(ckv, wuv_ref[...]).astype(BF16)
    kr = proj[:, 2816:2816 + LANES]
    kpe = _rope_slab(kr, cm, sm, mla_first, LANES - half_m, half_m)
    q_scale = (MLA_NOPE + MLA_ROPE) ** -0.5 * math.log2(math.e)
    for h in range(MLA_HEADS):
        lo = h * LANES
        qh = _rope_slab(qm[:, lo:lo + LANES], cm, sm, mla_first, LANES - half_m, half_m)
        qm_ref[:, lo:lo + LANES] = (qh * q_scale).astype(BF16)
        km_ref[:, lo:lo + LANES] = (kn[:, lo:lo + LANES] + kpe).astype(BF16)


def _in_proj(xp, xs, g, w_in_pad, qan, wuq_pad, kvan, wuk_pad, wuv, tabs, sp, ss):
    tm = TOKEN_TILE
    tp, ts = xp.shape[0], xs.shape[0]
    npt, nst = tp // tm, ts // tm
    t = tp + ts
    tiles_p, tiles_s = sp // tm, ss // tm

    def xp_map(i):
        return (jnp.minimum(i, npt - 1), 0)

    def xs_map(i):
        return (jnp.maximum(i - npt, 0), 0)

    def tab_map(i):
        return (jnp.where(i < npt, i % tiles_p, (i - npt) % tiles_s), 0)

    tok = lambda w: pl.BlockSpec((tm, w), lambda i: (i, 0))
    widths = (512, 512, 512, 512, 1024, 1024, 512)
    return pl.pallas_call(
        functools.partial(_in_proj_kernel, n_prompt_tiles=npt),
        out_shape=[jax.ShapeDtypeStruct((t, w), BF16) for w in widths],
        grid=(npt + nst,),
        in_specs=[pl.BlockSpec((tm, D_MODEL), xp_map), pl.BlockSpec((tm, D_MODEL), xs_map),
                  _const_spec((1, D_MODEL)), _const_spec((D_MODEL, D_IN_PAD)),
                  _const_spec((1, Q_LORA)), _const_spec((Q_LORA, MLA_HEADS * LANES)),
                  _const_spec((1, KV_LORA)), _const_spec((KV_LORA, MLA_HEADS * LANES)),
                  _const_spec((KV_LORA, MLA_WIDTH))]
                 + [pl.BlockSpec((tm, LANES), tab_map)] * 4,
        out_specs=[tok(w) for w in widths],
        compiler_params=_params(("arbitrary",)),
        name="in_proj",
    )(xp, xs, g, w_in_pad, qan, wuq_pad, kvan, wuk_pad, wuv, *tabs)


def _retention_kernel(lgf_ref, lgb_ref, q_ref, k_ref, v_ref, g_ref, gn_ref, o_ref, sf_scr, *, n_chunks):
    hp = pl.program_id(1)
    c = CHUNK
    lane = lax.broadcasted_iota(jnp.int32, (1, c), 1)
    row = lax.broadcasted_iota(jnp.int32, (c, 1), 0)
    lane_h0 = lane < RET_DK
    row_h0 = row < RET_DK
    lgf0, lgf1 = lgf_ref[2 * hp], lgf_ref[2 * hp + 1]
    lgb0, lgb1 = lgb_ref[2 * hp], lgb_ref[2 * hp + 1]
    lgf_lane = jnp.where(lane_h0, lgf0, lgf1)
    lgb_lane = jnp.where(lane_h0, lgb0, lgb1)
    t = row.astype(F32)
    q_dec_f = jnp.exp((t + 1.0) * lgf_lane)
    q_dec_b = jnp.exp((c - t) * lgb_lane)
    k_dec_f = jnp.exp((c - 1.0 - t) * lgf_lane)
    k_dec_b = jnp.exp(t * lgb_lane)
    chunk_dec_f = jnp.exp(c * jnp.where(row_h0, lgf0, lgf1))
    chunk_dec_b = jnp.exp(c * jnp.where(row_h0, lgb0, lgb1))
    same_head = row_h0 == lane_h0
    diff = t - lane.astype(F32)
    d_intra = []
    for lgf, lgb in ((lgf0, lgb0), (lgf1, lgb1)):
        fwd = jnp.where(diff >= 0, jnp.exp(jnp.where(diff >= 0, diff, 0.0) * lgf), 0.0)
        bwd = jnp.where(diff < 0, jnp.exp(jnp.where(diff < 0, -diff, 0.0) * lgb), 0.0)
        d_intra.append(fwd + bwd)
    avg = jnp.where(same_head, 1.0 / RET_DK, 0.0).astype(BF16)
    gn = gn_ref[...]
    head_masks = (lane_h0, jnp.logical_not(lane_h0))

    def chunk(n):
        return pl.ds(pl.multiple_of(n * c, c), c)

    def fwd_state(n, s_f):
        sf_scr[n] = s_f.astype(BF16)
        kd = (k_ref[chunk(n), :].astype(F32) * k_dec_f).astype(BF16)
        kv = _dot_tn(kd, v_ref[chunk(n), :])
        return chunk_dec_f * s_f + jnp.where(same_head, kv, 0.0)

    lax.fori_loop(0, n_chunks, fwd_state, jnp.zeros((c, c), F32))

    def out_chunk(j, s_b):
        n = n_chunks - 1 - j
        q = q_ref[chunk(n), :]
        k = k_ref[chunk(n), :]
        v = v_ref[chunk(n), :]
        qf = q.astype(F32)
        y = _dot((qf * q_dec_f).astype(BF16), sf_scr[n]) + _dot((qf * q_dec_b).astype(BF16), s_b.astype(BF16))
        for h in range(2):
            qh = jnp.where(head_masks[h], q, jnp.zeros_like(q))
            p = (_dot_nt(qh, k) * d_intra[h]).astype(BF16)
            y = y + jnp.where(head_masks[h], _dot(p, v), 0.0)
        mu = _split_dot(y, avg)
        d = y - mu
        var = _split_dot(d * d, avg)
        yn = d * lax.rsqrt(var + EPS) * gn
        gate = g_ref[chunk(n), :].astype(F32)
        o_ref[chunk(n), :] = (yn * (gate * jax.nn.sigmoid(gate))).astype(BF16)
        kd = (k.astype(F32) * k_dec_b).astype(BF16)
        return chunk_dec_b * s_b + jnp.where(same_head, _dot_tn(kd, v), 0.0)

    lax.fori_loop(0, n_chunks, out_chunk, jnp.zeros((c, c), F32))


def _retention(lgf, lgb, rq, rk, rv, rg, gn, seq_len, n_seq, row_block0):
    n_chunks = seq_len // CHUNK
    hp_count = RET_WIDTH // LANES
    blk = pl.BlockSpec((seq_len, LANES), lambda b, hp, *_: (row_block0 + b, hp))
    return pl.pallas_call(
        functools.partial(_retention_kernel, n_chunks=n_chunks),
        out_shape=jax.ShapeDtypeStruct((n_seq * seq_len, RET_WIDTH), BF16),
        grid_spec=pltpu.PrefetchScalarGridSpec(
            num_scalar_prefetch=2,
            grid=(n_seq, hp_count),
            in_specs=[blk, blk, blk, blk, pl.BlockSpec((1, LANES), lambda b, hp, *_: (0, hp))],
            out_specs=pl.BlockSpec((seq_len, LANES), lambda b, hp, *_: (b, hp)),
            scratch_shapes=[pltpu.VMEM((n_chunks, CHUNK, CHUNK), BF16)]),
        compiler_params=_params(("arbitrary", "arbitrary")),
        name="retention",
    )(lgf, lgb, rq, rk, rv, rg, gn)


def _mla_attn_kernel(q_ref, k_ref, v_ref, o_ref, *, tk, n_kv):
    tq = q_ref.shape[0]
    lane = lax.broadcasted_iota(jnp.int32, (1, LANES), 1)
    qs = (q_ref[:, :LANES], q_ref[:, LANES:])

    def body(j, carry):
        rows = pl.ds(pl.multiple_of(j * tk, tk), tk)
        v = v_ref[rows, :]
        new = []
        for h in range(2):
            m, l, acc = carry[h]
            s = _dot_nt(qs[h], k_ref[rows, h * LANES:(h + 1) * LANES])
            m_new = jnp.maximum(m, jnp.max(s, axis=-1, keepdims=True))
            alpha = jnp.exp2(m - m_new)
            p = jnp.exp2(s - m_new)
            l = alpha * l + jnp.sum(p, axis=-1, keepdims=True)
            acc = alpha * acc + _dot(p.astype(BF16), v)
            new.append((m_new, l, acc))
        return tuple(new)

    init = (jnp.full((tq, 1), NEG_BIG, F32), jnp.zeros((tq, 1), F32), jnp.zeros((tq, LANES), F32))
    (_, l0, acc0), (_, l1, acc1) = lax.fori_loop(0, n_kv, body, (init, init), unroll=min(2, n_kv))
    o_ref[...] = jnp.where(lane < MLA_V, acc0 / l0, acc1 / l1).astype(BF16)


def _mla_attn(qm, km, vm, seq_len, n_seq, row_block0):
    tq, tk = min(ATTN_TQ, seq_len), min(ATTN_TK, seq_len)
    hp_count = MLA_HEADS // 2
    nq = seq_len // tq
    return pl.pallas_call(
        functools.partial(_mla_attn_kernel, tk=tk, n_kv=seq_len // tk),
        out_shape=jax.ShapeDtypeStruct((n_seq * seq_len, MLA_WIDTH), BF16),
        grid=(n_seq, hp_count, nq),
        in_specs=[pl.BlockSpec((tq, 2 * LANES), lambda b, hp, i: ((row_block0 + b) * nq + i, hp)),
                  pl.BlockSpec((seq_len, 2 * LANES), lambda b, hp, i: (row_block0 + b, hp)),
                  pl.BlockSpec((seq_len, LANES), lambda b, hp, i: (row_block0 + b, hp))],
        out_specs=pl.BlockSpec((tq, LANES), lambda b, hp, i: (b * nq + i, hp)),
        compiler_params=_params(("arbitrary", "arbitrary", "arbitrary")),
        name="mla_attn",
    )(qm, km, vm)


def _mix_cross_kernel(xp_ref, xs_ref, rp_ref, rs_ref, mp_ref, ms_ref, wmix_ref, gx_ref, wxq_ref, kv_ref, wxo_ref,
                      gf_ref, rw_ref, rb_ref,
                      h_ref, xn_ref, gate_ref, slot_ref, stat_ref, cnt_ref, carry_scr, *, n_prompt_tiles):
    i = pl.program_id(0)
    tm = h_ref.shape[0]
    is_p = i < n_prompt_tiles
    x = jnp.where(is_p, xp_ref[...], xs_ref[...])
    ret = jnp.where(is_p, rp_ref[...], rs_ref[...])
    mla = jnp.where(is_p, mp_ref[...], ms_ref[...])
    h1 = x + _dot(ret, wmix_ref[0:RET_WIDTH, :]) + _dot(mla, wmix_ref[RET_WIDTH:, :])

    hn = _rms(h1, gx_ref[...]).astype(BF16)
    q = (_dot(hn, wxq_ref[...]) * (X_HEAD_DIM ** -0.5)).astype(BF16)
    heads = []
    for h in range(X_HEADS):
        lo = h * X_HEAD_DIM
        s = _dot_nt(q[:, lo:lo + X_HEAD_DIM], kv_ref[0, :, lo:lo + X_HEAD_DIM])
        e = jnp.exp(s - jnp.max(s, axis=-1, keepdims=True))
        p = (e / jnp.sum(e, axis=-1, keepdims=True)).astype(BF16)
        heads.append(_dot(p, kv_ref[0, :, D_MODEL + lo:D_MODEL + lo + X_HEAD_DIM]).astype(BF16))
    h2 = h1 + _dot(jnp.concatenate(heads, axis=-1), wxo_ref[...])
    h_ref[...] = h2

    xn = _rms(h2, gf_ref[...]).astype(BF16)
    xn_ref[...] = xn
    logits = _dot(xn, rw_ref[...]) + rb_ref[...]
    lane = lax.broadcasted_iota(jnp.int32, (tm, LANES), 1)
    lane_f = lane.astype(F32)
    work = logits
    vals, idxs, sels = [], [], []
    for _ in range(TOP_K):
        m = jnp.max(work, axis=-1, keepdims=True)
        idx = jnp.min(jnp.where(work == m, lane_f, float(LANES)), axis=-1, keepdims=True)
        sel = lane_f == idx
        work = jnp.where(sel, -jnp.inf, work)
        vals.append(m)
        idxs.append(idx)
        sels.append(sel)
    exps = [jnp.exp(v - vals[0]) for v in vals]
    denom = exps[0] + exps[1] + exps[2] + exps[3]
    onehot = jnp.where(sels[0] | sels[1] | sels[2] | sels[3], 1.0, 0.0)

    @pl.when(i == 0)
    def _():
        carry_scr[...] = jnp.zeros_like(carry_scr)

    r_iota = lax.broadcasted_iota(jnp.int32, (tm, tm), 0)
    c_iota = lax.broadcasted_iota(jnp.int32, (tm, tm), 1)
    lower = jnp.where(c_iota < r_iota, 1.0, 0.0).astype(BF16)
    before = _dot(lower, onehot.astype(BF16))
    tile_cnt = jnp.sum(onehot, axis=0, keepdims=True)
    group_rows = jnp.floor((tile_cnt + (ROW_CHUNK - 1)) * (1.0 / ROW_CHUNK)) * ROW_CHUNK
    carry_before = carry_scr[...]
    carry = carry_before + group_rows
    carry_scr[...] = carry
    cnt_ref[...] = carry.astype(jnp.int32)
    e_row = lax.broadcasted_iota(jnp.int32, (LANES, LANES), 0)
    e_col = lax.broadcasted_iota(jnp.int32, (LANES, LANES), 1)
    earlier = jnp.where(e_row < e_col, 1.0, 0.0).astype(BF16)
    group_start = _dot(jnp.broadcast_to(group_rows, (8, LANES)).astype(BF16), earlier)[0:1, :]
    row8 = lax.broadcasted_iota(jnp.int32, (8, LANES), 0)
    stats = jnp.where(row8 == 0, tile_cnt, jnp.where(row8 == 1, carry_before, jnp.where(row8 == 2, group_start, 0.0)))
    stat_ref[0] = stats.astype(jnp.int32)

    place = before + group_start
    gate_out = jnp.zeros((tm, LANES), F32)
    slot_out = jnp.zeros((tm, LANES), F32)
    for k in range(TOP_K):
        slot_k = jnp.sum(jnp.where(sels[k], place, 0.0), axis=-1, keepdims=True)
        gate_out = jnp.where(lane == k, exps[k] / denom, gate_out)
        slot_out = jnp.where(lane == k, slot_k, slot_out)
    gate_ref[...] = gate_out
    slot_ref[...] = slot_out.astype(jnp.int32)


def _mix_cross(xp, xs, ret_p, ret_s, mla_p, mla_s, wmix, gx, wxq, kvmem, wxo, gf, rw_pad, rb_pad, sp, ss):
    tm = TOKEN_TILE
    tp, ts = xp.shape[0], xs.shape[0]
    npt, nst = tp // tm, ts // tm
    t = tp + ts
    n_seq_p = tp // sp
    mem_len = kvmem.shape[1]

    def p_map(i):
        return (jnp.minimum(i, npt - 1), 0)

    def s_map(i):
        return (jnp.maximum(i - npt, 0), 0)

    def kv_map(i):
        return (jnp.where(i < npt, i // (sp // tm), n_seq_p + (i - npt) // (ss // tm)), 0, 0)

    tok = lambda w: pl.BlockSpec((tm, w), lambda i: (i, 0))
    return pl.pallas_call(
        functools.partial(_mix_cross_kernel, n_prompt_tiles=npt),
        out_shape=[jax.ShapeDtypeStruct((t, D_MODEL), F32), jax.ShapeDtypeStruct((t, D_MODEL), BF16),
                   jax.ShapeDtypeStruct((t, LANES), F32), jax.ShapeDtypeStruct((t, LANES), jnp.int32),
                   jax.ShapeDtypeStruct((npt + nst, 8, LANES), jnp.int32), jax.ShapeDtypeStruct((1, LANES), jnp.int32)],
        grid=(npt + nst,),
        in_specs=[pl.BlockSpec((tm, D_MODEL), p_map), pl.BlockSpec((tm, D_MODEL), s_map),
                  pl.BlockSpec((tm, RET_WIDTH), p_map), pl.BlockSpec((tm, RET_WIDTH), s_map),
                  pl.BlockSpec((tm, MLA_WIDTH), p_map), pl.BlockSpec((tm, MLA_WIDTH), s_map),
                  _const_spec((D_MODEL, D_MODEL)), _const_spec((1, D_MODEL)), _const_spec((D_MODEL, D_MODEL)),
                  pl.BlockSpec((1, mem_len, 2 * D_MODEL), kv_map),
                  _const_spec((D_MODEL, D_MODEL)), _const_spec((1, D_MODEL)),
                  _const_spec((D_MODEL, LANES)), _const_spec((1, LANES))],
        out_specs=[tok(D_MODEL), tok(D_MODEL), tok(LANES), tok(LANES),
                   pl.BlockSpec((1, 8, LANES), lambda i: (i, 0, 0)),
                   pl.BlockSpec((1, LANES), lambda i: (0, 0))],
        scratch_shapes=[pltpu.VMEM((1, LANES), F32)],
        compiler_params=_params(("arbitrary",)),
        name="mix_cross_router",
    )(xp, xs, ret_p, ret_s, mla_p, mla_s, wmix, gx, wxq, kvmem, wxo, gf, rw_pad, rb_pad)


def _to_rows(a):
    eye = (lax.broadcasted_iota(jnp.int32, (LANES, LANES), 0)
           == lax.broadcasted_iota(jnp.int32, (LANES, LANES), 1)).astype(BF16)
    hi = a.astype(BF16)
    r1 = a - hi.astype(F32)
    mid = r1.astype(BF16)
    lo = (r1 - mid.astype(F32)).astype(BF16)
    return _dot_nt(eye, hi) + _dot_nt(eye, mid) + _dot_nt(eye, lo)


def _chunk_copies(tile, nch_ref, lstart_ref, gstart_ref, local, hbm, sem, to_hbm, act):
    def per_expert(e, _):
        base = tile * N_EXPERTS + e
        n = nch_ref[base]
        ls = lstart_ref[base]
        gs = gstart_ref[base]
        for j in range(GROUP_BITS):
            rows = ROW_CHUNK << j

            @pl.when(((n >> j) & 1) == 1)
            def _(j=j, rows=rows):
                off = (n & ((1 << j) - 1)) * ROW_CHUNK
                l_rows = local.at[pl.ds(pl.multiple_of(ls + off, ROW_CHUNK), rows)]
                g_rows = hbm.at[pl.ds(pl.multiple_of(gs + off, ROW_CHUNK), rows)]
                act(pltpu.make_async_copy(l_rows, g_rows, sem) if to_hbm else pltpu.make_async_copy(g_rows, l_rows, sem))

        return 0

    lax.fori_loop(0, N_EXPERTS, per_expert, 0)


def _start(cp):
    cp.start()


def _wait(cp):
    cp.wait()


def _dispatch_kernel(nch_ref, lstart_ref, gstart_ref, zstart_ref, znch_ref,
                     x_ref, slot_ref, xs_hbm, buf, zero_scr, sem, zsem):
    i = pl.program_id(0)
    last = pl.num_programs(0) - 1
    tm = x_ref.shape[0]
    b = i % 2
    slot_rows = _to_rows(slot_ref[...].astype(F32))
    x = x_ref[...]

    def build(s, _):
        r0 = pl.multiple_of(s * LOCAL_STEP, LOCAL_STEP)
        rid = (lax.broadcasted_iota(jnp.int32, (LOCAL_STEP, tm), 0) + r0).astype(F32)
        hit = rid == slot_rows[0:1, :]
        for k in range(1, TOP_K):
            hit = hit | (rid == slot_rows[k:k + 1, :])
        buf[b, pl.ds(r0, LOCAL_STEP), :] = _dot(jnp.where(hit, 1.0, 0.0).astype(BF16), x)
        return 0

    lax.fori_loop(0, LOCAL_ROWS // LOCAL_STEP, build, 0, unroll=3)

    _chunk_copies(i, nch_ref, lstart_ref, gstart_ref, buf.at[b], xs_hbm, sem.at[b], True, _start)

    @pl.when(i > 0)
    def _():
        _chunk_copies(i - 1, nch_ref, lstart_ref, gstart_ref, buf.at[1 - b], xs_hbm, sem.at[1 - b], True, _wait)

    @pl.when(i == last)
    def _():
        _chunk_copies(i, nch_ref, lstart_ref, gstart_ref, buf.at[b], xs_hbm, sem.at[b], True, _wait)
        zero_scr[...] = jnp.zeros_like(zero_scr)

        def fill(act):
            def per_expert(e, _):
                def per_chunk(c, _):
                    row = pl.multiple_of(zstart_ref[e] + c * ROW_CHUNK, ROW_CHUNK)
                    act(pltpu.make_async_copy(zero_scr, xs_hbm.at[pl.ds(row, ROW_CHUNK)], zsem))
                    return 0

                lax.fori_loop(0, znch_ref[e], per_chunk, 0)
                return 0

            lax.fori_loop(0, N_EXPERTS, per_expert, 0)

        fill(_start)
        fill(_wait)


def _dispatch(tables, ztables, xn, slot, n_rows):
    tm = TOKEN_TILE
    t = xn.shape[0]
    return pl.pallas_call(
        _dispatch_kernel,
        out_shape=jax.ShapeDtypeStruct((n_rows, D_MODEL), F32),
        grid_spec=pltpu.PrefetchScalarGridSpec(
            num_scalar_prefetch=5,
            grid=(t // tm,),
            in_specs=[pl.BlockSpec((tm, D_MODEL), lambda i, *_: (i, 0)),
                      pl.BlockSpec((tm, LANES), lambda i, *_: (i, 0))],
            out_specs=pl.BlockSpec(memory_space=pl.ANY),
            scratch_shapes=[pltpu.VMEM((2, LOCAL_ROWS, D_MODEL), F32), pltpu.VMEM((ROW_CHUNK, D_MODEL), F32),
                            pltpu.SemaphoreType.DMA((2,)), pltpu.SemaphoreType.DMA(())]),
        compiler_params=_params(("arbitrary",)),
        name="moe_dispatch",
    )(*tables, *ztables, xn, slot)


def _experts_kernel(be_ref, bi_ref, na_ref, x_ref, wgu_ref, bgu_ref, wd_ref, bd_ref, o_ref):
    j = pl.program_id(0)

    @pl.when(j < na_ref[0])
    def _():
        x = x_ref[...].astype(BF16)
        gu = _dot(x, wgu_ref[0]) + bgu_ref[0]
        gate = jnp.minimum(gu[:, :D_MODEL], SWIGLU_LIMIT)
        up = jnp.clip(gu[:, D_MODEL:], -SWIGLU_LIMIT, SWIGLU_LIMIT)
        hid = (up + 1.0) * (gate * jax.nn.sigmoid(SWIGLU_ALPHA * gate))
        o_ref[...] = _dot(hid.astype(BF16), wd_ref[0]) + bd_ref[0]


def _experts(block_e, block_i, n_active, xs, wgu, bgu, wd, bd):
    n_blocks = xs.shape[0] // MOE_BLOCK
    row_map = lambda j, be, bi, na: (bi[j], 0)
    e_map = lambda j, be, bi, na: (be[j], 0, 0)
    return pl.pallas_call(
        _experts_kernel,
        out_shape=jax.ShapeDtypeStruct(xs.shape, F32),
        grid_spec=pltpu.PrefetchScalarGridSpec(
            num_scalar_prefetch=3,
            grid=(n_blocks,),
            in_specs=[pl.BlockSpec((MOE_BLOCK, D_MODEL), row_map),
                      pl.BlockSpec((1, D_MODEL, 2 * D_MODEL), e_map),
                      pl.BlockSpec((1, 1, 2 * D_MODEL), e_map),
                      pl.BlockSpec((1, D_MODEL, D_MODEL), e_map),
                      pl.BlockSpec((1, 1, D_MODEL), e_map)],
            out_specs=pl.BlockSpec((MOE_BLOCK, D_MODEL), row_map)),
        compiler_params=_params(("arbitrary",)),
        name="moe_experts",
    )(block_e, block_i, n_active, xs, wgu, bgu, wd, bd)


def _combine_kernel(nch_ref, lstart_ref, gstart_ref, slot_ref, gate_ref, h_ref, g_ref, ys_hbm, op_ref, os_ref,
                    buf, pick_scr, scaled_scr, sem, *, n_prompt_tiles):
    i = pl.program_id(0)
    n = pl.num_programs(0)
    tm = h_ref.shape[0]
    b = i % 2

    @pl.when(i == 0)
    def _():
        buf[...] = jnp.zeros_like(buf)
        _chunk_copies(0, nch_ref, lstart_ref, gstart_ref, buf.at[0], ys_hbm, sem.at[0], False, _start)

    @pl.when(i + 1 < n)
    def _():
        _chunk_copies(i + 1, nch_ref, lstart_ref, gstart_ref, buf.at[1 - b], ys_hbm, sem.at[1 - b], False, _start)

    _chunk_copies(i, nch_ref, lstart_ref, gstart_ref, buf.at[b], ys_hbm, sem.at[b], False, _wait)

    slot = slot_ref[...].astype(F32)
    slot_rows = _to_rows(slot)
    info = gate_ref[...] + pltpu.roll(slot, TOP_K, 1)
    info_hi = info.astype(BF16)
    rest = info - info_hi.astype(F32)
    info_mid = rest.astype(BF16)
    info_lo = (rest - info_mid.astype(F32)).astype(BF16)
    lane = lax.broadcasted_iota(jnp.int32, (LOCAL_STEP, LANES), 1)

    def step(s, _):
        r0 = pl.multiple_of(s * LOCAL_STEP, LOCAL_STEP)
        rid = (lax.broadcasted_iota(jnp.int32, (LOCAL_STEP, tm), 0) + r0).astype(F32)
        hit = rid == slot_rows[0:1, :]
        for k in range(1, TOP_K):
            hit = hit | (rid == slot_rows[k:k + 1, :])
        pick = jnp.where(hit, 1.0, 0.0).astype(BF16)
        pick_scr[pl.ds(r0, LOCAL_STEP), :] = pick
        own = _dot(pick, info_hi) + _dot(pick, info_mid) + _dot(pick, info_lo)
        rcol = (lax.broadcasted_iota(jnp.int32, (LOCAL_STEP, 1), 0) + r0).astype(F32)
        mine = jnp.where((own == rcol) & (lane >= TOP_K) & (lane < 2 * TOP_K), 1.0, 0.0)
        g_col = jnp.sum(own * pltpu.roll(mine, LANES - TOP_K, 1), axis=-1, keepdims=True)
        rows = buf[b, pl.ds(r0, LOCAL_STEP), :]
        scaled_scr[pl.ds(r0, LOCAL_STEP), :] = jnp.where(g_col != 0.0, rows * g_col, 0.0).astype(BF16)
        return 0

    lax.fori_loop(0, LOCAL_ROWS // LOCAL_STEP, step, 0, unroll=3)
    y = _dot_tn(pick_scr[...], scaled_scr[...])
    out = _rms(h_ref[...] + y, g_ref[...])

    @pl.when(i < n_prompt_tiles)
    def _():
        op_ref[...] = out

    @pl.when(i >= n_prompt_tiles)
    def _():
        os_ref[...] = out


def _combine(tables, slot, gates, h2, g_final, ys, tp, ts):
    tm = TOKEN_TILE
    npt, nst = tp // tm, ts // tm
    tok = lambda w: pl.BlockSpec((tm, w), lambda i, *_: (i, 0))
    return pl.pallas_call(
        functools.partial(_combine_kernel, n_prompt_tiles=npt),
        out_shape=[jax.ShapeDtypeStruct((tp, D_MODEL), F32), jax.ShapeDtypeStruct((ts, D_MODEL), F32)],
        grid_spec=pltpu.PrefetchScalarGridSpec(
            num_scalar_prefetch=3,
            grid=(npt + nst,),
            in_specs=[tok(LANES), tok(LANES), tok(D_MODEL),
                      pl.BlockSpec((1, D_MODEL), lambda i, *_: (0, 0)),
                      pl.BlockSpec(memory_space=pl.ANY)],
            out_specs=[pl.BlockSpec((tm, D_MODEL), lambda i, *_: (jnp.minimum(i, npt - 1), 0)),
                       pl.BlockSpec((tm, D_MODEL), lambda i, *_: (jnp.maximum(i - npt, 0), 0))],
            scratch_shapes=[pltpu.VMEM((2, LOCAL_ROWS, D_MODEL), F32), pltpu.VMEM((LOCAL_ROWS, tm), BF16),
                            pltpu.VMEM((LOCAL_ROWS, D_MODEL), BF16), pltpu.SemaphoreType.DMA((2,))]),
        compiler_params=_params(("arbitrary",)),
        name="moe_combine_norm",
    )(*tables, slot, gates, h2, g_final, ys)


def _rope_tables(seq_len):
    pos = jnp.arange(seq_len, dtype=F32)[:, None]
    lane = np.arange(LANES)

    def table(half, lane_freq, first, active):
        inv = ROPE_BASE ** (-jnp.arange(half, dtype=F32) / half)
        ang = pos * inv[None, :]
        cos = jnp.cos(ang)[:, lane_freq]
        sin = jnp.sin(ang)[:, lane_freq]
        c = jnp.where(active[None, :], cos, 1.0)
        s = jnp.where(active[None, :], jnp.where(first[None, :], -sin, sin), 0.0)
        return c, s

    half_r = RET_DK // 2
    cr, sr = table(half_r, lane % half_r, (lane % RET_DK) < half_r, np.ones(LANES, bool))
    half_m = MLA_ROPE // 2
    rel = lane - MLA_NOPE
    active = (rel >= 0) & (rel < MLA_ROPE)
    cm, sm = table(half_m, np.where(active, rel % half_m, 0), active & (rel < half_m), active)
    return cr, sr, cm, sm


def kernel(x_prompt, x_sample, mem_prompt, mem_sample, norm_mix, w_in, ret_decay_fwd, ret_decay_bwd, ret_gn, q_a_norm, w_uq, kv_a_norm, w_ukv, w_mix_out, norm_cross, norm_mem, w_xq, w_xkv, w_xo, norm_ffn, router_w, router_b, w_gu, b_gu, w_down, b_down, norm_final):
    assert norm_mix.shape[0] == 1, "single layer"
    bp, sp, d = x_prompt.shape
    bs, ss, _ = x_sample.shape
    tp, ts = bp * sp, bs * ss
    t = tp + ts
    assert d == D_MODEL and sp % TOKEN_TILE == 0 and ss % TOKEN_TILE == 0 and tp % ss == 0 and sp >= ss
    mem_len = mem_prompt.shape[1]

    w_in0 = w_in[0]
    w_in_pad = jnp.zeros((D_MODEL, D_IN_PAD), F32)
    w_in_pad = w_in_pad.at[:, :2816].set(w_in0[:, :2816])
    w_in_pad = w_in_pad.at[:, 2816 + MLA_NOPE:2816 + MLA_NOPE + MLA_ROPE].set(w_in0[:, 2816:]).astype(BF16)
    wuq_pad = jnp.pad(w_uq[0].reshape(Q_LORA, MLA_HEADS, MLA_NOPE + MLA_ROPE),
                      ((0, 0), (0, 0), (0, LANES - MLA_NOPE - MLA_ROPE))).reshape(Q_LORA, MLA_HEADS * LANES).astype(BF16)
    wukv = w_ukv[0].reshape(KV_LORA, MLA_HEADS, MLA_NOPE + MLA_V)
    wuk_pad = jnp.pad(wukv[:, :, :MLA_NOPE], ((0, 0), (0, 0), (0, LANES - MLA_NOPE))).reshape(KV_LORA, MLA_HEADS * LANES).astype(BF16)
    wuv = wukv[:, :, MLA_NOPE:].reshape(KV_LORA, MLA_WIDTH).astype(BF16)
    rw_pad = jnp.pad(router_w[0], ((0, 0), (0, LANES - N_EXPERTS))).astype(BF16)
    rb_pad = jnp.pad(router_b[0].astype(F32), (0, LANES - N_EXPERTS), constant_values=NEG_BIG)[None, :]
    lgf = jnp.log1p(-jnp.exp2(ret_decay_fwd[0].astype(F32)))
    lgb = jnp.log1p(-jnp.exp2(ret_decay_bwd[0].astype(F32)))
    tabs = _rope_tables(sp)

    xp = x_prompt.reshape(tp, D_MODEL)
    xs = x_sample.reshape(ts, D_MODEL)
    mem = jnp.concatenate([mem_prompt.reshape(-1, D_MODEL), mem_sample.reshape(-1, D_MODEL)], axis=0)

    kvmem = _mem_kv(mem, norm_mem[0][None, :], w_xkv[0].astype(BF16)).reshape(bp + bs, mem_len, 2 * D_MODEL)

    rq, rk, rv, rg, qm, km, vm = _in_proj(xp, xs, norm_mix[0][None, :], w_in_pad, q_a_norm[0][None, :], wuq_pad,
                                          kv_a_norm[0][None, :], wuk_pad, wuv, tabs, sp, ss)

    gn = ret_gn[0][None, :]
    ret_p = _retention(lgf, lgb, rq, rk, rv, rg, gn, sp, bp, 0)
    ret_s = _retention(lgf, lgb, rq, rk, rv, rg, gn, ss, bs, tp // ss)
    mla_p = _mla_attn(qm, km, vm, sp, bp, 0)
    mla_s = _mla_attn(qm, km, vm, ss, bs, tp // ss)

    h2, xn, gates, slot, stats, counts = _mix_cross(
        xp, xs, ret_p, ret_s, mla_p, mla_s, w_mix_out[0].astype(BF16), norm_cross[0][None, :], w_xq[0].astype(BF16),
        kvmem, w_xo[0].astype(BF16), norm_ffn[0][None, :], rw_pad, rb_pad, sp, ss)

    used = counts[0, :N_EXPERTS]
    padded = (used + MOE_BLOCK - 1) // MOE_BLOCK * MOE_BLOCK
    pend = jnp.cumsum(padded)
    pstart = pend - padded
    tile_cnt, tile_before, tile_lstart = (stats[:, r, :N_EXPERTS] for r in range(3))
    n_tiles = stats.shape[0]
    tables = (((tile_cnt + ROW_CHUNK - 1) // ROW_CHUNK).reshape(-1),
              tile_lstart.reshape(-1),
              (pstart[None, :] + tile_before).reshape(-1))
    ztables = (pstart + used, (padded - used) // ROW_CHUNK)
    n_blocks = -(-(t * TOP_K + N_EXPERTS * (n_tiles * (ROW_CHUNK - 1) + MOE_BLOCK - 1)) // MOE_BLOCK)
    blk = jnp.arange(n_blocks, dtype=jnp.int32)
    n_active = (pend[-1] // MOE_BLOCK).astype(jnp.int32)
    block_i = jnp.minimum(blk, n_active - 1)
    block_e = jnp.minimum(jnp.sum((block_i[:, None] * MOE_BLOCK >= pend[None, :]).astype(jnp.int32), axis=-1),
                          N_EXPERTS - 1).astype(jnp.int32)

    xs_sorted = _dispatch(tables, ztables, xn, slot, n_blocks * MOE_BLOCK)
    ys = _experts(block_e, block_i, n_active[None], xs_sorted, w_gu[0].astype(BF16), b_gu[0][:, None, :],
                  w_down[0].astype(BF16), b_down[0][:, None, :])
    out_p, out_s = _combine(tables, slot, gates, h2, norm_final[None, :], ys, tp, ts)
    return out_p.reshape(bp, sp, D_MODEL), out_s.reshape(bs, ss, D_MODEL)
```

```python
import functools
import math

import jax
import jax.numpy as jnp
import numpy as np
from jax import lax
from jax.experimental import pallas as pl
from jax.experimental.pallas import tpu as pltpu

D_MODEL = 1024
RET_HEADS = 8
RET_DK = 64
RET_WIDTH = 512
CHUNK = 128
MLA_HEADS = 8
MLA_NOPE = 64
MLA_ROPE = 32
MLA_V = 64
MLA_WIDTH = 512
Q_LORA = 512
KV_LORA = 256
ROPE_BASE = 10000.0
X_HEADS = 4
X_HEAD_DIM = 256
N_EXPERTS = 32
TOP_K = 4
SWIGLU_LIMIT = 7.0
SWIGLU_ALPHA = 1.702
MOE_BLOCK = 512
EPS = 1e-6

LANES = 128
VMEM_LIMIT = 56 * 1024 * 1024

TOKEN_TILE = 512
ATTN_TQ = 1024
ATTN_TK = 1024
RET_UNROLL = 8
ROW_CHUNK = 8
GROUP_BITS = (TOKEN_TILE // ROW_CHUNK).bit_length()
LOCAL_STEP = 256
LOCAL_ROWS = -(-(TOKEN_TILE * 4 + 32 * (ROW_CHUNK - 1)) // LOCAL_STEP) * LOCAL_STEP
D_IN_PAD = 4 * 512 + Q_LORA + KV_LORA + LANES
NEG_BIG = -1e30

F32 = jnp.float32
BF16 = jnp.bfloat16


def _params(sem, vmem=VMEM_LIMIT):
    return pltpu.CompilerParams(dimension_semantics=sem, vmem_limit_bytes=vmem)


def _const_spec(shape):
    nd = len(shape)
    return pl.BlockSpec(shape, lambda *_: (0,) * nd, pipeline_mode=pl.Buffered(1))


def _rms(x, g):
    ms = jnp.mean(x * x, axis=-1, keepdims=True)
    return x * lax.rsqrt(ms + EPS) * g


def _dot(a, b):
    return jnp.dot(a, b, preferred_element_type=F32)


def _dot_nt(a, b):
    return lax.dot_general(a, b, (((1,), (1,)), ((), ())), preferred_element_type=F32)


def _dot_tn(a, b):
    return lax.dot_general(a, b, (((0,), (0,)), ((), ())), preferred_element_type=F32)


def _mem_kv_kernel(mem_ref, g_ref, w_ref, o_ref):
    mn = _rms(mem_ref[...], g_ref[...])
    o_ref[...] = _dot(mn.astype(BF16), w_ref[...]).astype(BF16)


def _mem_kv(mem, g, w):
    rows, mem_len = mem.shape[0], 256
    return pl.pallas_call(
        _mem_kv_kernel,
        out_shape=jax.ShapeDtypeStruct((rows, 2 * D_MODEL), BF16),
        grid=(rows // mem_len,),
        in_specs=[pl.BlockSpec((mem_len, D_MODEL), lambda i: (i, 0)),
                  _const_spec((1, D_MODEL)),
                  _const_spec((D_MODEL, 2 * D_MODEL))],
        out_specs=pl.BlockSpec((mem_len, 2 * D_MODEL), lambda i: (i, 0)),
        compiler_params=_params(("arbitrary",)),
        name="mem_kv",
    )(mem, g, w)


def _rope_slab(x, c, ss, first, shift_up, shift_down):
    swap = jnp.where(first, pltpu.roll(x, shift_up, 1), pltpu.roll(x, shift_down, 1))
    return x * c + swap * ss


def _in_proj_kernel(xp_ref, xs_ref, g_ref, win_ref, qan_ref, wuq_ref, kvan_ref, wuk_ref, wuv_ref,
                    cr_ref, sr_ref, cm_ref, sm_ref,
                    rq_ref, rk_ref, rv_ref, rg_ref, qm_ref, km_ref, vm_ref, *, n_prompt_tiles):
    i = pl.program_id(0)
    x = jnp.where(i < n_prompt_tiles, xp_ref[...], xs_ref[...])
    xn = _rms(x, g_ref[...])
    proj = _dot(xn.astype(BF16), win_ref[...])

    lane = lax.broadcasted_iota(jnp.int32, (1, LANES), 1)
    ret_first = (lane % RET_DK) < (RET_DK // 2)
    mla_first = (lane >= MLA_NOPE) & (lane < MLA_NOPE + MLA_ROPE // 2)
    cr, sr, cm, sm = cr_ref[...], sr_ref[...], cm_ref[...], sm_ref[...]
    half_r, half_m = RET_DK // 2, MLA_ROPE // 2

    for s in range(RET_WIDTH // LANES):
        lo = s * LANES
        q = _rope_slab(proj[:, lo:lo + LANES], cr, sr, ret_first, LANES - half_r, half_r)
        rq_ref[:, lo:lo + LANES] = q.astype(BF16)
        k = _rope_slab(proj[:, 512 + lo:512 + lo + LANES], cr, sr, ret_first, LANES - half_r, half_r)
        rk_ref[:, lo:lo + LANES] = (k * (RET_DK ** -0.5)).astype(BF16)
    rv_ref[...] = proj[:, 1024:1536].astype(BF16)
    rg_ref[...] = proj[:, 1536:2048].astype(BF16)

    cq = _rms(proj[:, 2048:2048 + Q_LORA], qan_ref[...])
    qm = _dot(cq.astype(BF16), wuq_ref[...])
    ckv = _rms(proj[:, 2560:2560 + KV_LORA], kvan_ref[...]).astype(BF16)
    kn = _dot(ckv, wuk_ref[...])
    vm_ref[...] = _dot(ckv, wuv_ref[...]).astype(BF16)
    kr = proj[:, 2816:2816 + LANES]
    kpe = _rope_slab(kr, cm, sm, mla_first, LANES - half_m, half_m)
    q_scale = (MLA_NOPE + MLA_ROPE) ** -0.5 * math.log2(math.e)
    for h in range(MLA_HEADS):
        lo = h * LANES
        qh = _rope_slab(qm[:, lo:lo + LANES], cm, sm, mla_first, LANES - half_m, half_m)
        qm_ref[:, lo:lo + LANES] = (qh * q_scale).astype(BF16)
        km_ref[:, lo:lo + LANES] = (kn[:, lo:lo + LANES] + kpe).astype(BF16)


def _in_proj(xp, xs, g, w_in_pad, qan, wuq_pad, kvan, wuk_pad, wuv, tabs, sp, ss):
    tm = TOKEN_TILE
    tp, ts = xp.shape[0], xs.shape[0]
    npt, nst = tp // tm, ts // tm
    t = tp + ts
    tiles_p, tiles_s = sp // tm, ss // tm

    def xp_map(i):
        return (jnp.minimum(i, npt - 1), 0)

    def xs_map(i):
        return (jnp.maximum(i - npt, 0), 0)

    def tab_map(i):
        return (jnp.where(i < npt, i % tiles_p, (i - npt) % tiles_s), 0)

    tok = lambda w: pl.BlockSpec((tm, w), lambda i: (i, 0))
    widths = (512, 512, 512, 512, 1024, 1024, 512)
    return pl.pallas_call(
        functools.partial(_in_proj_kernel, n_prompt_tiles=npt),
        out_shape=[jax.ShapeDtypeStruct((t, w), BF16) for w in widths],
        grid=(npt + nst,),
        in_specs=[pl.BlockSpec((tm, D_MODEL), xp_map), pl.BlockSpec((tm, D_MODEL), xs_map),
                  _const_spec((1, D_MODEL)), _const_spec((D_MODEL, D_IN_PAD)),
                  _const_spec((1, Q_LORA)), _const_spec((Q_LORA, MLA_HEADS * LANES)),
                  _const_spec((1, KV_LORA)), _const_spec((KV_LORA, MLA_HEADS * LANES)),
                  _const_spec((KV_LORA, MLA_WIDTH))]
                 + [pl.BlockSpec((tm, LANES), tab_map)] * 4,
        out_specs=[tok(w) for w in widths],
        compiler_params=_params(("arbitrary",)),
        name="in_proj",
    )(xp, xs, g, w_in_pad, qan, wuq_pad, kvan, wuk_pad, wuv, *tabs)


def _retention_kernel(lgf_ref, lgb_ref, q_ref, k_ref, v_ref, g_ref, gn_ref, o_ref, kvf_scr, kvb_scr, st_scr, *, n_chunks):
    hp = pl.program_id(1)
    c = CHUNK
    lane = lax.broadcasted_iota(jnp.int32, (1, c), 1)
    row = lax.broadcasted_iota(jnp.int32, (c, 1), 0)
    lane_h0 = lane < RET_DK
    row_h0 = row < RET_DK
    lgf0, lgf1 = lgf_ref[2 * hp], lgf_ref[2 * hp + 1]
    lgb0, lgb1 = lgb_ref[2 * hp], lgb_ref[2 * hp + 1]
    lgf_lane = jnp.where(lane_h0, lgf0, lgf1)
    lgb_lane = jnp.where(lane_h0, lgb0, lgb1)
    t = row.astype(F32)
    q_dec_f = jnp.exp((t + 1.0) * lgf_lane)
    q_dec_b = jnp.exp((c - t) * lgb_lane)
    k_dec_f = jnp.exp((c - 1.0 - t) * lgf_lane)
    k_dec_b = jnp.exp(t * lgb_lane)
    chunk_dec_f = jnp.exp(c * jnp.where(row_h0, lgf0, lgf1))
    chunk_dec_b = jnp.exp(c * jnp.where(row_h0, lgb0, lgb1))
    same_head = row_h0 == lane_h0
    diff = t - lane.astype(F32)
    d_intra = []
    for lgf, lgb in ((lgf0, lgb0), (lgf1, lgb1)):
        fwd = jnp.where(diff >= 0, jnp.exp(jnp.where(diff >= 0, diff, 0.0) * lgf), 0.0)
        bwd = jnp.where(diff < 0, jnp.exp(jnp.where(diff < 0, -diff, 0.0) * lgb), 0.0)
        d_intra.append(fwd + bwd)
    avg = jnp.where(same_head, 1.0 / RET_DK, 0.0).astype(BF16)
    gn = gn_ref[...]

    def chunk(n):
        return pl.ds(pl.multiple_of(n * c, c), c)

    unroll = min(RET_UNROLL, n_chunks)

    def chunk_kv(n, _):
        k = k_ref[chunk(n), :].astype(F32)
        kd = jnp.concatenate([(k * k_dec_f).astype(BF16), (k * k_dec_b).astype(BF16)], axis=-1)
        kv = _dot_tn(kd, v_ref[chunk(n), :])
        kvf_scr[n] = jnp.where(same_head, kv[0:c], 0.0)
        kvb_scr[n] = jnp.where(same_head, kv[c:2 * c], 0.0)
        return 0

    lax.fori_loop(0, n_chunks, chunk_kv, 0, unroll=unroll)

    def fwd_state(n, s_f):
        st_scr[n, 0:c, :] = s_f.astype(BF16)
        return chunk_dec_f * s_f + kvf_scr[n]

    lax.fori_loop(0, n_chunks, fwd_state, jnp.zeros((c, c), F32))

    def bwd_state(j, s_b):
        n = n_chunks - 1 - j
        st_scr[n, c:2 * c, :] = s_b.astype(BF16)
        return chunk_dec_b * s_b + kvb_scr[n]

    lax.fori_loop(0, n_chunks, bwd_state, jnp.zeros((c, c), F32))

    d_both = jnp.concatenate(d_intra, axis=0)
    zero_q = jnp.zeros((c, c), BF16)

    def chunk_out(n):
        q = q_ref[chunk(n), :]
        qf = q.astype(F32)
        q_both = jnp.concatenate([(qf * q_dec_f).astype(BF16), (qf * q_dec_b).astype(BF16)], axis=-1)
        q_heads = jnp.concatenate([jnp.where(lane_h0, q, zero_q), jnp.where(lane_h0, zero_q, q)], axis=0)
        p = (_dot_nt(q_heads, k_ref[chunk(n), :]) * d_both).astype(BF16)
        inner = _dot(p, v_ref[chunk(n), :])
        return _dot(q_both, st_scr[n]) + jnp.where(lane_h0, inner[0:c], inner[c:2 * c])

    def split_rows(a):
        hi = a.astype(BF16)
        return jnp.concatenate([hi, (a - hi.astype(F32)).astype(BF16)], axis=0)

    def out_group(gi, _):
        rows = pl.ds(pl.multiple_of(gi * (unroll * c), unroll * c), unroll * c)
        y = jnp.concatenate([chunk_out(gi * unroll + u) for u in range(unroll)], axis=0)
        m = y.shape[0]
        mu2 = _dot(split_rows(y), avg)
        d = y - (mu2[0:m] + mu2[m:2 * m])
        var2 = _dot(split_rows(d * d), avg)
        yn = d * lax.rsqrt(var2[0:m] + var2[m:2 * m] + EPS) * gn
        gate = g_ref[rows, :].astype(F32)
        o_ref[rows, :] = (yn * (gate * jax.nn.sigmoid(gate))).astype(BF16)
        return 0

    lax.fori_loop(0, n_chunks // unroll, out_group, 0)


def _retention(lgf, lgb, rq, rk, rv, rg, gn, seq_len, n_seq, row_block0):
    n_chunks = seq_len // CHUNK
    hp_count = RET_WIDTH // LANES
    blk = pl.BlockSpec((seq_len, LANES), lambda b, hp, *_: (row_block0 + b, hp))
    return pl.pallas_call(
        functools.partial(_retention_kernel, n_chunks=n_chunks),
        out_shape=jax.ShapeDtypeStruct((n_seq * seq_len, RET_WIDTH), BF16),
        grid_spec=pltpu.PrefetchScalarGridSpec(
            num_scalar_prefetch=2,
            grid=(n_seq, hp_count),
            in_specs=[blk, blk, blk, blk, pl.BlockSpec((1, LANES), lambda b, hp, *_: (0, hp))],
            out_specs=pl.BlockSpec((seq_len, LANES), lambda b, hp, *_: (b, hp)),
            scratch_shapes=[pltpu.VMEM((n_chunks, CHUNK, CHUNK), F32), pltpu.VMEM((n_chunks, CHUNK, CHUNK), F32),
                            pltpu.VMEM((n_chunks, 2 * CHUNK, CHUNK), BF16)]),
        compiler_params=_params(("arbitrary", "arbitrary")),
        name="retention",
    )(lgf, lgb, rq, rk, rv, rg, gn)


def _mla_attn_kernel(q_ref, k_ref, v_ref, o_ref, *, tk, n_kv):
    tq = q_ref.shape[0]
    lane = lax.broadcasted_iota(jnp.int32, (1, LANES), 1)
    qs = (q_ref[:, :LANES], q_ref[:, LANES:])

    def body(j, carry):
        rows = pl.ds(pl.multiple_of(j * tk, tk), tk)
        v = v_ref[rows, :]
        new = []
        for h in range(2):
            m, l, acc = carry[h]
            s = _dot_nt(qs[h], k_ref[rows, h * LANES:(h + 1) * LANES])
            m_new = jnp.maximum(m, jnp.max(s, axis=-1, keepdims=True))
            alpha = jnp.exp2(m - m_new)
            p = jnp.exp2(s - m_new)
            l = alpha * l + jnp.sum(p, axis=-1, keepdims=True)
            acc = alpha * acc + _dot(p.astype(BF16), v)
            new.append((m_new, l, acc))
        return tuple(new)

    init = (jnp.full((tq, 1), NEG_BIG, F32), jnp.zeros((tq, 1), F32), jnp.zeros((tq, LANES), F32))
    (_, l0, acc0), (_, l1, acc1) = lax.fori_loop(0, n_kv, body, (init, init), unroll=min(2, n_kv))
    o_ref[...] = jnp.where(lane < MLA_V, acc0 / l0, acc1 / l1).astype(BF16)


def _mla_attn(qm, km, vm, seq_len, n_seq, row_block0):
    tq, tk = min(ATTN_TQ, seq_len), min(ATTN_TK, seq_len)
    hp_count = MLA_HEADS // 2
    nq = seq_len // tq
    return pl.pallas_call(
        functools.partial(_mla_attn_kernel, tk=tk, n_kv=seq_len // tk),
        out_shape=jax.ShapeDtypeStruct((n_seq * seq_len, MLA_WIDTH), BF16),
        grid=(n_seq, hp_count, nq),
        in_specs=[pl.BlockSpec((tq, 2 * LANES), lambda b, hp, i: ((row_block0 + b) * nq + i, hp)),
                  pl.BlockSpec((seq_len, 2 * LANES), lambda b, hp, i: (row_block0 + b, hp)),
                  pl.BlockSpec((seq_len, LANES), lambda b, hp, i: (row_block0 + b, hp))],
        out_specs=pl.BlockSpec((tq, LANES), lambda b, hp, i: (b * nq + i, hp)),
        compiler_params=_params(("arbitrary", "arbitrary", "arbitrary")),
        name="mla_attn",
    )(qm, km, vm)


def _mix_cross_kernel(xp_ref, xs_ref, rp_ref, rs_ref, mp_ref, ms_ref, wmix_ref, gx_ref, wxq_ref, kv_ref, wxo_ref,
                      gf_ref, rw_ref, rb_ref,
                      h_ref, xn_ref, gate_ref, slot_ref, stat_ref, cnt_ref, carry_scr, *, n_prompt_tiles):
    i = pl.program_id(0)
    tm = h_ref.shape[0]
    is_p = i < n_prompt_tiles
    x = jnp.where(is_p, xp_ref[...], xs_ref[...])
    ret = jnp.where(is_p, rp_ref[...], rs_ref[...])
    mla = jnp.where(is_p, mp_ref[...], ms_ref[...])
    h1 = x + _dot(ret, wmix_ref[0:RET_WIDTH, :]) + _dot(mla, wmix_ref[RET_WIDTH:, :])

    hn = _rms(h1, gx_ref[...]).astype(BF16)
    q = (_dot(hn, wxq_ref[...]) * (X_HEAD_DIM ** -0.5)).astype(BF16)
    heads = []
    for h in range(X_HEADS):
        lo = h * X_HEAD_DIM
        s = _dot_nt(q[:, lo:lo + X_HEAD_DIM], kv_ref[0, :, lo:lo + X_HEAD_DIM])
        e = jnp.exp(s - jnp.max(s, axis=-1, keepdims=True))
        p = (e / jnp.sum(e, axis=-1, keepdims=True)).astype(BF16)
        heads.append(_dot(p, kv_ref[0, :, D_MODEL + lo:D_MODEL + lo + X_HEAD_DIM]).astype(BF16))
    h2 = h1 + _dot(jnp.concatenate(heads, axis=-1), wxo_ref[...])
    h_ref[...] = h2

    xn = _rms(h2, gf_ref[...]).astype(BF16)
    xn_ref[...] = xn
    logits = _dot(xn, rw_ref[...]) + rb_ref[...]
    lane = lax.broadcasted_iota(jnp.int32, (tm, LANES), 1)
    lane_f = lane.astype(F32)
    work = logits
    vals, idxs, sels = [], [], []
    for _ in range(TOP_K):
        m = jnp.max(work, axis=-1, keepdims=True)
        idx = jnp.min(jnp.where(work == m, lane_f, float(LANES)), axis=-1, keepdims=True)
        sel = lane_f == idx
        work = jnp.where(sel, -jnp.inf, work)
        vals.append(m)
        idxs.append(idx)
        sels.append(sel)
    exps = [jnp.exp(v - vals[0]) for v in vals]
    denom = exps[0] + exps[1] + exps[2] + exps[3]
    onehot = jnp.where(sels[0] | sels[1] | sels[2] | sels[3], 1.0, 0.0)

    @pl.when(i == 0)
    def _():
        carry_scr[...] = jnp.zeros_like(carry_scr)

    r_iota = lax.broadcasted_iota(jnp.int32, (tm, tm), 0)
    c_iota = lax.broadcasted_iota(jnp.int32, (tm, tm), 1)
    lower = jnp.where(c_iota < r_iota, 1.0, 0.0).astype(BF16)
    before = _dot(lower, onehot.astype(BF16))
    tile_cnt = jnp.sum(onehot, axis=0, keepdims=True)
    group_rows = jnp.floor((tile_cnt + (ROW_CHUNK - 1)) * (1.0 / ROW_CHUNK)) * ROW_CHUNK
    carry_before = carry_scr[...]
    carry = carry_before + group_rows
    carry_scr[...] = carry
    cnt_ref[...] = carry.astype(jnp.int32)
    e_row = lax.broadcasted_iota(jnp.int32, (LANES, LANES), 0)
    e_col = lax.broadcasted_iota(jnp.int32, (LANES, LANES), 1)
    earlier = jnp.where(e_row < e_col, 1.0, 0.0).astype(BF16)
    group_start = _dot(jnp.broadcast_to(group_rows, (8, LANES)).astype(BF16), earlier)[0:1, :]
    row8 = lax.broadcasted_iota(jnp.int32, (8, LANES), 0)
    stats = jnp.where(row8 == 0, tile_cnt, jnp.where(row8 == 1, carry_before, jnp.where(row8 == 2, group_start, 0.0)))
    stat_ref[0] = stats.astype(jnp.int32)

    place = before + group_start
    gate_out = jnp.zeros((tm, LANES), F32)
    slot_out = jnp.zeros((tm, LANES), F32)
    for k in range(TOP_K):
        slot_k = jnp.sum(jnp.where(sels[k], place, 0.0), axis=-1, keepdims=True)
        gate_out = jnp.where(lane == k, exps[k] / denom, gate_out)
        slot_out = jnp.where(lane == k, slot_k, slot_out)
    gate_ref[...] = gate_out
    slot_ref[...] = slot_out.astype(jnp.int32)


def _mix_cross(xp, xs, ret_p, ret_s, mla_p, mla_s, wmix, gx, wxq, kvmem, wxo, gf, rw_pad, rb_pad, sp, ss):
    tm = TOKEN_TILE
    tp, ts = xp.shape[0], xs.shape[0]
    npt, nst = tp // tm, ts // tm
    t = tp + ts
    n_seq_p = tp // sp
    mem_len = kvmem.shape[1]

    def p_map(i):
        return (jnp.minimum(i, npt - 1), 0)

    def s_map(i):
        return (jnp.maximum(i - npt, 0), 0)

    def kv_map(i):
        return (jnp.where(i < npt, i // (sp // tm), n_seq_p + (i - npt) // (ss // tm)), 0, 0)

    tok = lambda w: pl.BlockSpec((tm, w), lambda i: (i, 0))
    return pl.pallas_call(
        functools.partial(_mix_cross_kernel, n_prompt_tiles=npt),
        out_shape=[jax.ShapeDtypeStruct((t, D_MODEL), F32), jax.ShapeDtypeStruct((t, D_MODEL), BF16),
                   jax.ShapeDtypeStruct((t, LANES), F32), jax.ShapeDtypeStruct((t, LANES), jnp.int32),
                   jax.ShapeDtypeStruct((npt + nst, 8, LANES), jnp.int32), jax.ShapeDtypeStruct((1, LANES), jnp.int32)],
        grid=(npt + nst,),
        in_specs=[pl.BlockSpec((tm, D_MODEL), p_map), pl.BlockSpec((tm, D_MODEL), s_map),
                  pl.BlockSpec((tm, RET_WIDTH), p_map), pl.BlockSpec((tm, RET_WIDTH), s_map),
                  pl.BlockSpec((tm, MLA_WIDTH), p_map), pl.BlockSpec((tm, MLA_WIDTH), s_map),
                  _const_spec((D_MODEL, D_MODEL)), _const_spec((1, D_MODEL)), _const_spec((D_MODEL, D_MODEL)),
                  pl.BlockSpec((1, mem_len, 2 * D_MODEL), kv_map),
                  _const_spec((D_MODEL, D_MODEL)), _const_spec((1, D_MODEL)),
                  _const_spec((D_MODEL, LANES)), _const_spec((1, LANES))],
        out_specs=[tok(D_MODEL), tok(D_MODEL), tok(LANES), tok(LANES),
                   pl.BlockSpec((1, 8, LANES), lambda i: (i, 0, 0)),
                   pl.BlockSpec((1, LANES), lambda i: (0, 0))],
        scratch_shapes=[pltpu.VMEM((1, LANES), F32)],
        compiler_params=_params(("arbitrary",)),
        name="mix_cross_router",
    )(xp, xs, ret_p, ret_s, mla_p, mla_s, wmix, gx, wxq, kvmem, wxo, gf, rw_pad, rb_pad)


def _to_rows(a):
    eye = (lax.broadcasted_iota(jnp.int32, (LANES, LANES), 0)
           == lax.broadcasted_iota(jnp.int32, (LANES, LANES), 1)).astype(BF16)
    hi = a.astype(BF16)
    r1 = a - hi.astype(F32)
    mid = r1.astype(BF16)
    lo = (r1 - mid.astype(F32)).astype(BF16)
    return _dot_nt(eye, hi) + _dot_nt(eye, mid) + _dot_nt(eye, lo)


def _chunk_copies(tile, nch_ref, lstart_ref, gstart_ref, local, hbm, sem, to_hbm, act):
    def per_expert(e, _):
        base = tile * N_EXPERTS + e
        n = nch_ref[base]
        ls = lstart_ref[base]
        gs = gstart_ref[base]
        for j in range(GROUP_BITS):
            rows = ROW_CHUNK << j

            @pl.when(((n >> j) & 1) == 1)
            def _(j=j, rows=rows):
                off = (n & ((1 << j) - 1)) * ROW_CHUNK
                l_rows = local.at[pl.ds(pl.multiple_of(ls + off, ROW_CHUNK), rows)]
                g_rows = hbm.at[pl.ds(pl.multiple_of(gs + off, ROW_CHUNK), rows)]
                act(pltpu.make_async_copy(l_rows, g_rows, sem) if to_hbm else pltpu.make_async_copy(g_rows, l_rows, sem))

        return 0

    lax.fori_loop(0, N_EXPERTS, per_expert, 0)


def _start(cp):
    cp.start()


def _wait(cp):
    cp.wait()


def _dispatch_kernel(nch_ref, lstart_ref, gstart_ref, zstart_ref, znch_ref,
                     x_ref, slot_ref, xs_hbm, buf, zero_scr, sem, zsem):
    i = pl.program_id(0)
    last = pl.num_programs(0) - 1
    tm = x_ref.shape[0]
    b = i % 2
    slot_rows = _to_rows(slot_ref[...].astype(F32))
    x = x_ref[...]

    def build(s, _):
        r0 = pl.multiple_of(s * LOCAL_STEP, LOCAL_STEP)
        rid = (lax.broadcasted_iota(jnp.int32, (LOCAL_STEP, tm), 0) + r0).astype(F32)
        hit = rid == slot_rows[0:1, :]
        for k in range(1, TOP_K):
            hit = hit | (rid == slot_rows[k:k + 1, :])
        buf[b, pl.ds(r0, LOCAL_STEP), :] = _dot(jnp.where(hit, 1.0, 0.0).astype(BF16), x)
        return 0

    lax.fori_loop(0, LOCAL_ROWS // LOCAL_STEP, build, 0, unroll=3)

    _chunk_copies(i, nch_ref, lstart_ref, gstart_ref, buf.at[b], xs_hbm, sem.at[b], True, _start)

    @pl.when(i > 0)
    def _():
        _chunk_copies(i - 1, nch_ref, lstart_ref, gstart_ref, buf.at[1 - b], xs_hbm, sem.at[1 - b], True, _wait)

    @pl.when(i == last)
    def _():
        _chunk_copies(i, nch_ref, lstart_ref, gstart_ref, buf.at[b], xs_hbm, sem.at[b], True, _wait)
        zero_scr[...] = jnp.zeros_like(zero_scr)

        def fill(act):
            def per_expert(e, _):
                def per_chunk(c, _):
                    row = pl.multiple_of(zstart_ref[e] + c * ROW_CHUNK, ROW_CHUNK)
                    act(pltpu.make_async_copy(zero_scr, xs_hbm.at[pl.ds(row, ROW_CHUNK)], zsem))
                    return 0

                lax.fori_loop(0, znch_ref[e], per_chunk, 0)
                return 0

            lax.fori_loop(0, N_EXPERTS, per_expert, 0)

        fill(_start)
        fill(_wait)


def _dispatch(tables, ztables, xn, slot, n_rows):
    tm = TOKEN_TILE
    t = xn.shape[0]
    return pl.pallas_call(
        _dispatch_kernel,
        out_shape=jax.ShapeDtypeStruct((n_rows, D_MODEL), F32),
        grid_spec=pltpu.PrefetchScalarGridSpec(
            num_scalar_prefetch=5,
            grid=(t // tm,),
            in_specs=[pl.BlockSpec((tm, D_MODEL), lambda i, *_: (i, 0)),
                      pl.BlockSpec((tm, LANES), lambda i, *_: (i, 0))],
            out_specs=pl.BlockSpec(memory_space=pl.ANY),
            scratch_shapes=[pltpu.VMEM((2, LOCAL_ROWS, D_MODEL), F32), pltpu.VMEM((ROW_CHUNK, D_MODEL), F32),
                            pltpu.SemaphoreType.DMA((2,)), pltpu.SemaphoreType.DMA(())]),
        compiler_params=_params(("arbitrary",)),
        name="moe_dispatch",
    )(*tables, *ztables, xn, slot)


def _experts_kernel(be_ref, bi_ref, na_ref, x_ref, wgu_ref, bgu_ref, wd_ref, bd_ref, o_ref):
    j = pl.program_id(0)

    @pl.when(j < na_ref[0])
    def _():
        x = x_ref[...].astype(BF16)
        gu = _dot(x, wgu_ref[0]) + bgu_ref[0]
        gate = jnp.minimum(gu[:, :D_MODEL], SWIGLU_LIMIT)
        up = jnp.clip(gu[:, D_MODEL:], -SWIGLU_LIMIT, SWIGLU_LIMIT)
        hid = (up + 1.0) * (gate * jax.nn.sigmoid(SWIGLU_ALPHA * gate))
        o_ref[...] = _dot(hid.astype(BF16), wd_ref[0]) + bd_ref[0]


def _experts(block_e, block_i, n_active, xs, wgu, bgu, wd, bd):
    n_blocks = xs.shape[0] // MOE_BLOCK
    row_map = lambda j, be, bi, na: (bi[j], 0)
    e_map = lambda j, be, bi, na: (be[j], 0, 0)
    return pl.pallas_call(
        _experts_kernel,
        out_shape=jax.ShapeDtypeStruct(xs.shape, F32),
        grid_spec=pltpu.PrefetchScalarGridSpec(
            num_scalar_prefetch=3,
            grid=(n_blocks,),
            in_specs=[pl.BlockSpec((MOE_BLOCK, D_MODEL), row_map),
                      pl.BlockSpec((1, D_MODEL, 2 * D_MODEL), e_map),
                      pl.BlockSpec((1, 1, 2 * D_MODEL), e_map),
                      pl.BlockSpec((1, D_MODEL, D_MODEL), e_map),
                      pl.BlockSpec((1, 1, D_MODEL), e_map)],
            out_specs=pl.BlockSpec((MOE_BLOCK, D_MODEL), row_map)),
        compiler_params=_params(("arbitrary",)),
        name="moe_experts",
    )(block_e, block_i, n_active, xs, wgu, bgu, wd, bd)


def _combine_kernel(nch_ref, lstart_ref, gstart_ref, slot_ref, gate_ref, h_ref, g_ref, ys_hbm, op_ref, os_ref,
                    buf, pick_scr, scaled_scr, sem, *, n_prompt_tiles):
    i = pl.program_id(0)
    n = pl.num_programs(0)
    tm = h_ref.shape[0]
    b = i % 2

    @pl.when(i == 0)
    def _():
        buf[...] = jnp.zeros_like(buf)
        _chunk_copies(0, nch_ref, lstart_ref, gstart_ref, buf.at[0], ys_hbm, sem.at[0], False, _start)

    @pl.when(i + 1 < n)
    def _():
        _chunk_copies(i + 1, nch_ref, lstart_ref, gstart_ref, buf.at[1 - b], ys_hbm, sem.at[1 - b], False, _start)

    _chunk_copies(i, nch_ref, lstart_ref, gstart_ref, buf.at[b], ys_hbm, sem.at[b], False, _wait)

    slot = slot_ref[...].astype(F32)
    slot_rows = _to_rows(slot)
    info = gate_ref[...] + pltpu.roll(slot, TOP_K, 1)
    info_hi = info.astype(BF16)
    rest = info - info_hi.astype(F32)
    info_mid = rest.astype(BF16)
    info_lo = (rest - info_mid.astype(F32)).astype(BF16)
    lane = lax.broadcasted_iota(jnp.int32, (LOCAL_STEP, LANES), 1)

    def step(s, _):
        r0 = pl.multiple_of(s * LOCAL_STEP, LOCAL_STEP)
        rid = (lax.broadcasted_iota(jnp.int32, (LOCAL_STEP, tm), 0) + r0).astype(F32)
        hit = rid == slot_rows[0:1, :]
        for k in range(1, TOP_K):
            hit = hit | (rid == slot_rows[k:k + 1, :])
        pick = jnp.where(hit, 1.0, 0.0).astype(BF16)
        pick_scr[pl.ds(r0, LOCAL_STEP), :] = pick
        own = _dot(pick, info_hi) + _dot(pick, info_mid) + _dot(pick, info_lo)
        rcol = (lax.broadcasted_iota(jnp.int32, (LOCAL_STEP, 1), 0) + r0).astype(F32)
        mine = jnp.where((own == rcol) & (lane >= TOP_K) & (lane < 2 * TOP_K), 1.0, 0.0)
        g_col = jnp.sum(own * pltpu.roll(mine, LANES - TOP_K, 1), axis=-1, keepdims=True)
        rows = buf[b, pl.ds(r0, LOCAL_STEP), :]
        scaled_scr[pl.ds(r0, LOCAL_STEP), :] = jnp.where(g_col != 0.0, rows * g_col, 0.0).astype(BF16)
        return 0

    lax.fori_loop(0, LOCAL_ROWS // LOCAL_STEP, step, 0, unroll=3)
    y = _dot_tn(pick_scr[...], scaled_scr[...])
    out = _rms(h_ref[...] + y, g_ref[...])

    @pl.when(i < n_prompt_tiles)
    def _():
        op_ref[...] = out

    @pl.when(i >= n_prompt_tiles)
    def _():
        os_ref[...] = out


def _combine(tables, slot, gates, h2, g_final, ys, tp, ts):
    tm = TOKEN_TILE
    npt, nst = tp // tm, ts // tm
    tok = lambda w: pl.BlockSpec((tm, w), lambda i, *_: (i, 0))
    return pl.pallas_call(
        functools.partial(_combine_kernel, n_prompt_tiles=npt),
        out_shape=[jax.ShapeDtypeStruct((tp, D_MODEL), F32), jax.ShapeDtypeStruct((ts, D_MODEL), F32)],
        grid_spec=pltpu.PrefetchScalarGridSpec(
            num_scalar_prefetch=3,
            grid=(npt + nst,),
            in_specs=[tok(LANES), tok(LANES), tok(D_MODEL),
                      pl.BlockSpec((1, D_MODEL), lambda i, *_: (0, 0)),
                      pl.BlockSpec(memory_space=pl.ANY)],
            out_specs=[pl.BlockSpec((tm, D_MODEL), lambda i, *_: (jnp.minimum(i, npt - 1), 0)),
                       pl.BlockSpec((tm, D_MODEL), lambda i, *_: (jnp.maximum(i - npt, 0), 0))],
            scratch_shapes=[pltpu.VMEM((2, LOCAL_ROWS, D_MODEL), F32), pltpu.VMEM((LOCAL_ROWS, tm), BF16),
                            pltpu.VMEM((LOCAL_ROWS, D_MODEL), BF16), pltpu.SemaphoreType.DMA((2,))]),
        compiler_params=_params(("arbitrary",)),
        name="moe_combine_norm",
    )(*tables, slot, gates, h2, g_final, ys)


def _rope_tables(seq_len):
    pos = jnp.arange(seq_len, dtype=F32)[:, None]
    lane = np.arange(LANES)

    def table(half, lane_freq, first, active):
        inv = ROPE_BASE ** (-jnp.arange(half, dtype=F32) / half)
        ang = pos * inv[None, :]
        cos = jnp.cos(ang)[:, lane_freq]
        sin = jnp.sin(ang)[:, lane_freq]
        c = jnp.where(active[None, :], cos, 1.0)
        s = jnp.where(active[None, :], jnp.where(first[None, :], -sin, sin), 0.0)
        return c, s

    half_r = RET_DK // 2
    cr, sr = table(half_r, lane % half_r, (lane % RET_DK) < half_r, np.ones(LANES, bool))
    half_m = MLA_ROPE // 2
    rel = lane - MLA_NOPE
    active = (rel >= 0) & (rel < MLA_ROPE)
    cm, sm = table(half_m, np.where(active, rel % half_m, 0), active & (rel < half_m), active)
    return cr, sr, cm, sm


def kernel(x_prompt, x_sample, mem_prompt, mem_sample, norm_mix, w_in, ret_decay_fwd, ret_decay_bwd, ret_gn, q_a_norm, w_uq, kv_a_norm, w_ukv, w_mix_out, norm_cross, norm_mem, w_xq, w_xkv, w_xo, norm_ffn, router_w, router_b, w_gu, b_gu, w_down, b_down, norm_final):
    assert norm_mix.shape[0] == 1, "single layer"
    bp, sp, d = x_prompt.shape
    bs, ss, _ = x_sample.shape
    tp, ts = bp * sp, bs * ss
    t = tp + ts
    assert d == D_MODEL and sp % TOKEN_TILE == 0 and ss % TOKEN_TILE == 0 and tp % ss == 0 and sp >= ss
    mem_len = mem_prompt.shape[1]

    w_in0 = w_in[0]
    w_in_pad = jnp.zeros((D_MODEL, D_IN_PAD), F32)
    w_in_pad = w_in_pad.at[:, :2816].set(w_in0[:, :2816])
    w_in_pad = w_in_pad.at[:, 2816 + MLA_NOPE:2816 + MLA_NOPE + MLA_ROPE].set(w_in0[:, 2816:]).astype(BF16)
    wuq_pad = jnp.pad(w_uq[0].reshape(Q_LORA, MLA_HEADS, MLA_NOPE + MLA_ROPE),
                      ((0, 0), (0, 0), (0, LANES - MLA_NOPE - MLA_ROPE))).reshape(Q_LORA, MLA_HEADS * LANES).astype(BF16)
    wukv = w_ukv[0].reshape(KV_LORA, MLA_HEADS, MLA_NOPE + MLA_V)
    wuk_pad = jnp.pad(wukv[:, :, :MLA_NOPE], ((0, 0), (0, 0), (0, LANES - MLA_NOPE))).reshape(KV_LORA, MLA_HEADS * LANES).astype(BF16)
    wuv = wukv[:, :, MLA_NOPE:].reshape(KV_LORA, MLA_WIDTH).astype(BF16)
    rw_pad = jnp.pad(router_w[0], ((0, 0), (0, LANES - N_EXPERTS))).astype(BF16)
    rb_pad = jnp.pad(router_b[0].astype(F32), (0, LANES - N_EXPERTS), constant_values=NEG_BIG)[None, :]
    lgf = jnp.log1p(-jnp.exp2(ret_decay_fwd[0].astype(F32)))
    lgb = jnp.log1p(-jnp.exp2(ret_decay_bwd[0].astype(F32)))
    tabs = _rope_tables(sp)

    xp = x_prompt.reshape(tp, D_MODEL)
    xs = x_sample.reshape(ts, D_MODEL)
    mem = jnp.concatenate([mem_prompt.reshape(-1, D_MODEL), mem_sample.reshape(-1, D_MODEL)], axis=0)

    kvmem = _mem_kv(mem, norm_mem[0][None, :], w_xkv[0].astype(BF16)).reshape(bp + bs, mem_len, 2 * D_MODEL)

    rq, rk, rv, rg, qm, km, vm = _in_proj(xp, xs, norm_mix[0][None, :], w_in_pad, q_a_norm[0][None, :], wuq_pad,
                                          kv_a_norm[0][None, :], wuk_pad, wuv, tabs, sp, ss)

    gn = ret_gn[0][None, :]
    ret_p = _retention(lgf, lgb, rq, rk, rv, rg, gn, sp, bp, 0)
    ret_s = _retention(lgf, lgb, rq, rk, rv, rg, gn, ss, bs, tp // ss)
    mla_p = _mla_attn(qm, km, vm, sp, bp, 0)
    mla_s = _mla_attn(qm, km, vm, ss, bs, tp // ss)

    h2, xn, gates, slot, stats, counts = _mix_cross(
        xp, xs, ret_p, ret_s, mla_p, mla_s, w_mix_out[0].astype(BF16), norm_cross[0][None, :], w_xq[0].astype(BF16),
        kvmem, w_xo[0].astype(BF16), norm_ffn[0][None, :], rw_pad, rb_pad, sp, ss)

    used = counts[0, :N_EXPERTS]
    padded = (used + MOE_BLOCK - 1) // MOE_BLOCK * MOE_BLOCK
    pend = jnp.cumsum(padded)
    pstart = pend - padded
    tile_cnt, tile_before, tile_lstart = (stats[:, r, :N_EXPERTS] for r in range(3))
    n_tiles = stats.shape[0]
    tables = (((tile_cnt + ROW_CHUNK - 1) // ROW_CHUNK).reshape(-1),
              tile_lstart.reshape(-1),
              (pstart[None, :] + tile_before).reshape(-1))
    ztables = (pstart + used, (padded - used) // ROW_CHUNK)
    n_blocks = -(-(t * TOP_K + N_EXPERTS * (n_tiles * (ROW_CHUNK - 1) + MOE_BLOCK - 1)) // MOE_BLOCK)
    blk = jnp.arange(n_blocks, dtype=jnp.int32)
    n_active = (pend[-1] // MOE_BLOCK).astype(jnp.int32)
    block_i = jnp.minimum(blk, n_active - 1)
    block_e = jnp.minimum(jnp.sum((block_i[:, None] * MOE_BLOCK >= pend[None, :]).astype(jnp.int32), axis=-1),
                          N_EXPERTS - 1).astype(jnp.int32)

    xs_sorted = _dispatch(tables, ztables, xn, slot, n_blocks * MOE_BLOCK)
    ys = _experts(block_e, block_i, n_active[None], xs_sorted, w_gu[0].astype(BF16), b_gu[0][:, None, :],
                  w_down[0].astype(BF16), b_down[0][:, None, :])
    out_p, out_s = _combine(tables, slot, gates, h2, norm_final[None, :], ys, tp, ts)
    return out_p.reshape(bp, sp, D_MODEL), out_s.reshape(bs, ss, D_MODEL)
```

```python
import functools
import math

import jax
import jax.numpy as jnp
import numpy as np
from jax import lax
from jax.experimental import pallas as pl
from jax.experimental.pallas import tpu as pltpu

D_MODEL = 1024
RET_HEADS = 8
RET_DK = 64
RET_WIDTH = 512
CHUNK = 128
MLA_HEADS = 8
MLA_NOPE = 64
MLA_ROPE = 32
MLA_V = 64
MLA_WIDTH = 512
Q_LORA = 512
KV_LORA = 256
ROPE_BASE = 10000.0
X_HEADS = 4
X_HEAD_DIM = 256
N_EXPERTS = 32
TOP_K = 4
SWIGLU_LIMIT = 7.0
SWIGLU_ALPHA = 1.702
MOE_BLOCK = 512
EPS = 1e-6

LANES = 128
VMEM_LIMIT = 56 * 1024 * 1024

TOKEN_TILE = 512
ROW_PARTS = 2
MIX_ROW_PARTS = 1
ATTN_TQ = 1024
ATTN_TK = 1024
RET_UNROLL = 8
ROW_CHUNK = 8
GROUP_BITS = (TOKEN_TILE // ROW_CHUNK).bit_length()
WAIT_CHUNKS = 16
LOCAL_STEP = 256
LOCAL_ROWS = -(-(TOKEN_TILE * 4 + 32 * (ROW_CHUNK - 1)) // LOCAL_STEP) * LOCAL_STEP
D_IN_PAD = 4 * 512 + Q_LORA + KV_LORA + LANES
NEG_BIG = -1e30

F32 = jnp.float32
BF16 = jnp.bfloat16


def _params(sem, vmem=VMEM_LIMIT):
    return pltpu.CompilerParams(dimension_semantics=sem, vmem_limit_bytes=vmem)


def _const_spec(shape):
    nd = len(shape)
    return pl.BlockSpec(shape, lambda *_: (0,) * nd, pipeline_mode=pl.Buffered(1))


def _rms(x, g):
    ms = jnp.mean(x * x, axis=-1, keepdims=True)
    return x * lax.rsqrt(ms + EPS) * g


def _dot(a, b):
    return jnp.dot(a, b, preferred_element_type=F32)


def _dot_nt(a, b):
    return lax.dot_general(a, b, (((1,), (1,)), ((), ())), preferred_element_type=F32)


def _dot_tn(a, b):
    return lax.dot_general(a, b, (((0,), (0,)), ((), ())), preferred_element_type=F32)


def _mem_kv_kernel(mem_ref, g_ref, w_ref, o_ref):
    mn = _rms(mem_ref[...], g_ref[...])
    o_ref[...] = _dot(mn.astype(BF16), w_ref[...]).astype(BF16)


def _mem_kv(mem, g, w):
    rows, mem_len = mem.shape[0], 256
    return pl.pallas_call(
        _mem_kv_kernel,
        out_shape=jax.ShapeDtypeStruct((rows, 2 * D_MODEL), BF16),
        grid=(rows // mem_len,),
        in_specs=[pl.BlockSpec((mem_len, D_MODEL), lambda i: (i, 0)),
                  _const_spec((1, D_MODEL)),
                  _const_spec((D_MODEL, 2 * D_MODEL))],
        out_specs=pl.BlockSpec((mem_len, 2 * D_MODEL), lambda i: (i, 0)),
        compiler_params=_params(("arbitrary",)),
        name="mem_kv",
    )(mem, g, w)


def _rope_slab(x, c, ss, first, shift_up, shift_down):
    swap = jnp.where(first, pltpu.roll(x, shift_up, 1), pltpu.roll(x, shift_down, 1))
    return x * c + swap * ss


def _in_proj_kernel(xp_ref, xs_ref, g_ref, win_ref, qan_ref, wuq_ref, kvan_ref, wuk_ref, wuv_ref,
                    cr_ref, sr_ref, cm_ref, sm_ref,
                    rq_ref, rk_ref, rv_ref, rg_ref, qm_ref, km_ref, vm_ref, *, n_prompt_tiles):
    i = pl.program_id(0)
    tm = rq_ref.shape[0]
    lane = lax.broadcasted_iota(jnp.int32, (1, LANES), 1)
    ret_first = (lane % RET_DK) < (RET_DK // 2)
    mla_first = (lane >= MLA_NOPE) & (lane < MLA_NOPE + MLA_ROPE // 2)
    half_r, half_m = RET_DK // 2, MLA_ROPE // 2
    q_scale = (MLA_NOPE + MLA_ROPE) ** -0.5 * math.log2(math.e)

    for part in range(ROW_PARTS):
        rows = slice(part * (tm // ROW_PARTS), (part + 1) * (tm // ROW_PARTS))
        x = jnp.where(i < n_prompt_tiles, xp_ref[rows, :], xs_ref[rows, :])
        xn = _rms(x, g_ref[...])
        proj = _dot(xn.astype(BF16), win_ref[...])
        cr, sr, cm, sm = cr_ref[rows, :], sr_ref[rows, :], cm_ref[rows, :], sm_ref[rows, :]

        for s in range(RET_WIDTH // LANES):
            lo = s * LANES
            q = _rope_slab(proj[:, lo:lo + LANES], cr, sr, ret_first, LANES - half_r, half_r)
            rq_ref[rows, lo:lo + LANES] = q.astype(BF16)
            k = _rope_slab(proj[:, 512 + lo:512 + lo + LANES], cr, sr, ret_first, LANES - half_r, half_r)
            rk_ref[rows, lo:lo + LANES] = (k * (RET_DK ** -0.5)).astype(BF16)
        rv_ref[rows, :] = proj[:, 1024:1536].astype(BF16)
        rg_ref[rows, :] = proj[:, 1536:2048].astype(BF16)

        cq = _rms(proj[:, 2048:2048 + Q_LORA], qan_ref[...])
        qm = _dot(cq.astype(BF16), wuq_ref[...])
        ckv = _rms(proj[:, 2560:2560 + KV_LORA], kvan_ref[...]).astype(BF16)
        kn = _dot(ckv, wuk_ref[...])
        vm_ref[rows, :] = _dot(ckv, wuv_ref[...]).astype(BF16)
        kr = proj[:, 2816:2816 + LANES]
        kpe = _rope_slab(kr, cm, sm, mla_first, LANES - half_m, half_m)
        for h in range(MLA_HEADS):
            lo = h * LANES
            qh = _rope_slab(qm[:, lo:lo + LANES], cm, sm, mla_first, LANES - half_m, half_m)
            qm_ref[rows, lo:lo + LANES] = (qh * q_scale).astype(BF16)
            km_ref[rows, lo:lo + LANES] = (kn[:, lo:lo + LANES] + kpe).astype(BF16)


def _in_proj(xp, xs, g, w_in_pad, qan, wuq_pad, kvan, wuk_pad, wuv, tabs, sp, ss):
    tm = TOKEN_TILE
    tp, ts = xp.shape[0], xs.shape[0]
    npt, nst = tp // tm, ts // tm
    t = tp + ts
    tiles_p, tiles_s = sp // tm, ss // tm

    def xp_map(i):
        return (jnp.minimum(i, npt - 1), 0)

    def xs_map(i):
        return (jnp.maximum(i - npt, 0), 0)

    def tab_map(i):
        return (jnp.where(i < npt, i % tiles_p, (i - npt) % tiles_s), 0)

    tok = lambda w: pl.BlockSpec((tm, w), lambda i: (i, 0))
    widths = (512, 512, 512, 512, 1024, 1024, 512)
    return pl.pallas_call(
        functools.partial(_in_proj_kernel, n_prompt_tiles=npt),
        out_shape=[jax.ShapeDtypeStruct((t, w), BF16) for w in widths],
        grid=(npt + nst,),
        in_specs=[pl.BlockSpec((tm, D_MODEL), xp_map), pl.BlockSpec((tm, D_MODEL), xs_map),
                  _const_spec((1, D_MODEL)), _const_spec((D_MODEL, D_IN_PAD)),
                  _const_spec((1, Q_LORA)), _const_spec((Q_LORA, MLA_HEADS * LANES)),
                  _const_spec((1, KV_LORA)), _const_spec((KV_LORA, MLA_HEADS * LANES)),
                  _const_spec((KV_LORA, MLA_WIDTH))]
                 + [pl.BlockSpec((tm, LANES), tab_map)] * 4,
        out_specs=[tok(w) for w in widths],
        compiler_params=_params(("arbitrary",)),
        name="in_proj",
    )(xp, xs, g, w_in_pad, qan, wuq_pad, kvan, wuk_pad, wuv, *tabs)


def _retention_kernel(lgf_ref, lgb_ref, q_ref, k_ref, v_ref, g_ref, gn_ref, o_ref, kvf_scr, kvb_scr, st_scr, *, n_chunks):
    hp = pl.program_id(1)
    c = CHUNK
    lane = lax.broadcasted_iota(jnp.int32, (1, c), 1)
    row = lax.broadcasted_iota(jnp.int32, (c, 1), 0)
    lane_h0 = lane < RET_DK
    row_h0 = row < RET_DK
    lgf0, lgf1 = lgf_ref[2 * hp], lgf_ref[2 * hp + 1]
    lgb0, lgb1 = lgb_ref[2 * hp], lgb_ref[2 * hp + 1]
    lgf_lane = jnp.where(lane_h0, lgf0, lgf1)
    lgb_lane = jnp.where(lane_h0, lgb0, lgb1)
    t = row.astype(F32)
    q_dec_f = jnp.exp((t + 1.0) * lgf_lane)
    q_dec_b = jnp.exp((c - t) * lgb_lane)
    k_dec_f = jnp.exp((c - 1.0 - t) * lgf_lane)
    k_dec_b = jnp.exp(t * lgb_lane)
    chunk_dec_f = jnp.exp(c * jnp.where(row_h0, lgf0, lgf1))
    chunk_dec_b = jnp.exp(c * jnp.where(row_h0, lgb0, lgb1))
    same_head = row_h0 == lane_h0
    diff = t - lane.astype(F32)
    d_intra = []
    for lgf, lgb in ((lgf0, lgb0), (lgf1, lgb1)):
        fwd = jnp.where(diff >= 0, jnp.exp(jnp.where(diff >= 0, diff, 0.0) * lgf), 0.0)
        bwd = jnp.where(diff < 0, jnp.exp(jnp.where(diff < 0, -diff, 0.0) * lgb), 0.0)
        d_intra.append(fwd + bwd)
    avg = jnp.where(same_head, 1.0 / RET_DK, 0.0).astype(BF16)
    gn = gn_ref[...]

    def chunk(n):
        return pl.ds(pl.multiple_of(n * c, c), c)

    unroll = min(RET_UNROLL, n_chunks)

    def chunk_kv(n, _):
        k = k_ref[chunk(n), :].astype(F32)
        kd = jnp.concatenate([(k * k_dec_f).astype(BF16), (k * k_dec_b).astype(BF16)], axis=-1)
        kv = _dot_tn(kd, v_ref[chunk(n), :])
        kvf_scr[n] = jnp.where(same_head, kv[0:c], 0.0)
        kvb_scr[n] = jnp.where(same_head, kv[c:2 * c], 0.0)
        return 0

    lax.fori_loop(0, n_chunks, chunk_kv, 0, unroll=unroll)

    def fwd_state(n, s_f):
        st_scr[n, 0:c, :] = s_f.astype(BF16)
        return chunk_dec_f * s_f + kvf_scr[n]

    lax.fori_loop(0, n_chunks, fwd_state, jnp.zeros((c, c), F32))

    def bwd_state(j, s_b):
        n = n_chunks - 1 - j
        st_scr[n, c:2 * c, :] = s_b.astype(BF16)
        return chunk_dec_b * s_b + kvb_scr[n]

    lax.fori_loop(0, n_chunks, bwd_state, jnp.zeros((c, c), F32))

    d_both = jnp.concatenate(d_intra, axis=0)
    zero_q = jnp.zeros((c, c), BF16)

    def chunk_out(n):
        q = q_ref[chunk(n), :]
        qf = q.astype(F32)
        q_both = jnp.concatenate([(qf * q_dec_f).astype(BF16), (qf * q_dec_b).astype(BF16)], axis=-1)
        q_heads = jnp.concatenate([jnp.where(lane_h0, q, zero_q), jnp.where(lane_h0, zero_q, q)], axis=0)
        p = (_dot_nt(q_heads, k_ref[chunk(n), :]) * d_both).astype(BF16)
        inner = _dot(p, v_ref[chunk(n), :])
        return _dot(q_both, st_scr[n]) + jnp.where(lane_h0, inner[0:c], inner[c:2 * c])

    def split_rows(a):
        hi = a.astype(BF16)
        return jnp.concatenate([hi, (a - hi.astype(F32)).astype(BF16)], axis=0)

    def out_group(gi, _):
        rows = pl.ds(pl.multiple_of(gi * (unroll * c), unroll * c), unroll * c)
        y = jnp.concatenate([chunk_out(gi * unroll + u) for u in range(unroll)], axis=0)
        m = y.shape[0]
        mu2 = _dot(split_rows(y), avg)
        d = y - (mu2[0:m] + mu2[m:2 * m])
        var2 = _dot(split_rows(d * d), avg)
        yn = d * lax.rsqrt(var2[0:m] + var2[m:2 * m] + EPS) * gn
        gate = g_ref[rows, :].astype(F32)
        o_ref[rows, :] = (yn * (gate * jax.nn.sigmoid(gate))).astype(BF16)
        return 0

    lax.fori_loop(0, n_chunks // unroll, out_group, 0)


def _retention(lgf, lgb, rq, rk, rv, rg, gn, seq_len, n_seq, row_block0):
    n_chunks = seq_len // CHUNK
    hp_count = RET_WIDTH // LANES
    blk = pl.BlockSpec((seq_len, LANES), lambda b, hp, *_: (row_block0 + b, hp))
    return pl.pallas_call(
        functools.partial(_retention_kernel, n_chunks=n_chunks),
        out_shape=jax.ShapeDtypeStruct((n_seq * seq_len, RET_WIDTH), BF16),
        grid_spec=pltpu.PrefetchScalarGridSpec(
            num_scalar_prefetch=2,
            grid=(n_seq, hp_count),
            in_specs=[blk, blk, blk, blk, pl.BlockSpec((1, LANES), lambda b, hp, *_: (0, hp))],
            out_specs=pl.BlockSpec((seq_len, LANES), lambda b, hp, *_: (b, hp)),
            scratch_shapes=[pltpu.VMEM((n_chunks, CHUNK, CHUNK), F32), pltpu.VMEM((n_chunks, CHUNK, CHUNK), F32),
                            pltpu.VMEM((n_chunks, 2 * CHUNK, CHUNK), BF16)]),
        compiler_params=_params(("arbitrary", "arbitrary")),
        name="retention",
    )(lgf, lgb, rq, rk, rv, rg, gn)


def _mla_attn_kernel(q_ref, k_ref, v_ref, o_ref, *, tk, n_kv):
    tq = q_ref.shape[0]
    lane = lax.broadcasted_iota(jnp.int32, (1, LANES), 1)
    qs = (q_ref[:, :LANES], q_ref[:, LANES:])

    def body(j, carry):
        rows = pl.ds(pl.multiple_of(j * tk, tk), tk)
        v = v_ref[rows, :]
        new = []
        for h in range(2):
            m, l, acc = carry[h]
            s = _dot_nt(qs[h], k_ref[rows, h * LANES:(h + 1) * LANES])
            m_new = jnp.maximum(m, jnp.max(s, axis=-1, keepdims=True))
            alpha = jnp.exp2(m - m_new)
            p = jnp.exp2(s - m_new)
            l = alpha * l + jnp.sum(p, axis=-1, keepdims=True)
            acc = alpha * acc + _dot(p.astype(BF16), v)
            new.append((m_new, l, acc))
        return tuple(new)

    init = (jnp.full((tq, 1), NEG_BIG, F32), jnp.zeros((tq, 1), F32), jnp.zeros((tq, LANES), F32))
    (_, l0, acc0), (_, l1, acc1) = lax.fori_loop(0, n_kv, body, (init, init), unroll=min(2, n_kv))
    o_ref[...] = jnp.where(lane < MLA_V, acc0 / l0, acc1 / l1).astype(BF16)


def _mla_attn(qm, km, vm, seq_len, n_seq, row_block0):
    tq, tk = min(ATTN_TQ, seq_len), min(ATTN_TK, seq_len)
    hp_count = MLA_HEADS // 2
    nq = seq_len // tq
    return pl.pallas_call(
        functools.partial(_mla_attn_kernel, tk=tk, n_kv=seq_len // tk),
        out_shape=jax.ShapeDtypeStruct((n_seq * seq_len, MLA_WIDTH), BF16),
        grid=(n_seq, hp_count, nq),
        in_specs=[pl.BlockSpec((tq, 2 * LANES), lambda b, hp, i: ((row_block0 + b) * nq + i, hp)),
                  pl.BlockSpec((seq_len, 2 * LANES), lambda b, hp, i: (row_block0 + b, hp)),
                  pl.BlockSpec((seq_len, LANES), lambda b, hp, i: (row_block0 + b, hp))],
        out_specs=pl.BlockSpec((tq, LANES), lambda b, hp, i: (b * nq + i, hp)),
        compiler_params=_params(("arbitrary", "arbitrary", "arbitrary")),
        name="mla_attn",
    )(qm, km, vm)


def _mix_cross_kernel(xp_ref, xs_ref, rp_ref, rs_ref, mp_ref, ms_ref, wmix_ref, gx_ref, wxq_ref, kv_ref, wxo_ref,
                      gf_ref, rw_ref, rb_ref,
                      h_ref, xn_ref, gate_ref, slot_ref, stat_ref, cnt_ref, carry_scr, *, n_prompt_tiles):
    i = pl.program_id(0)
    tm = h_ref.shape[0]
    is_p = i < n_prompt_tiles
    lane = lax.broadcasted_iota(jnp.int32, (tm, LANES), 1)

    def part_rows(rows):
        x = jnp.where(is_p, xp_ref[rows, :], xs_ref[rows, :])
        ret = jnp.where(is_p, rp_ref[rows, :], rs_ref[rows, :])
        mla = jnp.where(is_p, mp_ref[rows, :], ms_ref[rows, :])
        h1 = x + _dot(ret, wmix_ref[0:RET_WIDTH, :]) + _dot(mla, wmix_ref[RET_WIDTH:, :])

        hn = _rms(h1, gx_ref[...]).astype(BF16)
        q = (_dot(hn, wxq_ref[...]) * (X_HEAD_DIM ** -0.5)).astype(BF16)
        heads = []
        for h in range(X_HEADS):
            lo = h * X_HEAD_DIM
            s = _dot_nt(q[:, lo:lo + X_HEAD_DIM], kv_ref[0, :, lo:lo + X_HEAD_DIM])
            e = jnp.exp(s - jnp.max(s, axis=-1, keepdims=True))
            p = (e / jnp.sum(e, axis=-1, keepdims=True)).astype(BF16)
            heads.append(_dot(p, kv_ref[0, :, D_MODEL + lo:D_MODEL + lo + X_HEAD_DIM]).astype(BF16))
        h2 = h1 + _dot(jnp.concatenate(heads, axis=-1), wxo_ref[...])
        h_ref[rows, :] = h2

        xn = _rms(h2, gf_ref[...]).astype(BF16)
        xn_ref[rows, :] = xn
        work = _dot(xn, rw_ref[...]) + rb_ref[...]
        lane_f = lax.broadcasted_iota(jnp.int32, work.shape, 1).astype(F32)
        vals, picks = [], []
        for _ in range(TOP_K):
            m = jnp.max(work, axis=-1, keepdims=True)
            idx = jnp.min(jnp.where(work == m, lane_f, float(LANES)), axis=-1, keepdims=True)
            sel = lane_f == idx
            work = jnp.where(sel, -jnp.inf, work)
            vals.append(m)
            picks.append(jnp.where(sel, 1.0, 0.0))
        exps = [jnp.exp(v - vals[0]) for v in vals]
        denom = exps[0] + exps[1] + exps[2] + exps[3]
        return picks, [e / denom for e in exps]

    part = tm // MIX_ROW_PARTS
    parts = [part_rows(slice(r * part, (r + 1) * part)) for r in range(MIX_ROW_PARTS)]
    picks = [jnp.concatenate([p[0][k] for p in parts], axis=0) for k in range(TOP_K)]
    gates = [jnp.concatenate([p[1][k] for p in parts], axis=0) for k in range(TOP_K)]
    onehot = picks[0] + picks[1] + picks[2] + picks[3]


    @pl.when(i == 0)
    def _():
        carry_scr[...] = jnp.zeros_like(carry_scr)

    r_iota = lax.broadcasted_iota(jnp.int32, (tm, tm), 0)
    c_iota = lax.broadcasted_iota(jnp.int32, (tm, tm), 1)
    lower = jnp.where(c_iota < r_iota, 1.0, 0.0).astype(BF16)
    before = _dot(lower, onehot.astype(BF16))
    tile_cnt = jnp.sum(onehot, axis=0, keepdims=True)
    group_rows = jnp.floor((tile_cnt + (ROW_CHUNK - 1)) * (1.0 / ROW_CHUNK)) * ROW_CHUNK
    carry_before = carry_scr[...]
    carry = carry_before + group_rows
    carry_scr[...] = carry
    cnt_ref[...] = carry.astype(jnp.int32)
    e_row = lax.broadcasted_iota(jnp.int32, (LANES, LANES), 0)
    e_col = lax.broadcasted_iota(jnp.int32, (LANES, LANES), 1)
    earlier = jnp.where(e_row < e_col, 1.0, 0.0).astype(BF16)
    group_start = _dot(jnp.broadcast_to(group_rows, (8, LANES)).astype(BF16), earlier)[0:1, :]
    row8 = lax.broadcasted_iota(jnp.int32, (8, LANES), 0)
    stats = jnp.where(row8 == 0, tile_cnt, jnp.where(row8 == 1, carry_before, jnp.where(row8 == 2, group_start, 0.0)))
    stat_ref[0] = stats.astype(jnp.int32)

    place = before + group_start
    gate_out = jnp.zeros((tm, LANES), F32)
    slot_out = jnp.zeros((tm, LANES), F32)
    for k in range(TOP_K):
        slot_k = jnp.sum(picks[k] * place, axis=-1, keepdims=True)
        gate_out = jnp.where(lane == k, gates[k], gate_out)
        slot_out = jnp.where(lane == k, slot_k, slot_out)
    gate_ref[...] = gate_out
    slot_ref[...] = slot_out.astype(jnp.int32)


def _mix_cross(xp, xs, ret_p, ret_s, mla_p, mla_s, wmix, gx, wxq, kvmem, wxo, gf, rw_pad, rb_pad, sp, ss):
    tm = TOKEN_TILE
    tp, ts = xp.shape[0], xs.shape[0]
    npt, nst = tp // tm, ts // tm
    t = tp + ts
    n_seq_p = tp // sp
    mem_len = kvmem.shape[1]

    def p_map(i):
        return (jnp.minimum(i, npt - 1), 0)

    def s_map(i):
        return (jnp.maximum(i - npt, 0), 0)

    def kv_map(i):
        return (jnp.where(i < npt, i // (sp // tm), n_seq_p + (i - npt) // (ss // tm)), 0, 0)

    tok = lambda w: pl.BlockSpec((tm, w), lambda i: (i, 0))
    return pl.pallas_call(
        functools.partial(_mix_cross_kernel, n_prompt_tiles=npt),
        out_shape=[jax.ShapeDtypeStruct((t, D_MODEL), F32), jax.ShapeDtypeStruct((t, D_MODEL), BF16),
                   jax.ShapeDtypeStruct((t, LANES), F32), jax.ShapeDtypeStruct((t, LANES), jnp.int32),
                   jax.ShapeDtypeStruct((npt + nst, 8, LANES), jnp.int32), jax.ShapeDtypeStruct((1, LANES), jnp.int32)],
        grid=(npt + nst,),
        in_specs=[pl.BlockSpec((tm, D_MODEL), p_map), pl.BlockSpec((tm, D_MODEL), s_map),
                  pl.BlockSpec((tm, RET_WIDTH), p_map), pl.BlockSpec((tm, RET_WIDTH), s_map),
                  pl.BlockSpec((tm, MLA_WIDTH), p_map), pl.BlockSpec((tm, MLA_WIDTH), s_map),
                  _const_spec((D_MODEL, D_MODEL)), _const_spec((1, D_MODEL)), _const_spec((D_MODEL, D_MODEL)),
                  pl.BlockSpec((1, mem_len, 2 * D_MODEL), kv_map),
                  _const_spec((D_MODEL, D_MODEL)), _const_spec((1, D_MODEL)),
                  _const_spec((D_MODEL, LANES)), _const_spec((1, LANES))],
        out_specs=[tok(D_MODEL), tok(D_MODEL), tok(LANES), tok(LANES),
                   pl.BlockSpec((1, 8, LANES), lambda i: (i, 0, 0)),
                   pl.BlockSpec((1, LANES), lambda i: (0, 0))],
        scratch_shapes=[pltpu.VMEM((1, LANES), F32)],
        compiler_params=_params(("arbitrary",)),
        name="mix_cross_router",
    )(xp, xs, ret_p, ret_s, mla_p, mla_s, wmix, gx, wxq, kvmem, wxo, gf, rw_pad, rb_pad)


def _to_rows(a):
    eye = (lax.broadcasted_iota(jnp.int32, (LANES, LANES), 0)
           == lax.broadcasted_iota(jnp.int32, (LANES, LANES), 1)).astype(BF16)
    hi = a.astype(BF16)
    r1 = a - hi.astype(F32)
    mid = r1.astype(BF16)
    lo = (r1 - mid.astype(F32)).astype(BF16)
    return _dot_nt(eye, hi) + _dot_nt(eye, mid) + _dot_nt(eye, lo)


def _chunk_copies(tile, nch_ref, lstart_ref, gstart_ref, local, hbm, sem, to_hbm, act):
    def per_expert(e, _):
        base = tile * N_EXPERTS + e
        n = nch_ref[base]
        ls = lstart_ref[base]
        gs = gstart_ref[base]
        for j in range(GROUP_BITS):
            rows = ROW_CHUNK << j

            @pl.when(((n >> j) & 1) == 1)
            def _(j=j, rows=rows):
                off = (n & ((1 << j) - 1)) * ROW_CHUNK
                l_rows = local.at[pl.ds(pl.multiple_of(ls + off, ROW_CHUNK), rows)]
                g_rows = hbm.at[pl.ds(pl.multiple_of(gs + off, ROW_CHUNK), rows)]
                act(pltpu.make_async_copy(l_rows, g_rows, sem) if to_hbm else pltpu.make_async_copy(g_rows, l_rows, sem))

        return 0

    lax.fori_loop(0, N_EXPERTS, per_expert, 0)


def _wait_tile(tile, tot_ref, local, hbm, sem, to_hbm):
    total = tot_ref[tile]

    def wait_rows(rows):
        l_rows = local.at[pl.ds(0, rows)]
        g_rows = hbm.at[pl.ds(0, rows)]
        (pltpu.make_async_copy(l_rows, g_rows, sem) if to_hbm else pltpu.make_async_copy(g_rows, l_rows, sem)).wait()

    def big(_, c):
        wait_rows(ROW_CHUNK * WAIT_CHUNKS)
        return c

    def small(_, c):
        wait_rows(ROW_CHUNK)
        return c

    lax.fori_loop(0, total // WAIT_CHUNKS, big, 0)
    lax.fori_loop(0, total % WAIT_CHUNKS, small, 0)


def _start(cp):
    cp.start()


def _wait(cp):
    cp.wait()


def _dispatch_kernel(nch_ref, lstart_ref, gstart_ref, tot_ref, zstart_ref, znch_ref,
                     x_ref, slot_ref, xs_hbm, buf, zero_scr, sem, zsem):
    i = pl.program_id(0)
    last = pl.num_programs(0) - 1
    tm = x_ref.shape[0]
    b = i % 2
    slot_rows = _to_rows(slot_ref[...].astype(F32))
    x = x_ref[...]

    def build(s, _):
        r0 = pl.multiple_of(s * LOCAL_STEP, LOCAL_STEP)
        rid = (lax.broadcasted_iota(jnp.int32, (LOCAL_STEP, tm), 0) + r0).astype(F32)
        hit = rid == slot_rows[0:1, :]
        for k in range(1, TOP_K):
            hit = hit | (rid == slot_rows[k:k + 1, :])
        buf[b, pl.ds(r0, LOCAL_STEP), :] = _dot(jnp.where(hit, 1.0, 0.0).astype(BF16), x)
        return 0

    lax.fori_loop(0, LOCAL_ROWS // LOCAL_STEP, build, 0, unroll=3)

    _chunk_copies(i, nch_ref, lstart_ref, gstart_ref, buf.at[b], xs_hbm, sem.at[b], True, _start)

    @pl.when(i > 0)
    def _():
        _wait_tile(i - 1, tot_ref, buf.at[1 - b], xs_hbm, sem.at[1 - b], True)

    @pl.when(i == last)
    def _():
        _wait_tile(i, tot_ref, buf.at[b], xs_hbm, sem.at[b], True)
        zero_scr[...] = jnp.zeros_like(zero_scr)

        def fill(act):
            def per_expert(e, _):
                def per_chunk(c, _):
                    row = pl.multiple_of(zstart_ref[e] + c * ROW_CHUNK, ROW_CHUNK)
                    act(pltpu.make_async_copy(zero_scr, xs_hbm.at[pl.ds(row, ROW_CHUNK)], zsem))
                    return 0

                lax.fori_loop(0, znch_ref[e], per_chunk, 0)
                return 0

            lax.fori_loop(0, N_EXPERTS, per_expert, 0)

        fill(_start)
        fill(_wait)


def _dispatch(tables, ztables, xn, slot, n_rows):
    tm = TOKEN_TILE
    t = xn.shape[0]
    return pl.pallas_call(
        _dispatch_kernel,
        out_shape=jax.ShapeDtypeStruct((n_rows, D_MODEL), F32),
        grid_spec=pltpu.PrefetchScalarGridSpec(
            num_scalar_prefetch=6,
            grid=(t // tm,),
            in_specs=[pl.BlockSpec((tm, D_MODEL), lambda i, *_: (i, 0)),
                      pl.BlockSpec((tm, LANES), lambda i, *_: (i, 0))],
            out_specs=pl.BlockSpec(memory_space=pl.ANY),
            scratch_shapes=[pltpu.VMEM((2, LOCAL_ROWS, D_MODEL), F32), pltpu.VMEM((ROW_CHUNK, D_MODEL), F32),
                            pltpu.SemaphoreType.DMA((2,)), pltpu.SemaphoreType.DMA(())]),
        compiler_params=_params(("arbitrary",)),
        name="moe_dispatch",
    )(*tables, *ztables, xn, slot)


def _experts_kernel(be_ref, bi_ref, na_ref, x_ref, wgu_ref, bgu_ref, wd_ref, bd_ref, o_ref, wgu_scr, wd_scr):
    j = pl.program_id(0)
    active = j < na_ref[0]

    @pl.when(active & ((j == 0) | (be_ref[j] != be_ref[jnp.maximum(j - 1, 0)])))
    def _():
        wgu_scr[...] = wgu_ref[0].astype(BF16)
        wd_scr[...] = wd_ref[0].astype(BF16)

    @pl.when(active)
    def _():
        x = x_ref[...].astype(BF16)
        gu = _dot(x, wgu_scr[...]) + bgu_ref[0]
        gate = jnp.minimum(gu[:, :D_MODEL], SWIGLU_LIMIT)
        up = jnp.clip(gu[:, D_MODEL:], -SWIGLU_LIMIT, SWIGLU_LIMIT)
        hid = (up + 1.0) * (gate * jax.nn.sigmoid(SWIGLU_ALPHA * gate))
        o_ref[...] = _dot(hid.astype(BF16), wd_scr[...]) + bd_ref[0]


def _experts(block_e, block_i, n_active, xs, wgu, bgu, wd, bd):
    n_blocks = xs.shape[0] // MOE_BLOCK
    row_map = lambda j, be, bi, na: (bi[j], 0)
    e_map = lambda j, be, bi, na: (be[j], 0, 0)
    return pl.pallas_call(
        _experts_kernel,
        out_shape=jax.ShapeDtypeStruct(xs.shape, F32),
        grid_spec=pltpu.PrefetchScalarGridSpec(
            num_scalar_prefetch=3,
            grid=(n_blocks,),
            in_specs=[pl.BlockSpec((MOE_BLOCK, D_MODEL), row_map),
                      pl.BlockSpec((1, D_MODEL, 2 * D_MODEL), e_map),
                      pl.BlockSpec((1, 1, 2 * D_MODEL), e_map),
                      pl.BlockSpec((1, D_MODEL, D_MODEL), e_map),
                      pl.BlockSpec((1, 1, D_MODEL), e_map)],
            out_specs=pl.BlockSpec((MOE_BLOCK, D_MODEL), row_map),
            scratch_shapes=[pltpu.VMEM((D_MODEL, 2 * D_MODEL), BF16), pltpu.VMEM((D_MODEL, D_MODEL), BF16)]),
        compiler_params=_params(("arbitrary",)),
        name="moe_experts",
    )(block_e, block_i, n_active, xs, wgu, bgu, wd, bd)


def _combine_kernel(nch_ref, lstart_ref, gstart_ref, tot_ref, slot_ref, gate_ref, h_ref, g_ref, ys_hbm, op_ref, os_ref,
                    buf, pick_scr, scaled_scr, sem, *, n_prompt_tiles):
    i = pl.program_id(0)
    n = pl.num_programs(0)
    tm = h_ref.shape[0]
    b = i % 2

    @pl.when(i == 0)
    def _():
        buf[...] = jnp.zeros_like(buf)
        _chunk_copies(0, nch_ref, lstart_ref, gstart_ref, buf.at[0], ys_hbm, sem.at[0], False, _start)

    @pl.when(i + 1 < n)
    def _():
        _chunk_copies(i + 1, nch_ref, lstart_ref, gstart_ref, buf.at[1 - b], ys_hbm, sem.at[1 - b], False, _start)

    _wait_tile(i, tot_ref, buf.at[b], ys_hbm, sem.at[b], False)

    slot = slot_ref[...].astype(F32)
    slot_rows = _to_rows(slot)
    info = gate_ref[...] + pltpu.roll(slot, TOP_K, 1)
    info_hi = info.astype(BF16)
    rest = info - info_hi.astype(F32)
    info_mid = rest.astype(BF16)
    info_lo = (rest - info_mid.astype(F32)).astype(BF16)
    lane = lax.broadcasted_iota(jnp.int32, (LOCAL_STEP, LANES), 1)

    def step(s, _):
        r0 = pl.multiple_of(s * LOCAL_STEP, LOCAL_STEP)
        rid = (lax.broadcasted_iota(jnp.int32, (LOCAL_STEP, tm), 0) + r0).astype(F32)
        hit = rid == slot_rows[0:1, :]
        for k in range(1, TOP_K):
            hit = hit | (rid == slot_rows[k:k + 1, :])
        pick = jnp.where(hit, 1.0, 0.0).astype(BF16)
        pick_scr[pl.ds(r0, LOCAL_STEP), :] = pick
        own = _dot(pick, info_hi) + _dot(pick, info_mid) + _dot(pick, info_lo)
        rcol = (lax.broadcasted_iota(jnp.int32, (LOCAL_STEP, 1), 0) + r0).astype(F32)
        mine = jnp.where((own == rcol) & (lane >= TOP_K) & (lane < 2 * TOP_K), 1.0, 0.0)
        g_col = jnp.sum(own * pltpu.roll(mine, LANES - TOP_K, 1), axis=-1, keepdims=True)
        rows = buf[b, pl.ds(r0, LOCAL_STEP), :]
        scaled_scr[pl.ds(r0, LOCAL_STEP), :] = jnp.where(g_col != 0.0, rows * g_col, 0.0).astype(BF16)
        return 0

    lax.fori_loop(0, LOCAL_ROWS // LOCAL_STEP, step, 0, unroll=3)
    y = _dot_tn(pick_scr[...], scaled_scr[...])
    out = _rms(h_ref[...] + y, g_ref[...])

    @pl.when(i < n_prompt_tiles)
    def _():
        op_ref[...] = out

    @pl.when(i >= n_prompt_tiles)
    def _():
        os_ref[...] = out


def _combine(tables, slot, gates, h2, g_final, ys, tp, ts):
    tm = TOKEN_TILE
    npt, nst = tp // tm, ts // tm
    tok = lambda w: pl.BlockSpec((tm, w), lambda i, *_: (i, 0))
    return pl.pallas_call(
        functools.partial(_combine_kernel, n_prompt_tiles=npt),
        out_shape=[jax.ShapeDtypeStruct((tp, D_MODEL), F32), jax.ShapeDtypeStruct((ts, D_MODEL), F32)],
        grid_spec=pltpu.PrefetchScalarGridSpec(
            num_scalar_prefetch=4,
            grid=(npt + nst,),
            in_specs=[tok(LANES), tok(LANES), tok(D_MODEL),
                      pl.BlockSpec((1, D_MODEL), lambda i, *_: (0, 0)),
                      pl.BlockSpec(memory_space=pl.ANY)],
            out_specs=[pl.BlockSpec((tm, D_MODEL), lambda i, *_: (jnp.minimum(i, npt - 1), 0)),
                       pl.BlockSpec((tm, D_MODEL), lambda i, *_: (jnp.maximum(i - npt, 0), 0))],
            scratch_shapes=[pltpu.VMEM((2, LOCAL_ROWS, D_MODEL), F32), pltpu.VMEM((LOCAL_ROWS, tm), BF16),
                            pltpu.VMEM((LOCAL_ROWS, D_MODEL), BF16), pltpu.SemaphoreType.DMA((2,))]),
        compiler_params=_params(("arbitrary",)),
        name="moe_combine_norm",
    )(*tables, slot, gates, h2, g_final, ys)


def _rope_tables(seq_len):
    pos = jnp.arange(seq_len, dtype=F32)[:, None]
    lane = np.arange(LANES)

    def table(half, lane_freq, first, active):
        inv = ROPE_BASE ** (-jnp.arange(half, dtype=F32) / half)
        ang = pos * inv[None, :]
        cos = jnp.cos(ang)[:, lane_freq]
        sin = jnp.sin(ang)[:, lane_freq]
        c = jnp.where(active[None, :], cos, 1.0)
        s = jnp.where(active[None, :], jnp.where(first[None, :], -sin, sin), 0.0)
        return c, s

    half_r = RET_DK // 2
    cr, sr = table(half_r, lane % half_r, (lane % RET_DK) < half_r, np.ones(LANES, bool))
    half_m = MLA_ROPE // 2
    rel = lane - MLA_NOPE
    active = (rel >= 0) & (rel < MLA_ROPE)
    cm, sm = table(half_m, np.where(active, rel % half_m, 0), active & (rel < half_m), active)
    return cr, sr, cm, sm


def kernel(x_prompt, x_sample, mem_prompt, mem_sample, norm_mix, w_in, ret_decay_fwd, ret_decay_bwd, ret_gn, q_a_norm, w_uq, kv_a_norm, w_ukv, w_mix_out, norm_cross, norm_mem, w_xq, w_xkv, w_xo, norm_ffn, router_w, router_b, w_gu, b_gu, w_down, b_down, norm_final):
    assert norm_mix.shape[0] == 1, "single layer"
    bp, sp, d = x_prompt.shape
    bs, ss, _ = x_sample.shape
    tp, ts = bp * sp, bs * ss
    t = tp + ts
    assert d == D_MODEL and sp % TOKEN_TILE == 0 and ss % TOKEN_TILE == 0 and tp % ss == 0 and sp >= ss
    mem_len = mem_prompt.shape[1]

    w_in0 = w_in[0]
    w_in_pad = jnp.zeros((D_MODEL, D_IN_PAD), F32)
    w_in_pad = w_in_pad.at[:, :2816].set(w_in0[:, :2816])
    w_in_pad = w_in_pad.at[:, 2816 + MLA_NOPE:2816 + MLA_NOPE + MLA_ROPE].set(w_in0[:, 2816:]).astype(BF16)
    wuq_pad = jnp.pad(w_uq[0].reshape(Q_LORA, MLA_HEADS, MLA_NOPE + MLA_ROPE),
                      ((0, 0), (0, 0), (0, LANES - MLA_NOPE - MLA_ROPE))).reshape(Q_LORA, MLA_HEADS * LANES).astype(BF16)
    wukv = w_ukv[0].reshape(KV_LORA, MLA_HEADS, MLA_NOPE + MLA_V)
    wuk_pad = jnp.pad(wukv[:, :, :MLA_NOPE], ((0, 0), (0, 0), (0, LANES - MLA_NOPE))).reshape(KV_LORA, MLA_HEADS * LANES).astype(BF16)
    wuv = wukv[:, :, MLA_NOPE:].reshape(KV_LORA, MLA_WIDTH).astype(BF16)
    rw_pad = jnp.pad(router_w[0], ((0, 0), (0, LANES - N_EXPERTS))).astype(BF16)
    rb_pad = jnp.pad(router_b[0].astype(F32), (0, LANES - N_EXPERTS), constant_values=NEG_BIG)[None, :]
    lgf = jnp.log1p(-jnp.exp2(ret_decay_fwd[0].astype(F32)))
    lgb = jnp.log1p(-jnp.exp2(ret_decay_bwd[0].astype(F32)))
    tabs = _rope_tables(sp)

    xp = x_prompt.reshape(tp, D_MODEL)
    xs = x_sample.reshape(ts, D_MODEL)
    mem = jnp.concatenate([mem_prompt.reshape(-1, D_MODEL), mem_sample.reshape(-1, D_MODEL)], axis=0)

    kvmem = _mem_kv(mem, norm_mem[0][None, :], w_xkv[0].astype(BF16)).reshape(bp + bs, mem_len, 2 * D_MODEL)

    rq, rk, rv, rg, qm, km, vm = _in_proj(xp, xs, norm_mix[0][None, :], w_in_pad, q_a_norm[0][None, :], wuq_pad,
                                          kv_a_norm[0][None, :], wuk_pad, wuv, tabs, sp, ss)

    gn = ret_gn[0][None, :]
    ret_p = _retention(lgf, lgb, rq, rk, rv, rg, gn, sp, bp, 0)
    ret_s = _retention(lgf, lgb, rq, rk, rv, rg, gn, ss, bs, tp // ss)
    mla_p = _mla_attn(qm, km, vm, sp, bp, 0)
    mla_s = _mla_attn(qm, km, vm, ss, bs, tp // ss)

    h2, xn, gates, slot, stats, counts = _mix_cross(
        xp, xs, ret_p, ret_s, mla_p, mla_s, w_mix_out[0].astype(BF16), norm_cross[0][None, :], w_xq[0].astype(BF16),
        kvmem, w_xo[0].astype(BF16), norm_ffn[0][None, :], rw_pad, rb_pad, sp, ss)

    used = counts[0, :N_EXPERTS]
    padded = (used + MOE_BLOCK - 1) // MOE_BLOCK * MOE_BLOCK
    pend = jnp.cumsum(padded)
    pstart = pend - padded
    tile_cnt, tile_before, tile_lstart = (stats[:, r, :N_EXPERTS] for r in range(3))
    n_tiles = stats.shape[0]
    tile_chunks = (tile_cnt + ROW_CHUNK - 1) // ROW_CHUNK
    tables = (tile_chunks.reshape(-1),
              tile_lstart.reshape(-1),
              (pstart[None, :] + tile_before).reshape(-1),
              jnp.sum(tile_chunks, axis=-1))
    ztables = (pstart + used, (padded - used) // ROW_CHUNK)
    n_blocks = -(-(t * TOP_K + N_EXPERTS * (n_tiles * (ROW_CHUNK - 1) + MOE_BLOCK - 1)) // MOE_BLOCK)
    blk = jnp.arange(n_blocks, dtype=jnp.int32)
    n_active = (pend[-1] // MOE_BLOCK).astype(jnp.int32)
    block_i = jnp.minimum(blk, n_active - 1)
    block_e = jnp.minimum(jnp.sum((block_i[:, None] * MOE_BLOCK >= pend[None, :]).astype(jnp.int32), axis=-1),
                          N_EXPERTS - 1).astype(jnp.int32)

    xs_sorted = _dispatch(tables, ztables, xn, slot, n_blocks * MOE_BLOCK)
    ys = _experts(block_e, block_i, n_active[None], xs_sorted, w_gu[0], b_gu[0][:, None, :],
                  w_down[0], b_down[0][:, None, :])
    out_p, out_s = _combine(tables, slot, gates, h2, norm_final[None, :], ys, tp, ts)
    return out_p.reshape(bp, sp, D_MODEL), out_s.reshape(bs, ss, D_MODEL)
```

```python
import functools
import math

import jax
import jax.numpy as jnp
import numpy as np
from jax import lax
from jax.experimental import pallas as pl
from jax.experimental.pallas import tpu as pltpu

D_MODEL = 1024
RET_HEADS = 8
RET_DK = 64
RET_WIDTH = 512
CHUNK = 128
MLA_HEADS = 8
MLA_NOPE = 64
MLA_ROPE = 32
MLA_V = 64
MLA_WIDTH = 512
Q_LORA = 512
KV_LORA = 256
ROPE_BASE = 10000.0
X_HEADS = 4
X_HEAD_DIM = 256
N_EXPERTS = 32
TOP_K = 4
SWIGLU_LIMIT = 7.0
SWIGLU_ALPHA = 1.702
MOE_BLOCK = 512
EPS = 1e-6

LANES = 128
VMEM_LIMIT = 56 * 1024 * 1024

TOKEN_TILE = 512
ROW_PARTS = 2
MIX_ROW_PARTS = 1
ATTN_TQ = 1024
ATTN_TK = 1024
ATTN_UNROLL = 4
RET_UNROLL = 8
ROW_CHUNK = 8
GROUP_BITS = (TOKEN_TILE // ROW_CHUNK).bit_length()
WAIT_CHUNKS = 16
LOCAL_STEP = 256
LOCAL_ROWS = -(-(TOKEN_TILE * 4 + 32 * (ROW_CHUNK - 1)) // LOCAL_STEP) * LOCAL_STEP
D_IN_PAD = 4 * 512 + Q_LORA + KV_LORA + LANES
NEG_BIG = -1e30

F32 = jnp.float32
BF16 = jnp.bfloat16


def _params(sem, vmem=VMEM_LIMIT):
    return pltpu.CompilerParams(dimension_semantics=sem, vmem_limit_bytes=vmem)


def _const_spec(shape):
    nd = len(shape)
    return pl.BlockSpec(shape, lambda *_: (0,) * nd, pipeline_mode=pl.Buffered(1))


def _rms(x, g):
    ms = jnp.mean(x * x, axis=-1, keepdims=True)
    return x * lax.rsqrt(ms + EPS) * g


def _dot(a, b):
    return jnp.dot(a, b, preferred_element_type=F32)


def _dot_nt(a, b):
    return lax.dot_general(a, b, (((1,), (1,)), ((), ())), preferred_element_type=F32)


def _dot_tn(a, b):
    return lax.dot_general(a, b, (((0,), (0,)), ((), ())), preferred_element_type=F32)


def _mem_kv_kernel(mem_ref, g_ref, w_ref, o_ref):
    mn = _rms(mem_ref[...], g_ref[...])
    o_ref[...] = _dot(mn.astype(BF16), w_ref[...]).astype(BF16)


def _mem_kv(mem, g, w):
    rows, mem_len = mem.shape[0], 256
    return pl.pallas_call(
        _mem_kv_kernel,
        out_shape=jax.ShapeDtypeStruct((rows, 2 * D_MODEL), BF16),
        grid=(rows // mem_len,),
        in_specs=[pl.BlockSpec((mem_len, D_MODEL), lambda i: (i, 0)),
                  _const_spec((1, D_MODEL)),
                  _const_spec((D_MODEL, 2 * D_MODEL))],
        out_specs=pl.BlockSpec((mem_len, 2 * D_MODEL), lambda i: (i, 0)),
        compiler_params=_params(("arbitrary",)),
        name="mem_kv",
    )(mem, g, w)


def _rope_slab(x, c, ss, first, shift_up, shift_down):
    swap = jnp.where(first, pltpu.roll(x, shift_up, 1), pltpu.roll(x, shift_down, 1))
    return x * c + swap * ss


def _in_proj_kernel(xp_ref, xs_ref, g_ref, win_ref, qan_ref, wuq_ref, kvan_ref, wuk_ref, wuv_ref,
                    cr_ref, sr_ref, cm_ref, sm_ref,
                    rq_ref, rk_ref, rv_ref, rg_ref, qm_ref, km_ref, vm_ref, *, n_prompt_tiles):
    i = pl.program_id(0)
    tm = rq_ref.shape[0]
    lane = lax.broadcasted_iota(jnp.int32, (1, LANES), 1)
    ret_first = (lane % RET_DK) < (RET_DK // 2)
    mla_first = (lane >= MLA_NOPE) & (lane < MLA_NOPE + MLA_ROPE // 2)
    half_r, half_m = RET_DK // 2, MLA_ROPE // 2
    q_scale = (MLA_NOPE + MLA_ROPE) ** -0.5 * math.log2(math.e)

    for part in range(ROW_PARTS):
        rows = slice(part * (tm // ROW_PARTS), (part + 1) * (tm // ROW_PARTS))
        x = jnp.where(i < n_prompt_tiles, xp_ref[rows, :], xs_ref[rows, :])
        xn = _rms(x, g_ref[...])
        proj = _dot(xn.astype(BF16), win_ref[...])
        cr, sr, cm, sm = cr_ref[rows, :], sr_ref[rows, :], cm_ref[rows, :], sm_ref[rows, :]

        for s in range(RET_WIDTH // LANES):
            lo = s * LANES
            q = _rope_slab(proj[:, lo:lo + LANES], cr, sr, ret_first, LANES - half_r, half_r)
            rq_ref[rows, lo:lo + LANES] = q.astype(BF16)
            k = _rope_slab(proj[:, 512 + lo:512 + lo + LANES], cr, sr, ret_first, LANES - half_r, half_r)
            rk_ref[rows, lo:lo + LANES] = (k * (RET_DK ** -0.5)).astype(BF16)
        rv_ref[rows, :] = proj[:, 1024:1536].astype(BF16)
        rg_ref[rows, :] = proj[:, 1536:2048].astype(BF16)

        cq = _rms(proj[:, 2048:2048 + Q_LORA], qan_ref[...])
        qm = _dot(cq.astype(BF16), wuq_ref[...])
        ckv = _rms(proj[:, 2560:2560 + KV_LORA], kvan_ref[...]).astype(BF16)
        kn = _dot(ckv, wuk_ref[...])
        vm_ref[rows, :] = _dot(ckv, wuv_ref[...]).astype(BF16)
        kr = proj[:, 2816:2816 + LANES]
        kpe = _rope_slab(kr, cm, sm, mla_first, LANES - half_m, half_m)
        for h in range(MLA_HEADS):
            lo = h * LANES
            qh = _rope_slab(qm[:, lo:lo + LANES], cm, sm, mla_first, LANES - half_m, half_m)
            qm_ref[rows, lo:lo + LANES] = (qh * q_scale).astype(BF16)
            km_ref[rows, lo:lo + LANES] = (kn[:, lo:lo + LANES] + kpe).astype(BF16)


def _in_proj(xp, xs, g, w_in_pad, qan, wuq_pad, kvan, wuk_pad, wuv, tabs, sp, ss):
    tm = TOKEN_TILE
    tp, ts = xp.shape[0], xs.shape[0]
    npt, nst = tp // tm, ts // tm
    t = tp + ts
    tiles_p, tiles_s = sp // tm, ss // tm

    def xp_map(i):
        return (jnp.minimum(i, npt - 1), 0)

    def xs_map(i):
        return (jnp.maximum(i - npt, 0), 0)

    def tab_map(i):
        return (jnp.where(i < npt, i % tiles_p, (i - npt) % tiles_s), 0)

    tok = lambda w: pl.BlockSpec((tm, w), lambda i: (i, 0))
    widths = (512, 512, 512, 512, 1024, 1024, 512)
    return pl.pallas_call(
        functools.partial(_in_proj_kernel, n_prompt_tiles=npt),
        out_shape=[jax.ShapeDtypeStruct((t, w), BF16) for w in widths],
        grid=(npt + nst,),
        in_specs=[pl.BlockSpec((tm, D_MODEL), xp_map), pl.BlockSpec((tm, D_MODEL), xs_map),
                  _const_spec((1, D_MODEL)), _const_spec((D_MODEL, D_IN_PAD)),
                  _const_spec((1, Q_LORA)), _const_spec((Q_LORA, MLA_HEADS * LANES)),
                  _const_spec((1, KV_LORA)), _const_spec((KV_LORA, MLA_HEADS * LANES)),
                  _const_spec((KV_LORA, MLA_WIDTH))]
                 + [pl.BlockSpec((tm, LANES), tab_map)] * 4,
        out_specs=[tok(w) for w in widths],
        compiler_params=_params(("arbitrary",)),
        name="in_proj",
    )(xp, xs, g, w_in_pad, qan, wuq_pad, kvan, wuk_pad, wuv, *tabs)


def _retention_kernel(lgf_ref, lgb_ref, q_ref, k_ref, v_ref, g_ref, gn_ref, o_ref, kvf_scr, kvb_scr, st_scr, *, n_chunks):
    hp = pl.program_id(1)
    c = CHUNK
    lane = lax.broadcasted_iota(jnp.int32, (1, c), 1)
    row = lax.broadcasted_iota(jnp.int32, (c, 1), 0)
    lane_h0 = lane < RET_DK
    row_h0 = row < RET_DK
    lgf0, lgf1 = lgf_ref[2 * hp], lgf_ref[2 * hp + 1]
    lgb0, lgb1 = lgb_ref[2 * hp], lgb_ref[2 * hp + 1]
    lgf_lane = jnp.where(lane_h0, lgf0, lgf1)
    lgb_lane = jnp.where(lane_h0, lgb0, lgb1)
    t = row.astype(F32)
    q_dec_f = jnp.exp((t + 1.0) * lgf_lane)
    q_dec_b = jnp.exp((c - t) * lgb_lane)
    k_dec_f = jnp.exp((c - 1.0 - t) * lgf_lane)
    k_dec_b = jnp.exp(t * lgb_lane)
    chunk_dec_f = jnp.exp(c * jnp.where(row_h0, lgf0, lgf1))
    chunk_dec_b = jnp.exp(c * jnp.where(row_h0, lgb0, lgb1))
    same_head = row_h0 == lane_h0
    diff = t - lane.astype(F32)
    d_intra = []
    for lgf, lgb in ((lgf0, lgb0), (lgf1, lgb1)):
        fwd = jnp.where(diff >= 0, jnp.exp(jnp.where(diff >= 0, diff, 0.0) * lgf), 0.0)
        bwd = jnp.where(diff < 0, jnp.exp(jnp.where(diff < 0, -diff, 0.0) * lgb), 0.0)
        d_intra.append(fwd + bwd)
    avg = jnp.where(same_head, 1.0 / RET_DK, 0.0).astype(BF16)
    gn = gn_ref[...]

    def chunk(n):
        return pl.ds(pl.multiple_of(n * c, c), c)

    unroll = min(RET_UNROLL, n_chunks)

    def chunk_kv(n, _):
        k = k_ref[chunk(n), :].astype(F32)
        kd = jnp.concatenate([(k * k_dec_f).astype(BF16), (k * k_dec_b).astype(BF16)], axis=-1)
        kv = _dot_tn(kd, v_ref[chunk(n), :])
        kvf_scr[n] = jnp.where(same_head, kv[0:c], 0.0)
        kvb_scr[n] = jnp.where(same_head, kv[c:2 * c], 0.0)
        return 0

    lax.fori_loop(0, n_chunks, chunk_kv, 0, unroll=unroll)

    def fwd_state(n, s_f):
        st_scr[n, 0:c, :] = s_f.astype(BF16)
        return chunk_dec_f * s_f + kvf_scr[n]

    lax.fori_loop(0, n_chunks, fwd_state, jnp.zeros((c, c), F32))

    def bwd_state(j, s_b):
        n = n_chunks - 1 - j
        st_scr[n, c:2 * c, :] = s_b.astype(BF16)
        return chunk_dec_b * s_b + kvb_scr[n]

    lax.fori_loop(0, n_chunks, bwd_state, jnp.zeros((c, c), F32))

    d_both = jnp.concatenate(d_intra, axis=0)
    zero_q = jnp.zeros((c, c), BF16)

    def chunk_out(n):
        q = q_ref[chunk(n), :]
        qf = q.astype(F32)
        q_both = jnp.concatenate([(qf * q_dec_f).astype(BF16), (qf * q_dec_b).astype(BF16)], axis=-1)
        q_heads = jnp.concatenate([jnp.where(lane_h0, q, zero_q), jnp.where(lane_h0, zero_q, q)], axis=0)
        p = (_dot_nt(q_heads, k_ref[chunk(n), :]) * d_both).astype(BF16)
        inner = _dot(p, v_ref[chunk(n), :])
        return _dot(q_both, st_scr[n]) + jnp.where(lane_h0, inner[0:c], inner[c:2 * c])

    def split_rows(a):
        hi = a.astype(BF16)
        return jnp.concatenate([hi, (a - hi.astype(F32)).astype(BF16)], axis=0)

    def out_group(gi, _):
        rows = pl.ds(pl.multiple_of(gi * (unroll * c), unroll * c), unroll * c)
        y = jnp.concatenate([chunk_out(gi * unroll + u) for u in range(unroll)], axis=0)
        m = y.shape[0]
        mu2 = _dot(split_rows(y), avg)
        d = y - (mu2[0:m] + mu2[m:2 * m])
        var2 = _dot(split_rows(d * d), avg)
        yn = d * lax.rsqrt(var2[0:m] + var2[m:2 * m] + EPS) * gn
        gate = g_ref[rows, :].astype(F32)
        o_ref[rows, :] = (yn * (gate * jax.nn.sigmoid(gate))).astype(BF16)
        return 0

    lax.fori_loop(0, n_chunks // unroll, out_group, 0)


def _retention(lgf, lgb, rq, rk, rv, rg, gn, seq_len, n_seq, row_block0):
    n_chunks = seq_len // CHUNK
    hp_count = RET_WIDTH // LANES
    blk = pl.BlockSpec((seq_len, LANES), lambda b, hp, *_: (row_block0 + b, hp))
    return pl.pallas_call(
        functools.partial(_retention_kernel, n_chunks=n_chunks),
        out_shape=jax.ShapeDtypeStruct((n_seq * seq_len, RET_WIDTH), BF16),
        grid_spec=pltpu.PrefetchScalarGridSpec(
            num_scalar_prefetch=2,
            grid=(n_seq, hp_count),
            in_specs=[blk, blk, blk, blk, pl.BlockSpec((1, LANES), lambda b, hp, *_: (0, hp))],
            out_specs=pl.BlockSpec((seq_len, LANES), lambda b, hp, *_: (b, hp)),
            scratch_shapes=[pltpu.VMEM((n_chunks, CHUNK, CHUNK), F32), pltpu.VMEM((n_chunks, CHUNK, CHUNK), F32),
                            pltpu.VMEM((n_chunks, 2 * CHUNK, CHUNK), BF16)]),
        compiler_params=_params(("arbitrary", "arbitrary")),
        name="retention",
    )(lgf, lgb, rq, rk, rv, rg, gn)


def _mla_attn_kernel(q_ref, k_ref, v_ref, o_ref, *, tk, n_kv):
    tq = q_ref.shape[0]
    lane = lax.broadcasted_iota(jnp.int32, (1, LANES), 1)
    qs = (q_ref[:, :LANES], q_ref[:, LANES:])

    def body(j, carry):
        rows = pl.ds(pl.multiple_of(j * tk, tk), tk)
        v = v_ref[rows, :]
        new = []
        for h in range(2):
            m, l, acc = carry[h]
            s = _dot_nt(qs[h], k_ref[rows, h * LANES:(h + 1) * LANES])
            m_new = jnp.maximum(m, jnp.max(s, axis=-1, keepdims=True))
            alpha = jnp.exp2(m - m_new)
            p = jnp.exp2(s - m_new)
            l = alpha * l + jnp.sum(p, axis=-1, keepdims=True)
            acc = alpha * acc + _dot(p.astype(BF16), v)
            new.append((m_new, l, acc))
        return tuple(new)

    init = (jnp.full((tq, 1), NEG_BIG, F32), jnp.zeros((tq, 1), F32), jnp.zeros((tq, LANES), F32))
    (_, l0, acc0), (_, l1, acc1) = lax.fori_loop(0, n_kv, body, (init, init), unroll=min(ATTN_UNROLL, n_kv))
    o_ref[...] = jnp.where(lane < MLA_V, acc0 / l0, acc1 / l1).astype(BF16)


def _mla_attn(qm, km, vm, seq_len, n_seq, row_block0):
    tq, tk = min(ATTN_TQ, seq_len), min(ATTN_TK, seq_len)
    hp_count = MLA_HEADS // 2
    nq = seq_len // tq
    return pl.pallas_call(
        functools.partial(_mla_attn_kernel, tk=tk, n_kv=seq_len // tk),
        out_shape=jax.ShapeDtypeStruct((n_seq * seq_len, MLA_WIDTH), BF16),
        grid=(n_seq, hp_count, nq),
        in_specs=[pl.BlockSpec((tq, 2 * LANES), lambda b, hp, i: ((row_block0 + b) * nq + i, hp)),
                  pl.BlockSpec((seq_len, 2 * LANES), lambda b, hp, i: (row_block0 + b, hp)),
                  pl.BlockSpec((seq_len, LANES), lambda b, hp, i: (row_block0 + b, hp))],
        out_specs=pl.BlockSpec((tq, LANES), lambda b, hp, i: (b * nq + i, hp)),
        compiler_params=_params(("arbitrary", "arbitrary", "arbitrary")),
        name="mla_attn",
    )(qm, km, vm)


def _mix_cross_kernel(xp_ref, xs_ref, rp_ref, rs_ref, mp_ref, ms_ref, wmix_ref, gx_ref, wxq_ref, kv_ref, wxo_ref,
                      gf_ref, rw_ref, rb_ref,
                      h_ref, xn_ref, gate_ref, slot_ref, stat_ref, cnt_ref, carry_scr, *, n_prompt_tiles):
    i = pl.program_id(0)
    tm = h_ref.shape[0]
    is_p = i < n_prompt_tiles
    lane = lax.broadcasted_iota(jnp.int32, (tm, LANES), 1)

    def part_rows(rows):
        x = jnp.where(is_p, xp_ref[rows, :], xs_ref[rows, :])
        ret = jnp.where(is_p, rp_ref[rows, :], rs_ref[rows, :])
        mla = jnp.where(is_p, mp_ref[rows, :], ms_ref[rows, :])
        h1 = x + _dot(ret, wmix_ref[0:RET_WIDTH, :]) + _dot(mla, wmix_ref[RET_WIDTH:, :])

        hn = _rms(h1, gx_ref[...]).astype(BF16)
        q = (_dot(hn, wxq_ref[...]) * (X_HEAD_DIM ** -0.5)).astype(BF16)
        heads = []
        for h in range(X_HEADS):
            lo = h * X_HEAD_DIM
            s = _dot_nt(q[:, lo:lo + X_HEAD_DIM], kv_ref[0, :, lo:lo + X_HEAD_DIM])
            e = jnp.exp(s - jnp.max(s, axis=-1, keepdims=True))
            p = (e / jnp.sum(e, axis=-1, keepdims=True)).astype(BF16)
            heads.append(_dot(p, kv_ref[0, :, D_MODEL + lo:D_MODEL + lo + X_HEAD_DIM]).astype(BF16))
        h2 = h1 + _dot(jnp.concatenate(heads, axis=-1), wxo_ref[...])
        h_ref[rows, :] = h2

        xn = _rms(h2, gf_ref[...]).astype(BF16)
        xn_ref[rows, :] = xn
        work = _dot(xn, rw_ref[...]) + rb_ref[...]
        lane_f = lax.broadcasted_iota(jnp.int32, work.shape, 1).astype(F32)
        vals, picks = [], []
        for _ in range(TOP_K):
            m = jnp.max(work, axis=-1, keepdims=True)
            idx = jnp.min(jnp.where(work == m, lane_f, float(LANES)), axis=-1, keepdims=True)
            sel = lane_f == idx
            work = jnp.where(sel, -jnp.inf, work)
            vals.append(m)
            picks.append(jnp.where(sel, 1.0, 0.0))
        exps = [jnp.exp(v - vals[0]) for v in vals]
        denom = exps[0] + exps[1] + exps[2] + exps[3]
        return picks, [e / denom for e in exps]

    part = tm // MIX_ROW_PARTS
    parts = [part_rows(slice(r * part, (r + 1) * part)) for r in range(MIX_ROW_PARTS)]
    picks = [jnp.concatenate([p[0][k] for p in parts], axis=0) for k in range(TOP_K)]
    gates = [jnp.concatenate([p[1][k] for p in parts], axis=0) for k in range(TOP_K)]
    onehot = picks[0] + picks[1] + picks[2] + picks[3]


    @pl.when(i == 0)
    def _():
        carry_scr[...] = jnp.zeros_like(carry_scr)

    r_iota = lax.broadcasted_iota(jnp.int32, (tm, tm), 0)
    c_iota = lax.broadcasted_iota(jnp.int32, (tm, tm), 1)
    lower = jnp.where(c_iota < r_iota, 1.0, 0.0).astype(BF16)
    before = _dot(lower, onehot.astype(BF16))
    tile_cnt = jnp.sum(onehot, axis=0, keepdims=True)
    group_rows = jnp.floor((tile_cnt + (ROW_CHUNK - 1)) * (1.0 / ROW_CHUNK)) * ROW_CHUNK
    carry_before = carry_scr[...]
    carry = carry_before + group_rows
    carry_scr[...] = carry
    cnt_ref[...] = carry.astype(jnp.int32)
    e_row = lax.broadcasted_iota(jnp.int32, (LANES, LANES), 0)
    e_col = lax.broadcasted_iota(jnp.int32, (LANES, LANES), 1)
    earlier = jnp.where(e_row < e_col, 1.0, 0.0).astype(BF16)
    group_start = _dot(jnp.broadcast_to(group_rows, (8, LANES)).astype(BF16), earlier)[0:1, :]
    row8 = lax.broadcasted_iota(jnp.int32, (8, LANES), 0)
    stats = jnp.where(row8 == 0, tile_cnt, jnp.where(row8 == 1, carry_before, jnp.where(row8 == 2, group_start, 0.0)))
    stat_ref[0] = stats.astype(jnp.int32)

    place = before + group_start
    gate_out = jnp.zeros((tm, LANES), F32)
    slot_out = jnp.zeros((tm, LANES), F32)
    for k in range(TOP_K):
        slot_k = jnp.sum(picks[k] * place, axis=-1, keepdims=True)
        gate_out = jnp.where(lane == k, gates[k], gate_out)
        slot_out = jnp.where(lane == k, slot_k, slot_out)
    gate_ref[...] = gate_out
    slot_ref[...] = slot_out.astype(jnp.int32)


def _mix_cross(xp, xs, ret_p, ret_s, mla_p, mla_s, wmix, gx, wxq, kvmem, wxo, gf, rw_pad, rb_pad, sp, ss):
    tm = TOKEN_TILE
    tp, ts = xp.shape[0], xs.shape[0]
    npt, nst = tp // tm, ts // tm
    t = tp + ts
    n_seq_p = tp // sp
    mem_len = kvmem.shape[1]

    def p_map(i):
        return (jnp.minimum(i, npt - 1), 0)

    def s_map(i):
        return (jnp.maximum(i - npt, 0), 0)

    def kv_map(i):
        return (jnp.where(i < npt, i // (sp // tm), n_seq_p + (i - npt) // (ss // tm)), 0, 0)

    tok = lambda w: pl.BlockSpec((tm, w), lambda i: (i, 0))
    return pl.pallas_call(
        functools.partial(_mix_cross_kernel, n_prompt_tiles=npt),
        out_shape=[jax.ShapeDtypeStruct((t, D_MODEL), F32), jax.ShapeDtypeStruct((t, D_MODEL), BF16),
                   jax.ShapeDtypeStruct((t, LANES), F32), jax.ShapeDtypeStruct((t, LANES), jnp.int32),
                   jax.ShapeDtypeStruct((npt + nst, 8, LANES), jnp.int32), jax.ShapeDtypeStruct((1, LANES), jnp.int32)],
        grid=(npt + nst,),
        in_specs=[pl.BlockSpec((tm, D_MODEL), p_map), pl.BlockSpec((tm, D_MODEL), s_map),
                  pl.BlockSpec((tm, RET_WIDTH), p_map), pl.BlockSpec((tm, RET_WIDTH), s_map),
                  pl.BlockSpec((tm, MLA_WIDTH), p_map), pl.BlockSpec((tm, MLA_WIDTH), s_map),
                  _const_spec((D_MODEL, D_MODEL)), _const_spec((1, D_MODEL)), _const_spec((D_MODEL, D_MODEL)),
                  pl.BlockSpec((1, mem_len, 2 * D_MODEL), kv_map),
                  _const_spec((D_MODEL, D_MODEL)), _const_spec((1, D_MODEL)),
                  _const_spec((D_MODEL, LANES)), _const_spec((1, LANES))],
        out_specs=[tok(D_MODEL), tok(D_MODEL), tok(LANES), tok(LANES),
                   pl.BlockSpec((1, 8, LANES), lambda i: (i, 0, 0)),
                   pl.BlockSpec((1, LANES), lambda i: (0, 0))],
        scratch_shapes=[pltpu.VMEM((1, LANES), F32)],
        compiler_params=_params(("arbitrary",)),
        name="mix_cross_router",
    )(xp, xs, ret_p, ret_s, mla_p, mla_s, wmix, gx, wxq, kvmem, wxo, gf, rw_pad, rb_pad)


def _to_rows(a):
    eye = (lax.broadcasted_iota(jnp.int32, (LANES, LANES), 0)
           == lax.broadcasted_iota(jnp.int32, (LANES, LANES), 1)).astype(BF16)
    hi = a.astype(BF16)
    r1 = a - hi.astype(F32)
    mid = r1.astype(BF16)
    lo = (r1 - mid.astype(F32)).astype(BF16)
    return _dot_nt(eye, hi) + _dot_nt(eye, mid) + _dot_nt(eye, lo)


def _chunk_copies(tile, nch_ref, lstart_ref, gstart_ref, local, hbm, sem, to_hbm, act):
    def per_expert(e, _):
        base = tile * N_EXPERTS + e
        n = nch_ref[base]
        ls = lstart_ref[base]
        gs = gstart_ref[base]
        for j in range(GROUP_BITS):
            rows = ROW_CHUNK << j

            @pl.when(((n >> j) & 1) == 1)
            def _(j=j, rows=rows):
                off = (n & ((1 << j) - 1)) * ROW_CHUNK
                l_rows = local.at[pl.ds(pl.multiple_of(ls + off, ROW_CHUNK), rows)]
                g_rows = hbm.at[pl.ds(pl.multiple_of(gs + off, ROW_CHUNK), rows)]
                act(pltpu.make_async_copy(l_rows, g_rows, sem) if to_hbm else pltpu.make_async_copy(g_rows, l_rows, sem))

        return 0

    lax.fori_loop(0, N_EXPERTS, per_expert, 0)


def _wait_tile(tile, tot_ref, local, hbm, sem, to_hbm):
    total = tot_ref[tile]

    def wait_rows(rows):
        l_rows = local.at[pl.ds(0, rows)]
        g_rows = hbm.at[pl.ds(0, rows)]
        (pltpu.make_async_copy(l_rows, g_rows, sem) if to_hbm else pltpu.make_async_copy(g_rows, l_rows, sem)).wait()

    def big(_, c):
        wait_rows(ROW_CHUNK * WAIT_CHUNKS)
        return c

    def small(_, c):
        wait_rows(ROW_CHUNK)
        return c

    lax.fori_loop(0, total // WAIT_CHUNKS, big, 0)
    lax.fori_loop(0, total % WAIT_CHUNKS, small, 0)


def _start(cp):
    cp.start()


def _wait(cp):
    cp.wait()


def _dispatch_kernel(nch_ref, lstart_ref, gstart_ref, tot_ref, zstart_ref, znch_ref,
                     x_ref, slot_ref, xs_hbm, buf, zero_scr, sem, zsem):
    i = pl.program_id(0)
    last = pl.num_programs(0) - 1
    tm = x_ref.shape[0]
    b = i % 2
    slot_rows = _to_rows(slot_ref[...].astype(F32))
    x = x_ref[...]

    def build(s, _):
        r0 = pl.multiple_of(s * LOCAL_STEP, LOCAL_STEP)
        rid = (lax.broadcasted_iota(jnp.int32, (LOCAL_STEP, tm), 0) + r0).astype(F32)
        hit = rid == slot_rows[0:1, :]
        for k in range(1, TOP_K):
            hit = hit | (rid == slot_rows[k:k + 1, :])
        buf[b, pl.ds(r0, LOCAL_STEP), :] = _dot(jnp.where(hit, 1.0, 0.0).astype(BF16), x)
        return 0

    lax.fori_loop(0, LOCAL_ROWS // LOCAL_STEP, build, 0, unroll=3)

    _chunk_copies(i, nch_ref, lstart_ref, gstart_ref, buf.at[b], xs_hbm, sem.at[b], True, _start)

    @pl.when(i > 0)
    def _():
        _wait_tile(i - 1, tot_ref, buf.at[1 - b], xs_hbm, sem.at[1 - b], True)

    @pl.when(i == last)
    def _():
        _wait_tile(i, tot_ref, buf.at[b], xs_hbm, sem.at[b], True)
        zero_scr[...] = jnp.zeros_like(zero_scr)

        def fill(act):
            def per_expert(e, _):
                def per_chunk(c, _):
                    row = pl.multiple_of(zstart_ref[e] + c * ROW_CHUNK, ROW_CHUNK)
                    act(pltpu.make_async_copy(zero_scr, xs_hbm.at[pl.ds(row, ROW_CHUNK)], zsem))
                    return 0

                lax.fori_loop(0, znch_ref[e], per_chunk, 0)
                return 0

            lax.fori_loop(0, N_EXPERTS, per_expert, 0)

        fill(_start)
        fill(_wait)


def _dispatch(tables, ztables, xn, slot, n_rows):
    tm = TOKEN_TILE
    t = xn.shape[0]
    return pl.pallas_call(
        _dispatch_kernel,
        out_shape=jax.ShapeDtypeStruct((n_rows, D_MODEL), F32),
        grid_spec=pltpu.PrefetchScalarGridSpec(
            num_scalar_prefetch=6,
            grid=(t // tm,),
            in_specs=[pl.BlockSpec((tm, D_MODEL), lambda i, *_: (i, 0)),
                      pl.BlockSpec((tm, LANES), lambda i, *_: (i, 0))],
            out_specs=pl.BlockSpec(memory_space=pl.ANY),
            scratch_shapes=[pltpu.VMEM((2, LOCAL_ROWS, D_MODEL), F32), pltpu.VMEM((ROW_CHUNK, D_MODEL), F32),
                            pltpu.SemaphoreType.DMA((2,)), pltpu.SemaphoreType.DMA(())]),
        compiler_params=_params(("arbitrary",)),
        name="moe_dispatch",
    )(*tables, *ztables, xn, slot)


def _experts_kernel(be_ref, bi_ref, na_ref, x_ref, wgu_ref, bgu_ref, wd_ref, bd_ref, o_ref, wgu_scr, wd_scr):
    j = pl.program_id(0)
    active = j < na_ref[0]

    @pl.when(active & ((j == 0) | (be_ref[j] != be_ref[jnp.maximum(j - 1, 0)])))
    def _():
        wgu_scr[...] = wgu_ref[0].astype(BF16)
        wd_scr[...] = wd_ref[0].astype(BF16)

    @pl.when(active)
    def _():
        x = x_ref[...].astype(BF16)
        gu = _dot(x, wgu_scr[...]) + bgu_ref[0]
        gate = jnp.minimum(gu[:, :D_MODEL], SWIGLU_LIMIT)
        up = jnp.clip(gu[:, D_MODEL:], -SWIGLU_LIMIT, SWIGLU_LIMIT)
        hid = (up + 1.0) * (gate * jax.nn.sigmoid(SWIGLU_ALPHA * gate))
        o_ref[...] = _dot(hid.astype(BF16), wd_scr[...]) + bd_ref[0]


def _experts(block_e, block_i, n_active, xs, wgu, bgu, wd, bd):
    n_blocks = xs.shape[0] // MOE_BLOCK
    row_map = lambda j, be, bi, na: (bi[j], 0)
    e_map = lambda j, be, bi, na: (be[j], 0, 0)
    return pl.pallas_call(
        _experts_kernel,
        out_shape=jax.ShapeDtypeStruct(xs.shape, F32),
        grid_spec=pltpu.PrefetchScalarGridSpec(
            num_scalar_prefetch=3,
            grid=(n_blocks,),
            in_specs=[pl.BlockSpec((MOE_BLOCK, D_MODEL), row_map),
                      pl.BlockSpec((1, D_MODEL, 2 * D_MODEL), e_map),
                      pl.BlockSpec((1, 1, 2 * D_MODEL), e_map),
                      pl.BlockSpec((1, D_MODEL, D_MODEL), e_map),
                      pl.BlockSpec((1, 1, D_MODEL), e_map)],
            out_specs=pl.BlockSpec((MOE_BLOCK, D_MODEL), row_map),
            scratch_shapes=[pltpu.VMEM((D_MODEL, 2 * D_MODEL), BF16), pltpu.VMEM((D_MODEL, D_MODEL), BF16)]),
        compiler_params=_params(("arbitrary",)),
        name="moe_experts",
    )(block_e, block_i, n_active, xs, wgu, bgu, wd, bd)


def _combine_kernel(nch_ref, lstart_ref, gstart_ref, tot_ref, slot_ref, gate_ref, h_ref, g_ref, ys_hbm, op_ref, os_ref,
                    buf, pick_scr, scaled_scr, sem, *, n_prompt_tiles):
    i = pl.program_id(0)
    n = pl.num_programs(0)
    tm = h_ref.shape[0]
    b = i % 2

    @pl.when(i == 0)
    def _():
        buf[...] = jnp.zeros_like(buf)
        _chunk_copies(0, nch_ref, lstart_ref, gstart_ref, buf.at[0], ys_hbm, sem.at[0], False, _start)

    @pl.when(i + 1 < n)
    def _():
        _chunk_copies(i + 1, nch_ref, lstart_ref, gstart_ref, buf.at[1 - b], ys_hbm, sem.at[1 - b], False, _start)

    _wait_tile(i, tot_ref, buf.at[b], ys_hbm, sem.at[b], False)

    slot = slot_ref[...].astype(F32)
    slot_rows = _to_rows(slot)
    info = gate_ref[...] + pltpu.roll(slot, TOP_K, 1)
    info_hi = info.astype(BF16)
    rest = info - info_hi.astype(F32)
    info_mid = rest.astype(BF16)
    info_lo = (rest - info_mid.astype(F32)).astype(BF16)
    lane = lax.broadcasted_iota(jnp.int32, (LOCAL_STEP, LANES), 1)

    def step(s, _):
        r0 = pl.multiple_of(s * LOCAL_STEP, LOCAL_STEP)
        rid = (lax.broadcasted_iota(jnp.int32, (LOCAL_STEP, tm), 0) + r0).astype(F32)
        hit = rid == slot_rows[0:1, :]
        for k in range(1, TOP_K):
            hit = hit | (rid == slot_rows[k:k + 1, :])
        pick = jnp.where(hit, 1.0, 0.0).astype(BF16)
        pick_scr[pl.ds(r0, LOCAL_STEP), :] = pick
        own = _dot(pick, info_hi) + _dot(pick, info_mid) + _dot(pick, info_lo)
        rcol = (lax.broadcasted_iota(jnp.int32, (LOCAL_STEP, 1), 0) + r0).astype(F32)
        mine = jnp.where((own == rcol) & (lane >= TOP_K) & (lane < 2 * TOP_K), 1.0, 0.0)
        g_col = jnp.sum(own * pltpu.roll(mine, LANES - TOP_K, 1), axis=-1, keepdims=True)
        rows = buf[b, pl.ds(r0, LOCAL_STEP), :]
        scaled_scr[pl.ds(r0, LOCAL_STEP), :] = jnp.where(g_col != 0.0, rows * g_col, 0.0).astype(BF16)
        return 0

    lax.fori_loop(0, LOCAL_ROWS // LOCAL_STEP, step, 0, unroll=3)
    y = _dot_tn(pick_scr[...], scaled_scr[...])
    out = _rms(h_ref[...] + y, g_ref[...])

    @pl.when(i < n_prompt_tiles)
    def _():
        op_ref[...] = out

    @pl.when(i >= n_prompt_tiles)
    def _():
        os_ref[...] = out


def _combine(tables, slot, gates, h2, g_final, ys, tp, ts):
    tm = TOKEN_TILE
    npt, nst = tp // tm, ts // tm
    tok = lambda w: pl.BlockSpec((tm, w), lambda i, *_: (i, 0))
    return pl.pallas_call(
        functools.partial(_combine_kernel, n_prompt_tiles=npt),
        out_shape=[jax.ShapeDtypeStruct((tp, D_MODEL), F32), jax.ShapeDtypeStruct((ts, D_MODEL), F32)],
        grid_spec=pltpu.PrefetchScalarGridSpec(
            num_scalar_prefetch=4,
            grid=(npt + nst,),
            in_specs=[tok(LANES), tok(LANES), tok(D_MODEL),
                      pl.BlockSpec((1, D_MODEL), lambda i, *_: (0, 0)),
                      pl.BlockSpec(memory_space=pl.ANY)],
            out_specs=[pl.BlockSpec((tm, D_MODEL), lambda i, *_: (jnp.minimum(i, npt - 1), 0)),
                       pl.BlockSpec((tm, D_MODEL), lambda i, *_: (jnp.maximum(i - npt, 0), 0))],
            scratch_shapes=[pltpu.VMEM((2, LOCAL_ROWS, D_MODEL), F32), pltpu.VMEM((LOCAL_ROWS, tm), BF16),
                            pltpu.VMEM((LOCAL_ROWS, D_MODEL), BF16), pltpu.SemaphoreType.DMA((2,))]),
        compiler_params=_params(("arbitrary",)),
        name="moe_combine_norm",
    )(*tables, slot, gates, h2, g_final, ys)


def _rope_tables(seq_len):
    pos = jnp.arange(seq_len, dtype=F32)[:, None]
    lane = np.arange(LANES)

    def table(half, lane_freq, first, active):
        inv = ROPE_BASE ** (-jnp.arange(half, dtype=F32) / half)
        ang = pos * inv[None, :]
        cos = jnp.cos(ang)[:, lane_freq]
        sin = jnp.sin(ang)[:, lane_freq]
        c = jnp.where(active[None, :], cos, 1.0)
        s = jnp.where(active[None, :], jnp.where(first[None, :], -sin, sin), 0.0)
        return c, s

    half_r = RET_DK // 2
    cr, sr = table(half_r, lane % half_r, (lane % RET_DK) < half_r, np.ones(LANES, bool))
    half_m = MLA_ROPE // 2
    rel = lane - MLA_NOPE
    active = (rel >= 0) & (rel < MLA_ROPE)
    cm, sm = table(half_m, np.where(active, rel % half_m, 0), active & (rel < half_m), active)
    return cr, sr, cm, sm


def kernel(x_prompt, x_sample, mem_prompt, mem_sample, norm_mix, w_in, ret_decay_fwd, ret_decay_bwd, ret_gn, q_a_norm, w_uq, kv_a_norm, w_ukv, w_mix_out, norm_cross, norm_mem, w_xq, w_xkv, w_xo, norm_ffn, router_w, router_b, w_gu, b_gu, w_down, b_down, norm_final):
    assert norm_mix.shape[0] == 1, "single layer"
    bp, sp, d = x_prompt.shape
    bs, ss, _ = x_sample.shape
    tp, ts = bp * sp, bs * ss
    t = tp + ts
    assert d == D_MODEL and sp % TOKEN_TILE == 0 and ss % TOKEN_TILE == 0 and tp % ss == 0 and sp >= ss
    mem_len = mem_prompt.shape[1]

    w_in0 = w_in[0]
    w_in_pad = jnp.zeros((D_MODEL, D_IN_PAD), F32)
    w_in_pad = w_in_pad.at[:, :2816].set(w_in0[:, :2816])
    w_in_pad = w_in_pad.at[:, 2816 + MLA_NOPE:2816 + MLA_NOPE + MLA_ROPE].set(w_in0[:, 2816:]).astype(BF16)
    wuq_pad = jnp.pad(w_uq[0].reshape(Q_LORA, MLA_HEADS, MLA_NOPE + MLA_ROPE),
                      ((0, 0), (0, 0), (0, LANES - MLA_NOPE - MLA_ROPE))).reshape(Q_LORA, MLA_HEADS * LANES).astype(BF16)
    wukv = w_ukv[0].reshape(KV_LORA, MLA_HEADS, MLA_NOPE + MLA_V)
    wuk_pad = jnp.pad(wukv[:, :, :MLA_NOPE], ((0, 0), (0, 0), (0, LANES - MLA_NOPE))).reshape(KV_LORA, MLA_HEADS * LANES).astype(BF16)
    wuv = wukv[:, :, MLA_NOPE:].reshape(KV_LORA, MLA_WIDTH).astype(BF16)
    rw_pad = jnp.pad(router_w[0], ((0, 0), (0, LANES - N_EXPERTS))).astype(BF16)
    rb_pad = jnp.pad(router_b[0].astype(F32), (0, LANES - N_EXPERTS), constant_values=NEG_BIG)[None, :]
    lgf = jnp.log1p(-jnp.exp2(ret_decay_fwd[0].astype(F32)))
    lgb = jnp.log1p(-jnp.exp2(ret_decay_bwd[0].astype(F32)))
    tabs = _rope_tables(sp)

    xp = x_prompt.reshape(tp, D_MODEL)
    xs = x_sample.reshape(ts, D_MODEL)
    mem = jnp.concatenate([mem_prompt.reshape(-1, D_MODEL), mem_sample.reshape(-1, D_MODEL)], axis=0)

    kvmem = _mem_kv(mem, norm_mem[0][None, :], w_xkv[0].astype(BF16)).reshape(bp + bs, mem_len, 2 * D_MODEL)

    rq, rk, rv, rg, qm, km, vm = _in_proj(xp, xs, norm_mix[0][None, :], w_in_pad, q_a_norm[0][None, :], wuq_pad,
                                          kv_a_norm[0][None, :], wuk_pad, wuv, tabs, sp, ss)

    gn = ret_gn[0][None, :]
    ret_p = _retention(lgf, lgb, rq, rk, rv, rg, gn, sp, bp, 0)
    ret_s = _retention(lgf, lgb, rq, rk, rv, rg, gn, ss, bs, tp // ss)
    mla_p = _mla_attn(qm, km, vm, sp, bp, 0)
    mla_s = _mla_attn(qm, km, vm, ss, bs, tp // ss)

    h2, xn, gates, slot, stats, counts = _mix_cross(
        xp, xs, ret_p, ret_s, mla_p, mla_s, w_mix_out[0].astype(BF16), norm_cross[0][None, :], w_xq[0].astype(BF16),
        kvmem, w_xo[0].astype(BF16), norm_ffn[0][None, :], rw_pad, rb_pad, sp, ss)

    used = counts[0, :N_EXPERTS]
    padded = (used + MOE_BLOCK - 1) // MOE_BLOCK * MOE_BLOCK
    pend = jnp.cumsum(padded)
    pstart = pend - padded
    tile_cnt, tile_before, tile_lstart = (stats[:, r, :N_EXPERTS] for r in range(3))
    n_tiles = stats.shape[0]
    tile_chunks = (tile_cnt + ROW_CHUNK - 1) // ROW_CHUNK
    tables = (tile_chunks.reshape(-1),
              tile_lstart.reshape(-1),
              (pstart[None, :] + tile_before).reshape(-1),
              jnp.sum(tile_chunks, axis=-1))
    ztables = (pstart + used, (padded - used) // ROW_CHUNK)
    n_blocks = -(-(t * TOP_K + N_EXPERTS * (n_tiles * (ROW_CHUNK - 1) + MOE_BLOCK - 1)) // MOE_BLOCK)
    blk = jnp.arange(n_blocks, dtype=jnp.int32)
    n_active = (pend[-1] // MOE_BLOCK).astype(jnp.int32)
    block_i = jnp.minimum(blk, n_active - 1)
    block_e = jnp.minimum(jnp.sum((block_i[:, None] * MOE_BLOCK >= pend[None, :]).astype(jnp.int32), axis=-1),
                          N_EXPERTS - 1).astype(jnp.int32)

    xs_sorted = _dispatch(tables, ztables, xn, slot, n_blocks * MOE_BLOCK)
    ys = _experts(block_e, block_i, n_active[None], xs_sorted, w_gu[0], b_gu[0][:, None, :],
                  w_down[0], b_down[0][:, None, :])
    out_p, out_s = _combine(tables, slot, gates, h2, norm_final[None, :], ys, tp, ts)
    return out_p.reshape(bp, sp, D_MODEL), out_s.reshape(bs, ss, D_MODEL)
```

```python
import functools
import math

import jax
import jax.numpy as jnp
import numpy as np
from jax import lax
from jax.experimental import pallas as pl
from jax.experimental.pallas import tpu as pltpu

D_MODEL = 1024
RET_HEADS = 8
RET_DK = 64
RET_WIDTH = 512
CHUNK = 128
MLA_HEADS = 8
MLA_NOPE = 64
MLA_ROPE = 32
MLA_V = 64
MLA_WIDTH = 512
Q_LORA = 512
KV_LORA = 256
ROPE_BASE = 10000.0
X_HEADS = 4
X_HEAD_DIM = 256
N_EXPERTS = 32
TOP_K = 4
SWIGLU_LIMIT = 7.0
SWIGLU_ALPHA = 1.702
MOE_BLOCK = 512
EPS = 1e-6

LANES = 128
VMEM_LIMIT = 56 * 1024 * 1024

TOKEN_TILE = 512
ROW_PARTS = 2
MIX_ROW_PARTS = 1
ATTN_TQ = 1024
ATTN_TK = 1024
ATTN_UNROLL = 4
RET_UNROLL = 8
ROW_CHUNK = 8
GROUP_BITS = (TOKEN_TILE // ROW_CHUNK).bit_length()
WAIT_CHUNKS = 16
LOCAL_STEP = 256
LOCAL_ROWS = -(-(TOKEN_TILE * 4 + 32 * (ROW_CHUNK - 1)) // LOCAL_STEP) * LOCAL_STEP
D_IN_PAD = 4 * 512 + Q_LORA + KV_LORA + LANES
NEG_BIG = -1e30

F32 = jnp.float32
BF16 = jnp.bfloat16


def _params(sem, vmem=VMEM_LIMIT):
    return pltpu.CompilerParams(dimension_semantics=sem, vmem_limit_bytes=vmem)


def _const_spec(shape):
    nd = len(shape)
    return pl.BlockSpec(shape, lambda *_: (0,) * nd, pipeline_mode=pl.Buffered(1))


def _rms(x, g):
    ms = jnp.mean(x * x, axis=-1, keepdims=True)
    return x * lax.rsqrt(ms + EPS) * g


def _dot(a, b):
    return jnp.dot(a, b, preferred_element_type=F32)


def _dot_nt(a, b):
    return lax.dot_general(a, b, (((1,), (1,)), ((), ())), preferred_element_type=F32)


def _dot_tn(a, b):
    return lax.dot_general(a, b, (((0,), (0,)), ((), ())), preferred_element_type=F32)


def _mem_kv_kernel(mem_ref, g_ref, w_ref, o_ref):
    mn = _rms(mem_ref[...], g_ref[...])
    o_ref[...] = _dot(mn.astype(BF16), w_ref[...]).astype(BF16)


def _mem_kv(mem, g, w):
    rows, mem_len = mem.shape[0], 256
    return pl.pallas_call(
        _mem_kv_kernel,
        out_shape=jax.ShapeDtypeStruct((rows, 2 * D_MODEL), BF16),
        grid=(rows // mem_len,),
        in_specs=[pl.BlockSpec((mem_len, D_MODEL), lambda i: (i, 0)),
                  _const_spec((1, D_MODEL)),
                  _const_spec((D_MODEL, 2 * D_MODEL))],
        out_specs=pl.BlockSpec((mem_len, 2 * D_MODEL), lambda i: (i, 0)),
        compiler_params=_params(("arbitrary",)),
        name="mem_kv",
    )(mem, g, w)


def _rope_slab(x, c, ss, first, shift_up, shift_down):
    swap = jnp.where(first, pltpu.roll(x, shift_up, 1), pltpu.roll(x, shift_down, 1))
    return x * c + swap * ss


def _in_proj_kernel(xp_ref, xs_ref, g_ref, win_ref, qan_ref, wuq_ref, kvan_ref, wuk_ref, wuv_ref,
                    cr_ref, sr_ref, cm_ref, sm_ref,
                    rq_ref, rk_ref, rv_ref, rg_ref, qm_ref, km_ref, vm_ref, *, n_prompt_tiles):
    i = pl.program_id(0)
    tm = rq_ref.shape[0]
    lane = lax.broadcasted_iota(jnp.int32, (1, LANES), 1)
    ret_first = (lane % RET_DK) < (RET_DK // 2)
    mla_first = (lane >= MLA_NOPE) & (lane < MLA_NOPE + MLA_ROPE // 2)
    half_r, half_m = RET_DK // 2, MLA_ROPE // 2
    q_scale = (MLA_NOPE + MLA_ROPE) ** -0.5 * math.log2(math.e)

    for part in range(ROW_PARTS):
        rows = slice(part * (tm // ROW_PARTS), (part + 1) * (tm // ROW_PARTS))
        x = jnp.where(i < n_prompt_tiles, xp_ref[rows, :], xs_ref[rows, :])
        xn = _rms(x, g_ref[...])
        proj = _dot(xn.astype(BF16), win_ref[...])
        cr, sr, cm, sm = cr_ref[rows, :], sr_ref[rows, :], cm_ref[rows, :], sm_ref[rows, :]

        for s in range(RET_WIDTH // LANES):
            lo = s * LANES
            q = _rope_slab(proj[:, lo:lo + LANES], cr, sr, ret_first, LANES - half_r, half_r)
            rq_ref[rows, lo:lo + LANES] = q.astype(BF16)
            k = _rope_slab(proj[:, 512 + lo:512 + lo + LANES], cr, sr, ret_first, LANES - half_r, half_r)
            rk_ref[rows, lo:lo + LANES] = (k * (RET_DK ** -0.5)).astype(BF16)
        rv_ref[rows, :] = proj[:, 1024:1536].astype(BF16)
        rg_ref[rows, :] = proj[:, 1536:2048].astype(BF16)

        cq = _rms(proj[:, 2048:2048 + Q_LORA], qan_ref[...])
        qm = _dot(cq.astype(BF16), wuq_ref[...])
        ckv = _rms(proj[:, 2560:2560 + KV_LORA], kvan_ref[...]).astype(BF16)
        kn = _dot(ckv, wuk_ref[...])
        vm_ref[rows, :] = _dot(ckv, wuv_ref[...]).astype(BF16)
        kr = proj[:, 2816:2816 + LANES]
        kpe = _rope_slab(kr, cm, sm, mla_first, LANES - half_m, half_m)
        for h in range(MLA_HEADS):
            lo = h * LANES
            qh = _rope_slab(qm[:, lo:lo + LANES], cm, sm, mla_first, LANES - half_m, half_m)
            qm_ref[rows, lo:lo + LANES] = (qh * q_scale).astype(BF16)
            km_ref[rows, lo:lo + LANES] = (kn[:, lo:lo + LANES] + kpe).astype(BF16)


def _in_proj(xp, xs, g, w_in_pad, qan, wuq_pad, kvan, wuk_pad, wuv, tabs, sp, ss):
    tm = TOKEN_TILE
    tp, ts = xp.shape[0], xs.shape[0]
    npt, nst = tp // tm, ts // tm
    t = tp + ts
    tiles_p, tiles_s = sp // tm, ss // tm

    def xp_map(i):
        return (jnp.minimum(i, npt - 1), 0)

    def xs_map(i):
        return (jnp.maximum(i - npt, 0), 0)

    def tab_map(i):
        return (jnp.where(i < npt, i % tiles_p, (i - npt) % tiles_s), 0)

    tok = lambda w: pl.BlockSpec((tm, w), lambda i: (i, 0))
    widths = (512, 512, 512, 512, 1024, 1024, 512)
    return pl.pallas_call(
        functools.partial(_in_proj_kernel, n_prompt_tiles=npt),
        out_shape=[jax.ShapeDtypeStruct((t, w), BF16) for w in widths],
        grid=(npt + nst,),
        in_specs=[pl.BlockSpec((tm, D_MODEL), xp_map), pl.BlockSpec((tm, D_MODEL), xs_map),
                  _const_spec((1, D_MODEL)), _const_spec((D_MODEL, D_IN_PAD)),
                  _const_spec((1, Q_LORA)), _const_spec((Q_LORA, MLA_HEADS * LANES)),
                  _const_spec((1, KV_LORA)), _const_spec((KV_LORA, MLA_HEADS * LANES)),
                  _const_spec((KV_LORA, MLA_WIDTH))]
                 + [pl.BlockSpec((tm, LANES), tab_map)] * 4,
        out_specs=[tok(w) for w in widths],
        compiler_params=_params(("arbitrary",)),
        name="in_proj",
    )(xp, xs, g, w_in_pad, qan, wuq_pad, kvan, wuk_pad, wuv, *tabs)


def _retention_kernel(lgf_ref, lgb_ref, q_ref, k_ref, v_ref, g_ref, gn_ref, o_ref, kvf_scr, kvb_scr, st_scr, *, n_chunks):
    hp = pl.program_id(1)
    c = CHUNK
    lane = lax.broadcasted_iota(jnp.int32, (1, c), 1)
    row = lax.broadcasted_iota(jnp.int32, (c, 1), 0)
    lane_h0 = lane < RET_DK
    row_h0 = row < RET_DK
    lgf0, lgf1 = lgf_ref[2 * hp], lgf_ref[2 * hp + 1]
    lgb0, lgb1 = lgb_ref[2 * hp], lgb_ref[2 * hp + 1]
    lgf_lane = jnp.where(lane_h0, lgf0, lgf1)
    lgb_lane = jnp.where(lane_h0, lgb0, lgb1)
    t = row.astype(F32)
    q_dec_f = jnp.exp((t + 1.0) * lgf_lane)
    q_dec_b = jnp.exp((c - t) * lgb_lane)
    k_dec_f = jnp.exp((c - 1.0 - t) * lgf_lane)
    k_dec_b = jnp.exp(t * lgb_lane)
    chunk_dec_f = jnp.exp(c * jnp.where(row_h0, lgf0, lgf1))
    chunk_dec_b = jnp.exp(c * jnp.where(row_h0, lgb0, lgb1))
    same_head = row_h0 == lane_h0
    diff = t - lane.astype(F32)
    d_intra = []
    for lgf, lgb in ((lgf0, lgb0), (lgf1, lgb1)):
        fwd = jnp.where(diff >= 0, jnp.exp(jnp.where(diff >= 0, diff, 0.0) * lgf), 0.0)
        bwd = jnp.where(diff < 0, jnp.exp(jnp.where(diff < 0, -diff, 0.0) * lgb), 0.0)
        d_intra.append(fwd + bwd)
    avg = jnp.where(same_head, 1.0 / RET_DK, 0.0).astype(BF16)
    gn = gn_ref[...]

    def chunk(n):
        return pl.ds(pl.multiple_of(n * c, c), c)

    unroll = min(RET_UNROLL, n_chunks)

    def chunk_kv(n, _):
        k = k_ref[chunk(n), :].astype(F32)
        kd = jnp.concatenate([(k * k_dec_f).astype(BF16), (k * k_dec_b).astype(BF16)], axis=-1)
        kv = _dot_tn(kd, v_ref[chunk(n), :])
        kvf_scr[n] = jnp.where(same_head, kv[0:c], 0.0)
        kvb_scr[n] = jnp.where(same_head, kv[c:2 * c], 0.0)
        return 0

    lax.fori_loop(0, n_chunks, chunk_kv, 0, unroll=unroll)

    def fwd_state(n, s_f):
        st_scr[n, 0:c, :] = s_f.astype(BF16)
        return chunk_dec_f * s_f + kvf_scr[n]

    lax.fori_loop(0, n_chunks, fwd_state, jnp.zeros((c, c), F32))

    def bwd_state(j, s_b):
        n = n_chunks - 1 - j
        st_scr[n, c:2 * c, :] = s_b.astype(BF16)
        return chunk_dec_b * s_b + kvb_scr[n]

    lax.fori_loop(0, n_chunks, bwd_state, jnp.zeros((c, c), F32))

    d_both = jnp.concatenate(d_intra, axis=0)
    zero_q = jnp.zeros((c, c), BF16)

    def chunk_out(n):
        q = q_ref[chunk(n), :]
        qf = q.astype(F32)
        q_both = jnp.concatenate([(qf * q_dec_f).astype(BF16), (qf * q_dec_b).astype(BF16)], axis=-1)
        q_heads = jnp.concatenate([jnp.where(lane_h0, q, zero_q), jnp.where(lane_h0, zero_q, q)], axis=0)
        p = (_dot_nt(q_heads, k_ref[chunk(n), :]) * d_both).astype(BF16)
        inner = _dot(p, v_ref[chunk(n), :])
        return _dot(q_both, st_scr[n]) + jnp.where(lane_h0, inner[0:c], inner[c:2 * c])

    def split_rows(a):
        hi = a.astype(BF16)
        return jnp.concatenate([hi, (a - hi.astype(F32)).astype(BF16)], axis=0)

    def out_group(gi, _):
        rows = pl.ds(pl.multiple_of(gi * (unroll * c), unroll * c), unroll * c)
        y = jnp.concatenate([chunk_out(gi * unroll + u) for u in range(unroll)], axis=0)
        m = y.shape[0]
        mu2 = _dot(split_rows(y), avg)
        d = y - (mu2[0:m] + mu2[m:2 * m])
        var2 = _dot(split_rows(d * d), avg)
        yn = d * lax.rsqrt(var2[0:m] + var2[m:2 * m] + EPS) * gn
        gate = g_ref[rows, :].astype(F32)
        o_ref[rows, :] = (yn * (gate * jax.nn.sigmoid(gate))).astype(BF16)
        return 0

    lax.fori_loop(0, n_chunks // unroll, out_group, 0)


def _retention(lgf, lgb, rq, rk, rv, rg, gn, seq_len, n_seq, row_block0):
    n_chunks = seq_len // CHUNK
    hp_count = RET_WIDTH // LANES
    blk = pl.BlockSpec((seq_len, LANES), lambda b, hp, *_: (row_block0 + b, hp))
    return pl.pallas_call(
        functools.partial(_retention_kernel, n_chunks=n_chunks),
        out_shape=jax.ShapeDtypeStruct((n_seq * seq_len, RET_WIDTH), BF16),
        grid_spec=pltpu.PrefetchScalarGridSpec(
            num_scalar_prefetch=2,
            grid=(n_seq, hp_count),
            in_specs=[blk, blk, blk, blk, pl.BlockSpec((1, LANES), lambda b, hp, *_: (0, hp))],
            out_specs=pl.BlockSpec((seq_len, LANES), lambda b, hp, *_: (b, hp)),
            scratch_shapes=[pltpu.VMEM((n_chunks, CHUNK, CHUNK), F32), pltpu.VMEM((n_chunks, CHUNK, CHUNK), F32),
                            pltpu.VMEM((n_chunks, 2 * CHUNK, CHUNK), BF16)]),
        compiler_params=_params(("arbitrary", "arbitrary")),
        name="retention",
    )(lgf, lgb, rq, rk, rv, rg, gn)


def _mla_attn_kernel(q_ref, k_ref, v_ref, o_ref, *, tk, n_kv):
    tq = q_ref.shape[0]
    lane = lax.broadcasted_iota(jnp.int32, (1, LANES), 1)
    qs = (q_ref[:, :LANES], q_ref[:, LANES:])

    def body(j, carry):
        rows = pl.ds(pl.multiple_of(j * tk, tk), tk)
        v = v_ref[rows, :]
        new = []
        for h in range(2):
            m, l, acc = carry[h]
            s = _dot_nt(qs[h], k_ref[rows, h * LANES:(h + 1) * LANES])
            m_new = jnp.maximum(m, jnp.max(s, axis=-1, keepdims=True))
            alpha = jnp.exp2(m - m_new)
            p = jnp.exp2(s - m_new)
            l = alpha * l + jnp.sum(p, axis=-1, keepdims=True)
            acc = alpha * acc + _dot(p.astype(BF16), v)
            new.append((m_new, l, acc))
        return tuple(new)

    init = (jnp.full((tq, 1), NEG_BIG, F32), jnp.zeros((tq, 1), F32), jnp.zeros((tq, LANES), F32))
    (_, l0, acc0), (_, l1, acc1) = lax.fori_loop(0, n_kv, body, (init, init), unroll=min(ATTN_UNROLL, n_kv))
    o_ref[...] = jnp.where(lane < MLA_V, acc0 / l0, acc1 / l1).astype(BF16)


def _mla_attn(qm, km, vm, seq_len, n_seq, row_block0):
    tq, tk = min(ATTN_TQ, seq_len), min(ATTN_TK, seq_len)
    hp_count = MLA_HEADS // 2
    nq = seq_len // tq
    return pl.pallas_call(
        functools.partial(_mla_attn_kernel, tk=tk, n_kv=seq_len // tk),
        out_shape=jax.ShapeDtypeStruct((n_seq * seq_len, MLA_WIDTH), BF16),
        grid=(n_seq, hp_count, nq),
        in_specs=[pl.BlockSpec((tq, 2 * LANES), lambda b, hp, i: ((row_block0 + b) * nq + i, hp)),
                  pl.BlockSpec((seq_len, 2 * LANES), lambda b, hp, i: (row_block0 + b, hp)),
                  pl.BlockSpec((seq_len, LANES), lambda b, hp, i: (row_block0 + b, hp))],
        out_specs=pl.BlockSpec((tq, LANES), lambda b, hp, i: (b * nq + i, hp)),
        compiler_params=_params(("arbitrary", "arbitrary", "arbitrary")),
        name="mla_attn",
    )(qm, km, vm)


def _mix_cross_kernel(xp_ref, xs_ref, rp_ref, rs_ref, mp_ref, ms_ref, wmix_ref, gx_ref, wxq_ref, kv_ref, wxo_ref,
                      gf_ref, rw_ref, rb_ref,
                      h_ref, xn_ref, gate_ref, slot_ref, stat_ref, cnt_ref, carry_scr, *, n_prompt_tiles):
    i = pl.program_id(0)
    tm = h_ref.shape[0]
    is_p = i < n_prompt_tiles
    lane = lax.broadcasted_iota(jnp.int32, (tm, LANES), 1)

    def part_rows(rows):
        x = jnp.where(is_p, xp_ref[rows, :], xs_ref[rows, :])
        ret = jnp.where(is_p, rp_ref[rows, :], rs_ref[rows, :])
        mla = jnp.where(is_p, mp_ref[rows, :], ms_ref[rows, :])
        h1 = x + _dot(ret, wmix_ref[0:RET_WIDTH, :]) + _dot(mla, wmix_ref[RET_WIDTH:, :])

        hn = _rms(h1, gx_ref[...]).astype(BF16)
        q = (_dot(hn, wxq_ref[...]) * (X_HEAD_DIM ** -0.5)).astype(BF16)
        heads = []
        for h in range(X_HEADS):
            lo = h * X_HEAD_DIM
            s = _dot_nt(q[:, lo:lo + X_HEAD_DIM], kv_ref[0, :, lo:lo + X_HEAD_DIM])
            e = jnp.exp(s - jnp.max(s, axis=-1, keepdims=True))
            p = (e / jnp.sum(e, axis=-1, keepdims=True)).astype(BF16)
            heads.append(_dot(p, kv_ref[0, :, D_MODEL + lo:D_MODEL + lo + X_HEAD_DIM]).astype(BF16))
        h2 = h1 + _dot(jnp.concatenate(heads, axis=-1), wxo_ref[...])
        h_ref[rows, :] = h2

        xn = _rms(h2, gf_ref[...]).astype(BF16)
        xn_ref[rows, :] = xn
        work = _dot(xn, rw_ref[...]) + rb_ref[...]
        lane_f = lax.broadcasted_iota(jnp.int32, work.shape, 1).astype(F32)
        vals, picks = [], []
        for _ in range(TOP_K):
            m = jnp.max(work, axis=-1, keepdims=True)
            idx = jnp.min(jnp.where(work == m, lane_f, float(LANES)), axis=-1, keepdims=True)
            sel = lane_f == idx
            work = jnp.where(sel, -jnp.inf, work)
            vals.append(m)
            picks.append(jnp.where(sel, 1.0, 0.0))
        exps = [jnp.exp(v - vals[0]) for v in vals]
        denom = exps[0] + exps[1] + exps[2] + exps[3]
        return picks, [e / denom for e in exps]

    part = tm // MIX_ROW_PARTS
    parts = [part_rows(slice(r * part, (r + 1) * part)) for r in range(MIX_ROW_PARTS)]
    picks = [jnp.concatenate([p[0][k] for p in parts], axis=0) for k in range(TOP_K)]
    gates = [jnp.concatenate([p[1][k] for p in parts], axis=0) for k in range(TOP_K)]
    onehot = picks[0] + picks[1] + picks[2] + picks[3]


    @pl.when(i == 0)
    def _():
        carry_scr[...] = jnp.zeros_like(carry_scr)

    r_iota = lax.broadcasted_iota(jnp.int32, (tm, tm), 0)
    c_iota = lax.broadcasted_iota(jnp.int32, (tm, tm), 1)
    lower = jnp.where(c_iota < r_iota, 1.0, 0.0).astype(BF16)
    before = _dot(lower, onehot.astype(BF16))
    tile_cnt = jnp.sum(onehot, axis=0, keepdims=True)
    group_rows = jnp.floor((tile_cnt + (ROW_CHUNK - 1)) * (1.0 / ROW_CHUNK)) * ROW_CHUNK
    carry_before = carry_scr[...]
    carry = carry_before + group_rows
    carry_scr[...] = carry
    cnt_ref[...] = carry.astype(jnp.int32)
    e_row = lax.broadcasted_iota(jnp.int32, (LANES, LANES), 0)
    e_col = lax.broadcasted_iota(jnp.int32, (LANES, LANES), 1)
    earlier = jnp.where(e_row < e_col, 1.0, 0.0).astype(BF16)
    group_start = _dot(jnp.broadcast_to(group_rows, (8, LANES)).astype(BF16), earlier)[0:1, :]
    row8 = lax.broadcasted_iota(jnp.int32, (8, LANES), 0)
    stats = jnp.where(row8 == 0, tile_cnt, jnp.where(row8 == 1, carry_before, jnp.where(row8 == 2, group_start, 0.0)))
    stat_ref[0] = stats.astype(jnp.int32)

    place = before + group_start
    gate_out = jnp.zeros((tm, LANES), F32)
    slot_out = jnp.zeros((tm, LANES), F32)
    for k in range(TOP_K):
        slot_k = jnp.sum(picks[k] * place, axis=-1, keepdims=True)
        gate_out = jnp.where(lane == k, gates[k], gate_out)
        slot_out = jnp.where(lane == k, slot_k, slot_out)
    gate_ref[...] = gate_out
    slot_ref[...] = slot_out.astype(jnp.int32)


def _mix_cross(xp, xs, ret_p, ret_s, mla_p, mla_s, wmix, gx, wxq, kvmem, wxo, gf, rw_pad, rb_pad, sp, ss):
    tm = TOKEN_TILE
    tp, ts = xp.shape[0], xs.shape[0]
    npt, nst = tp // tm, ts // tm
    t = tp + ts
    n_seq_p = tp // sp
    mem_len = kvmem.shape[1]

    def p_map(i):
        return (jnp.minimum(i, npt - 1), 0)

    def s_map(i):
        return (jnp.maximum(i - npt, 0), 0)

    def kv_map(i):
        return (jnp.where(i < npt, i // (sp // tm), n_seq_p + (i - npt) // (ss // tm)), 0, 0)

    tok = lambda w: pl.BlockSpec((tm, w), lambda i: (i, 0))
    return pl.pallas_call(
        functools.partial(_mix_cross_kernel, n_prompt_tiles=npt),
        out_shape=[jax.ShapeDtypeStruct((t, D_MODEL), F32), jax.ShapeDtypeStruct((t, D_MODEL), BF16),
                   jax.ShapeDtypeStruct((t, LANES), F32), jax.ShapeDtypeStruct((t, LANES), jnp.int32),
                   jax.ShapeDtypeStruct((npt + nst, 8, LANES), jnp.int32), jax.ShapeDtypeStruct((1, LANES), jnp.int32)],
        grid=(npt + nst,),
        in_specs=[pl.BlockSpec((tm, D_MODEL), p_map), pl.BlockSpec((tm, D_MODEL), s_map),
                  pl.BlockSpec((tm, RET_WIDTH), p_map), pl.BlockSpec((tm, RET_WIDTH), s_map),
                  pl.BlockSpec((tm, MLA_WIDTH), p_map), pl.BlockSpec((tm, MLA_WIDTH), s_map),
                  _const_spec((D_MODEL, D_MODEL)), _const_spec((1, D_MODEL)), _const_spec((D_MODEL, D_MODEL)),
                  pl.BlockSpec((1, mem_len, 2 * D_MODEL), kv_map),
                  _const_spec((D_MODEL, D_MODEL)), _const_spec((1, D_MODEL)),
                  _const_spec((D_MODEL, LANES)), _const_spec((1, LANES))],
        out_specs=[tok(D_MODEL), tok(D_MODEL), tok(LANES), tok(LANES),
                   pl.BlockSpec((1, 8, LANES), lambda i: (i, 0, 0)),
                   pl.BlockSpec((1, LANES), lambda i: (0, 0))],
        scratch_shapes=[pltpu.VMEM((1, LANES), F32)],
        compiler_params=_params(("arbitrary",)),
        name="mix_cross_router",
    )(xp, xs, ret_p, ret_s, mla_p, mla_s, wmix, gx, wxq, kvmem, wxo, gf, rw_pad, rb_pad)


def _to_rows(a):
    eye = (lax.broadcasted_iota(jnp.int32, (LANES, LANES), 0)
           == lax.broadcasted_iota(jnp.int32, (LANES, LANES), 1)).astype(BF16)
    hi = a.astype(BF16)
    r1 = a - hi.astype(F32)
    mid = r1.astype(BF16)
    lo = (r1 - mid.astype(F32)).astype(BF16)
    return _dot_nt(eye, hi) + _dot_nt(eye, mid) + _dot_nt(eye, lo)


def _chunk_copies(tile, nch_ref, lstart_ref, gstart_ref, local, hbm, sem, to_hbm, act):
    def per_expert(e, _):
        base = tile * N_EXPERTS + e
        n = nch_ref[base]
        ls = lstart_ref[base]
        gs = gstart_ref[base]
        def bit_copy(j):
            rows = ROW_CHUNK << j

            @pl.when(((n >> j) & 1) == 1)
            def _():
                off = (n & ((1 << j) - 1)) * ROW_CHUNK
                l_rows = local.at[pl.ds(pl.multiple_of(ls + off, ROW_CHUNK), rows)]
                g_rows = hbm.at[pl.ds(pl.multiple_of(gs + off, ROW_CHUNK), rows)]
                act(pltpu.make_async_copy(l_rows, g_rows, sem) if to_hbm else pltpu.make_async_copy(g_rows, l_rows, sem))

        for j in range(GROUP_BITS):
            bit_copy(j)
        return 0

    lax.fori_loop(0, N_EXPERTS, per_expert, 0)


def _wait_tile(tile, tot_ref, local, hbm, sem, to_hbm):
    total = tot_ref[tile]

    def wait_rows(rows):
        l_rows = local.at[pl.ds(0, rows)]
        g_rows = hbm.at[pl.ds(0, rows)]
        (pltpu.make_async_copy(l_rows, g_rows, sem) if to_hbm else pltpu.make_async_copy(g_rows, l_rows, sem)).wait()

    def big(_, c):
        wait_rows(ROW_CHUNK * WAIT_CHUNKS)
        return c

    def small(_, c):
        wait_rows(ROW_CHUNK)
        return c

    lax.fori_loop(0, total // WAIT_CHUNKS, big, 0)
    lax.fori_loop(0, total % WAIT_CHUNKS, small, 0)


def _start(cp):
    cp.start()


def _wait(cp):
    cp.wait()


def _dispatch_kernel(nch_ref, lstart_ref, gstart_ref, tot_ref, zstart_ref, znch_ref,
                     x_ref, slot_ref, xs_hbm, pick_ref, buf, zero_scr, sem, zsem):
    i = pl.program_id(0)
    last = pl.num_programs(0) - 1
    tm = x_ref.shape[0]
    b = i % 2
    slot_rows = _to_rows(slot_ref[...].astype(F32))
    x = x_ref[...]

    def build(s, _):
        r0 = pl.multiple_of(s * LOCAL_STEP, LOCAL_STEP)
        rid = (lax.broadcasted_iota(jnp.int32, (LOCAL_STEP, tm), 0) + r0).astype(F32)
        hit = rid == slot_rows[0:1, :]
        for k in range(1, TOP_K):
            hit = hit | (rid == slot_rows[k:k + 1, :])
        pick = jnp.where(hit, 1.0, 0.0).astype(BF16)
        pick_ref[pl.ds(r0, LOCAL_STEP), :] = pick
        buf[b, pl.ds(r0, LOCAL_STEP), :] = _dot(pick, x)
        return 0

    lax.fori_loop(0, LOCAL_ROWS // LOCAL_STEP, build, 0, unroll=3)

    _chunk_copies(i, nch_ref, lstart_ref, gstart_ref, buf.at[b], xs_hbm, sem.at[b], True, _start)

    @pl.when(i > 0)
    def _():
        _wait_tile(i - 1, tot_ref, buf.at[1 - b], xs_hbm, sem.at[1 - b], True)

    @pl.when(i == last)
    def _():
        _wait_tile(i, tot_ref, buf.at[b], xs_hbm, sem.at[b], True)
        zero_scr[...] = jnp.zeros_like(zero_scr)

        def fill(act):
            def per_expert(e, _):
                def per_chunk(c, _):
                    row = pl.multiple_of(zstart_ref[e] + c * ROW_CHUNK, ROW_CHUNK)
                    act(pltpu.make_async_copy(zero_scr, xs_hbm.at[pl.ds(row, ROW_CHUNK)], zsem))
                    return 0

                lax.fori_loop(0, znch_ref[e], per_chunk, 0)
                return 0

            lax.fori_loop(0, N_EXPERTS, per_expert, 0)

        fill(_start)
        fill(_wait)


def _dispatch(tables, ztables, xn, slot, n_rows):
    tm = TOKEN_TILE
    t = xn.shape[0]
    return pl.pallas_call(
        _dispatch_kernel,
        out_shape=[jax.ShapeDtypeStruct((n_rows, D_MODEL), F32),
                   jax.ShapeDtypeStruct((t // tm * LOCAL_ROWS, tm), BF16)],
        grid_spec=pltpu.PrefetchScalarGridSpec(
            num_scalar_prefetch=6,
            grid=(t // tm,),
            in_specs=[pl.BlockSpec((tm, D_MODEL), lambda i, *_: (i, 0)),
                      pl.BlockSpec((tm, LANES), lambda i, *_: (i, 0))],
            out_specs=[pl.BlockSpec(memory_space=pl.ANY),
                       pl.BlockSpec((LOCAL_ROWS, tm), lambda i, *_: (i, 0))],
            scratch_shapes=[pltpu.VMEM((2, LOCAL_ROWS, D_MODEL), F32), pltpu.VMEM((ROW_CHUNK, D_MODEL), F32),
                            pltpu.SemaphoreType.DMA((2,)), pltpu.SemaphoreType.DMA(())]),
        compiler_params=_params(("arbitrary",)),
        name="moe_dispatch",
    )(*tables, *ztables, xn, slot)


def _experts_kernel(be_ref, bi_ref, na_ref, x_ref, wgu_ref, bgu_ref, wd_ref, bd_ref, o_ref, wgu_scr, wd_scr):
    j = pl.program_id(0)
    active = j < na_ref[0]

    @pl.when(active & ((j == 0) | (be_ref[j] != be_ref[jnp.maximum(j - 1, 0)])))
    def _():
        wgu_scr[...] = wgu_ref[0].astype(BF16)
        wd_scr[...] = wd_ref[0].astype(BF16)

    @pl.when(active)
    def _():
        x = x_ref[...].astype(BF16)
        gu = _dot(x, wgu_scr[...]) + bgu_ref[0]
        gate = jnp.minimum(gu[:, :D_MODEL], SWIGLU_LIMIT)
        up = jnp.clip(gu[:, D_MODEL:], -SWIGLU_LIMIT, SWIGLU_LIMIT)
        hid = (up + 1.0) * (gate * jax.nn.sigmoid(SWIGLU_ALPHA * gate))
        o_ref[...] = _dot(hid.astype(BF16), wd_scr[...]) + bd_ref[0]


def _experts(block_e, block_i, n_active, xs, wgu, bgu, wd, bd):
    n_blocks = xs.shape[0] // MOE_BLOCK
    row_map = lambda j, be, bi, na: (bi[j], 0)
    e_map = lambda j, be, bi, na: (be[j], 0, 0)
    return pl.pallas_call(
        _experts_kernel,
        out_shape=jax.ShapeDtypeStruct(xs.shape, F32),
        grid_spec=pltpu.PrefetchScalarGridSpec(
            num_scalar_prefetch=3,
            grid=(n_blocks,),
            in_specs=[pl.BlockSpec((MOE_BLOCK, D_MODEL), row_map),
                      pl.BlockSpec((1, D_MODEL, 2 * D_MODEL), e_map),
                      pl.BlockSpec((1, 1, 2 * D_MODEL), e_map),
                      pl.BlockSpec((1, D_MODEL, D_MODEL), e_map),
                      pl.BlockSpec((1, 1, D_MODEL), e_map)],
            out_specs=pl.BlockSpec((MOE_BLOCK, D_MODEL), row_map),
            scratch_shapes=[pltpu.VMEM((D_MODEL, 2 * D_MODEL), BF16), pltpu.VMEM((D_MODEL, D_MODEL), BF16)]),
        compiler_params=_params(("arbitrary",)),
        name="moe_experts",
    )(block_e, block_i, n_active, xs, wgu, bgu, wd, bd)


def _combine_kernel(nch_ref, lstart_ref, gstart_ref, tot_ref, slot_ref, gate_ref, pick_ref, h_ref, g_ref, ys_hbm,
                    op_ref, os_ref, buf, scaled_scr, sem, *, n_prompt_tiles):
    i = pl.program_id(0)
    n = pl.num_programs(0)
    tm = h_ref.shape[0]
    b = i % 2

    @pl.when(i == 0)
    def _():
        buf[...] = jnp.zeros_like(buf)
        _chunk_copies(0, nch_ref, lstart_ref, gstart_ref, buf.at[0], ys_hbm, sem.at[0], False, _start)

    @pl.when(i + 1 < n)
    def _():
        _chunk_copies(i + 1, nch_ref, lstart_ref, gstart_ref, buf.at[1 - b], ys_hbm, sem.at[1 - b], False, _start)

    _wait_tile(i, tot_ref, buf.at[b], ys_hbm, sem.at[b], False)

    slot = slot_ref[...].astype(F32)
    info = gate_ref[...] + pltpu.roll(slot, TOP_K, 1)
    info_hi = info.astype(BF16)
    rest = info - info_hi.astype(F32)
    info_mid = rest.astype(BF16)
    info_lo = (rest - info_mid.astype(F32)).astype(BF16)
    lane = lax.broadcasted_iota(jnp.int32, (LOCAL_STEP, LANES), 1)

    def step(s, _):
        r0 = pl.multiple_of(s * LOCAL_STEP, LOCAL_STEP)
        pick = pick_ref[pl.ds(r0, LOCAL_STEP), :]
        own = _dot(pick, info_hi) + _dot(pick, info_mid) + _dot(pick, info_lo)
        rcol = (lax.broadcasted_iota(jnp.int32, (LOCAL_STEP, 1), 0) + r0).astype(F32)
        mine = jnp.where((own == rcol) & (lane >= TOP_K) & (lane < 2 * TOP_K), 1.0, 0.0)
        g_col = jnp.sum(own * pltpu.roll(mine, LANES - TOP_K, 1), axis=-1, keepdims=True)
        rows = buf[b, pl.ds(r0, LOCAL_STEP), :]
        scaled_scr[pl.ds(r0, LOCAL_STEP), :] = jnp.where(g_col != 0.0, rows * g_col, 0.0).astype(BF16)
        return 0

    lax.fori_loop(0, LOCAL_ROWS // LOCAL_STEP, step, 0, unroll=3)
    y = _dot_tn(pick_ref[...], scaled_scr[...])
    out = _rms(h_ref[...] + y, g_ref[...])

    @pl.when(i < n_prompt_tiles)
    def _():
        op_ref[...] = out

    @pl.when(i >= n_prompt_tiles)
    def _():
        os_ref[...] = out


def _combine(tables, slot, gates, pick, h2, g_final, ys, tp, ts):
    tm = TOKEN_TILE
    npt, nst = tp // tm, ts // tm
    tok = lambda w: pl.BlockSpec((tm, w), lambda i, *_: (i, 0))
    return pl.pallas_call(
        functools.partial(_combine_kernel, n_prompt_tiles=npt),
        out_shape=[jax.ShapeDtypeStruct((tp, D_MODEL), F32), jax.ShapeDtypeStruct((ts, D_MODEL), F32)],
        grid_spec=pltpu.PrefetchScalarGridSpec(
            num_scalar_prefetch=4,
            grid=(npt + nst,),
            in_specs=[tok(LANES), tok(LANES), pl.BlockSpec((LOCAL_ROWS, tm), lambda i, *_: (i, 0)), tok(D_MODEL),
                      pl.BlockSpec((1, D_MODEL), lambda i, *_: (0, 0)),
                      pl.BlockSpec(memory_space=pl.ANY)],
            out_specs=[pl.BlockSpec((tm, D_MODEL), lambda i, *_: (jnp.minimum(i, npt - 1), 0)),
                       pl.BlockSpec((tm, D_MODEL), lambda i, *_: (jnp.maximum(i - npt, 0), 0))],
            scratch_shapes=[pltpu.VMEM((2, LOCAL_ROWS, D_MODEL), F32), pltpu.VMEM((LOCAL_ROWS, D_MODEL), BF16),
                            pltpu.SemaphoreType.DMA((2,))]),
        compiler_params=_params(("arbitrary",)),
        name="moe_combine_norm",
    )(*tables, slot, gates, pick, h2, g_final, ys)


def _rope_tables(seq_len):
    pos = jnp.arange(seq_len, dtype=F32)[:, None]
    lane = np.arange(LANES)

    def table(half, lane_freq, first, active):
        inv = ROPE_BASE ** (-jnp.arange(half, dtype=F32) / half)
        ang = pos * inv[None, :]
        cos = jnp.cos(ang)[:, lane_freq]
        sin = jnp.sin(ang)[:, lane_freq]
        c = jnp.where(active[None, :], cos, 1.0)
        s = jnp.where(active[None, :], jnp.where(first[None, :], -sin, sin), 0.0)
        return c, s

    half_r = RET_DK // 2
    cr, sr = table(half_r, lane % half_r, (lane % RET_DK) < half_r, np.ones(LANES, bool))
    half_m = MLA_ROPE // 2
    rel = lane - MLA_NOPE
    active = (rel >= 0) & (rel < MLA_ROPE)
    cm, sm = table(half_m, np.where(active, rel % half_m, 0), active & (rel < half_m), active)
    return cr, sr, cm, sm


def kernel(x_prompt, x_sample, mem_prompt, mem_sample, norm_mix, w_in, ret_decay_fwd, ret_decay_bwd, ret_gn, q_a_norm, w_uq, kv_a_norm, w_ukv, w_mix_out, norm_cross, norm_mem, w_xq, w_xkv, w_xo, norm_ffn, router_w, router_b, w_gu, b_gu, w_down, b_down, norm_final):
    assert norm_mix.shape[0] == 1, "single layer"
    bp, sp, d = x_prompt.shape
    bs, ss, _ = x_sample.shape
    tp, ts = bp * sp, bs * ss
    t = tp + ts
    assert d == D_MODEL and sp % TOKEN_TILE == 0 and ss % TOKEN_TILE == 0 and tp % ss == 0 and sp >= ss
    mem_len = mem_prompt.shape[1]

    w_in0 = w_in[0]
    w_in_pad = jnp.zeros((D_MODEL, D_IN_PAD), F32)
    w_in_pad = w_in_pad.at[:, :2816].set(w_in0[:, :2816])
    w_in_pad = w_in_pad.at[:, 2816 + MLA_NOPE:2816 + MLA_NOPE + MLA_ROPE].set(w_in0[:, 2816:]).astype(BF16)
    wuq_pad = jnp.pad(w_uq[0].reshape(Q_LORA, MLA_HEADS, MLA_NOPE + MLA_ROPE),
                      ((0, 0), (0, 0), (0, LANES - MLA_NOPE - MLA_ROPE))).reshape(Q_LORA, MLA_HEADS * LANES).astype(BF16)
    wukv = w_ukv[0].reshape(KV_LORA, MLA_HEADS, MLA_NOPE + MLA_V)
    wuk_pad = jnp.pad(wukv[:, :, :MLA_NOPE], ((0, 0), (0, 0), (0, LANES - MLA_NOPE))).reshape(KV_LORA, MLA_HEADS * LANES).astype(BF16)
    wuv = wukv[:, :, MLA_NOPE:].reshape(KV_LORA, MLA_WIDTH).astype(BF16)
    rw_pad = jnp.pad(router_w[0], ((0, 0), (0, LANES - N_EXPERTS))).astype(BF16)
    rb_pad = jnp.pad(router_b[0].astype(F32), (0, LANES - N_EXPERTS), constant_values=NEG_BIG)[None, :]
    lgf = jnp.log1p(-jnp.exp2(ret_decay_fwd[0].astype(F32)))
    lgb = jnp.log1p(-jnp.exp2(ret_decay_bwd[0].astype(F32)))
    tabs = _rope_tables(sp)

    xp = x_prompt.reshape(tp, D_MODEL)
    xs = x_sample.reshape(ts, D_MODEL)
    mem = jnp.concatenate([mem_prompt.reshape(-1, D_MODEL), mem_sample.reshape(-1, D_MODEL)], axis=0)

    kvmem = _mem_kv(mem, norm_mem[0][None, :], w_xkv[0].astype(BF16)).reshape(bp + bs, mem_len, 2 * D_MODEL)

    rq, rk, rv, rg, qm, km, vm = _in_proj(xp, xs, norm_mix[0][None, :], w_in_pad, q_a_norm[0][None, :], wuq_pad,
                                          kv_a_norm[0][None, :], wuk_pad, wuv, tabs, sp, ss)

    gn = ret_gn[0][None, :]
    ret_p = _retention(lgf, lgb, rq, rk, rv, rg, gn, sp, bp, 0)
    ret_s = _retention(lgf, lgb, rq, rk, rv, rg, gn, ss, bs, tp // ss)
    mla_p = _mla_attn(qm, km, vm, sp, bp, 0)
    mla_s = _mla_attn(qm, km, vm, ss, bs, tp // ss)

    h2, xn, gates, slot, stats, counts = _mix_cross(
        xp, xs, ret_p, ret_s, mla_p, mla_s, w_mix_out[0].astype(BF16), norm_cross[0][None, :], w_xq[0].astype(BF16),
        kvmem, w_xo[0].astype(BF16), norm_ffn[0][None, :], rw_pad, rb_pad, sp, ss)

    used = counts[0, :N_EXPERTS]
    padded = (used + MOE_BLOCK - 1) // MOE_BLOCK * MOE_BLOCK
    pend = jnp.cumsum(padded)
    pstart = pend - padded
    tile_cnt, tile_before, tile_lstart = (stats[:, r, :N_EXPERTS] for r in range(3))
    n_tiles = stats.shape[0]
    tile_chunks = (tile_cnt + ROW_CHUNK - 1) // ROW_CHUNK
    tables = (tile_chunks.reshape(-1),
              tile_lstart.reshape(-1),
              (pstart[None, :] + tile_before).reshape(-1),
              jnp.sum(tile_chunks, axis=-1))
    ztables = (pstart + used, (padded - used) // ROW_CHUNK)
    n_blocks = -(-(t * TOP_K + N_EXPERTS * (n_tiles * (ROW_CHUNK - 1) + MOE_BLOCK - 1)) // MOE_BLOCK)
    blk = jnp.arange(n_blocks, dtype=jnp.int32)
    n_active = (pend[-1] // MOE_BLOCK).astype(jnp.int32)
    block_i = jnp.minimum(blk, n_active - 1)
    block_e = jnp.minimum(jnp.sum((block_i[:, None] * MOE_BLOCK >= pend[None, :]).astype(jnp.int32), axis=-1),
                          N_EXPERTS - 1).astype(jnp.int32)

    xs_sorted, pick = _dispatch(tables, ztables, xn, slot, n_blocks * MOE_BLOCK)
    ys = _experts(block_e, block_i, n_active[None], xs_sorted, w_gu[0], b_gu[0][:, None, :],
                  w_down[0], b_down[0][:, None, :])
    out_p, out_s = _combine(tables, slot, gates, pick, h2, norm_final[None, :], ys, tp, ts)
    return out_p.reshape(bp, sp, D_MODEL), out_s.reshape(bs, ss, D_MODEL)
```

```python
import functools
import math

import jax
import jax.numpy as jnp
import numpy as np
from jax import lax
from jax.experimental import pallas as pl
from jax.experimental.pallas import tpu as pltpu

D_MODEL = 1024
RET_HEADS = 8
RET_DK = 64
RET_WIDTH = 512
CHUNK = 128
MLA_HEADS = 8
MLA_NOPE = 64
MLA_ROPE = 32
MLA_V = 64
MLA_WIDTH = 512
Q_LORA = 512
KV_LORA = 256
ROPE_BASE = 10000.0
X_HEADS = 4
X_HEAD_DIM = 256
N_EXPERTS = 32
TOP_K = 4
SWIGLU_LIMIT = 7.0
SWIGLU_ALPHA = 1.702
MOE_BLOCK = 512
EPS = 1e-6

LANES = 128
VMEM_LIMIT = 56 * 1024 * 1024

TOKEN_TILE = 512
ROW_PARTS = 2
MIX_ROW_PARTS = 1
ATTN_TQ = 1024
ATTN_TK = 1024
ATTN_UNROLL = 4
RET_UNROLL = 8
ROW_CHUNK = 8
GROUP_BITS = (TOKEN_TILE // ROW_CHUNK).bit_length()
WAIT_CHUNKS = 16
LOCAL_STEP = 256
LOCAL_ROWS = -(-(TOKEN_TILE * 4 + 32 * (ROW_CHUNK - 1)) // LOCAL_STEP) * LOCAL_STEP
D_IN_PAD = 4 * 512 + Q_LORA + KV_LORA + LANES
NEG_BIG = -1e30

F32 = jnp.float32
BF16 = jnp.bfloat16


def _params(sem, vmem=VMEM_LIMIT):
    return pltpu.CompilerParams(dimension_semantics=sem, vmem_limit_bytes=vmem)


def _const_spec(shape):
    nd = len(shape)
    return pl.BlockSpec(shape, lambda *_: (0,) * nd, pipeline_mode=pl.Buffered(1))


def _rms(x, g):
    ms = jnp.mean(x * x, axis=-1, keepdims=True)
    return x * lax.rsqrt(ms + EPS) * g


def _dot(a, b):
    return jnp.dot(a, b, preferred_element_type=F32)


def _dot_nt(a, b):
    return lax.dot_general(a, b, (((1,), (1,)), ((), ())), preferred_element_type=F32)


def _dot_tn(a, b):
    return lax.dot_general(a, b, (((0,), (0,)), ((), ())), preferred_element_type=F32)


def _mem_kv_kernel(mem_ref, g_ref, w_ref, o_ref):
    mn = _rms(mem_ref[...], g_ref[...])
    o_ref[...] = _dot(mn.astype(BF16), w_ref[...]).astype(BF16)


def _mem_kv(mem, g, w):
    rows, mem_len = mem.shape[0], 256
    return pl.pallas_call(
        _mem_kv_kernel,
        out_shape=jax.ShapeDtypeStruct((rows, 2 * D_MODEL), BF16),
        grid=(rows // mem_len,),
        in_specs=[pl.BlockSpec((mem_len, D_MODEL), lambda i: (i, 0)),
                  _const_spec((1, D_MODEL)),
                  _const_spec((D_MODEL, 2 * D_MODEL))],
        out_specs=pl.BlockSpec((mem_len, 2 * D_MODEL), lambda i: (i, 0)),
        compiler_params=_params(("arbitrary",)),
        name="mem_kv",
    )(mem, g, w)


def _rope_slab(x, c, ss, first, shift_up, shift_down):
    swap = jnp.where(first, pltpu.roll(x, shift_up, 1), pltpu.roll(x, shift_down, 1))
    return x * c + swap * ss


def _in_proj_kernel(xp_ref, xs_ref, g_ref, win_ref, qan_ref, wuq_ref, kvan_ref, wuk_ref, wuv_ref,
                    cr_ref, sr_ref, cm_ref, sm_ref,
                    rq_ref, rk_ref, rv_ref, rg_ref, qm_ref, km_ref, vm_ref, *, n_prompt_tiles):
    i = pl.program_id(0)
    tm = rq_ref.shape[0]
    lane = lax.broadcasted_iota(jnp.int32, (1, LANES), 1)
    ret_first = (lane % RET_DK) < (RET_DK // 2)
    mla_first = (lane >= MLA_NOPE) & (lane < MLA_NOPE + MLA_ROPE // 2)
    half_r, half_m = RET_DK // 2, MLA_ROPE // 2
    q_scale = (MLA_NOPE + MLA_ROPE) ** -0.5 * math.log2(math.e)

    for part in range(ROW_PARTS):
        rows = slice(part * (tm // ROW_PARTS), (part + 1) * (tm // ROW_PARTS))
        x = jnp.where(i < n_prompt_tiles, xp_ref[rows, :], xs_ref[rows, :])
        xn = _rms(x, g_ref[...])
        proj = _dot(xn.astype(BF16), win_ref[...])
        cr, sr, cm, sm = cr_ref[rows, :], sr_ref[rows, :], cm_ref[rows, :], sm_ref[rows, :]

        for s in range(RET_WIDTH // LANES):
            lo = s * LANES
            q = _rope_slab(proj[:, lo:lo + LANES], cr, sr, ret_first, LANES - half_r, half_r)
            rq_ref[rows, lo:lo + LANES] = q.astype(BF16)
            k = _rope_slab(proj[:, 512 + lo:512 + lo + LANES], cr, sr, ret_first, LANES - half_r, half_r)
            rk_ref[rows, lo:lo + LANES] = (k * (RET_DK ** -0.5)).astype(BF16)
        rv_ref[rows, :] = proj[:, 1024:1536].astype(BF16)
        rg_ref[rows, :] = proj[:, 1536:2048].astype(BF16)

        cq = _rms(proj[:, 2048:2048 + Q_LORA], qan_ref[...])
        qm = _dot(cq.astype(BF16), wuq_ref[...])
        ckv = _rms(proj[:, 2560:2560 + KV_LORA], kvan_ref[...]).astype(BF16)
        kn = _dot(ckv, wuk_ref[...])
        vm_ref[rows, :] = _dot(ckv, wuv_ref[...]).astype(BF16)
        kr = proj[:, 2816:2816 + LANES]
        kpe = _rope_slab(kr, cm, sm, mla_first, LANES - half_m, half_m)
        for h in range(MLA_HEADS):
            lo = h * LANES
            qh = _rope_slab(qm[:, lo:lo + LANES], cm, sm, mla_first, LANES - half_m, half_m)
            qm_ref[rows, lo:lo + LANES] = (qh * q_scale).astype(BF16)
            km_ref[rows, lo:lo + LANES] = (kn[:, lo:lo + LANES] + kpe).astype(BF16)


def _in_proj(xp, xs, g, w_in_pad, qan, wuq_pad, kvan, wuk_pad, wuv, tabs, sp, ss):
    tm = TOKEN_TILE
    tp, ts = xp.shape[0], xs.shape[0]
    npt, nst = tp // tm, ts // tm
    t = tp + ts
    tiles_p, tiles_s = sp // tm, ss // tm

    def xp_map(i):
        return (jnp.minimum(i, npt - 1), 0)

    def xs_map(i):
        return (jnp.maximum(i - npt, 0), 0)

    def tab_map(i):
        return (jnp.where(i < npt, i % tiles_p, (i - npt) % tiles_s), 0)

    tok = lambda w: pl.BlockSpec((tm, w), lambda i: (i, 0))
    widths = (512, 512, 512, 512, 1024, 1024, 512)
    return pl.pallas_call(
        functools.partial(_in_proj_kernel, n_prompt_tiles=npt),
        out_shape=[jax.ShapeDtypeStruct((t, w), BF16) for w in widths],
        grid=(npt + nst,),
        in_specs=[pl.BlockSpec((tm, D_MODEL), xp_map), pl.BlockSpec((tm, D_MODEL), xs_map),
                  _const_spec((1, D_MODEL)), _const_spec((D_MODEL, D_IN_PAD)),
                  _const_spec((1, Q_LORA)), _const_spec((Q_LORA, MLA_HEADS * LANES)),
                  _const_spec((1, KV_LORA)), _const_spec((KV_LORA, MLA_HEADS * LANES)),
                  _const_spec((KV_LORA, MLA_WIDTH))]
                 + [pl.BlockSpec((tm, LANES), tab_map)] * 4,
        out_specs=[tok(w) for w in widths],
        compiler_params=_params(("arbitrary",)),
        name="in_proj",
    )(xp, xs, g, w_in_pad, qan, wuq_pad, kvan, wuk_pad, wuv, *tabs)


def _retention_kernel(lgf_ref, lgb_ref, q_ref, k_ref, v_ref, g_ref, gn_ref, o_ref, kvf_scr, kvb_scr, st_scr, *, n_chunks):
    hp = pl.program_id(1)
    c = CHUNK
    lane = lax.broadcasted_iota(jnp.int32, (1, c), 1)
    row = lax.broadcasted_iota(jnp.int32, (c, 1), 0)
    lane_h0 = lane < RET_DK
    row_h0 = row < RET_DK
    lgf0, lgf1 = lgf_ref[2 * hp], lgf_ref[2 * hp + 1]
    lgb0, lgb1 = lgb_ref[2 * hp], lgb_ref[2 * hp + 1]
    lgf_lane = jnp.where(lane_h0, lgf0, lgf1)
    lgb_lane = jnp.where(lane_h0, lgb0, lgb1)
    t = row.astype(F32)
    q_dec_f = jnp.exp((t + 1.0) * lgf_lane)
    q_dec_b = jnp.exp((c - t) * lgb_lane)
    k_dec_f = jnp.exp((c - 1.0 - t) * lgf_lane)
    k_dec_b = jnp.exp(t * lgb_lane)
    chunk_dec_f = jnp.exp(c * jnp.where(row_h0, lgf0, lgf1))
    chunk_dec_b = jnp.exp(c * jnp.where(row_h0, lgb0, lgb1))
    same_head = row_h0 == lane_h0
    diff = t - lane.astype(F32)
    d_intra = []
    for lgf, lgb in ((lgf0, lgb0), (lgf1, lgb1)):
        fwd = jnp.where(diff >= 0, jnp.exp(jnp.where(diff >= 0, diff, 0.0) * lgf), 0.0)
        bwd = jnp.where(diff < 0, jnp.exp(jnp.where(diff < 0, -diff, 0.0) * lgb), 0.0)
        d_intra.append(fwd + bwd)
    avg = jnp.where(same_head, 1.0 / RET_DK, 0.0).astype(BF16)
    gn = gn_ref[...]

    def chunk(n):
        return pl.ds(pl.multiple_of(n * c, c), c)

    unroll = min(RET_UNROLL, n_chunks)

    def chunk_kv(n, _):
        k = k_ref[chunk(n), :].astype(F32)
        kd = jnp.concatenate([(k * k_dec_f).astype(BF16), (k * k_dec_b).astype(BF16)], axis=-1)
        kv = _dot_tn(kd, v_ref[chunk(n), :])
        kvf_scr[n] = jnp.where(same_head, kv[0:c], 0.0)
        kvb_scr[n] = jnp.where(same_head, kv[c:2 * c], 0.0)
        return 0

    lax.fori_loop(0, n_chunks, chunk_kv, 0, unroll=unroll)

    def fwd_state(n, s_f):
        st_scr[n, 0:c, :] = s_f.astype(BF16)
        return chunk_dec_f * s_f + kvf_scr[n]

    lax.fori_loop(0, n_chunks, fwd_state, jnp.zeros((c, c), F32))

    def bwd_state(j, s_b):
        n = n_chunks - 1 - j
        st_scr[n, c:2 * c, :] = s_b.astype(BF16)
        return chunk_dec_b * s_b + kvb_scr[n]

    lax.fori_loop(0, n_chunks, bwd_state, jnp.zeros((c, c), F32))

    d_both = jnp.concatenate(d_intra, axis=0)
    zero_q = jnp.zeros((c, c), BF16)

    def chunk_out(n):
        q = q_ref[chunk(n), :]
        qf = q.astype(F32)
        q_both = jnp.concatenate([(qf * q_dec_f).astype(BF16), (qf * q_dec_b).astype(BF16)], axis=-1)
        q_heads = jnp.concatenate([jnp.where(lane_h0, q, zero_q), jnp.where(lane_h0, zero_q, q)], axis=0)
        p = (_dot_nt(q_heads, k_ref[chunk(n), :]) * d_both).astype(BF16)
        inner = _dot(p, v_ref[chunk(n), :])
        return _dot(q_both, st_scr[n]) + jnp.where(lane_h0, inner[0:c], inner[c:2 * c])

    def split_rows(a):
        hi = a.astype(BF16)
        return jnp.concatenate([hi, (a - hi.astype(F32)).astype(BF16)], axis=0)

    def out_group(gi, _):
        rows = pl.ds(pl.multiple_of(gi * (unroll * c), unroll * c), unroll * c)
        y = jnp.concatenate([chunk_out(gi * unroll + u) for u in range(unroll)], axis=0)
        m = y.shape[0]
        mu2 = _dot(split_rows(y), avg)
        d = y - (mu2[0:m] + mu2[m:2 * m])
        var2 = _dot(split_rows(d * d), avg)
        yn = d * lax.rsqrt(var2[0:m] + var2[m:2 * m] + EPS) * gn
        gate = g_ref[rows, :].astype(F32)
        o_ref[rows, :] = (yn * (gate * jax.nn.sigmoid(gate))).astype(BF16)
        return 0

    lax.fori_loop(0, n_chunks // unroll, out_group, 0)


def _retention(lgf, lgb, rq, rk, rv, rg, gn, seq_len, n_seq, row_block0):
    n_chunks = seq_len // CHUNK
    hp_count = RET_WIDTH // LANES
    blk = pl.BlockSpec((seq_len, LANES), lambda b, hp, *_: (row_block0 + b, hp))
    return pl.pallas_call(
        functools.partial(_retention_kernel, n_chunks=n_chunks),
        out_shape=jax.ShapeDtypeStruct((n_seq * seq_len, RET_WIDTH), BF16),
        grid_spec=pltpu.PrefetchScalarGridSpec(
            num_scalar_prefetch=2,
            grid=(n_seq, hp_count),
            in_specs=[blk, blk, blk, blk, pl.BlockSpec((1, LANES), lambda b, hp, *_: (0, hp))],
            out_specs=pl.BlockSpec((seq_len, LANES), lambda b, hp, *_: (b, hp)),
            scratch_shapes=[pltpu.VMEM((n_chunks, CHUNK, CHUNK), F32), pltpu.VMEM((n_chunks, CHUNK, CHUNK), F32),
                            pltpu.VMEM((n_chunks, 2 * CHUNK, CHUNK), BF16)]),
        compiler_params=_params(("arbitrary", "arbitrary")),
        name="retention",
    )(lgf, lgb, rq, rk, rv, rg, gn)


def _mla_attn_kernel(q_ref, k_ref, v_ref, o_ref, *, tk, n_kv):
    tq = q_ref.shape[0]
    lane = lax.broadcasted_iota(jnp.int32, (1, LANES), 1)
    qs = (q_ref[:, :LANES], q_ref[:, LANES:])

    def body(j, carry):
        rows = pl.ds(pl.multiple_of(j * tk, tk), tk)
        v = v_ref[rows, :]
        new = []
        for h in range(2):
            m, l, acc = carry[h]
            s = _dot_nt(qs[h], k_ref[rows, h * LANES:(h + 1) * LANES])
            m_new = jnp.maximum(m, jnp.max(s, axis=-1, keepdims=True))
            alpha = jnp.exp2(m - m_new)
            p = jnp.exp2(s - m_new)
            l = alpha * l + jnp.sum(p, axis=-1, keepdims=True)
            acc = alpha * acc + _dot(p.astype(BF16), v)
            new.append((m_new, l, acc))
        return tuple(new)

    init = (jnp.full((tq, 1), NEG_BIG, F32), jnp.zeros((tq, 1), F32), jnp.zeros((tq, LANES), F32))
    (_, l0, acc0), (_, l1, acc1) = lax.fori_loop(0, n_kv, body, (init, init), unroll=min(ATTN_UNROLL, n_kv))
    o_ref[...] = jnp.where(lane < MLA_V, acc0 / l0, acc1 / l1).astype(BF16)


def _mla_attn(qm, km, vm, seq_len, n_seq, row_block0):
    tq, tk = min(ATTN_TQ, seq_len), min(ATTN_TK, seq_len)
    hp_count = MLA_HEADS // 2
    nq = seq_len // tq
    return pl.pallas_call(
        functools.partial(_mla_attn_kernel, tk=tk, n_kv=seq_len // tk),
        out_shape=jax.ShapeDtypeStruct((n_seq * seq_len, MLA_WIDTH), BF16),
        grid=(n_seq, hp_count, nq),
        in_specs=[pl.BlockSpec((tq, 2 * LANES), lambda b, hp, i: ((row_block0 + b) * nq + i, hp)),
                  pl.BlockSpec((seq_len, 2 * LANES), lambda b, hp, i: (row_block0 + b, hp)),
                  pl.BlockSpec((seq_len, LANES), lambda b, hp, i: (row_block0 + b, hp))],
        out_specs=pl.BlockSpec((tq, LANES), lambda b, hp, i: (b * nq + i, hp)),
        compiler_params=_params(("arbitrary", "arbitrary", "arbitrary")),
        name="mla_attn",
    )(qm, km, vm)


def _mix_cross_kernel(xp_ref, xs_ref, rp_ref, rs_ref, mp_ref, ms_ref, wmix_ref, gx_ref, wxq_ref, kv_ref, wxo_ref,
                      gf_ref, rw_ref, rb_ref,
                      h_ref, xn_ref, gate_ref, slot_ref, stat_ref, cnt_ref, carry_scr, *, n_prompt_tiles):
    i = pl.program_id(0)
    tm = h_ref.shape[0]
    is_p = i < n_prompt_tiles
    lane = lax.broadcasted_iota(jnp.int32, (tm, LANES), 1)

    def part_rows(rows):
        x = jnp.where(is_p, xp_ref[rows, :], xs_ref[rows, :])
        ret = jnp.where(is_p, rp_ref[rows, :], rs_ref[rows, :])
        mla = jnp.where(is_p, mp_ref[rows, :], ms_ref[rows, :])
        h1 = x + _dot(ret, wmix_ref[0:RET_WIDTH, :]) + _dot(mla, wmix_ref[RET_WIDTH:, :])

        hn = _rms(h1, gx_ref[...]).astype(BF16)
        q = (_dot(hn, wxq_ref[...]) * (X_HEAD_DIM ** -0.5)).astype(BF16)
        heads = []
        for h in range(X_HEADS):
            lo = h * X_HEAD_DIM
            s = _dot_nt(q[:, lo:lo + X_HEAD_DIM], kv_ref[0, :, lo:lo + X_HEAD_DIM])
            e = jnp.exp(s - jnp.max(s, axis=-1, keepdims=True))
            p = (e / jnp.sum(e, axis=-1, keepdims=True)).astype(BF16)
            heads.append(_dot(p, kv_ref[0, :, D_MODEL + lo:D_MODEL + lo + X_HEAD_DIM]).astype(BF16))
        h2 = h1 + _dot(jnp.concatenate(heads, axis=-1), wxo_ref[...])
        h_ref[rows, :] = h2

        xn = _rms(h2, gf_ref[...]).astype(BF16)
        xn_ref[rows, :] = xn
        work = _dot(xn, rw_ref[...]) + rb_ref[...]
        lane_f = lax.broadcasted_iota(jnp.int32, work.shape, 1).astype(F32)
        vals, picks = [], []
        for _ in range(TOP_K):
            m = jnp.max(work, axis=-1, keepdims=True)
            idx = jnp.min(jnp.where(work == m, lane_f, float(LANES)), axis=-1, keepdims=True)
            sel = lane_f == idx
            work = jnp.where(sel, -jnp.inf, work)
            vals.append(m)
            picks.append(jnp.where(sel, 1.0, 0.0))
        exps = [jnp.exp(v - vals[0]) for v in vals]
        denom = exps[0] + exps[1] + exps[2] + exps[3]
        return picks, [e / denom for e in exps]

    part = tm // MIX_ROW_PARTS
    parts = [part_rows(slice(r * part, (r + 1) * part)) for r in range(MIX_ROW_PARTS)]
    picks = [jnp.concatenate([p[0][k] for p in parts], axis=0) for k in range(TOP_K)]
    gates = [jnp.concatenate([p[1][k] for p in parts], axis=0) for k in range(TOP_K)]
    onehot = picks[0] + picks[1] + picks[2] + picks[3]


    @pl.when(i == 0)
    def _():
        carry_scr[...] = jnp.zeros_like(carry_scr)

    r_iota = lax.broadcasted_iota(jnp.int32, (tm, tm), 0)
    c_iota = lax.broadcasted_iota(jnp.int32, (tm, tm), 1)
    lower = jnp.where(c_iota < r_iota, 1.0, 0.0).astype(BF16)
    before = _dot(lower, onehot.astype(BF16))
    tile_cnt = jnp.sum(onehot, axis=0, keepdims=True)
    group_rows = jnp.floor((tile_cnt + (ROW_CHUNK - 1)) * (1.0 / ROW_CHUNK)) * ROW_CHUNK
    carry_before = carry_scr[...]
    carry = carry_before + group_rows
    carry_scr[...] = carry
    cnt_ref[...] = carry.astype(jnp.int32)
    e_row = lax.broadcasted_iota(jnp.int32, (LANES, LANES), 0)
    e_col = lax.broadcasted_iota(jnp.int32, (LANES, LANES), 1)
    earlier = jnp.where(e_row < e_col, 1.0, 0.0).astype(BF16)
    group_start = _dot(jnp.broadcast_to(group_rows, (8, LANES)).astype(BF16), earlier)[0:1, :]
    row8 = lax.broadcasted_iota(jnp.int32, (8, LANES), 0)
    stats = jnp.where(row8 == 0, tile_cnt, jnp.where(row8 == 1, carry_before, jnp.where(row8 == 2, group_start, 0.0)))
    stat_ref[0] = stats.astype(jnp.int32)

    place = before + group_start
    gate_out = jnp.zeros((tm, LANES), F32)
    slot_out = jnp.zeros((tm, LANES), F32)
    for k in range(TOP_K):
        slot_k = jnp.sum(picks[k] * place, axis=-1, keepdims=True)
        gate_out = jnp.where(lane == k, gates[k], gate_out)
        slot_out = jnp.where(lane == k, slot_k, slot_out)
    gate_ref[...] = gate_out
    slot_ref[...] = slot_out.astype(jnp.int32)


def _mix_cross(xp, xs, ret_p, ret_s, mla_p, mla_s, wmix, gx, wxq, kvmem, wxo, gf, rw_pad, rb_pad, sp, ss):
    tm = TOKEN_TILE
    tp, ts = xp.shape[0], xs.shape[0]
    npt, nst = tp // tm, ts // tm
    t = tp + ts
    n_seq_p = tp // sp
    mem_len = kvmem.shape[1]

    def p_map(i):
        return (jnp.minimum(i, npt - 1), 0)

    def s_map(i):
        return (jnp.maximum(i - npt, 0), 0)

    def kv_map(i):
        return (jnp.where(i < npt, i // (sp // tm), n_seq_p + (i - npt) // (ss // tm)), 0, 0)

    tok = lambda w: pl.BlockSpec((tm, w), lambda i: (i, 0))
    return pl.pallas_call(
        functools.partial(_mix_cross_kernel, n_prompt_tiles=npt),
        out_shape=[jax.ShapeDtypeStruct((t, D_MODEL), F32), jax.ShapeDtypeStruct((t, D_MODEL), BF16),
                   jax.ShapeDtypeStruct((t, LANES), F32), jax.ShapeDtypeStruct((t, LANES), jnp.int32),
                   jax.ShapeDtypeStruct((npt + nst, 8, LANES), jnp.int32), jax.ShapeDtypeStruct((1, LANES), jnp.int32)],
        grid=(npt + nst,),
        in_specs=[pl.BlockSpec((tm, D_MODEL), p_map), pl.BlockSpec((tm, D_MODEL), s_map),
                  pl.BlockSpec((tm, RET_WIDTH), p_map), pl.BlockSpec((tm, RET_WIDTH), s_map),
                  pl.BlockSpec((tm, MLA_WIDTH), p_map), pl.BlockSpec((tm, MLA_WIDTH), s_map),
                  _const_spec((D_MODEL, D_MODEL)), _const_spec((1, D_MODEL)), _const_spec((D_MODEL, D_MODEL)),
                  pl.BlockSpec((1, mem_len, 2 * D_MODEL), kv_map),
                  _const_spec((D_MODEL, D_MODEL)), _const_spec((1, D_MODEL)),
                  _const_spec((D_MODEL, LANES)), _const_spec((1, LANES))],
        out_specs=[tok(D_MODEL), tok(D_MODEL), tok(LANES), tok(LANES),
                   pl.BlockSpec((1, 8, LANES), lambda i: (i, 0, 0)),
                   pl.BlockSpec((1, LANES), lambda i: (0, 0))],
        scratch_shapes=[pltpu.VMEM((1, LANES), F32)],
        compiler_params=_params(("arbitrary",)),
        name="mix_cross_router",
    )(xp, xs, ret_p, ret_s, mla_p, mla_s, wmix, gx, wxq, kvmem, wxo, gf, rw_pad, rb_pad)


def _to_rows(a):
    eye = (lax.broadcasted_iota(jnp.int32, (LANES, LANES), 0)
           == lax.broadcasted_iota(jnp.int32, (LANES, LANES), 1)).astype(BF16)
    hi = a.astype(BF16)
    r1 = a - hi.astype(F32)
    mid = r1.astype(BF16)
    lo = (r1 - mid.astype(F32)).astype(BF16)
    return _dot_nt(eye, hi) + _dot_nt(eye, mid) + _dot_nt(eye, lo)


def _chunk_copies(tile, nch_ref, lstart_ref, gstart_ref, local, hbm, sem, to_hbm, act):
    def per_expert(e, _):
        base = tile * N_EXPERTS + e
        n = nch_ref[base]
        ls = lstart_ref[base]
        gs = gstart_ref[base]
        def bit_copy(j):
            rows = ROW_CHUNK << j

            @pl.when(((n >> j) & 1) == 1)
            def _():
                off = (n & ((1 << j) - 1)) * ROW_CHUNK
                l_rows = local.at[pl.ds(pl.multiple_of(ls + off, ROW_CHUNK), rows)]
                g_rows = hbm.at[pl.ds(pl.multiple_of(gs + off, ROW_CHUNK), rows)]
                act(pltpu.make_async_copy(l_rows, g_rows, sem) if to_hbm else pltpu.make_async_copy(g_rows, l_rows, sem))

        for j in range(GROUP_BITS):
            bit_copy(j)
        return 0

    lax.fori_loop(0, N_EXPERTS, per_expert, 0)


def _wait_tile(tile, tot_ref, local, hbm, sem, to_hbm):
    total = tot_ref[tile]

    def wait_rows(rows):
        l_rows = local.at[pl.ds(0, rows)]
        g_rows = hbm.at[pl.ds(0, rows)]
        (pltpu.make_async_copy(l_rows, g_rows, sem) if to_hbm else pltpu.make_async_copy(g_rows, l_rows, sem)).wait()

    def big(_, c):
        wait_rows(ROW_CHUNK * WAIT_CHUNKS)
        return c

    def small(_, c):
        wait_rows(ROW_CHUNK)
        return c

    lax.fori_loop(0, total // WAIT_CHUNKS, big, 0)
    lax.fori_loop(0, total % WAIT_CHUNKS, small, 0)


def _start(cp):
    cp.start()


def _wait(cp):
    cp.wait()


def _dispatch_kernel(nch_ref, lstart_ref, gstart_ref, tot_ref, zstart_ref, znch_ref,
                     x_ref, slot_ref, xs_hbm, pick_ref, buf, zero_scr, sem, zsem):
    i = pl.program_id(0)
    last = pl.num_programs(0) - 1
    tm = x_ref.shape[0]
    b = i % 2
    slot_rows = _to_rows(slot_ref[...].astype(F32))
    x = x_ref[...]

    def build(s, _):
        r0 = pl.multiple_of(s * LOCAL_STEP, LOCAL_STEP)
        rid = (lax.broadcasted_iota(jnp.int32, (LOCAL_STEP, tm), 0) + r0).astype(F32)
        hit = rid == slot_rows[0:1, :]
        for k in range(1, TOP_K):
            hit = hit | (rid == slot_rows[k:k + 1, :])
        pick = jnp.where(hit, 1.0, 0.0).astype(BF16)
        pick_ref[pl.ds(r0, LOCAL_STEP), :] = pick
        buf[b, pl.ds(r0, LOCAL_STEP), :] = _dot(pick, x)
        return 0

    lax.fori_loop(0, LOCAL_ROWS // LOCAL_STEP, build, 0, unroll=3)

    _chunk_copies(i, nch_ref, lstart_ref, gstart_ref, buf.at[b], xs_hbm, sem.at[b], True, _start)

    @pl.when(i > 0)
    def _():
        _wait_tile(i - 1, tot_ref, buf.at[1 - b], xs_hbm, sem.at[1 - b], True)

    @pl.when(i == last)
    def _():
        _wait_tile(i, tot_ref, buf.at[b], xs_hbm, sem.at[b], True)
        zero_scr[...] = jnp.zeros_like(zero_scr)

        def fill(act):
            def per_expert(e, _):
                def per_chunk(c, _):
                    row = pl.multiple_of(zstart_ref[e] + c * ROW_CHUNK, ROW_CHUNK)
                    act(pltpu.make_async_copy(zero_scr, xs_hbm.at[pl.ds(row, ROW_CHUNK)], zsem))
                    return 0

                lax.fori_loop(0, znch_ref[e], per_chunk, 0)
                return 0

            lax.fori_loop(0, N_EXPERTS, per_expert, 0)

        fill(_start)
        fill(_wait)


def _dispatch(tables, ztables, xn, slot, n_rows):
    tm = TOKEN_TILE
    t = xn.shape[0]
    return pl.pallas_call(
        _dispatch_kernel,
        out_shape=[jax.ShapeDtypeStruct((n_rows, D_MODEL), F32),
                   jax.ShapeDtypeStruct((t // tm * LOCAL_ROWS, tm), BF16)],
        grid_spec=pltpu.PrefetchScalarGridSpec(
            num_scalar_prefetch=6,
            grid=(t // tm,),
            in_specs=[pl.BlockSpec((tm, D_MODEL), lambda i, *_: (i, 0)),
                      pl.BlockSpec((tm, LANES), lambda i, *_: (i, 0))],
            out_specs=[pl.BlockSpec(memory_space=pl.ANY),
                       pl.BlockSpec((LOCAL_ROWS, tm), lambda i, *_: (i, 0))],
            scratch_shapes=[pltpu.VMEM((2, LOCAL_ROWS, D_MODEL), F32), pltpu.VMEM((ROW_CHUNK, D_MODEL), F32),
                            pltpu.SemaphoreType.DMA((2,)), pltpu.SemaphoreType.DMA(())]),
        compiler_params=_params(("arbitrary",)),
        name="moe_dispatch",
    )(*tables, *ztables, xn, slot)


def _experts_kernel(be_ref, bi_ref, na_ref, first_ref, seg_ref, nxt_ref, x_ref, bgu_ref, bd_ref, wgu_hbm, wd_hbm, o_ref,
                    wgu_f32, wd_f32, wgu_scr, wd_scr, sem):
    j = pl.program_id(0)
    active = j < na_ref[0]

    def weight_copies(e, slot):
        return (pltpu.make_async_copy(wgu_hbm.at[e], wgu_f32.at[slot], sem.at[slot, 0]),
                pltpu.make_async_copy(wd_hbm.at[e], wd_f32.at[slot], sem.at[slot, 1]))

    @pl.when(j == 0)
    def _():
        for cp in weight_copies(be_ref[0], 0):
            cp.start()

    @pl.when(active & (first_ref[j] == 1))
    def _():
        slot = seg_ref[j] % 2
        for cp in weight_copies(be_ref[j], slot):
            cp.wait()
        wgu_scr[...] = wgu_f32[slot].astype(BF16)
        wd_scr[...] = wd_f32[slot].astype(BF16)

        @pl.when(nxt_ref[j] >= 0)
        def _():
            for cp in weight_copies(nxt_ref[j], 1 - slot):
                cp.start()

    @pl.when(active)
    def _():
        x = x_ref[...].astype(BF16)
        gu = _dot(x, wgu_scr[...]) + bgu_ref[0]
        gate = jnp.minimum(gu[:, :D_MODEL], SWIGLU_LIMIT)
        up = jnp.clip(gu[:, D_MODEL:], -SWIGLU_LIMIT, SWIGLU_LIMIT)
        hid = (up + 1.0) * (gate * jax.nn.sigmoid(SWIGLU_ALPHA * gate))
        o_ref[...] = _dot(hid.astype(BF16), wd_scr[...]) + bd_ref[0]


def _experts(block_tables, xs, wgu, bgu, wd, bd):
    n_blocks = xs.shape[0] // MOE_BLOCK
    row_map = lambda j, be, bi, *_: (bi[j], 0)
    e_map = lambda j, be, *_: (be[j], 0, 0)
    return pl.pallas_call(
        _experts_kernel,
        out_shape=jax.ShapeDtypeStruct(xs.shape, F32),
        grid_spec=pltpu.PrefetchScalarGridSpec(
            num_scalar_prefetch=6,
            grid=(n_blocks,),
            in_specs=[pl.BlockSpec((MOE_BLOCK, D_MODEL), row_map),
                      pl.BlockSpec((1, 1, 2 * D_MODEL), e_map),
                      pl.BlockSpec((1, 1, D_MODEL), e_map),
                      pl.BlockSpec(memory_space=pl.ANY),
                      pl.BlockSpec(memory_space=pl.ANY)],
            out_specs=pl.BlockSpec((MOE_BLOCK, D_MODEL), row_map),
            scratch_shapes=[pltpu.VMEM((2, D_MODEL, 2 * D_MODEL), F32), pltpu.VMEM((2, D_MODEL, D_MODEL), F32),
                            pltpu.VMEM((D_MODEL, 2 * D_MODEL), BF16), pltpu.VMEM((D_MODEL, D_MODEL), BF16),
                            pltpu.SemaphoreType.DMA((2, 2))]),
        compiler_params=_params(("arbitrary",)),
        name="moe_experts",
    )(*block_tables, xs, bgu, bd, wgu, wd)


def _combine_kernel(nch_ref, lstart_ref, gstart_ref, tot_ref, slot_ref, gate_ref, pick_ref, h_ref, g_ref, ys_hbm,
                    op_ref, os_ref, buf, scaled_scr, sem, *, n_prompt_tiles):
    i = pl.program_id(0)
    n = pl.num_programs(0)
    tm = h_ref.shape[0]
    b = i % 2

    @pl.when(i == 0)
    def _():
        buf[...] = jnp.zeros_like(buf)
        _chunk_copies(0, nch_ref, lstart_ref, gstart_ref, buf.at[0], ys_hbm, sem.at[0], False, _start)

    @pl.when(i + 1 < n)
    def _():
        _chunk_copies(i + 1, nch_ref, lstart_ref, gstart_ref, buf.at[1 - b], ys_hbm, sem.at[1 - b], False, _start)

    _wait_tile(i, tot_ref, buf.at[b], ys_hbm, sem.at[b], False)

    slot = slot_ref[...].astype(F32)
    info = gate_ref[...] + pltpu.roll(slot, TOP_K, 1)
    info_hi = info.astype(BF16)
    info2 = jnp.concatenate([info_hi, (info - info_hi.astype(F32)).astype(BF16)], axis=-1)
    lane = lax.broadcasted_iota(jnp.int32, (LOCAL_STEP, LANES), 1)

    def step(s, _):
        r0 = pl.multiple_of(s * LOCAL_STEP, LOCAL_STEP)
        pick = pick_ref[pl.ds(r0, LOCAL_STEP), :]
        own2 = _dot(pick, info2)
        own = own2[:, :LANES] + own2[:, LANES:]
        rcol = (lax.broadcasted_iota(jnp.int32, (LOCAL_STEP, 1), 0) + r0).astype(F32)
        mine = jnp.where((own == rcol) & (lane >= TOP_K) & (lane < 2 * TOP_K), 1.0, 0.0)
        g_col = jnp.sum(own * pltpu.roll(mine, LANES - TOP_K, 1), axis=-1, keepdims=True)
        rows = buf[b, pl.ds(r0, LOCAL_STEP), :]
        scaled_scr[pl.ds(r0, LOCAL_STEP), :] = jnp.where(g_col != 0.0, rows * g_col, 0.0).astype(BF16)
        return 0

    lax.fori_loop(0, LOCAL_ROWS // LOCAL_STEP, step, 0, unroll=3)
    y = _dot_tn(pick_ref[...], scaled_scr[...])
    out = _rms(h_ref[...] + y, g_ref[...])

    @pl.when(i < n_prompt_tiles)
    def _():
        op_ref[...] = out

    @pl.when(i >= n_prompt_tiles)
    def _():
        os_ref[...] = out


def _combine(tables, slot, gates, pick, h2, g_final, ys, tp, ts):
    tm = TOKEN_TILE
    npt, nst = tp // tm, ts // tm
    tok = lambda w: pl.BlockSpec((tm, w), lambda i, *_: (i, 0))
    return pl.pallas_call(
        functools.partial(_combine_kernel, n_prompt_tiles=npt),
        out_shape=[jax.ShapeDtypeStruct((tp, D_MODEL), F32), jax.ShapeDtypeStruct((ts, D_MODEL), F32)],
        grid_spec=pltpu.PrefetchScalarGridSpec(
            num_scalar_prefetch=4,
            grid=(npt + nst,),
            in_specs=[tok(LANES), tok(LANES), pl.BlockSpec((LOCAL_ROWS, tm), lambda i, *_: (i, 0)), tok(D_MODEL),
                      pl.BlockSpec((1, D_MODEL), lambda i, *_: (0, 0)),
                      pl.BlockSpec(memory_space=pl.ANY)],
            out_specs=[pl.BlockSpec((tm, D_MODEL), lambda i, *_: (jnp.minimum(i, npt - 1), 0)),
                       pl.BlockSpec((tm, D_MODEL), lambda i, *_: (jnp.maximum(i - npt, 0), 0))],
            scratch_shapes=[pltpu.VMEM((2, LOCAL_ROWS, D_MODEL), F32), pltpu.VMEM((LOCAL_ROWS, D_MODEL), BF16),
                            pltpu.SemaphoreType.DMA((2,))]),
        compiler_params=_params(("arbitrary",)),
        name="moe_combine_norm",
    )(*tables, slot, gates, pick, h2, g_final, ys)


def _rope_tables(seq_len):
    pos = jnp.arange(seq_len, dtype=F32)[:, None]
    lane = np.arange(LANES)

    def table(half, lane_freq, first, active):
        inv = ROPE_BASE ** (-jnp.arange(half, dtype=F32) / half)
        ang = pos * inv[None, :]
        cos = jnp.cos(ang)[:, lane_freq]
        sin = jnp.sin(ang)[:, lane_freq]
        c = jnp.where(active[None, :], cos, 1.0)
        s = jnp.where(active[None, :], jnp.where(first[None, :], -sin, sin), 0.0)
        return c, s

    half_r = RET_DK // 2
    cr, sr = table(half_r, lane % half_r, (lane % RET_DK) < half_r, np.ones(LANES, bool))
    half_m = MLA_ROPE // 2
    rel = lane - MLA_NOPE
    active = (rel >= 0) & (rel < MLA_ROPE)
    cm, sm = table(half_m, np.where(active, rel % half_m, 0), active & (rel < half_m), active)
    return cr, sr, cm, sm


def kernel(x_prompt, x_sample, mem_prompt, mem_sample, norm_mix, w_in, ret_decay_fwd, ret_decay_bwd, ret_gn, q_a_norm, w_uq, kv_a_norm, w_ukv, w_mix_out, norm_cross, norm_mem, w_xq, w_xkv, w_xo, norm_ffn, router_w, router_b, w_gu, b_gu, w_down, b_down, norm_final):
    assert norm_mix.shape[0] == 1, "single layer"
    bp, sp, d = x_prompt.shape
    bs, ss, _ = x_sample.shape
    tp, ts = bp * sp, bs * ss
    t = tp + ts
    assert d == D_MODEL and sp % TOKEN_TILE == 0 and ss % TOKEN_TILE == 0 and tp % ss == 0 and sp >= ss
    mem_len = mem_prompt.shape[1]

    w_in0 = w_in[0]
    w_in_pad = jnp.zeros((D_MODEL, D_IN_PAD), F32)
    w_in_pad = w_in_pad.at[:, :2816].set(w_in0[:, :2816])
    w_in_pad = w_in_pad.at[:, 2816 + MLA_NOPE:2816 + MLA_NOPE + MLA_ROPE].set(w_in0[:, 2816:]).astype(BF16)
    wuq_pad = jnp.pad(w_uq[0].reshape(Q_LORA, MLA_HEADS, MLA_NOPE + MLA_ROPE),
                      ((0, 0), (0, 0), (0, LANES - MLA_NOPE - MLA_ROPE))).reshape(Q_LORA, MLA_HEADS * LANES).astype(BF16)
    wukv = w_ukv[0].reshape(KV_LORA, MLA_HEADS, MLA_NOPE + MLA_V)
    wuk_pad = jnp.pad(wukv[:, :, :MLA_NOPE], ((0, 0), (0, 0), (0, LANES - MLA_NOPE))).reshape(KV_LORA, MLA_HEADS * LANES).astype(BF16)
    wuv = wukv[:, :, MLA_NOPE:].reshape(KV_LORA, MLA_WIDTH).astype(BF16)
    rw_pad = jnp.pad(router_w[0], ((0, 0), (0, LANES - N_EXPERTS))).astype(BF16)
    rb_pad = jnp.pad(router_b[0].astype(F32), (0, LANES - N_EXPERTS), constant_values=NEG_BIG)[None, :]
    lgf = jnp.log1p(-jnp.exp2(ret_decay_fwd[0].astype(F32)))
    lgb = jnp.log1p(-jnp.exp2(ret_decay_bwd[0].astype(F32)))
    tabs = _rope_tables(sp)

    xp = x_prompt.reshape(tp, D_MODEL)
    xs = x_sample.reshape(ts, D_MODEL)
    mem = jnp.concatenate([mem_prompt.reshape(-1, D_MODEL), mem_sample.reshape(-1, D_MODEL)], axis=0)

    kvmem = _mem_kv(mem, norm_mem[0][None, :], w_xkv[0].astype(BF16)).reshape(bp + bs, mem_len, 2 * D_MODEL)

    rq, rk, rv, rg, qm, km, vm = _in_proj(xp, xs, norm_mix[0][None, :], w_in_pad, q_a_norm[0][None, :], wuq_pad,
                                          kv_a_norm[0][None, :], wuk_pad, wuv, tabs, sp, ss)

    gn = ret_gn[0][None, :]
    ret_p = _retention(lgf, lgb, rq, rk, rv, rg, gn, sp, bp, 0)
    ret_s = _retention(lgf, lgb, rq, rk, rv, rg, gn, ss, bs, tp // ss)
    mla_p = _mla_attn(qm, km, vm, sp, bp, 0)
    mla_s = _mla_attn(qm, km, vm, ss, bs, tp // ss)

    h2, xn, gates, slot, stats, counts = _mix_cross(
        xp, xs, ret_p, ret_s, mla_p, mla_s, w_mix_out[0].astype(BF16), norm_cross[0][None, :], w_xq[0].astype(BF16),
        kvmem, w_xo[0].astype(BF16), norm_ffn[0][None, :], rw_pad, rb_pad, sp, ss)

    used = counts[0, :N_EXPERTS]
    padded = (used + MOE_BLOCK - 1) // MOE_BLOCK * MOE_BLOCK
    pend = jnp.cumsum(padded)
    pstart = pend - padded
    tile_cnt, tile_before, tile_lstart = (stats[:, r, :N_EXPERTS] for r in range(3))
    n_tiles = stats.shape[0]
    tile_chunks = (tile_cnt + ROW_CHUNK - 1) // ROW_CHUNK
    tables = (tile_chunks.reshape(-1),
              tile_lstart.reshape(-1),
              (pstart[None, :] + tile_before).reshape(-1),
              jnp.sum(tile_chunks, axis=-1))
    ztables = (pstart + used, (padded - used) // ROW_CHUNK)
    n_blocks = -(-(t * TOP_K + N_EXPERTS * (n_tiles * (ROW_CHUNK - 1) + MOE_BLOCK - 1)) // MOE_BLOCK)
    blk = jnp.arange(n_blocks, dtype=jnp.int32)
    n_active = (pend[-1] // MOE_BLOCK).astype(jnp.int32)
    block_i = jnp.minimum(blk, n_active - 1)
    block_e = jnp.minimum(jnp.sum((block_i[:, None] * MOE_BLOCK >= pend[None, :]).astype(jnp.int32), axis=-1),
                          N_EXPERTS - 1).astype(jnp.int32)
    block_first = jnp.concatenate([jnp.ones((1,), jnp.int32), (block_e[1:] != block_e[:-1]).astype(jnp.int32)])
    block_seg = jnp.cumsum(block_first) - 1
    next_blk = jnp.sum(jnp.where(block_e[:, None] == jnp.arange(N_EXPERTS)[None, :], pend[None, :], 0), axis=-1) // MOE_BLOCK
    block_next = jnp.where(next_blk < n_active, block_e[jnp.minimum(next_blk, n_blocks - 1)], -1).astype(jnp.int32)

    xs_sorted, pick = _dispatch(tables, ztables, xn, slot, n_blocks * MOE_BLOCK)
    ys = _experts((block_e, block_i, n_active[None], block_first, block_seg.astype(jnp.int32), block_next),
                  xs_sorted, w_gu[0], b_gu[0][:, None, :], w_down[0], b_down[0][:, None, :])
    out_p, out_s = _combine(tables, slot, gates, pick, h2, norm_final[None, :], ys, tp, ts)
    return out_p.reshape(bp, sp, D_MODEL), out_s.reshape(bs, ss, D_MODEL)
```

```python
import functools
import math

import jax
import jax.numpy as jnp
import numpy as np
from jax import lax
from jax.experimental import pallas as pl
from jax.experimental.pallas import tpu as pltpu

D_MODEL = 1024
RET_HEADS = 8
RET_DK = 64
RET_WIDTH = 512
CHUNK = 128
MLA_HEADS = 8
MLA_NOPE = 64
MLA_ROPE = 32
MLA_V = 64
MLA_WIDTH = 512
Q_LORA = 512
KV_LORA = 256
ROPE_BASE = 10000.0
X_HEADS = 4
X_HEAD_DIM = 256
N_EXPERTS = 32
TOP_K = 4
SWIGLU_LIMIT = 7.0
SWIGLU_ALPHA = 1.702
MOE_BLOCK = 512
EPS = 1e-6

LANES = 128
VMEM_LIMIT = 56 * 1024 * 1024

TOKEN_TILE = 512
ROW_PARTS = 2
MIX_ROW_PARTS = 1
ATTN_TQ = 1024
ATTN_TK = 1024
ATTN_UNROLL = 4
RET_UNROLL = 8
ROW_CHUNK = 8
GROUP_BITS = (TOKEN_TILE // ROW_CHUNK).bit_length()
WAIT_CHUNKS = 16
LOCAL_STEP = 256
LOCAL_ROWS = -(-(TOKEN_TILE * 4 + 32 * (ROW_CHUNK - 1)) // LOCAL_STEP) * LOCAL_STEP
D_IN_PAD = 4 * 512 + Q_LORA + KV_LORA + LANES
NEG_BIG = -1e30

F32 = jnp.float32
BF16 = jnp.bfloat16


def _params(sem, vmem=VMEM_LIMIT):
    return pltpu.CompilerParams(dimension_semantics=sem, vmem_limit_bytes=vmem)


def _const_spec(shape):
    nd = len(shape)
    return pl.BlockSpec(shape, lambda *_: (0,) * nd, pipeline_mode=pl.Buffered(1))


def _rms(x, g):
    ms = jnp.mean(x * x, axis=-1, keepdims=True)
    return x * lax.rsqrt(ms + EPS) * g


def _dot(a, b):
    return jnp.dot(a, b, preferred_element_type=F32)


def _dot_nt(a, b):
    return lax.dot_general(a, b, (((1,), (1,)), ((), ())), preferred_element_type=F32)


def _dot_tn(a, b):
    return lax.dot_general(a, b, (((0,), (0,)), ((), ())), preferred_element_type=F32)


def _mem_kv_kernel(mem_ref, g_ref, w_ref, o_ref):
    mn = _rms(mem_ref[...], g_ref[...])
    o_ref[...] = _dot(mn.astype(BF16), w_ref[...]).astype(BF16)


def _mem_kv(mem, g, w):
    rows, mem_len = mem.shape[0], 256
    return pl.pallas_call(
        _mem_kv_kernel,
        out_shape=jax.ShapeDtypeStruct((rows, 2 * D_MODEL), BF16),
        grid=(rows // mem_len,),
        in_specs=[pl.BlockSpec((mem_len, D_MODEL), lambda i: (i, 0)),
                  _const_spec((1, D_MODEL)),
                  _const_spec((D_MODEL, 2 * D_MODEL))],
        out_specs=pl.BlockSpec((mem_len, 2 * D_MODEL), lambda i: (i, 0)),
        compiler_params=_params(("arbitrary",)),
        name="mem_kv",
    )(mem, g, w)


def _rope_slab(x, c, ss, first, shift_up, shift_down):
    swap = jnp.where(first, pltpu.roll(x, shift_up, 1), pltpu.roll(x, shift_down, 1))
    return x * c + swap * ss


def _in_proj_kernel(xp_ref, xs_ref, g_ref, win_ref, qan_ref, wuq_ref, kvan_ref, wuk_ref, wuv_ref,
                    cr_ref, sr_ref, cm_ref, sm_ref,
                    rq_ref, rk_ref, rv_ref, rg_ref, qm_ref, km_ref, vm_ref, *, n_prompt_tiles):
    i = pl.program_id(0)
    tm = rq_ref.shape[0]
    lane = lax.broadcasted_iota(jnp.int32, (1, LANES), 1)
    ret_first = (lane % RET_DK) < (RET_DK // 2)
    mla_first = (lane >= MLA_NOPE) & (lane < MLA_NOPE + MLA_ROPE // 2)
    half_r, half_m = RET_DK // 2, MLA_ROPE // 2
    q_scale = (MLA_NOPE + MLA_ROPE) ** -0.5 * math.log2(math.e)

    for part in range(ROW_PARTS):
        rows = slice(part * (tm // ROW_PARTS), (part + 1) * (tm // ROW_PARTS))
        x = jnp.where(i < n_prompt_tiles, xp_ref[rows, :], xs_ref[rows, :])
        xn = _rms(x, g_ref[...])
        proj = _dot(xn.astype(BF16), win_ref[...])
        cr, sr, cm, sm = cr_ref[rows, :], sr_ref[rows, :], cm_ref[rows, :], sm_ref[rows, :]

        for s in range(RET_WIDTH // LANES):
            lo = s * LANES
            q = _rope_slab(proj[:, lo:lo + LANES], cr, sr, ret_first, LANES - half_r, half_r)
            rq_ref[rows, lo:lo + LANES] = q.astype(BF16)
            k = _rope_slab(proj[:, 512 + lo:512 + lo + LANES], cr, sr, ret_first, LANES - half_r, half_r)
            rk_ref[rows, lo:lo + LANES] = (k * (RET_DK ** -0.5)).astype(BF16)
        rv_ref[rows, :] = proj[:, 1024:1536].astype(BF16)
        rg_ref[rows, :] = proj[:, 1536:2048].astype(BF16)

        cq = _rms(proj[:, 2048:2048 + Q_LORA], qan_ref[...])
        qm = _dot(cq.astype(BF16), wuq_ref[...])
        ckv = _rms(proj[:, 2560:2560 + KV_LORA], kvan_ref[...]).astype(BF16)
        kn = _dot(ckv, wuk_ref[...])
        vm = _dot(ckv, wuv_ref[...])
        for h in range(MLA_HEADS):
            lo = h * LANES
            vm_ref[rows, lo:lo + LANES] = jnp.where(lane == MLA_V, 1.0, vm[:, lo:lo + LANES]).astype(BF16)
        kr = proj[:, 2816:2816 + LANES]
        kpe = _rope_slab(kr, cm, sm, mla_first, LANES - half_m, half_m)
        for h in range(MLA_HEADS):
            lo = h * LANES
            qh = _rope_slab(qm[:, lo:lo + LANES], cm, sm, mla_first, LANES - half_m, half_m)
            qm_ref[rows, lo:lo + LANES] = (qh * q_scale).astype(BF16)
            km_ref[rows, lo:lo + LANES] = (kn[:, lo:lo + LANES] + kpe).astype(BF16)


def _in_proj(xp, xs, g, w_in_pad, qan, wuq_pad, kvan, wuk_pad, wuv, tabs, sp, ss):
    tm = TOKEN_TILE
    tp, ts = xp.shape[0], xs.shape[0]
    npt, nst = tp // tm, ts // tm
    t = tp + ts
    tiles_p, tiles_s = sp // tm, ss // tm

    def xp_map(i):
        return (jnp.minimum(i, npt - 1), 0)

    def xs_map(i):
        return (jnp.maximum(i - npt, 0), 0)

    def tab_map(i):
        return (jnp.where(i < npt, i % tiles_p, (i - npt) % tiles_s), 0)

    tok = lambda w: pl.BlockSpec((tm, w), lambda i: (i, 0))
    widths = (512, 512, 512, 512, 1024, 1024, 1024)
    return pl.pallas_call(
        functools.partial(_in_proj_kernel, n_prompt_tiles=npt),
        out_shape=[jax.ShapeDtypeStruct((t, w), BF16) for w in widths],
        grid=(npt + nst,),
        in_specs=[pl.BlockSpec((tm, D_MODEL), xp_map), pl.BlockSpec((tm, D_MODEL), xs_map),
                  _const_spec((1, D_MODEL)), _const_spec((D_MODEL, D_IN_PAD)),
                  _const_spec((1, Q_LORA)), _const_spec((Q_LORA, MLA_HEADS * LANES)),
                  _const_spec((1, KV_LORA)), _const_spec((KV_LORA, MLA_HEADS * LANES)),
                  _const_spec((KV_LORA, MLA_HEADS * LANES))]
                 + [pl.BlockSpec((tm, LANES), tab_map)] * 4,
        out_specs=[tok(w) for w in widths],
        compiler_params=_params(("arbitrary",)),
        name="in_proj",
    )(xp, xs, g, w_in_pad, qan, wuq_pad, kvan, wuk_pad, wuv, *tabs)


def _retention_kernel(lgf_ref, lgb_ref, q_ref, k_ref, v_ref, g_ref, gn_ref, o_ref, kvf_scr, kvb_scr, st_scr, *, n_chunks):
    hp = pl.program_id(1)
    c = CHUNK
    lane = lax.broadcasted_iota(jnp.int32, (1, c), 1)
    row = lax.broadcasted_iota(jnp.int32, (c, 1), 0)
    lane_h0 = lane < RET_DK
    row_h0 = row < RET_DK
    lgf0, lgf1 = lgf_ref[2 * hp], lgf_ref[2 * hp + 1]
    lgb0, lgb1 = lgb_ref[2 * hp], lgb_ref[2 * hp + 1]
    lgf_lane = jnp.where(lane_h0, lgf0, lgf1)
    lgb_lane = jnp.where(lane_h0, lgb0, lgb1)
    t = row.astype(F32)
    q_dec_f = jnp.exp((t + 1.0) * lgf_lane)
    q_dec_b = jnp.exp((c - t) * lgb_lane)
    k_dec_f = jnp.exp((c - 1.0 - t) * lgf_lane)
    k_dec_b = jnp.exp(t * lgb_lane)
    chunk_dec_f = jnp.exp(c * jnp.where(row_h0, lgf0, lgf1))
    chunk_dec_b = jnp.exp(c * jnp.where(row_h0, lgb0, lgb1))
    same_head = row_h0 == lane_h0
    diff = t - lane.astype(F32)
    d_intra = []
    for lgf, lgb in ((lgf0, lgb0), (lgf1, lgb1)):
        fwd = jnp.where(diff >= 0, jnp.exp(jnp.where(diff >= 0, diff, 0.0) * lgf), 0.0)
        bwd = jnp.where(diff < 0, jnp.exp(jnp.where(diff < 0, -diff, 0.0) * lgb), 0.0)
        d_intra.append(fwd + bwd)
    avg = jnp.where(same_head, 1.0 / RET_DK, 0.0).astype(BF16)
    gn = gn_ref[...]

    def chunk(n):
        return pl.ds(pl.multiple_of(n * c, c), c)

    unroll = min(RET_UNROLL, n_chunks)

    def chunk_kv(n, _):
        k = k_ref[chunk(n), :].astype(F32)
        kd = jnp.concatenate([(k * k_dec_f).astype(BF16), (k * k_dec_b).astype(BF16)], axis=-1)
        kv = _dot_tn(kd, v_ref[chunk(n), :])
        kvf_scr[n] = jnp.where(same_head, kv[0:c], 0.0)
        kvb_scr[n] = jnp.where(same_head, kv[c:2 * c], 0.0)
        return 0

    lax.fori_loop(0, n_chunks, chunk_kv, 0, unroll=unroll)

    def fwd_state(n, s_f):
        st_scr[n, 0:c, :] = s_f.astype(BF16)
        return chunk_dec_f * s_f + kvf_scr[n]

    lax.fori_loop(0, n_chunks, fwd_state, jnp.zeros((c, c), F32))

    def bwd_state(j, s_b):
        n = n_chunks - 1 - j
        st_scr[n, c:2 * c, :] = s_b.astype(BF16)
        return chunk_dec_b * s_b + kvb_scr[n]

    lax.fori_loop(0, n_chunks, bwd_state, jnp.zeros((c, c), F32))

    d_both = jnp.concatenate(d_intra, axis=0)
    zero_q = jnp.zeros((c, c), BF16)

    def chunk_out(n):
        q = q_ref[chunk(n), :]
        qf = q.astype(F32)
        q_both = jnp.concatenate([(qf * q_dec_f).astype(BF16), (qf * q_dec_b).astype(BF16)], axis=-1)
        q_heads = jnp.concatenate([jnp.where(lane_h0, q, zero_q), jnp.where(lane_h0, zero_q, q)], axis=0)
        p = (_dot_nt(q_heads, k_ref[chunk(n), :]) * d_both).astype(BF16)
        inner = _dot(p, v_ref[chunk(n), :])
        return _dot(q_both, st_scr[n]) + jnp.where(lane_h0, inner[0:c], inner[c:2 * c])

    def split_rows(a):
        hi = a.astype(BF16)
        return jnp.concatenate([hi, (a - hi.astype(F32)).astype(BF16)], axis=0)

    def out_group(gi, _):
        rows = pl.ds(pl.multiple_of(gi * (unroll * c), unroll * c), unroll * c)
        y = jnp.concatenate([chunk_out(gi * unroll + u) for u in range(unroll)], axis=0)
        m = y.shape[0]
        mu2 = _dot(split_rows(y), avg)
        d = y - (mu2[0:m] + mu2[m:2 * m])
        var2 = _dot(split_rows(d * d), avg)
        yn = d * lax.rsqrt(var2[0:m] + var2[m:2 * m] + EPS) * gn
        gate = g_ref[rows, :].astype(F32)
        o_ref[rows, :] = (yn * (gate * jax.nn.sigmoid(gate))).astype(BF16)
        return 0

    lax.fori_loop(0, n_chunks // unroll, out_group, 0)


def _retention(lgf, lgb, rq, rk, rv, rg, gn, seq_len, n_seq, row_block0):
    n_chunks = seq_len // CHUNK
    hp_count = RET_WIDTH // LANES
    blk = pl.BlockSpec((seq_len, LANES), lambda b, hp, *_: (row_block0 + b, hp))
    return pl.pallas_call(
        functools.partial(_retention_kernel, n_chunks=n_chunks),
        out_shape=jax.ShapeDtypeStruct((n_seq * seq_len, RET_WIDTH), BF16),
        grid_spec=pltpu.PrefetchScalarGridSpec(
            num_scalar_prefetch=2,
            grid=(n_seq, hp_count),
            in_specs=[blk, blk, blk, blk, pl.BlockSpec((1, LANES), lambda b, hp, *_: (0, hp))],
            out_specs=pl.BlockSpec((seq_len, LANES), lambda b, hp, *_: (b, hp)),
            scratch_shapes=[pltpu.VMEM((n_chunks, CHUNK, CHUNK), F32), pltpu.VMEM((n_chunks, CHUNK, CHUNK), F32),
                            pltpu.VMEM((n_chunks, 2 * CHUNK, CHUNK), BF16)]),
        compiler_params=_params(("arbitrary", "arbitrary")),
        name="retention",
    )(lgf, lgb, rq, rk, rv, rg, gn)


def _mla_attn_kernel(q_ref, k_ref, v_ref, o_ref, *, tk, n_kv):
    tq = q_ref.shape[0]
    lane = lax.broadcasted_iota(jnp.int32, (1, LANES), 1)
    qs = (q_ref[:, :LANES], q_ref[:, LANES:])

    def body(j, carry):
        rows = pl.ds(pl.multiple_of(j * tk, tk), tk)
        new = []
        for h in range(2):
            m, acc = carry[h]
            s = _dot_nt(qs[h], k_ref[rows, h * LANES:(h + 1) * LANES])
            m_new = jnp.maximum(m, jnp.max(s, axis=-1, keepdims=True))
            alpha = jnp.exp2(m - m_new)
            p = jnp.exp2((s - m_new).astype(BF16))
            acc = alpha * acc + _dot(p, v_ref[rows, h * LANES:(h + 1) * LANES])
            new.append((m_new, acc))
        return tuple(new)

    init = (jnp.full((tq, 1), NEG_BIG, F32), jnp.zeros((tq, LANES), F32))
    (_, acc0), (_, acc1) = lax.fori_loop(0, n_kv, body, (init, init), unroll=min(ATTN_UNROLL, n_kv))
    out0 = acc0 / acc0[:, MLA_V:MLA_V + 1]
    out1 = acc1 / acc1[:, MLA_V:MLA_V + 1]
    o_ref[...] = jnp.where(lane < MLA_V, out0, pltpu.roll(out1, MLA_V, 1)).astype(BF16)


def _mla_attn(qm, km, vm, seq_len, n_seq, row_block0):
    tq, tk = min(ATTN_TQ, seq_len), min(ATTN_TK, seq_len)
    hp_count = MLA_HEADS // 2
    nq = seq_len // tq
    return pl.pallas_call(
        functools.partial(_mla_attn_kernel, tk=tk, n_kv=seq_len // tk),
        out_shape=jax.ShapeDtypeStruct((n_seq * seq_len, MLA_WIDTH), BF16),
        grid=(n_seq, hp_count, nq),
        in_specs=[pl.BlockSpec((tq, 2 * LANES), lambda b, hp, i: ((row_block0 + b) * nq + i, hp)),
                  pl.BlockSpec((seq_len, 2 * LANES), lambda b, hp, i: (row_block0 + b, hp)),
                  pl.BlockSpec((seq_len, 2 * LANES), lambda b, hp, i: (row_block0 + b, hp))],
        out_specs=pl.BlockSpec((tq, LANES), lambda b, hp, i: (b * nq + i, hp)),
        compiler_params=_params(("arbitrary", "arbitrary", "arbitrary")),
        name="mla_attn",
    )(qm, km, vm)


def _mix_cross_kernel(xp_ref, xs_ref, rp_ref, rs_ref, mp_ref, ms_ref, wmix_ref, gx_ref, wxq_ref, kv_ref, wxo_ref,
                      gf_ref, rw_ref, rb_ref,
                      h_ref, xn_ref, gate_ref, slot_ref, stat_ref, cnt_ref, carry_scr, *, n_prompt_tiles):
    i = pl.program_id(0)
    tm = h_ref.shape[0]
    is_p = i < n_prompt_tiles
    lane = lax.broadcasted_iota(jnp.int32, (tm, LANES), 1)

    def part_rows(rows):
        x = jnp.where(is_p, xp_ref[rows, :], xs_ref[rows, :])
        ret = jnp.where(is_p, rp_ref[rows, :], rs_ref[rows, :])
        mla = jnp.where(is_p, mp_ref[rows, :], ms_ref[rows, :])
        h1 = x + _dot(ret, wmix_ref[0:RET_WIDTH, :]) + _dot(mla, wmix_ref[RET_WIDTH:, :])

        hn = _rms(h1, gx_ref[...]).astype(BF16)
        q = (_dot(hn, wxq_ref[...]) * (X_HEAD_DIM ** -0.5)).astype(BF16)
        heads = []
        for h in range(X_HEADS):
            lo = h * X_HEAD_DIM
            s = _dot_nt(q[:, lo:lo + X_HEAD_DIM], kv_ref[0, :, lo:lo + X_HEAD_DIM])
            e = jnp.exp(s - jnp.max(s, axis=-1, keepdims=True))
            p = (e / jnp.sum(e, axis=-1, keepdims=True)).astype(BF16)
            heads.append(_dot(p, kv_ref[0, :, D_MODEL + lo:D_MODEL + lo + X_HEAD_DIM]).astype(BF16))
        h2 = h1 + _dot(jnp.concatenate(heads, axis=-1), wxo_ref[...])
        h_ref[rows, :] = h2

        xn = _rms(h2, gf_ref[...]).astype(BF16)
        xn_ref[rows, :] = xn
        work = _dot(xn, rw_ref[...]) + rb_ref[...]
        lane_f = lax.broadcasted_iota(jnp.int32, work.shape, 1).astype(F32)
        vals, picks = [], []
        for _ in range(TOP_K):
            m = jnp.max(work, axis=-1, keepdims=True)
            idx = jnp.min(jnp.where(work == m, lane_f, float(LANES)), axis=-1, keepdims=True)
            sel = lane_f == idx
            work = jnp.where(sel, -jnp.inf, work)
            vals.append(m)
            picks.append(jnp.where(sel, 1.0, 0.0))
        exps = [jnp.exp(v - vals[0]) for v in vals]
        denom = exps[0] + exps[1] + exps[2] + exps[3]
        return picks, [e / denom for e in exps]

    part = tm // MIX_ROW_PARTS
    parts = [part_rows(slice(r * part, (r + 1) * part)) for r in range(MIX_ROW_PARTS)]
    picks = [jnp.concatenate([p[0][k] for p in parts], axis=0) for k in range(TOP_K)]
    gates = [jnp.concatenate([p[1][k] for p in parts], axis=0) for k in range(TOP_K)]
    onehot = picks[0] + picks[1] + picks[2] + picks[3]


    @pl.when(i == 0)
    def _():
        carry_scr[...] = jnp.zeros_like(carry_scr)

    r_iota = lax.broadcasted_iota(jnp.int32, (tm, tm), 0)
    c_iota = lax.broadcasted_iota(jnp.int32, (tm, tm), 1)
    lower = jnp.where(c_iota < r_iota, 1.0, 0.0).astype(BF16)
    before = _dot(lower, onehot.astype(BF16))
    tile_cnt = jnp.sum(onehot, axis=0, keepdims=True)
    group_rows = jnp.floor((tile_cnt + (ROW_CHUNK - 1)) * (1.0 / ROW_CHUNK)) * ROW_CHUNK
    carry_before = carry_scr[...]
    carry = carry_before + group_rows
    carry_scr[...] = carry
    cnt_ref[...] = carry.astype(jnp.int32)
    e_row = lax.broadcasted_iota(jnp.int32, (LANES, LANES), 0)
    e_col = lax.broadcasted_iota(jnp.int32, (LANES, LANES), 1)
    earlier = jnp.where(e_row < e_col, 1.0, 0.0).astype(BF16)
    group_start = _dot(jnp.broadcast_to(group_rows, (8, LANES)).astype(BF16), earlier)[0:1, :]
    row8 = lax.broadcasted_iota(jnp.int32, (8, LANES), 0)
    stats = jnp.where(row8 == 0, tile_cnt, jnp.where(row8 == 1, carry_before, jnp.where(row8 == 2, group_start, 0.0)))
    stat_ref[0] = stats.astype(jnp.int32)

    place = before + group_start
    gate_out = jnp.zeros((tm, LANES), F32)
    slot_out = jnp.zeros((tm, LANES), F32)
    for k in range(TOP_K):
        slot_k = jnp.sum(picks[k] * place, axis=-1, keepdims=True)
        gate_out = jnp.where(lane == k, gates[k], gate_out)
        slot_out = jnp.where(lane == k, slot_k, slot_out)
    gate_ref[...] = gate_out
    slot_ref[...] = slot_out.astype(jnp.int32)


def _mix_cross(xp, xs, ret_p, ret_s, mla_p, mla_s, wmix, gx, wxq, kvmem, wxo, gf, rw_pad, rb_pad, sp, ss):
    tm = TOKEN_TILE
    tp, ts = xp.shape[0], xs.shape[0]
    npt, nst = tp // tm, ts // tm
    t = tp + ts
    n_seq_p = tp // sp
    mem_len = kvmem.shape[1]

    def p_map(i):
        return (jnp.minimum(i, npt - 1), 0)

    def s_map(i):
        return (jnp.maximum(i - npt, 0), 0)

    def kv_map(i):
        return (jnp.where(i < npt, i // (sp // tm), n_seq_p + (i - npt) // (ss // tm)), 0, 0)

    tok = lambda w: pl.BlockSpec((tm, w), lambda i: (i, 0))
    return pl.pallas_call(
        functools.partial(_mix_cross_kernel, n_prompt_tiles=npt),
        out_shape=[jax.ShapeDtypeStruct((t, D_MODEL), F32), jax.ShapeDtypeStruct((t, D_MODEL), BF16),
                   jax.ShapeDtypeStruct((t, LANES), F32), jax.ShapeDtypeStruct((t, LANES), jnp.int32),
                   jax.ShapeDtypeStruct((npt + nst, 8, LANES), jnp.int32), jax.ShapeDtypeStruct((1, LANES), jnp.int32)],
        grid=(npt + nst,),
        in_specs=[pl.BlockSpec((tm, D_MODEL), p_map), pl.BlockSpec((tm, D_MODEL), s_map),
                  pl.BlockSpec((tm, RET_WIDTH), p_map), pl.BlockSpec((tm, RET_WIDTH), s_map),
                  pl.BlockSpec((tm, MLA_WIDTH), p_map), pl.BlockSpec((tm, MLA_WIDTH), s_map),
                  _const_spec((D_MODEL, D_MODEL)), _const_spec((1, D_MODEL)), _const_spec((D_MODEL, D_MODEL)),
                  pl.BlockSpec((1, mem_len, 2 * D_MODEL), kv_map),
                  _const_spec((D_MODEL, D_MODEL)), _const_spec((1, D_MODEL)),
                  _const_spec((D_MODEL, LANES)), _const_spec((1, LANES))],
        out_specs=[tok(D_MODEL), tok(D_MODEL), tok(LANES), tok(LANES),
                   pl.BlockSpec((1, 8, LANES), lambda i: (i, 0, 0)),
                   pl.BlockSpec((1, LANES), lambda i: (0, 0))],
        scratch_shapes=[pltpu.VMEM((1, LANES), F32)],
        compiler_params=_params(("arbitrary",)),
        name="mix_cross_router",
    )(xp, xs, ret_p, ret_s, mla_p, mla_s, wmix, gx, wxq, kvmem, wxo, gf, rw_pad, rb_pad)


def _to_rows(a):
    eye = (lax.broadcasted_iota(jnp.int32, (LANES, LANES), 0)
           == lax.broadcasted_iota(jnp.int32, (LANES, LANES), 1)).astype(BF16)
    hi = a.astype(BF16)
    r1 = a - hi.astype(F32)
    mid = r1.astype(BF16)
    lo = (r1 - mid.astype(F32)).astype(BF16)
    return _dot_nt(eye, hi) + _dot_nt(eye, mid) + _dot_nt(eye, lo)


def _chunk_copies(tile, nch_ref, lstart_ref, gstart_ref, local, hbm, sem, to_hbm, act):
    def per_expert(e, _):
        base = tile * N_EXPERTS + e
        n = nch_ref[base]
        ls = lstart_ref[base]
        gs = gstart_ref[base]
        def bit_copy(j):
            rows = ROW_CHUNK << j

            @pl.when(((n >> j) & 1) == 1)
            def _():
                off = (n & ((1 << j) - 1)) * ROW_CHUNK
                l_rows = local.at[pl.ds(pl.multiple_of(ls + off, ROW_CHUNK), rows)]
                g_rows = hbm.at[pl.ds(pl.multiple_of(gs + off, ROW_CHUNK), rows)]
                act(pltpu.make_async_copy(l_rows, g_rows, sem) if to_hbm else pltpu.make_async_copy(g_rows, l_rows, sem))

        for j in range(GROUP_BITS):
            bit_copy(j)
        return 0

    lax.fori_loop(0, N_EXPERTS, per_expert, 0)


def _wait_tile(tile, tot_ref, local, hbm, sem, to_hbm):
    total = tot_ref[tile]

    def wait_rows(rows):
        l_rows = local.at[pl.ds(0, rows)]
        g_rows = hbm.at[pl.ds(0, rows)]
        (pltpu.make_async_copy(l_rows, g_rows, sem) if to_hbm else pltpu.make_async_copy(g_rows, l_rows, sem)).wait()

    def big(_, c):
        wait_rows(ROW_CHUNK * WAIT_CHUNKS)
        return c

    def small(_, c):
        wait_rows(ROW_CHUNK)
        return c

    lax.fori_loop(0, total // WAIT_CHUNKS, big, 0)
    lax.fori_loop(0, total % WAIT_CHUNKS, small, 0)


def _start(cp):
    cp.start()


def _wait(cp):
    cp.wait()


def _dispatch_kernel(nch_ref, lstart_ref, gstart_ref, tot_ref, zstart_ref, znch_ref,
                     x_ref, slot_ref, xs_hbm, pick_ref, buf, zero_scr, sem, zsem):
    i = pl.program_id(0)
    last = pl.num_programs(0) - 1
    tm = x_ref.shape[0]
    b = i % 2
    slot_rows = _to_rows(slot_ref[...].astype(F32))
    x = x_ref[...]

    def build(s, _):
        r0 = pl.multiple_of(s * LOCAL_STEP, LOCAL_STEP)
        rid = (lax.broadcasted_iota(jnp.int32, (LOCAL_STEP, tm), 0) + r0).astype(F32)
        hit = rid == slot_rows[0:1, :]
        for k in range(1, TOP_K):
            hit = hit | (rid == slot_rows[k:k + 1, :])
        pick = jnp.where(hit, 1.0, 0.0).astype(BF16)
        pick_ref[pl.ds(r0, LOCAL_STEP), :] = pick
        buf[b, pl.ds(r0, LOCAL_STEP), :] = _dot(pick, x)
        return 0

    lax.fori_loop(0, LOCAL_ROWS // LOCAL_STEP, build, 0, unroll=3)

    _chunk_copies(i, nch_ref, lstart_ref, gstart_ref, buf.at[b], xs_hbm, sem.at[b], True, _start)

    @pl.when(i > 0)
    def _():
        _wait_tile(i - 1, tot_ref, buf.at[1 - b], xs_hbm, sem.at[1 - b], True)

    @pl.when(i == last)
    def _():
        _wait_tile(i, tot_ref, buf.at[b], xs_hbm, sem.at[b], True)
        zero_scr[...] = jnp.zeros_like(zero_scr)

        def fill(act):
            def per_expert(e, _):
                def per_chunk(c, _):
                    row = pl.multiple_of(zstart_ref[e] + c * ROW_CHUNK, ROW_CHUNK)
                    act(pltpu.make_async_copy(zero_scr, xs_hbm.at[pl.ds(row, ROW_CHUNK)], zsem))
                    return 0

                lax.fori_loop(0, znch_ref[e], per_chunk, 0)
                return 0

            lax.fori_loop(0, N_EXPERTS, per_expert, 0)

        fill(_start)
        fill(_wait)


def _dispatch(tables, ztables, xn, slot, n_rows):
    tm = TOKEN_TILE
    t = xn.shape[0]
    return pl.pallas_call(
        _dispatch_kernel,
        out_shape=[jax.ShapeDtypeStruct((n_rows, D_MODEL), F32),
                   jax.ShapeDtypeStruct((t // tm * LOCAL_ROWS, tm), BF16)],
        grid_spec=pltpu.PrefetchScalarGridSpec(
            num_scalar_prefetch=6,
            grid=(t // tm,),
            in_specs=[pl.BlockSpec((tm, D_MODEL), lambda i, *_: (i, 0)),
                      pl.BlockSpec((tm, LANES), lambda i, *_: (i, 0))],
            out_specs=[pl.BlockSpec(memory_space=pl.ANY),
                       pl.BlockSpec((LOCAL_ROWS, tm), lambda i, *_: (i, 0))],
            scratch_shapes=[pltpu.VMEM((2, LOCAL_ROWS, D_MODEL), F32), pltpu.VMEM((ROW_CHUNK, D_MODEL), F32),
                            pltpu.SemaphoreType.DMA((2,)), pltpu.SemaphoreType.DMA(())]),
        compiler_params=_params(("arbitrary",)),
        name="moe_dispatch",
    )(*tables, *ztables, xn, slot)


def _experts_kernel(be_ref, bi_ref, na_ref, first_ref, seg_ref, nxt_ref, x_ref, bgu_ref, bd_ref, wgu_hbm, wd_hbm, o_ref,
                    wgu_f32, wd_f32, wgu_scr, wd_scr, sem):
    j = pl.program_id(0)
    active = j < na_ref[0]

    def weight_copies(e, slot):
        return (pltpu.make_async_copy(wgu_hbm.at[e], wgu_f32.at[slot], sem.at[slot, 0]),
                pltpu.make_async_copy(wd_hbm.at[e], wd_f32.at[slot], sem.at[slot, 1]))

    @pl.when(j == 0)
    def _():
        for cp in weight_copies(be_ref[0], 0):
            cp.start()

    @pl.when(active & (first_ref[j] == 1))
    def _():
        slot = seg_ref[j] % 2
        for cp in weight_copies(be_ref[j], slot):
            cp.wait()
        wgu_scr[...] = wgu_f32[slot].astype(BF16)
        wd_scr[...] = wd_f32[slot].astype(BF16)

        @pl.when(nxt_ref[j] >= 0)
        def _():
            for cp in weight_copies(nxt_ref[j], 1 - slot):
                cp.start()

    @pl.when(active)
    def _():
        x = x_ref[...].astype(BF16)
        gu = _dot(x, wgu_scr[...]) + bgu_ref[0]
        gate = jnp.minimum(gu[:, :D_MODEL], SWIGLU_LIMIT)
        up = jnp.clip(gu[:, D_MODEL:], -SWIGLU_LIMIT, SWIGLU_LIMIT)
        hid = (up + 1.0) * (gate * jax.nn.sigmoid(SWIGLU_ALPHA * gate))
        o_ref[...] = _dot(hid.astype(BF16), wd_scr[...]) + bd_ref[0]


def _experts(block_tables, xs, wgu, bgu, wd, bd):
    n_blocks = xs.shape[0] // MOE_BLOCK
    row_map = lambda j, be, bi, *_: (bi[j], 0)
    e_map = lambda j, be, *_: (be[j], 0, 0)
    return pl.pallas_call(
        _experts_kernel,
        out_shape=jax.ShapeDtypeStruct(xs.shape, F32),
        grid_spec=pltpu.PrefetchScalarGridSpec(
            num_scalar_prefetch=6,
            grid=(n_blocks,),
            in_specs=[pl.BlockSpec((MOE_BLOCK, D_MODEL), row_map),
                      pl.BlockSpec((1, 1, 2 * D_MODEL), e_map),
                      pl.BlockSpec((1, 1, D_MODEL), e_map),
                      pl.BlockSpec(memory_space=pl.ANY),
                      pl.BlockSpec(memory_space=pl.ANY)],
            out_specs=pl.BlockSpec((MOE_BLOCK, D_MODEL), row_map),
            scratch_shapes=[pltpu.VMEM((2, D_MODEL, 2 * D_MODEL), F32), pltpu.VMEM((2, D_MODEL, D_MODEL), F32),
                            pltpu.VMEM((D_MODEL, 2 * D_MODEL), BF16), pltpu.VMEM((D_MODEL, D_MODEL), BF16),
                            pltpu.SemaphoreType.DMA((2, 2))]),
        compiler_params=_params(("arbitrary",)),
        name="moe_experts",
    )(*block_tables, xs, bgu, bd, wgu, wd)


def _combine_kernel(nch_ref, lstart_ref, gstart_ref, tot_ref, slot_ref, gate_ref, pick_ref, h_ref, g_ref, ys_hbm,
                    op_ref, os_ref, buf, scaled_scr, sem, *, n_prompt_tiles):
    i = pl.program_id(0)
    n = pl.num_programs(0)
    tm = h_ref.shape[0]
    b = i % 2

    @pl.when(i == 0)
    def _():
        buf[...] = jnp.zeros_like(buf)
        _chunk_copies(0, nch_ref, lstart_ref, gstart_ref, buf.at[0], ys_hbm, sem.at[0], False, _start)

    @pl.when(i + 1 < n)
    def _():
        _chunk_copies(i + 1, nch_ref, lstart_ref, gstart_ref, buf.at[1 - b], ys_hbm, sem.at[1 - b], False, _start)

    _wait_tile(i, tot_ref, buf.at[b], ys_hbm, sem.at[b], False)

    slot = slot_ref[...].astype(F32)
    info = gate_ref[...] + pltpu.roll(slot, TOP_K, 1)
    info_hi = info.astype(BF16)
    info2 = jnp.concatenate([info_hi, (info - info_hi.astype(F32)).astype(BF16)], axis=-1)
    lane = lax.broadcasted_iota(jnp.int32, (LOCAL_STEP, LANES), 1)

    def step(s, _):
        r0 = pl.multiple_of(s * LOCAL_STEP, LOCAL_STEP)
        pick = pick_ref[pl.ds(r0, LOCAL_STEP), :]
        own2 = _dot(pick, info2)
        own = own2[:, :LANES] + own2[:, LANES:]
        rcol = (lax.broadcasted_iota(jnp.int32, (LOCAL_STEP, 1), 0) + r0).astype(F32)
        mine = jnp.where((own == rcol) & (lane >= TOP_K) & (lane < 2 * TOP_K), 1.0, 0.0)
        g_col = jnp.sum(own * pltpu.roll(mine, LANES - TOP_K, 1), axis=-1, keepdims=True)
        rows = buf[b, pl.ds(r0, LOCAL_STEP), :]
        scaled_scr[pl.ds(r0, LOCAL_STEP), :] = jnp.where(g_col != 0.0, rows * g_col, 0.0).astype(BF16)
        return 0

    lax.fori_loop(0, LOCAL_ROWS // LOCAL_STEP, step, 0, unroll=3)
    y = _dot_tn(pick_ref[...], scaled_scr[...])
    out = _rms(h_ref[...] + y, g_ref[...])

    @pl.when(i < n_prompt_tiles)
    def _():
        op_ref[...] = out

    @pl.when(i >= n_prompt_tiles)
    def _():
        os_ref[...] = out


def _combine(tables, slot, gates, pick, h2, g_final, ys, tp, ts):
    tm = TOKEN_TILE
    npt, nst = tp // tm, ts // tm
    tok = lambda w: pl.BlockSpec((tm, w), lambda i, *_: (i, 0))
    return pl.pallas_call(
        functools.partial(_combine_kernel, n_prompt_tiles=npt),
        out_shape=[jax.ShapeDtypeStruct((tp, D_MODEL), F32), jax.ShapeDtypeStruct((ts, D_MODEL), F32)],
        grid_spec=pltpu.PrefetchScalarGridSpec(
            num_scalar_prefetch=4,
            grid=(npt + nst,),
            in_specs=[tok(LANES), tok(LANES), pl.BlockSpec((LOCAL_ROWS, tm), lambda i, *_: (i, 0)), tok(D_MODEL),
                      pl.BlockSpec((1, D_MODEL), lambda i, *_: (0, 0)),
                      pl.BlockSpec(memory_space=pl.ANY)],
            out_specs=[pl.BlockSpec((tm, D_MODEL), lambda i, *_: (jnp.minimum(i, npt - 1), 0)),
                       pl.BlockSpec((tm, D_MODEL), lambda i, *_: (jnp.maximum(i - npt, 0), 0))],
            scratch_shapes=[pltpu.VMEM((2, LOCAL_ROWS, D_MODEL), F32), pltpu.VMEM((LOCAL_ROWS, D_MODEL), BF16),
                            pltpu.SemaphoreType.DMA((2,))]),
        compiler_params=_params(("arbitrary",)),
        name="moe_combine_norm",
    )(*tables, slot, gates, pick, h2, g_final, ys)


def _rope_tables(seq_len):
    pos = jnp.arange(seq_len, dtype=F32)[:, None]
    lane = np.arange(LANES)

    def table(half, lane_freq, first, active):
        inv = ROPE_BASE ** (-jnp.arange(half, dtype=F32) / half)
        ang = pos * inv[None, :]
        cos = jnp.cos(ang)[:, lane_freq]
        sin = jnp.sin(ang)[:, lane_freq]
        c = jnp.where(active[None, :], cos, 1.0)
        s = jnp.where(active[None, :], jnp.where(first[None, :], -sin, sin), 0.0)
        return c, s

    half_r = RET_DK // 2
    cr, sr = table(half_r, lane % half_r, (lane % RET_DK) < half_r, np.ones(LANES, bool))
    half_m = MLA_ROPE // 2
    rel = lane - MLA_NOPE
    active = (rel >= 0) & (rel < MLA_ROPE)
    cm, sm = table(half_m, np.where(active, rel % half_m, 0), active & (rel < half_m), active)
    return cr, sr, cm, sm


def kernel(x_prompt, x_sample, mem_prompt, mem_sample, norm_mix, w_in, ret_decay_fwd, ret_decay_bwd, ret_gn, q_a_norm, w_uq, kv_a_norm, w_ukv, w_mix_out, norm_cross, norm_mem, w_xq, w_xkv, w_xo, norm_ffn, router_w, router_b, w_gu, b_gu, w_down, b_down, norm_final):
    assert norm_mix.shape[0] == 1, "single layer"
    bp, sp, d = x_prompt.shape
    bs, ss, _ = x_sample.shape
    tp, ts = bp * sp, bs * ss
    t = tp + ts
    assert d == D_MODEL and sp % TOKEN_TILE == 0 and ss % TOKEN_TILE == 0 and tp % ss == 0 and sp >= ss
    mem_len = mem_prompt.shape[1]

    w_in0 = w_in[0]
    w_in_pad = jnp.zeros((D_MODEL, D_IN_PAD), F32)
    w_in_pad = w_in_pad.at[:, :2816].set(w_in0[:, :2816])
    w_in_pad = w_in_pad.at[:, 2816 + MLA_NOPE:2816 + MLA_NOPE + MLA_ROPE].set(w_in0[:, 2816:]).astype(BF16)
    wuq_pad = jnp.pad(w_uq[0].reshape(Q_LORA, MLA_HEADS, MLA_NOPE + MLA_ROPE),
                      ((0, 0), (0, 0), (0, LANES - MLA_NOPE - MLA_ROPE))).reshape(Q_LORA, MLA_HEADS * LANES).astype(BF16)
    wukv = w_ukv[0].reshape(KV_LORA, MLA_HEADS, MLA_NOPE + MLA_V)
    wuk_pad = jnp.pad(wukv[:, :, :MLA_NOPE], ((0, 0), (0, 0), (0, LANES - MLA_NOPE))).reshape(KV_LORA, MLA_HEADS * LANES).astype(BF16)
    wuv = jnp.pad(wukv[:, :, MLA_NOPE:], ((0, 0), (0, 0), (0, LANES - MLA_V))).reshape(KV_LORA, MLA_HEADS * LANES).astype(BF16)
    rw_pad = jnp.pad(router_w[0], ((0, 0), (0, LANES - N_EXPERTS))).astype(BF16)
    rb_pad = jnp.pad(router_b[0].astype(F32), (0, LANES - N_EXPERTS), constant_values=NEG_BIG)[None, :]
    lgf = jnp.log1p(-jnp.exp2(ret_decay_fwd[0].astype(F32)))
    lgb = jnp.log1p(-jnp.exp2(ret_decay_bwd[0].astype(F32)))
    tabs = _rope_tables(sp)

    xp = x_prompt.reshape(tp, D_MODEL)
    xs = x_sample.reshape(ts, D_MODEL)
    mem = jnp.concatenate([mem_prompt.reshape(-1, D_MODEL), mem_sample.reshape(-1, D_MODEL)], axis=0)

    kvmem = _mem_kv(mem, norm_mem[0][None, :], w_xkv[0].astype(BF16)).reshape(bp + bs, mem_len, 2 * D_MODEL)

    rq, rk, rv, rg, qm, km, vm = _in_proj(xp, xs, norm_mix[0][None, :], w_in_pad, q_a_norm[0][None, :], wuq_pad,
                                          kv_a_norm[0][None, :], wuk_pad, wuv, tabs, sp, ss)

    gn = ret_gn[0][None, :]
    ret_p = _retention(lgf, lgb, rq, rk, rv, rg, gn, sp, bp, 0)
    ret_s = _retention(lgf, lgb, rq, rk, rv, rg, gn, ss, bs, tp // ss)
    mla_p = _mla_attn(qm, km, vm, sp, bp, 0)
    mla_s = _mla_attn(qm, km, vm, ss, bs, tp // ss)

    h2, xn, gates, slot, stats, counts = _mix_cross(
        xp, xs, ret_p, ret_s, mla_p, mla_s, w_mix_out[0].astype(BF16), norm_cross[0][None, :], w_xq[0].astype(BF16),
        kvmem, w_xo[0].astype(BF16), norm_ffn[0][None, :], rw_pad, rb_pad, sp, ss)

    used = counts[0, :N_EXPERTS]
    padded = (used + MOE_BLOCK - 1) // MOE_BLOCK * MOE_BLOCK
    pend = jnp.cumsum(padded)
    pstart = pend - padded
    tile_cnt, tile_before, tile_lstart = (stats[:, r, :N_EXPERTS] for r in range(3))
    n_tiles = stats.shape[0]
    tile_chunks = (tile_cnt + ROW_CHUNK - 1) // ROW_CHUNK
    tables = (tile_chunks.reshape(-1),
              tile_lstart.reshape(-1),
              (pstart[None, :] + tile_before).reshape(-1),
              jnp.sum(tile_chunks, axis=-1))
    ztables = (pstart + used, (padded - used) // ROW_CHUNK)
    n_blocks = -(-(t * TOP_K + N_EXPERTS * (n_tiles * (ROW_CHUNK - 1) + MOE_BLOCK - 1)) // MOE_BLOCK)
    blk = jnp.arange(n_blocks, dtype=jnp.int32)
    n_active = (pend[-1] // MOE_BLOCK).astype(jnp.int32)
    block_i = jnp.minimum(blk, n_active - 1)
    block_e = jnp.minimum(jnp.sum((block_i[:, None] * MOE_BLOCK >= pend[None, :]).astype(jnp.int32), axis=-1),
                          N_EXPERTS - 1).astype(jnp.int32)
    block_first = jnp.concatenate([jnp.ones((1,), jnp.int32), (block_e[1:] != block_e[:-1]).astype(jnp.int32)])
    block_seg = jnp.cumsum(block_first) - 1
    next_blk = jnp.sum(jnp.where(block_e[:, None] == jnp.arange(N_EXPERTS)[None, :], pend[None, :], 0), axis=-1) // MOE_BLOCK
    block_next = jnp.where(next_blk < n_active, block_e[jnp.minimum(next_blk, n_blocks - 1)], -1).astype(jnp.int32)

    xs_sorted, pick = _dispatch(tables, ztables, xn, slot, n_blocks * MOE_BLOCK)
    ys = _experts((block_e, block_i, n_active[None], block_first, block_seg.astype(jnp.int32), block_next),
                  xs_sorted, w_gu[0], b_gu[0][:, None, :], w_down[0], b_down[0][:, None, :])
    out_p, out_s = _combine(tables, slot, gates, pick, h2, norm_final[None, :], ys, tp, ts)
    return out_p.reshape(bp, sp, D_MODEL), out_s.reshape(bs, ss, D_MODEL)
```

```python
import functools
import math

import jax
import jax.numpy as jnp
import numpy as np
from jax import lax
from jax.experimental import pallas as pl
from jax.experimental.pallas import tpu as pltpu

D_MODEL = 1024
RET_HEADS = 8
RET_DK = 64
RET_WIDTH = 512
CHUNK = 128
MLA_HEADS = 8
MLA_NOPE = 64
MLA_ROPE = 32
MLA_V = 64
MLA_WIDTH = 512
Q_LORA = 512
KV_LORA = 256
ROPE_BASE = 10000.0
X_HEADS = 4
X_HEAD_DIM = 256
N_EXPERTS = 32
TOP_K = 4
SWIGLU_LIMIT = 7.0
SWIGLU_ALPHA = 1.702
MOE_BLOCK = 512
EPS = 1e-6

LANES = 128
VMEM_LIMIT = 56 * 1024 * 1024

TOKEN_TILE = 512
ROW_PARTS = 2
ATTN_TQ = 1024
ATTN_TK = 1024
ATTN_UNROLL = 4
RET_UNROLL = 16
ROW_CHUNK = 8
GROUP_BITS = (TOKEN_TILE // ROW_CHUNK).bit_length()
WAIT_CHUNKS = 16
LOCAL_STEP = 256
LOCAL_ROWS = -(-(TOKEN_TILE * 4 + 32 * (ROW_CHUNK - 1)) // LOCAL_STEP) * LOCAL_STEP
D_IN_PAD = 4 * 512 + Q_LORA + KV_LORA + LANES
NEG_BIG = -1e30

F32 = jnp.float32
BF16 = jnp.bfloat16


def _params(sem, vmem=VMEM_LIMIT):
    return pltpu.CompilerParams(dimension_semantics=sem, vmem_limit_bytes=vmem)


def _const_spec(shape):
    nd = len(shape)
    return pl.BlockSpec(shape, lambda *_: (0,) * nd, pipeline_mode=pl.Buffered(1))


def _rms(x, g):
    ms = jnp.mean(x * x, axis=-1, keepdims=True)
    return x * lax.rsqrt(ms + EPS) * g


def _dot(a, b):
    return jnp.dot(a, b, preferred_element_type=F32)


def _dot_nt(a, b):
    return lax.dot_general(a, b, (((1,), (1,)), ((), ())), preferred_element_type=F32)


def _dot_tn(a, b):
    return lax.dot_general(a, b, (((0,), (0,)), ((), ())), preferred_element_type=F32)


def _mem_kv_kernel(mem_ref, g_ref, w_ref, o_ref):
    mn = _rms(mem_ref[...], g_ref[...])
    o_ref[...] = _dot(mn.astype(BF16), w_ref[...]).astype(BF16)


def _mem_kv(mem, g, w):
    rows, mem_len = mem.shape[0], 256
    return pl.pallas_call(
        _mem_kv_kernel,
        out_shape=jax.ShapeDtypeStruct((rows, 2 * D_MODEL), BF16),
        grid=(rows // mem_len,),
        in_specs=[pl.BlockSpec((mem_len, D_MODEL), lambda i: (i, 0)),
                  _const_spec((1, D_MODEL)),
                  _const_spec((D_MODEL, 2 * D_MODEL))],
        out_specs=pl.BlockSpec((mem_len, 2 * D_MODEL), lambda i: (i, 0)),
        compiler_params=_params(("arbitrary",)),
        name="mem_kv",
    )(mem, g, w)


def _rope_slab(x, c, ss, first, shift_up, shift_down):
    swap = jnp.where(first, pltpu.roll(x, shift_up, 1), pltpu.roll(x, shift_down, 1))
    return x * c + swap * ss


def _in_proj_kernel(xp_ref, xs_ref, g_ref, win_ref, qan_ref, wuq_ref, kvan_ref, wuk_ref, wuv_ref,
                    cr_ref, sr_ref, cm_ref, sm_ref,
                    rq_ref, rk_ref, rv_ref, rg_ref, qm_ref, km_ref, vm_ref, *, n_prompt_tiles):
    i = pl.program_id(0)
    tm = rq_ref.shape[0]
    lane = lax.broadcasted_iota(jnp.int32, (1, LANES), 1)
    ret_first = (lane % RET_DK) < (RET_DK // 2)
    mla_first = (lane >= MLA_NOPE) & (lane < MLA_NOPE + MLA_ROPE // 2)
    half_r, half_m = RET_DK // 2, MLA_ROPE // 2
    q_scale = (MLA_NOPE + MLA_ROPE) ** -0.5 * math.log2(math.e)

    for part in range(ROW_PARTS):
        rows = slice(part * (tm // ROW_PARTS), (part + 1) * (tm // ROW_PARTS))
        x = jnp.where(i < n_prompt_tiles, xp_ref[rows, :], xs_ref[rows, :])
        xn = _rms(x, g_ref[...])
        proj = _dot(xn.astype(BF16), win_ref[...])
        cr, sr, cm, sm = cr_ref[rows, :], sr_ref[rows, :], cm_ref[rows, :], sm_ref[rows, :]

        for s in range(RET_WIDTH // LANES):
            lo = s * LANES
            q = _rope_slab(proj[:, lo:lo + LANES], cr, sr, ret_first, LANES - half_r, half_r)
            rq_ref[rows, lo:lo + LANES] = q.astype(BF16)
            k = _rope_slab(proj[:, 512 + lo:512 + lo + LANES], cr, sr, ret_first, LANES - half_r, half_r)
            rk_ref[rows, lo:lo + LANES] = (k * (RET_DK ** -0.5)).astype(BF16)
        rv_ref[rows, :] = proj[:, 1024:1536].astype(BF16)
        rg_ref[rows, :] = proj[:, 1536:2048].astype(BF16)

        cq = _rms(proj[:, 2048:2048 + Q_LORA], qan_ref[...])
        qm = _dot(cq.astype(BF16), wuq_ref[...])
        ckv = _rms(proj[:, 2560:2560 + KV_LORA], kvan_ref[...]).astype(BF16)
        kn = _dot(ckv, wuk_ref[...])
        vm = _dot(ckv, wuv_ref[...])
        for h in range(MLA_HEADS):
            lo = h * LANES
            vm_ref[rows, lo:lo + LANES] = jnp.where(lane == MLA_V, 1.0, vm[:, lo:lo + LANES]).astype(BF16)
        kr = proj[:, 2816:2816 + LANES]
        kpe = _rope_slab(kr, cm, sm, mla_first, LANES - half_m, half_m)
        for h in range(MLA_HEADS):
            lo = h * LANES
            qh = _rope_slab(qm[:, lo:lo + LANES], cm, sm, mla_first, LANES - half_m, half_m)
            qm_ref[rows, lo:lo + LANES] = (qh * q_scale).astype(BF16)
            km_ref[rows, lo:lo + LANES] = (kn[:, lo:lo + LANES] + kpe).astype(BF16)


def _in_proj(xp, xs, g, w_in_pad, qan, wuq_pad, kvan, wuk_pad, wuv, tabs, sp, ss):
    tm = TOKEN_TILE
    tp, ts = xp.shape[0], xs.shape[0]
    npt, nst = tp // tm, ts // tm
    t = tp + ts
    tiles_p, tiles_s = sp // tm, ss // tm

    def xp_map(i):
        return (jnp.minimum(i, npt - 1), 0)

    def xs_map(i):
        return (jnp.maximum(i - npt, 0), 0)

    def tab_map(i):
        return (jnp.where(i < npt, i % tiles_p, (i - npt) % tiles_s), 0)

    tok = lambda w: pl.BlockSpec((tm, w), lambda i: (i, 0))
    widths = (512, 512, 512, 512, 1024, 1024, 1024)
    return pl.pallas_call(
        functools.partial(_in_proj_kernel, n_prompt_tiles=npt),
        out_shape=[jax.ShapeDtypeStruct((t, w), BF16) for w in widths],
        grid=(npt + nst,),
        in_specs=[pl.BlockSpec((tm, D_MODEL), xp_map), pl.BlockSpec((tm, D_MODEL), xs_map),
                  _const_spec((1, D_MODEL)), _const_spec((D_MODEL, D_IN_PAD)),
                  _const_spec((1, Q_LORA)), _const_spec((Q_LORA, MLA_HEADS * LANES)),
                  _const_spec((1, KV_LORA)), _const_spec((KV_LORA, MLA_HEADS * LANES)),
                  _const_spec((KV_LORA, MLA_HEADS * LANES))]
                 + [pl.BlockSpec((tm, LANES), tab_map)] * 4,
        out_specs=[tok(w) for w in widths],
        compiler_params=_params(("arbitrary",)),
        name="in_proj",
    )(xp, xs, g, w_in_pad, qan, wuq_pad, kvan, wuk_pad, wuv, *tabs)


def _retention_kernel(lgf_ref, lgb_ref, q_ref, k_ref, v_ref, g_ref, gn_ref, o_ref, kvf_scr, kvb_scr, st_scr, *, n_chunks):
    hp = pl.program_id(1)
    c = CHUNK
    lane = lax.broadcasted_iota(jnp.int32, (1, c), 1)
    row = lax.broadcasted_iota(jnp.int32, (c, 1), 0)
    lane_h0 = lane < RET_DK
    row_h0 = row < RET_DK
    lgf0, lgf1 = lgf_ref[2 * hp], lgf_ref[2 * hp + 1]
    lgb0, lgb1 = lgb_ref[2 * hp], lgb_ref[2 * hp + 1]
    lgf_lane = jnp.where(lane_h0, lgf0, lgf1)
    lgb_lane = jnp.where(lane_h0, lgb0, lgb1)
    t = row.astype(F32)
    q_dec_f = jnp.exp((t + 1.0) * lgf_lane)
    q_dec_b = jnp.exp((c - t) * lgb_lane)
    k_dec_f = jnp.exp((c - 1.0 - t) * lgf_lane)
    k_dec_b = jnp.exp(t * lgb_lane)
    chunk_dec_f = jnp.exp(c * jnp.where(row_h0, lgf0, lgf1))
    chunk_dec_b = jnp.exp(c * jnp.where(row_h0, lgb0, lgb1))
    same_head = row_h0 == lane_h0
    diff = t - lane.astype(F32)
    d_intra = []
    for lgf, lgb in ((lgf0, lgb0), (lgf1, lgb1)):
        fwd = jnp.where(diff >= 0, jnp.exp(jnp.where(diff >= 0, diff, 0.0) * lgf), 0.0)
        bwd = jnp.where(diff < 0, jnp.exp(jnp.where(diff < 0, -diff, 0.0) * lgb), 0.0)
        d_intra.append(fwd + bwd)
    avg = jnp.where(same_head, 1.0 / RET_DK, 0.0).astype(BF16)
    gn = gn_ref[...]

    def chunk(n):
        return pl.ds(pl.multiple_of(n * c, c), c)

    unroll = min(RET_UNROLL, n_chunks)

    def chunk_kv(n, _):
        k = k_ref[chunk(n), :].astype(F32)
        kd = jnp.concatenate([(k * k_dec_f).astype(BF16), (k * k_dec_b).astype(BF16)], axis=-1)
        kv = _dot_tn(kd, v_ref[chunk(n), :])
        kvf_scr[n] = jnp.where(same_head, kv[0:c], 0.0)
        kvb_scr[n] = jnp.where(same_head, kv[c:2 * c], 0.0)
        return 0

    lax.fori_loop(0, n_chunks, chunk_kv, 0, unroll=unroll)

    def fwd_state(n, s_f):
        st_scr[n, 0:c, :] = s_f.astype(BF16)
        return chunk_dec_f * s_f + kvf_scr[n]

    lax.fori_loop(0, n_chunks, fwd_state, jnp.zeros((c, c), F32))

    def bwd_state(j, s_b):
        n = n_chunks - 1 - j
        st_scr[n, c:2 * c, :] = s_b.astype(BF16)
        return chunk_dec_b * s_b + kvb_scr[n]

    lax.fori_loop(0, n_chunks, bwd_state, jnp.zeros((c, c), F32))

    d_both = jnp.concatenate(d_intra, axis=0)
    zero_q = jnp.zeros((c, c), BF16)

    def chunk_out(n):
        q = q_ref[chunk(n), :]
        qf = q.astype(F32)
        q_both = jnp.concatenate([(qf * q_dec_f).astype(BF16), (qf * q_dec_b).astype(BF16)], axis=-1)
        q_heads = jnp.concatenate([jnp.where(lane_h0, q, zero_q), jnp.where(lane_h0, zero_q, q)], axis=0)
        p = (_dot_nt(q_heads, k_ref[chunk(n), :]) * d_both).astype(BF16)
        inner = _dot(p, v_ref[chunk(n), :])
        return _dot(q_both, st_scr[n]) + jnp.where(lane_h0, inner[0:c], inner[c:2 * c])

    def split_rows(a):
        hi = a.astype(BF16)
        return jnp.concatenate([hi, (a - hi.astype(F32)).astype(BF16)], axis=0)

    def out_group(gi, _):
        rows = pl.ds(pl.multiple_of(gi * (unroll * c), unroll * c), unroll * c)
        y = jnp.concatenate([chunk_out(gi * unroll + u) for u in range(unroll)], axis=0)
        m = y.shape[0]
        mu2 = _dot(split_rows(y), avg)
        d = y - (mu2[0:m] + mu2[m:2 * m])
        var2 = _dot(split_rows(d * d), avg)
        yn = d * lax.rsqrt(var2[0:m] + var2[m:2 * m] + EPS) * gn
        gate = g_ref[rows, :].astype(F32)
        o_ref[rows, :] = (yn * (gate * jax.nn.sigmoid(gate))).astype(BF16)
        return 0

    lax.fori_loop(0, n_chunks // unroll, out_group, 0)


def _retention(lgf, lgb, rq, rk, rv, rg, gn, seq_len, n_seq, row_block0):
    n_chunks = seq_len // CHUNK
    hp_count = RET_WIDTH // LANES
    blk = pl.BlockSpec((seq_len, LANES), lambda b, hp, *_: (row_block0 + b, hp))
    return pl.pallas_call(
        functools.partial(_retention_kernel, n_chunks=n_chunks),
        out_shape=jax.ShapeDtypeStruct((n_seq * seq_len, RET_WIDTH), BF16),
        grid_spec=pltpu.PrefetchScalarGridSpec(
            num_scalar_prefetch=2,
            grid=(n_seq, hp_count),
            in_specs=[blk, blk, blk, blk, pl.BlockSpec((1, LANES), lambda b, hp, *_: (0, hp))],
            out_specs=pl.BlockSpec((seq_len, LANES), lambda b, hp, *_: (b, hp)),
            scratch_shapes=[pltpu.VMEM((n_chunks, CHUNK, CHUNK), F32), pltpu.VMEM((n_chunks, CHUNK, CHUNK), F32),
                            pltpu.VMEM((n_chunks, 2 * CHUNK, CHUNK), BF16)]),
        compiler_params=_params(("arbitrary", "arbitrary")),
        name="retention",
    )(lgf, lgb, rq, rk, rv, rg, gn)


def _mla_attn_kernel(q_ref, k_ref, v_ref, o_ref, *, tk, n_kv):
    tq = q_ref.shape[0]
    lane = lax.broadcasted_iota(jnp.int32, (1, LANES), 1)
    qs = (q_ref[:, :LANES], q_ref[:, LANES:])

    def body(j, carry):
        rows = pl.ds(pl.multiple_of(j * tk, tk), tk)
        new = []
        for h in range(2):
            m, acc = carry[h]
            s = _dot_nt(qs[h], k_ref[rows, h * LANES:(h + 1) * LANES])
            m_new = jnp.maximum(m, jnp.max(s, axis=-1, keepdims=True))
            alpha = jnp.exp2(m - m_new)
            p = jnp.exp2((s - m_new).astype(BF16))
            acc = alpha * acc + _dot(p, v_ref[rows, h * LANES:(h + 1) * LANES])
            new.append((m_new, acc))
        return tuple(new)

    init = (jnp.full((tq, 1), NEG_BIG, F32), jnp.zeros((tq, LANES), F32))
    (_, acc0), (_, acc1) = lax.fori_loop(0, n_kv, body, (init, init), unroll=min(ATTN_UNROLL, n_kv))
    out0 = acc0 / acc0[:, MLA_V:MLA_V + 1]
    out1 = acc1 / acc1[:, MLA_V:MLA_V + 1]
    o_ref[...] = jnp.where(lane < MLA_V, out0, pltpu.roll(out1, MLA_V, 1)).astype(BF16)


def _mla_attn(qm, km, vm, seq_len, n_seq, row_block0):
    tq, tk = min(ATTN_TQ, seq_len), min(ATTN_TK, seq_len)
    hp_count = MLA_HEADS // 2
    nq = seq_len // tq
    return pl.pallas_call(
        functools.partial(_mla_attn_kernel, tk=tk, n_kv=seq_len // tk),
        out_shape=jax.ShapeDtypeStruct((n_seq * seq_len, MLA_WIDTH), BF16),
        grid=(n_seq, hp_count, nq),
        in_specs=[pl.BlockSpec((tq, 2 * LANES), lambda b, hp, i: ((row_block0 + b) * nq + i, hp)),
                  pl.BlockSpec((seq_len, 2 * LANES), lambda b, hp, i: (row_block0 + b, hp)),
                  pl.BlockSpec((seq_len, 2 * LANES), lambda b, hp, i: (row_block0 + b, hp))],
        out_specs=pl.BlockSpec((tq, LANES), lambda b, hp, i: (b * nq + i, hp)),
        compiler_params=_params(("arbitrary", "arbitrary", "arbitrary")),
        name="mla_attn",
    )(qm, km, vm)


def _mix_cross_kernel(xp_ref, xs_ref, rp_ref, rs_ref, mp_ref, ms_ref, wmix_ref, gx_ref, wxq_ref, kv_ref, wxo_ref,
                      gf_ref, rwt_ref, rbc_ref,
                      h_ref, xn_ref, route_ref, stat_ref, cnt_ref, carry_scr, *, n_prompt_tiles):
    i = pl.program_id(0)
    tm = h_ref.shape[0]
    is_p = i < n_prompt_tiles
    x = jnp.where(is_p, xp_ref[...], xs_ref[...])
    ret = jnp.where(is_p, rp_ref[...], rs_ref[...])
    mla = jnp.where(is_p, mp_ref[...], ms_ref[...])
    h1 = x + _dot(ret, wmix_ref[0:RET_WIDTH, :]) + _dot(mla, wmix_ref[RET_WIDTH:, :])

    hn = _rms(h1, gx_ref[...]).astype(BF16)
    q = (_dot(hn, wxq_ref[...]) * (X_HEAD_DIM ** -0.5)).astype(BF16)
    heads = []
    for h in range(X_HEADS):
        lo = h * X_HEAD_DIM
        s = _dot_nt(q[:, lo:lo + X_HEAD_DIM], kv_ref[0, :, lo:lo + X_HEAD_DIM])
        e = jnp.exp(s - jnp.max(s, axis=-1, keepdims=True))
        p = (e / jnp.sum(e, axis=-1, keepdims=True)).astype(BF16)
        heads.append(_dot(p, kv_ref[0, :, D_MODEL + lo:D_MODEL + lo + X_HEAD_DIM]).astype(BF16))
    h2 = h1 + _dot(jnp.concatenate(heads, axis=-1), wxo_ref[...])
    h_ref[...] = h2
    xn = _rms(h2, gf_ref[...]).astype(BF16)
    xn_ref[...] = xn

    ne = N_EXPERTS
    work = _dot_nt(rwt_ref[...], xn) + rbc_ref[:, 0:1]
    e_f = lax.broadcasted_iota(jnp.int32, (ne, tm), 0).astype(F32)
    vals, picks = [], []
    for _ in range(TOP_K):
        m = jnp.max(work, axis=0, keepdims=True)
        idx = jnp.min(jnp.where(work == m, e_f, float(ne)), axis=0, keepdims=True)
        sel = e_f == idx
        work = jnp.where(sel, -jnp.inf, work)
        vals.append(m)
        picks.append(jnp.where(sel, 1.0, 0.0))
    exps = [jnp.exp(v - vals[0]) for v in vals]
    denom = exps[0] + exps[1] + exps[2] + exps[3]
    onehot = picks[0] + picks[1] + picks[2] + picks[3]

    @pl.when(i == 0)
    def _():
        carry_scr[...] = jnp.zeros_like(carry_scr)

    t_row = lax.broadcasted_iota(jnp.int32, (tm, tm), 0)
    t_col = lax.broadcasted_iota(jnp.int32, (tm, tm), 1)
    earlier_tok = jnp.where(t_row < t_col, 1.0, 0.0).astype(BF16)
    before = _dot(onehot.astype(BF16), earlier_tok)
    tile_cnt = jnp.sum(onehot, axis=1, keepdims=True)
    group_rows = jnp.floor((tile_cnt + (ROW_CHUNK - 1)) * (1.0 / ROW_CHUNK)) * ROW_CHUNK
    carry_before = carry_scr[...]
    carry = carry_before + group_rows
    carry_scr[...] = carry
    cnt_ref[...] = carry.astype(jnp.int32)
    e_row = lax.broadcasted_iota(jnp.int32, (LANES, LANES), 0)
    e_col = lax.broadcasted_iota(jnp.int32, (LANES, LANES), 1)
    earlier_exp = jnp.where(e_col < e_row, 1.0, 0.0).astype(BF16)
    rows_pad = jnp.concatenate([jnp.broadcast_to(group_rows, (ne, LANES)), jnp.zeros((LANES - ne, LANES), F32)], axis=0)
    group_start = _dot(earlier_exp, rows_pad.astype(BF16))[0:ne]
    lane = lax.broadcasted_iota(jnp.int32, (ne, LANES), 1)
    stats = jnp.where(lane == 0, tile_cnt, jnp.where(lane == 1, carry_before, jnp.where(lane == 2, group_start, 0.0)))
    stat_ref[0] = stats.astype(jnp.int32)

    place = before + group_start[:, 0:1]
    row8 = lax.broadcasted_iota(jnp.int32, (2 * TOP_K, tm), 0)
    route = jnp.zeros((2 * TOP_K, tm), F32)
    for k in range(TOP_K):
        slot_k = jnp.sum(picks[k] * place, axis=0, keepdims=True)
        route = jnp.where(row8 == k, exps[k] / denom, jnp.where(row8 == TOP_K + k, slot_k, route))
    route_ref[...] = route


def _mix_cross(xp, xs, ret_p, ret_s, mla_p, mla_s, wmix, gx, wxq, kvmem, wxo, gf, rw_t, rb_col, sp, ss):
    tm = TOKEN_TILE
    tp, ts = xp.shape[0], xs.shape[0]
    npt, nst = tp // tm, ts // tm
    t = tp + ts
    n_seq_p = tp // sp
    mem_len = kvmem.shape[1]

    def p_map(i):
        return (jnp.minimum(i, npt - 1), 0)

    def s_map(i):
        return (jnp.maximum(i - npt, 0), 0)

    def kv_map(i):
        return (jnp.where(i < npt, i // (sp // tm), n_seq_p + (i - npt) // (ss // tm)), 0, 0)

    tok = lambda w: pl.BlockSpec((tm, w), lambda i: (i, 0))
    return pl.pallas_call(
        functools.partial(_mix_cross_kernel, n_prompt_tiles=npt),
        out_shape=[jax.ShapeDtypeStruct((t, D_MODEL), F32), jax.ShapeDtypeStruct((t, D_MODEL), BF16),
                   jax.ShapeDtypeStruct((2 * TOP_K, t), F32),
                   jax.ShapeDtypeStruct((npt + nst, N_EXPERTS, LANES), jnp.int32),
                   jax.ShapeDtypeStruct((N_EXPERTS, LANES), jnp.int32)],
        grid=(npt + nst,),
        in_specs=[pl.BlockSpec((tm, D_MODEL), p_map), pl.BlockSpec((tm, D_MODEL), s_map),
                  pl.BlockSpec((tm, RET_WIDTH), p_map), pl.BlockSpec((tm, RET_WIDTH), s_map),
                  pl.BlockSpec((tm, MLA_WIDTH), p_map), pl.BlockSpec((tm, MLA_WIDTH), s_map),
                  _const_spec((D_MODEL, D_MODEL)), _const_spec((1, D_MODEL)), _const_spec((D_MODEL, D_MODEL)),
                  pl.BlockSpec((1, mem_len, 2 * D_MODEL), kv_map),
                  _const_spec((D_MODEL, D_MODEL)), _const_spec((1, D_MODEL)),
                  _const_spec((N_EXPERTS, D_MODEL)), _const_spec((N_EXPERTS, LANES))],
        out_specs=[tok(D_MODEL), tok(D_MODEL),
                   pl.BlockSpec((2 * TOP_K, tm), lambda i: (0, i)),
                   pl.BlockSpec((1, N_EXPERTS, LANES), lambda i: (i, 0, 0)),
                   pl.BlockSpec((N_EXPERTS, LANES), lambda i: (0, 0))],
        scratch_shapes=[pltpu.VMEM((N_EXPERTS, LANES), F32)],
        compiler_params=_params(("arbitrary",)),
        name="mix_cross_router",
    )(xp, xs, ret_p, ret_s, mla_p, mla_s, wmix, gx, wxq, kvmem, wxo, gf, rw_t, rb_col)


def _chunk_copies(tile, nch_ref, lstart_ref, gstart_ref, local, hbm, sem, to_hbm, act, unroll=1):
    def per_expert(e, _):
        base = tile * N_EXPERTS + e
        n = nch_ref[base]
        ls = lstart_ref[base]
        gs = gstart_ref[base]
        def bit_copy(j):
            rows = ROW_CHUNK << j

            @pl.when(((n >> j) & 1) == 1)
            def _():
                off = (n & ((1 << j) - 1)) * ROW_CHUNK
                l_rows = local.at[pl.ds(pl.multiple_of(ls + off, ROW_CHUNK), rows)]
                g_rows = hbm.at[pl.ds(pl.multiple_of(gs + off, ROW_CHUNK), rows)]
                act(pltpu.make_async_copy(l_rows, g_rows, sem) if to_hbm else pltpu.make_async_copy(g_rows, l_rows, sem))

        for j in range(GROUP_BITS):
            bit_copy(j)
        return 0

    lax.fori_loop(0, N_EXPERTS, per_expert, 0, unroll=unroll)


def _wait_tile(tile, tot_ref, local, hbm, sem, to_hbm):
    total = tot_ref[tile]

    def wait_rows(rows):
        l_rows = local.at[pl.ds(0, rows)]
        g_rows = hbm.at[pl.ds(0, rows)]
        (pltpu.make_async_copy(l_rows, g_rows, sem) if to_hbm else pltpu.make_async_copy(g_rows, l_rows, sem)).wait()

    def big(_, c):
        wait_rows(ROW_CHUNK * WAIT_CHUNKS)
        return c

    def small(_, c):
        wait_rows(ROW_CHUNK)
        return c

    lax.fori_loop(0, total // WAIT_CHUNKS, big, 0)
    lax.fori_loop(0, total % WAIT_CHUNKS, small, 0)


def _start(cp):
    cp.start()


def _wait(cp):
    cp.wait()


def _dispatch_kernel(nch_ref, lstart_ref, gstart_ref, tot_ref, zstart_ref, znch_ref,
                     x_ref, route_ref, xs_hbm, pick_ref, buf, zero_scr, sem, zsem):
    i = pl.program_id(0)
    last = pl.num_programs(0) - 1
    tm = x_ref.shape[0]
    b = i % 2
    slot_rows = route_ref[TOP_K:2 * TOP_K, :]
    x = x_ref[...]

    def build(s, _):
        r0 = pl.multiple_of(s * LOCAL_STEP, LOCAL_STEP)
        rid = (lax.broadcasted_iota(jnp.int32, (LOCAL_STEP, tm), 0) + r0).astype(F32)
        hit = rid == slot_rows[0:1, :]
        for k in range(1, TOP_K):
            hit = hit | (rid == slot_rows[k:k + 1, :])
        pick = jnp.where(hit, 1.0, 0.0).astype(BF16)
        pick_ref[pl.ds(r0, LOCAL_STEP), :] = pick
        buf[b, pl.ds(r0, LOCAL_STEP), :] = _dot(pick, x)
        return 0

    lax.fori_loop(0, LOCAL_ROWS // LOCAL_STEP, build, 0, unroll=3)

    _chunk_copies(i, nch_ref, lstart_ref, gstart_ref, buf.at[b], xs_hbm, sem.at[b], True, _start)

    @pl.when(i > 0)
    def _():
        _wait_tile(i - 1, tot_ref, buf.at[1 - b], xs_hbm, sem.at[1 - b], True)

    @pl.when(i == last)
    def _():
        _wait_tile(i, tot_ref, buf.at[b], xs_hbm, sem.at[b], True)
        zero_scr[...] = jnp.zeros_like(zero_scr)

        def fill(act):
            def per_expert(e, _):
                def per_chunk(c, _):
                    row = pl.multiple_of(zstart_ref[e] + c * ROW_CHUNK, ROW_CHUNK)
                    act(pltpu.make_async_copy(zero_scr, xs_hbm.at[pl.ds(row, ROW_CHUNK)], zsem))
                    return 0

                lax.fori_loop(0, znch_ref[e], per_chunk, 0)
                return 0

            lax.fori_loop(0, N_EXPERTS, per_expert, 0)

        fill(_start)
        fill(_wait)


def _dispatch(tables, ztables, xn, route, n_rows):
    tm = TOKEN_TILE
    t = xn.shape[0]
    return pl.pallas_call(
        _dispatch_kernel,
        out_shape=[jax.ShapeDtypeStruct((n_rows, D_MODEL), F32),
                   jax.ShapeDtypeStruct((t // tm * LOCAL_ROWS, tm), BF16)],
        grid_spec=pltpu.PrefetchScalarGridSpec(
            num_scalar_prefetch=6,
            grid=(t // tm,),
            in_specs=[pl.BlockSpec((tm, D_MODEL), lambda i, *_: (i, 0)),
                      pl.BlockSpec((2 * TOP_K, tm), lambda i, *_: (0, i))],
            out_specs=[pl.BlockSpec(memory_space=pl.ANY),
                       pl.BlockSpec((LOCAL_ROWS, tm), lambda i, *_: (i, 0))],
            scratch_shapes=[pltpu.VMEM((2, LOCAL_ROWS, D_MODEL), F32), pltpu.VMEM((ROW_CHUNK, D_MODEL), F32),
                            pltpu.SemaphoreType.DMA((2,)), pltpu.SemaphoreType.DMA(())]),
        compiler_params=_params(("arbitrary",)),
        name="moe_dispatch",
    )(*tables, *ztables, xn, route)


def _experts_kernel(be_ref, bi_ref, na_ref, first_ref, seg_ref, nxt_ref, x_ref, bgu_ref, bd_ref, wgu_hbm, wd_hbm, o_ref,
                    wgu_f32, wd_f32, wgu_scr, wd_scr, sem):
    j = pl.program_id(0)
    active = j < na_ref[0]

    def weight_copies(e, slot):
        return (pltpu.make_async_copy(wgu_hbm.at[e], wgu_f32.at[slot], sem.at[slot, 0]),
                pltpu.make_async_copy(wd_hbm.at[e], wd_f32.at[slot], sem.at[slot, 1]))

    @pl.when(j == 0)
    def _():
        for cp in weight_copies(be_ref[0], 0):
            cp.start()

    @pl.when(active & (first_ref[j] == 1))
    def _():
        slot = seg_ref[j] % 2
        for cp in weight_copies(be_ref[j], slot):
            cp.wait()
        wgu_scr[...] = wgu_f32[slot].astype(BF16)
        wd_scr[...] = wd_f32[slot].astype(BF16)

        @pl.when(nxt_ref[j] >= 0)
        def _():
            for cp in weight_copies(nxt_ref[j], 1 - slot):
                cp.start()

    @pl.when(active)
    def _():
        x = x_ref[...].astype(BF16)
        gu = _dot(x, wgu_scr[...]) + bgu_ref[0]
        gate = jnp.minimum(gu[:, :D_MODEL], SWIGLU_LIMIT)
        up = jnp.clip(gu[:, D_MODEL:], -SWIGLU_LIMIT, SWIGLU_LIMIT)
        hid = (up + 1.0) * (gate * jax.nn.sigmoid(SWIGLU_ALPHA * gate))
        o_ref[...] = _dot(hid.astype(BF16), wd_scr[...]) + bd_ref[0]


def _experts(block_tables, xs, wgu, bgu, wd, bd):
    n_blocks = xs.shape[0] // MOE_BLOCK
    row_map = lambda j, be, bi, *_: (bi[j], 0)
    e_map = lambda j, be, *_: (be[j], 0, 0)
    return pl.pallas_call(
        _experts_kernel,
        out_shape=jax.ShapeDtypeStruct(xs.shape, F32),
        grid_spec=pltpu.PrefetchScalarGridSpec(
            num_scalar_prefetch=6,
            grid=(n_blocks,),
            in_specs=[pl.BlockSpec((MOE_BLOCK, D_MODEL), row_map),
                      pl.BlockSpec((1, 1, 2 * D_MODEL), e_map),
                      pl.BlockSpec((1, 1, D_MODEL), e_map),
                      pl.BlockSpec(memory_space=pl.ANY),
                      pl.BlockSpec(memory_space=pl.ANY)],
            out_specs=pl.BlockSpec((MOE_BLOCK, D_MODEL), row_map),
            scratch_shapes=[pltpu.VMEM((2, D_MODEL, 2 * D_MODEL), F32), pltpu.VMEM((2, D_MODEL, D_MODEL), F32),
                            pltpu.VMEM((D_MODEL, 2 * D_MODEL), BF16), pltpu.VMEM((D_MODEL, D_MODEL), BF16),
                            pltpu.SemaphoreType.DMA((2, 2))]),
        compiler_params=_params(("arbitrary",)),
        name="moe_experts",
    )(*block_tables, xs, bgu, bd, wgu, wd)


def _combine_kernel(nch_ref, lstart_ref, gstart_ref, tot_ref, route_ref, pick_ref, h_ref, g_ref, ys_hbm,
                    op_ref, os_ref, buf, scaled_scr, sem, *, n_prompt_tiles):
    i = pl.program_id(0)
    n = pl.num_programs(0)
    tm = h_ref.shape[0]
    b = i % 2

    @pl.when(i == 0)
    def _():
        buf[...] = jnp.zeros_like(buf)
        _chunk_copies(0, nch_ref, lstart_ref, gstart_ref, buf.at[0], ys_hbm, sem.at[0], False, _start)

    _wait_tile(i, tot_ref, buf.at[b], ys_hbm, sem.at[b], False)

    info = route_ref[...]
    info_hi = info.astype(BF16).astype(F32)
    gap = jnp.zeros((LANES - 2 * TOP_K, tm), F32)
    info2 = jnp.concatenate([info_hi, gap, info - info_hi, gap], axis=0).astype(BF16)
    lane = lax.broadcasted_iota(jnp.int32, (LOCAL_STEP, LANES), 1)

    def step(s, _):
        r0 = pl.multiple_of(s * LOCAL_STEP, LOCAL_STEP)
        pick = pick_ref[pl.ds(r0, LOCAL_STEP), :]
        own2 = _dot_nt(pick, info2)
        own = own2[:, :LANES] + own2[:, LANES:]
        rcol = (lax.broadcasted_iota(jnp.int32, (LOCAL_STEP, 1), 0) + r0).astype(F32)
        mine = jnp.where((own == rcol) & (lane >= TOP_K) & (lane < 2 * TOP_K), 1.0, 0.0)
        g_col = jnp.sum(own * pltpu.roll(mine, LANES - TOP_K, 1), axis=-1, keepdims=True)
        rows = buf[b, pl.ds(r0, LOCAL_STEP), :]
        scaled_scr[pl.ds(r0, LOCAL_STEP), :] = jnp.where(g_col != 0.0, rows * g_col, 0.0).astype(BF16)
        return 0

    lax.fori_loop(0, LOCAL_ROWS // LOCAL_STEP, step, 0, unroll=3)
    _chunk_copies(i + 1, nch_ref, lstart_ref, gstart_ref, buf.at[1 - b], ys_hbm, sem.at[1 - b], False, _start, unroll=True)
    y = _dot_tn(pick_ref[...], scaled_scr[...])
    out = _rms(h_ref[...] + y, g_ref[...])

    @pl.when(i < n_prompt_tiles)
    def _():
        op_ref[...] = out

    @pl.when(i >= n_prompt_tiles)
    def _():
        os_ref[...] = out


def _combine(tables, route, pick, h2, g_final, ys, tp, ts):
    tm = TOKEN_TILE
    npt, nst = tp // tm, ts // tm
    tok = lambda w: pl.BlockSpec((tm, w), lambda i, *_: (i, 0))
    return pl.pallas_call(
        functools.partial(_combine_kernel, n_prompt_tiles=npt),
        out_shape=[jax.ShapeDtypeStruct((tp, D_MODEL), F32), jax.ShapeDtypeStruct((ts, D_MODEL), F32)],
        grid_spec=pltpu.PrefetchScalarGridSpec(
            num_scalar_prefetch=4,
            grid=(npt + nst,),
            in_specs=[pl.BlockSpec((2 * TOP_K, tm), lambda i, *_: (0, i)),
                      pl.BlockSpec((LOCAL_ROWS, tm), lambda i, *_: (i, 0)), tok(D_MODEL),
                      pl.BlockSpec((1, D_MODEL), lambda i, *_: (0, 0)),
                      pl.BlockSpec(memory_space=pl.ANY)],
            out_specs=[pl.BlockSpec((tm, D_MODEL), lambda i, *_: (jnp.minimum(i, npt - 1), 0)),
                       pl.BlockSpec((tm, D_MODEL), lambda i, *_: (jnp.maximum(i - npt, 0), 0))],
            scratch_shapes=[pltpu.VMEM((2, LOCAL_ROWS, D_MODEL), F32), pltpu.VMEM((LOCAL_ROWS, D_MODEL), BF16),
                            pltpu.SemaphoreType.DMA((2,))]),
        compiler_params=_params(("arbitrary",)),
        name="moe_combine_norm",
    )(*tables, route, pick, h2, g_final, ys)


def _rope_tables(seq_len):
    pos = jnp.arange(seq_len, dtype=F32)[:, None]
    lane = np.arange(LANES)

    def table(half, lane_freq, first, active):
        inv = ROPE_BASE ** (-jnp.arange(half, dtype=F32) / half)
        ang = pos * inv[None, :]
        cos = jnp.cos(ang)[:, lane_freq]
        sin = jnp.sin(ang)[:, lane_freq]
        c = jnp.where(active[None, :], cos, 1.0)
        s = jnp.where(active[None, :], jnp.where(first[None, :], -sin, sin), 0.0)
        return c, s

    half_r = RET_DK // 2
    cr, sr = table(half_r, lane % half_r, (lane % RET_DK) < half_r, np.ones(LANES, bool))
    half_m = MLA_ROPE // 2
    rel = lane - MLA_NOPE
    active = (rel >= 0) & (rel < MLA_ROPE)
    cm, sm = table(half_m, np.where(active, rel % half_m, 0), active & (rel < half_m), active)
    return cr, sr, cm, sm


def kernel(x_prompt, x_sample, mem_prompt, mem_sample, norm_mix, w_in, ret_decay_fwd, ret_decay_bwd, ret_gn, q_a_norm, w_uq, kv_a_norm, w_ukv, w_mix_out, norm_cross, norm_mem, w_xq, w_xkv, w_xo, norm_ffn, router_w, router_b, w_gu, b_gu, w_down, b_down, norm_final):
    assert norm_mix.shape[0] == 1, "single layer"
    bp, sp, d = x_prompt.shape
    bs, ss, _ = x_sample.shape
    tp, ts = bp * sp, bs * ss
    t = tp + ts
    assert d == D_MODEL and sp % TOKEN_TILE == 0 and ss % TOKEN_TILE == 0 and tp % ss == 0 and sp >= ss
    mem_len = mem_prompt.shape[1]

    w_in0 = w_in[0]
    w_in_pad = jnp.zeros((D_MODEL, D_IN_PAD), F32)
    w_in_pad = w_in_pad.at[:, :2816].set(w_in0[:, :2816])
    w_in_pad = w_in_pad.at[:, 2816 + MLA_NOPE:2816 + MLA_NOPE + MLA_ROPE].set(w_in0[:, 2816:]).astype(BF16)
    wuq_pad = jnp.pad(w_uq[0].reshape(Q_LORA, MLA_HEADS, MLA_NOPE + MLA_ROPE),
                      ((0, 0), (0, 0), (0, LANES - MLA_NOPE - MLA_ROPE))).reshape(Q_LORA, MLA_HEADS * LANES).astype(BF16)
    wukv = w_ukv[0].reshape(KV_LORA, MLA_HEADS, MLA_NOPE + MLA_V)
    wuk_pad = jnp.pad(wukv[:, :, :MLA_NOPE], ((0, 0), (0, 0), (0, LANES - MLA_NOPE))).reshape(KV_LORA, MLA_HEADS * LANES).astype(BF16)
    wuv = jnp.pad(wukv[:, :, MLA_NOPE:], ((0, 0), (0, 0), (0, LANES - MLA_V))).reshape(KV_LORA, MLA_HEADS * LANES).astype(BF16)
    rw_t = router_w[0].T.astype(BF16)
    rb_col = jnp.broadcast_to(router_b[0].astype(F32)[:, None], (N_EXPERTS, LANES))
    lgf = jnp.log1p(-jnp.exp2(ret_decay_fwd[0].astype(F32)))
    lgb = jnp.log1p(-jnp.exp2(ret_decay_bwd[0].astype(F32)))
    tabs = _rope_tables(sp)

    xp = x_prompt.reshape(tp, D_MODEL)
    xs = x_sample.reshape(ts, D_MODEL)
    mem = jnp.concatenate([mem_prompt.reshape(-1, D_MODEL), mem_sample.reshape(-1, D_MODEL)], axis=0)

    kvmem = _mem_kv(mem, norm_mem[0][None, :], w_xkv[0].astype(BF16)).reshape(bp + bs, mem_len, 2 * D_MODEL)

    rq, rk, rv, rg, qm, km, vm = _in_proj(xp, xs, norm_mix[0][None, :], w_in_pad, q_a_norm[0][None, :], wuq_pad,
                                          kv_a_norm[0][None, :], wuk_pad, wuv, tabs, sp, ss)

    gn = ret_gn[0][None, :]
    ret_p = _retention(lgf, lgb, rq, rk, rv, rg, gn, sp, bp, 0)
    ret_s = _retention(lgf, lgb, rq, rk, rv, rg, gn, ss, bs, tp // ss)
    mla_p = _mla_attn(qm, km, vm, sp, bp, 0)
    mla_s = _mla_attn(qm, km, vm, ss, bs, tp // ss)

    h2, xn, route, stats, counts = _mix_cross(
        xp, xs, ret_p, ret_s, mla_p, mla_s, w_mix_out[0].astype(BF16), norm_cross[0][None, :], w_xq[0].astype(BF16),
        kvmem, w_xo[0].astype(BF16), norm_ffn[0][None, :], rw_t, rb_col, sp, ss)

    used = counts[:, 0]
    padded = (used + MOE_BLOCK - 1) // MOE_BLOCK * MOE_BLOCK
    pend = jnp.cumsum(padded)
    pstart = pend - padded
    tile_cnt, tile_before, tile_lstart = (stats[:, :, r] for r in range(3))
    n_tiles = stats.shape[0]
    tile_chunks = (tile_cnt + ROW_CHUNK - 1) // ROW_CHUNK
    no_tile = jnp.zeros((N_EXPERTS,), jnp.int32)
    tables = (jnp.concatenate([tile_chunks.reshape(-1), no_tile]),
              jnp.concatenate([tile_lstart.reshape(-1), no_tile]),
              jnp.concatenate([(pstart[None, :] + tile_before).reshape(-1), no_tile]),
              jnp.sum(tile_chunks, axis=-1))
    ztables = (pstart + used, (padded - used) // ROW_CHUNK)
    n_blocks = -(-(t * TOP_K + N_EXPERTS * (n_tiles * (ROW_CHUNK - 1) + MOE_BLOCK - 1)) // MOE_BLOCK)
    blk = jnp.arange(n_blocks, dtype=jnp.int32)
    n_active = (pend[-1] // MOE_BLOCK).astype(jnp.int32)
    block_i = jnp.minimum(blk, n_active - 1)
    block_e = jnp.minimum(jnp.sum((block_i[:, None] * MOE_BLOCK >= pend[None, :]).astype(jnp.int32), axis=-1),
                          N_EXPERTS - 1).astype(jnp.int32)
    block_first = jnp.concatenate([jnp.ones((1,), jnp.int32), (block_e[1:] != block_e[:-1]).astype(jnp.int32)])
    block_seg = jnp.cumsum(block_first) - 1
    next_blk = jnp.sum(jnp.where(block_e[:, None] == jnp.arange(N_EXPERTS)[None, :], pend[None, :], 0), axis=-1) // MOE_BLOCK
    block_next = jnp.where(next_blk < n_active, block_e[jnp.minimum(next_blk, n_blocks - 1)], -1).astype(jnp.int32)

    xs_sorted, pick = _dispatch(tables, ztables, xn, route, n_blocks * MOE_BLOCK)
    ys = _experts((block_e, block_i, n_active[None], block_first, block_seg.astype(jnp.int32), block_next),
                  xs_sorted, w_gu[0], b_gu[0][:, None, :], w_down[0], b_down[0][:, None, :])
    out_p, out_s = _combine(tables, route, pick, h2, norm_final[None, :], ys, tp, ts)
    return out_p.reshape(bp, sp, D_MODEL), out_s.reshape(bs, ss, D_MODEL)
```

```python
import functools
import math

import jax
import jax.numpy as jnp
import numpy as np
from jax import lax
from jax.experimental import pallas as pl
from jax.experimental.pallas import tpu as pltpu

D_MODEL = 1024
RET_HEADS = 8
RET_DK = 64
RET_WIDTH = 512
CHUNK = 128
MLA_HEADS = 8
MLA_NOPE = 64
MLA_ROPE = 32
MLA_V = 64
MLA_WIDTH = 512
Q_LORA = 512
KV_LORA = 256
ROPE_BASE = 10000.0
X_HEADS = 4
X_HEAD_DIM = 256
N_EXPERTS = 32
TOP_K = 4
SWIGLU_LIMIT = 7.0
SWIGLU_ALPHA = 1.702
MOE_BLOCK = 512
EPS = 1e-6

LANES = 128
VMEM_LIMIT = 56 * 1024 * 1024

TOKEN_TILE = 512
ROW_PARTS = 2
ATTN_TQ = 1024
ATTN_TK = 1024
ATTN_UNROLL = 4
RET_UNROLL = 16
ROW_CHUNK = 8
GROUP_BITS = (TOKEN_TILE // ROW_CHUNK).bit_length()
WAIT_CHUNKS = 16
LOCAL_STEP = 256
LOCAL_ROWS = -(-(TOKEN_TILE * 4 + 32 * (ROW_CHUNK - 1)) // LOCAL_STEP) * LOCAL_STEP
D_IN_PAD = 4 * 512 + Q_LORA + KV_LORA + LANES
NEG_BIG = -1e30

F32 = jnp.float32
BF16 = jnp.bfloat16


def _params(sem, vmem=VMEM_LIMIT):
    return pltpu.CompilerParams(dimension_semantics=sem, vmem_limit_bytes=vmem)


def _const_spec(shape):
    nd = len(shape)
    return pl.BlockSpec(shape, lambda *_: (0,) * nd, pipeline_mode=pl.Buffered(1))


def _rms(x, g):
    ms = jnp.mean(x * x, axis=-1, keepdims=True)
    return x * lax.rsqrt(ms + EPS) * g


def _dot(a, b):
    return jnp.dot(a, b, preferred_element_type=F32)


def _dot_nt(a, b):
    return lax.dot_general(a, b, (((1,), (1,)), ((), ())), preferred_element_type=F32)


def _dot_tn(a, b):
    return lax.dot_general(a, b, (((0,), (0,)), ((), ())), preferred_element_type=F32)


def _mem_kv_kernel(mem_ref, g_ref, w_ref, o_ref):
    mn = _rms(mem_ref[...], g_ref[...])
    o_ref[...] = _dot(mn.astype(BF16), w_ref[...]).astype(BF16)


def _mem_kv(mem, g, w):
    rows, mem_len = mem.shape[0], 256
    return pl.pallas_call(
        _mem_kv_kernel,
        out_shape=jax.ShapeDtypeStruct((rows, 2 * D_MODEL), BF16),
        grid=(rows // mem_len,),
        in_specs=[pl.BlockSpec((mem_len, D_MODEL), lambda i: (i, 0)),
                  _const_spec((1, D_MODEL)),
                  _const_spec((D_MODEL, 2 * D_MODEL))],
        out_specs=pl.BlockSpec((mem_len, 2 * D_MODEL), lambda i: (i, 0)),
        compiler_params=_params(("arbitrary",)),
        name="mem_kv",
    )(mem, g, w)


def _rope_slab(x, c, ss, first, shift_up, shift_down):
    swap = jnp.where(first, pltpu.roll(x, shift_up, 1), pltpu.roll(x, shift_down, 1))
    return x * c + swap * ss


def _in_proj_kernel(xp_ref, xs_ref, g_ref, win_ref, qan_ref, wuq_ref, kvan_ref, wuk_ref, wuv_ref,
                    cr_ref, sr_ref, cm_ref, sm_ref,
                    rq_ref, rk_ref, rv_ref, rg_ref, qm_ref, km_ref, vm_ref, *, n_prompt_tiles):
    i = pl.program_id(0)
    tm = rq_ref.shape[0]
    lane = lax.broadcasted_iota(jnp.int32, (1, LANES), 1)
    ret_first = (lane % RET_DK) < (RET_DK // 2)
    mla_first = (lane >= MLA_NOPE) & (lane < MLA_NOPE + MLA_ROPE // 2)
    half_r, half_m = RET_DK // 2, MLA_ROPE // 2
    q_scale = (MLA_NOPE + MLA_ROPE) ** -0.5 * math.log2(math.e)

    for part in range(ROW_PARTS):
        rows = slice(part * (tm // ROW_PARTS), (part + 1) * (tm // ROW_PARTS))
        x = jnp.where(i < n_prompt_tiles, xp_ref[rows, :], xs_ref[rows, :])
        xn = _rms(x, g_ref[...])
        proj = _dot(xn.astype(BF16), win_ref[...])
        cr, sr, cm, sm = cr_ref[rows, :], sr_ref[rows, :], cm_ref[rows, :], sm_ref[rows, :]

        for s in range(RET_WIDTH // LANES):
            lo = s * LANES
            q = _rope_slab(proj[:, lo:lo + LANES], cr, sr, ret_first, LANES - half_r, half_r)
            rq_ref[rows, lo:lo + LANES] = q.astype(BF16)
            k = _rope_slab(proj[:, 512 + lo:512 + lo + LANES], cr, sr, ret_first, LANES - half_r, half_r)
            rk_ref[rows, lo:lo + LANES] = (k * (RET_DK ** -0.5)).astype(BF16)
        rv_ref[rows, :] = proj[:, 1024:1536].astype(BF16)
        rg_ref[rows, :] = proj[:, 1536:2048].astype(BF16)

        cq = _rms(proj[:, 2048:2048 + Q_LORA], qan_ref[...])
        qm = _dot(cq.astype(BF16), wuq_ref[...])
        ckv = _rms(proj[:, 2560:2560 + KV_LORA], kvan_ref[...]).astype(BF16)
        kn = _dot(ckv, wuk_ref[...])
        vm = _dot(ckv, wuv_ref[...])
        for h in range(MLA_HEADS):
            lo = h * LANES
            vm_ref[rows, lo:lo + LANES] = jnp.where(lane == MLA_V, 1.0, vm[:, lo:lo + LANES]).astype(BF16)
        kr = proj[:, 2816:2816 + LANES]
        kpe = _rope_slab(kr, cm, sm, mla_first, LANES - half_m, half_m)
        for h in range(MLA_HEADS):
            lo = h * LANES
            qh = _rope_slab(qm[:, lo:lo + LANES], cm, sm, mla_first, LANES - half_m, half_m)
            qm_ref[rows, lo:lo + LANES] = (qh * q_scale).astype(BF16)
            km_ref[rows, lo:lo + LANES] = (kn[:, lo:lo + LANES] + kpe).astype(BF16)


def _in_proj(xp, xs, g, w_in_pad, qan, wuq_pad, kvan, wuk_pad, wuv, tabs, sp, ss):
    tm = TOKEN_TILE
    tp, ts = xp.shape[0], xs.shape[0]
    npt, nst = tp // tm, ts // tm
    t = tp + ts
    tiles_p, tiles_s = sp // tm, ss // tm

    def xp_map(i):
        return (jnp.minimum(i, npt - 1), 0)

    def xs_map(i):
        return (jnp.maximum(i - npt, 0), 0)

    def tab_map(i):
        return (jnp.where(i < npt, i % tiles_p, (i - npt) % tiles_s), 0)

    tok = lambda w: pl.BlockSpec((tm, w), lambda i: (i, 0))
    widths = (512, 512, 512, 512, 1024, 1024, 1024)
    return pl.pallas_call(
        functools.partial(_in_proj_kernel, n_prompt_tiles=npt),
        out_shape=[jax.ShapeDtypeStruct((t, w), BF16) for w in widths],
        grid=(npt + nst,),
        in_specs=[pl.BlockSpec((tm, D_MODEL), xp_map), pl.BlockSpec((tm, D_MODEL), xs_map),
                  _const_spec((1, D_MODEL)), _const_spec((D_MODEL, D_IN_PAD)),
                  _const_spec((1, Q_LORA)), _const_spec((Q_LORA, MLA_HEADS * LANES)),
                  _const_spec((1, KV_LORA)), _const_spec((KV_LORA, MLA_HEADS * LANES)),
                  _const_spec((KV_LORA, MLA_HEADS * LANES))]
                 + [pl.BlockSpec((tm, LANES), tab_map)] * 4,
        out_specs=[tok(w) for w in widths],
        compiler_params=_params(("arbitrary",)),
        name="in_proj",
    )(xp, xs, g, w_in_pad, qan, wuq_pad, kvan, wuk_pad, wuv, *tabs)


def _retention_kernel(lgf_ref, lgb_ref, q_ref, k_ref, v_ref, g_ref, gn_ref, o_ref, kvf_scr, kvb_scr, st_scr, *, n_chunks):
    hp = pl.program_id(1)
    c = CHUNK
    lane = lax.broadcasted_iota(jnp.int32, (1, c), 1)
    row = lax.broadcasted_iota(jnp.int32, (c, 1), 0)
    lane_h0 = lane < RET_DK
    row_h0 = row < RET_DK
    lgf0, lgf1 = lgf_ref[2 * hp], lgf_ref[2 * hp + 1]
    lgb0, lgb1 = lgb_ref[2 * hp], lgb_ref[2 * hp + 1]
    lgf_lane = jnp.where(lane_h0, lgf0, lgf1)
    lgb_lane = jnp.where(lane_h0, lgb0, lgb1)
    t = row.astype(F32)
    q_dec_f = jnp.exp((t + 1.0) * lgf_lane)
    q_dec_b = jnp.exp((c - t) * lgb_lane)
    k_dec_f = jnp.exp((c - 1.0 - t) * lgf_lane)
    k_dec_b = jnp.exp(t * lgb_lane)
    chunk_dec_f = jnp.exp(c * jnp.where(row_h0, lgf0, lgf1))
    chunk_dec_b = jnp.exp(c * jnp.where(row_h0, lgb0, lgb1))
    same_head = row_h0 == lane_h0
    diff = t - lane.astype(F32)
    d_intra = []
    for lgf, lgb in ((lgf0, lgb0), (lgf1, lgb1)):
        fwd = jnp.where(diff >= 0, jnp.exp(jnp.where(diff >= 0, diff, 0.0) * lgf), 0.0)
        bwd = jnp.where(diff < 0, jnp.exp(jnp.where(diff < 0, -diff, 0.0) * lgb), 0.0)
        d_intra.append(fwd + bwd)
    avg = jnp.where(same_head, 1.0 / RET_DK, 0.0).astype(BF16)
    gn = gn_ref[...]

    def chunk(n):
        return pl.ds(pl.multiple_of(n * c, c), c)

    unroll = min(RET_UNROLL, n_chunks)

    def chunk_kv(n, _):
        k = k_ref[chunk(n), :].astype(F32)
        kd = jnp.concatenate([(k * k_dec_f).astype(BF16), (k * k_dec_b).astype(BF16)], axis=-1)
        kv = _dot_tn(kd, v_ref[chunk(n), :])
        kvf_scr[n] = jnp.where(same_head, kv[0:c], 0.0)
        kvb_scr[n] = jnp.where(same_head, kv[c:2 * c], 0.0)
        return 0

    lax.fori_loop(0, n_chunks, chunk_kv, 0, unroll=unroll)

    def fwd_state(n, s_f):
        st_scr[n, 0:c, :] = s_f.astype(BF16)
        return chunk_dec_f * s_f + kvf_scr[n]

    lax.fori_loop(0, n_chunks, fwd_state, jnp.zeros((c, c), F32))

    def bwd_state(j, s_b):
        n = n_chunks - 1 - j
        st_scr[n, c:2 * c, :] = s_b.astype(BF16)
        return chunk_dec_b * s_b + kvb_scr[n]

    lax.fori_loop(0, n_chunks, bwd_state, jnp.zeros((c, c), F32))

    d_both = jnp.concatenate(d_intra, axis=0)
    zero_q = jnp.zeros((c, c), BF16)

    def chunk_out(n):
        q = q_ref[chunk(n), :]
        qf = q.astype(F32)
        q_both = jnp.concatenate([(qf * q_dec_f).astype(BF16), (qf * q_dec_b).astype(BF16)], axis=-1)
        q_heads = jnp.concatenate([jnp.where(lane_h0, q, zero_q), jnp.where(lane_h0, zero_q, q)], axis=0)
        p = (_dot_nt(q_heads, k_ref[chunk(n), :]) * d_both).astype(BF16)
        inner = _dot(p, v_ref[chunk(n), :])
        return _dot(q_both, st_scr[n]) + jnp.where(lane_h0, inner[0:c], inner[c:2 * c])

    def split_rows(a):
        hi = a.astype(BF16)
        return jnp.concatenate([hi, (a - hi.astype(F32)).astype(BF16)], axis=0)

    def out_group(gi, _):
        rows = pl.ds(pl.multiple_of(gi * (unroll * c), unroll * c), unroll * c)
        y = jnp.concatenate([chunk_out(gi * unroll + u) for u in range(unroll)], axis=0)
        m = y.shape[0]
        mu2 = _dot(split_rows(y), avg)
        d = y - (mu2[0:m] + mu2[m:2 * m])
        var2 = _dot(split_rows(d * d), avg)
        yn = d * lax.rsqrt(var2[0:m] + var2[m:2 * m] + EPS) * gn
        gate = g_ref[rows, :].astype(F32)
        o_ref[rows, :] = (yn * (gate * jax.nn.sigmoid(gate))).astype(BF16)
        return 0

    lax.fori_loop(0, n_chunks // unroll, out_group, 0)


def _retention(lgf, lgb, rq, rk, rv, rg, gn, seq_len, n_seq, row_block0):
    n_chunks = seq_len // CHUNK
    hp_count = RET_WIDTH // LANES
    blk = pl.BlockSpec((seq_len, LANES), lambda b, hp, *_: (row_block0 + b, hp))
    return pl.pallas_call(
        functools.partial(_retention_kernel, n_chunks=n_chunks),
        out_shape=jax.ShapeDtypeStruct((n_seq * seq_len, RET_WIDTH), BF16),
        grid_spec=pltpu.PrefetchScalarGridSpec(
            num_scalar_prefetch=2,
            grid=(n_seq, hp_count),
            in_specs=[blk, blk, blk, blk, pl.BlockSpec((1, LANES), lambda b, hp, *_: (0, hp))],
            out_specs=pl.BlockSpec((seq_len, LANES), lambda b, hp, *_: (b, hp)),
            scratch_shapes=[pltpu.VMEM((n_chunks, CHUNK, CHUNK), F32), pltpu.VMEM((n_chunks, CHUNK, CHUNK), F32),
                            pltpu.VMEM((n_chunks, 2 * CHUNK, CHUNK), BF16)]),
        compiler_params=_params(("arbitrary", "arbitrary")),
        name="retention",
    )(lgf, lgb, rq, rk, rv, rg, gn)


def _mla_attn_kernel(q_ref, k_ref, v_ref, o_ref, *, tk, n_kv):
    tq = q_ref.shape[0]
    lane = lax.broadcasted_iota(jnp.int32, (1, LANES), 1)
    qs = (q_ref[:, :LANES], q_ref[:, LANES:])

    def body(j, carry):
        rows = pl.ds(pl.multiple_of(j * tk, tk), tk)
        new = []
        for h in range(2):
            m, acc = carry[h]
            s = _dot_nt(qs[h], k_ref[rows, h * LANES:(h + 1) * LANES])
            m_new = jnp.maximum(m, jnp.max(s, axis=-1, keepdims=True))
            alpha = jnp.exp2(m - m_new)
            p = jnp.exp2((s - m_new).astype(BF16))
            acc = alpha * acc + _dot(p, v_ref[rows, h * LANES:(h + 1) * LANES])
            new.append((m_new, acc))
        return tuple(new)

    init = (jnp.full((tq, 1), NEG_BIG, F32), jnp.zeros((tq, LANES), F32))
    (_, acc0), (_, acc1) = lax.fori_loop(0, n_kv, body, (init, init), unroll=min(ATTN_UNROLL, n_kv))
    out0 = acc0 / acc0[:, MLA_V:MLA_V + 1]
    out1 = acc1 / acc1[:, MLA_V:MLA_V + 1]
    o_ref[...] = jnp.where(lane < MLA_V, out0, pltpu.roll(out1, MLA_V, 1)).astype(BF16)


def _mla_attn(qm, km, vm, seq_len, n_seq, row_block0):
    tq, tk = min(ATTN_TQ, seq_len), min(ATTN_TK, seq_len)
    hp_count = MLA_HEADS // 2
    nq = seq_len // tq
    return pl.pallas_call(
        functools.partial(_mla_attn_kernel, tk=tk, n_kv=seq_len // tk),
        out_shape=jax.ShapeDtypeStruct((n_seq * seq_len, MLA_WIDTH), BF16),
        grid=(n_seq, hp_count, nq),
        in_specs=[pl.BlockSpec((tq, 2 * LANES), lambda b, hp, i: ((row_block0 + b) * nq + i, hp)),
                  pl.BlockSpec((seq_len, 2 * LANES), lambda b, hp, i: (row_block0 + b, hp)),
                  pl.BlockSpec((seq_len, 2 * LANES), lambda b, hp, i: (row_block0 + b, hp))],
        out_specs=pl.BlockSpec((tq, LANES), lambda b, hp, i: (b * nq + i, hp)),
        compiler_params=_params(("arbitrary", "arbitrary", "arbitrary")),
        name="mla_attn",
    )(qm, km, vm)


def _mix_cross_kernel(xp_ref, xs_ref, rp_ref, rs_ref, mp_ref, ms_ref, wmix_ref, gx_ref, wxq_ref, kv_ref, wxo_ref,
                      gf_ref, rwt_ref, rbc_ref,
                      h_ref, xn_ref, route_ref, stat_ref, cnt_ref, carry_scr, *, n_prompt_tiles):
    i = pl.program_id(0)
    tm = h_ref.shape[0]
    is_p = i < n_prompt_tiles
    x = jnp.where(is_p, xp_ref[...], xs_ref[...])
    ret = jnp.where(is_p, rp_ref[...], rs_ref[...])
    mla = jnp.where(is_p, mp_ref[...], ms_ref[...])
    h1 = x + _dot(ret, wmix_ref[0:RET_WIDTH, :]) + _dot(mla, wmix_ref[RET_WIDTH:, :])

    hn = _rms(h1, gx_ref[...]).astype(BF16)
    q = (_dot(hn, wxq_ref[...]) * (X_HEAD_DIM ** -0.5)).astype(BF16)
    heads = []
    for h in range(X_HEADS):
        lo = h * X_HEAD_DIM
        s = _dot_nt(q[:, lo:lo + X_HEAD_DIM], kv_ref[0, :, lo:lo + X_HEAD_DIM])
        e = jnp.exp(s - jnp.max(s, axis=-1, keepdims=True))
        p = (e / jnp.sum(e, axis=-1, keepdims=True)).astype(BF16)
        heads.append(_dot(p, kv_ref[0, :, D_MODEL + lo:D_MODEL + lo + X_HEAD_DIM]).astype(BF16))
    h2 = h1 + _dot(jnp.concatenate(heads, axis=-1), wxo_ref[...])
    h_ref[...] = h2
    xn = _rms(h2, gf_ref[...]).astype(BF16)
    xn_ref[...] = xn

    ne = N_EXPERTS
    work = _dot_nt(rwt_ref[...], xn) + rbc_ref[:, 0:1]
    e_f = lax.broadcasted_iota(jnp.int32, (ne, tm), 0).astype(F32)
    vals, picks = [], []
    for _ in range(TOP_K):
        m = jnp.max(work, axis=0, keepdims=True)
        idx = jnp.min(jnp.where(work == m, e_f, float(ne)), axis=0, keepdims=True)
        sel = e_f == idx
        work = jnp.where(sel, -jnp.inf, work)
        vals.append(m)
        picks.append(jnp.where(sel, 1.0, 0.0))
    exps = [jnp.exp(v - vals[0]) for v in vals]
    denom = exps[0] + exps[1] + exps[2] + exps[3]
    onehot = picks[0] + picks[1] + picks[2] + picks[3]

    @pl.when(i == 0)
    def _():
        carry_scr[...] = jnp.zeros_like(carry_scr)

    t_row = lax.broadcasted_iota(jnp.int32, (tm, tm), 0)
    t_col = lax.broadcasted_iota(jnp.int32, (tm, tm), 1)
    earlier_tok = jnp.where(t_row < t_col, 1.0, 0.0).astype(BF16)
    before = _dot(onehot.astype(BF16), earlier_tok)
    tile_cnt = jnp.sum(onehot, axis=1, keepdims=True)
    group_rows = jnp.floor((tile_cnt + (ROW_CHUNK - 1)) * (1.0 / ROW_CHUNK)) * ROW_CHUNK
    carry_before = carry_scr[...]
    carry = carry_before + group_rows
    carry_scr[...] = carry
    cnt_ref[...] = carry.astype(jnp.int32)
    e_row = lax.broadcasted_iota(jnp.int32, (LANES, LANES), 0)
    e_col = lax.broadcasted_iota(jnp.int32, (LANES, LANES), 1)
    earlier_exp = jnp.where(e_col < e_row, 1.0, 0.0).astype(BF16)
    rows_pad = jnp.concatenate([jnp.broadcast_to(group_rows, (ne, LANES)), jnp.zeros((LANES - ne, LANES), F32)], axis=0)
    group_start = _dot(earlier_exp, rows_pad.astype(BF16))[0:ne]
    lane = lax.broadcasted_iota(jnp.int32, (ne, LANES), 1)
    stats = jnp.where(lane == 0, tile_cnt, jnp.where(lane == 1, carry_before, jnp.where(lane == 2, group_start, 0.0)))
    stat_ref[0] = stats.astype(jnp.int32)

    place = before + group_start[:, 0:1]
    row8 = lax.broadcasted_iota(jnp.int32, (2 * TOP_K, tm), 0)
    route = jnp.zeros((2 * TOP_K, tm), F32)
    for k in range(TOP_K):
        slot_k = jnp.sum(picks[k] * place, axis=0, keepdims=True)
        route = jnp.where(row8 == k, exps[k] / denom, jnp.where(row8 == TOP_K + k, slot_k, route))
    route_ref[...] = route


def _mix_cross(xp, xs, ret_p, ret_s, mla_p, mla_s, wmix, gx, wxq, kvmem, wxo, gf, rw_t, rb_col, sp, ss):
    tm = TOKEN_TILE
    tp, ts = xp.shape[0], xs.shape[0]
    npt, nst = tp // tm, ts // tm
    t = tp + ts
    n_seq_p = tp // sp
    mem_len = kvmem.shape[1]

    def p_map(i):
        return (jnp.minimum(i, npt - 1), 0)

    def s_map(i):
        return (jnp.maximum(i - npt, 0), 0)

    def kv_map(i):
        return (jnp.where(i < npt, i // (sp // tm), n_seq_p + (i - npt) // (ss // tm)), 0, 0)

    tok = lambda w: pl.BlockSpec((tm, w), lambda i: (i, 0))
    return pl.pallas_call(
        functools.partial(_mix_cross_kernel, n_prompt_tiles=npt),
        out_shape=[jax.ShapeDtypeStruct((t, D_MODEL), F32), jax.ShapeDtypeStruct((t, D_MODEL), BF16),
                   jax.ShapeDtypeStruct((2 * TOP_K, t), F32),
                   jax.ShapeDtypeStruct((npt + nst, N_EXPERTS, LANES), jnp.int32),
                   jax.ShapeDtypeStruct((N_EXPERTS, LANES), jnp.int32)],
        grid=(npt + nst,),
        in_specs=[pl.BlockSpec((tm, D_MODEL), p_map), pl.BlockSpec((tm, D_MODEL), s_map),
                  pl.BlockSpec((tm, RET_WIDTH), p_map), pl.BlockSpec((tm, RET_WIDTH), s_map),
                  pl.BlockSpec((tm, MLA_WIDTH), p_map), pl.BlockSpec((tm, MLA_WIDTH), s_map),
                  _const_spec((D_MODEL, D_MODEL)), _const_spec((1, D_MODEL)), _const_spec((D_MODEL, D_MODEL)),
                  pl.BlockSpec((1, mem_len, 2 * D_MODEL), kv_map),
                  _const_spec((D_MODEL, D_MODEL)), _const_spec((1, D_MODEL)),
                  _const_spec((N_EXPERTS, D_MODEL)), _const_spec((N_EXPERTS, LANES))],
        out_specs=[tok(D_MODEL), tok(D_MODEL),
                   pl.BlockSpec((2 * TOP_K, tm), lambda i: (0, i)),
                   pl.BlockSpec((1, N_EXPERTS, LANES), lambda i: (i, 0, 0)),
                   pl.BlockSpec((N_EXPERTS, LANES), lambda i: (0, 0))],
        scratch_shapes=[pltpu.VMEM((N_EXPERTS, LANES), F32)],
        compiler_params=_params(("arbitrary",)),
        name="mix_cross_router",
    )(xp, xs, ret_p, ret_s, mla_p, mla_s, wmix, gx, wxq, kvmem, wxo, gf, rw_t, rb_col)


def _chunk_copies(tile, nch_ref, lstart_ref, gstart_ref, local, hbm, sem, to_hbm, act):
    def per_expert(e, _):
        base = tile * N_EXPERTS + e
        n = nch_ref[base]
        ls = lstart_ref[base]
        gs = gstart_ref[base]
        def bit_copy(j):
            rows = ROW_CHUNK << j

            @pl.when(((n >> j) & 1) == 1)
            def _():
                off = (n & ((1 << j) - 1)) * ROW_CHUNK
                l_rows = local.at[pl.ds(pl.multiple_of(ls + off, ROW_CHUNK), rows)]
                g_rows = hbm.at[pl.ds(pl.multiple_of(gs + off, ROW_CHUNK), rows)]
                act(pltpu.make_async_copy(l_rows, g_rows, sem) if to_hbm else pltpu.make_async_copy(g_rows, l_rows, sem))

        for j in range(GROUP_BITS):
            bit_copy(j)
        return 0

    lax.fori_loop(0, N_EXPERTS, per_expert, 0)


def _wait_tile(tile, tot_ref, local, hbm, sem, to_hbm):
    total = tot_ref[tile]

    def wait_rows(rows):
        l_rows = local.at[pl.ds(0, rows)]
        g_rows = hbm.at[pl.ds(0, rows)]
        (pltpu.make_async_copy(l_rows, g_rows, sem) if to_hbm else pltpu.make_async_copy(g_rows, l_rows, sem)).wait()

    def big(_, c):
        wait_rows(ROW_CHUNK * WAIT_CHUNKS)
        return c

    def small(_, c):
        wait_rows(ROW_CHUNK)
        return c

    lax.fori_loop(0, total // WAIT_CHUNKS, big, 0)
    lax.fori_loop(0, total % WAIT_CHUNKS, small, 0)


def _start(cp):
    cp.start()


def _wait(cp):
    cp.wait()


def _dispatch_kernel(nch_ref, lstart_ref, gstart_ref, tot_ref, zstart_ref, znch_ref,
                     x_ref, route_ref, xs_hbm, pick_ref, buf, zero_scr, sem, zsem):
    i = pl.program_id(0)
    last = pl.num_programs(0) - 1
    tm = x_ref.shape[0]
    b = i % 2
    slot_rows = route_ref[TOP_K:2 * TOP_K, :]
    x = x_ref[...]

    def build(s, _):
        r0 = pl.multiple_of(s * LOCAL_STEP, LOCAL_STEP)
        rid = (lax.broadcasted_iota(jnp.int32, (LOCAL_STEP, tm), 0) + r0).astype(F32)
        hit = rid == slot_rows[0:1, :]
        for k in range(1, TOP_K):
            hit = hit | (rid == slot_rows[k:k + 1, :])
        pick = jnp.where(hit, 1.0, 0.0).astype(BF16)
        pick_ref[pl.ds(r0, LOCAL_STEP), :] = pick
        buf[b, pl.ds(r0, LOCAL_STEP), :] = _dot(pick, x)
        return 0

    lax.fori_loop(0, LOCAL_ROWS // LOCAL_STEP, build, 0, unroll=3)

    _chunk_copies(i, nch_ref, lstart_ref, gstart_ref, buf.at[b], xs_hbm, sem.at[b], True, _start)

    @pl.when(i > 0)
    def _():
        _wait_tile(i - 1, tot_ref, buf.at[1 - b], xs_hbm, sem.at[1 - b], True)

    @pl.when(i == last)
    def _():
        _wait_tile(i, tot_ref, buf.at[b], xs_hbm, sem.at[b], True)
        zero_scr[...] = jnp.zeros_like(zero_scr)

        def fill(act):
            def per_expert(e, _):
                def per_chunk(c, _):
                    row = pl.multiple_of(zstart_ref[e] + c * ROW_CHUNK, ROW_CHUNK)
                    act(pltpu.make_async_copy(zero_scr, xs_hbm.at[pl.ds(row, ROW_CHUNK)], zsem))
                    return 0

                lax.fori_loop(0, znch_ref[e], per_chunk, 0)
                return 0

            lax.fori_loop(0, N_EXPERTS, per_expert, 0)

        fill(_start)
        fill(_wait)


def _dispatch(tables, ztables, xn, route, n_rows):
    tm = TOKEN_TILE
    t = xn.shape[0]
    return pl.pallas_call(
        _dispatch_kernel,
        out_shape=[jax.ShapeDtypeStruct((n_rows, D_MODEL), F32),
                   jax.ShapeDtypeStruct((t // tm * LOCAL_ROWS, tm), BF16)],
        grid_spec=pltpu.PrefetchScalarGridSpec(
            num_scalar_prefetch=6,
            grid=(t // tm,),
            in_specs=[pl.BlockSpec((tm, D_MODEL), lambda i, *_: (i, 0)),
                      pl.BlockSpec((2 * TOP_K, tm), lambda i, *_: (0, i))],
            out_specs=[pl.BlockSpec(memory_space=pl.ANY),
                       pl.BlockSpec((LOCAL_ROWS, tm), lambda i, *_: (i, 0))],
            scratch_shapes=[pltpu.VMEM((2, LOCAL_ROWS, D_MODEL), F32), pltpu.VMEM((ROW_CHUNK, D_MODEL), F32),
                            pltpu.SemaphoreType.DMA((2,)), pltpu.SemaphoreType.DMA(())]),
        compiler_params=_params(("arbitrary",)),
        name="moe_dispatch",
    )(*tables, *ztables, xn, route)


def _experts_kernel(be_ref, bi_ref, na_ref, first_ref, seg_ref, nxt_ref, x_ref, bgu_ref, bd_ref, wgu_hbm, wd_hbm, o_ref,
                    wgu_f32, wd_f32, wgu_scr, wd_scr, sem):
    j = pl.program_id(0)
    active = j < na_ref[0]

    def weight_copies(e, slot):
        return (pltpu.make_async_copy(wgu_hbm.at[e], wgu_f32.at[slot], sem.at[slot, 0]),
                pltpu.make_async_copy(wd_hbm.at[e], wd_f32.at[slot], sem.at[slot, 1]))

    @pl.when(j == 0)
    def _():
        for cp in weight_copies(be_ref[0], 0):
            cp.start()

    @pl.when(active & (first_ref[j] == 1))
    def _():
        slot = seg_ref[j] % 2
        for cp in weight_copies(be_ref[j], slot):
            cp.wait()
        wgu_scr[...] = wgu_f32[slot].astype(BF16)
        wd_scr[...] = wd_f32[slot].astype(BF16)

        @pl.when(nxt_ref[j] >= 0)
        def _():
            for cp in weight_copies(nxt_ref[j], 1 - slot):
                cp.start()

    @pl.when(active)
    def _():
        x = x_ref[...].astype(BF16)
        gu = _dot(x, wgu_scr[...]) + bgu_ref[0]
        gate = jnp.minimum(gu[:, :D_MODEL], SWIGLU_LIMIT)
        up = jnp.clip(gu[:, D_MODEL:], -SWIGLU_LIMIT, SWIGLU_LIMIT)
        hid = (up + 1.0) * (gate * jax.nn.sigmoid(SWIGLU_ALPHA * gate))
        o_ref[...] = _dot(hid.astype(BF16), wd_scr[...]) + bd_ref[0]


def _experts(block_tables, xs, wgu, bgu, wd, bd):
    n_blocks = xs.shape[0] // MOE_BLOCK
    row_map = lambda j, be, bi, *_: (bi[j], 0)
    e_map = lambda j, be, *_: (be[j], 0, 0)
    return pl.pallas_call(
        _experts_kernel,
        out_shape=jax.ShapeDtypeStruct(xs.shape, F32),
        grid_spec=pltpu.PrefetchScalarGridSpec(
            num_scalar_prefetch=6,
            grid=(n_blocks,),
            in_specs=[pl.BlockSpec((MOE_BLOCK, D_MODEL), row_map),
                      pl.BlockSpec((1, 1, 2 * D_MODEL), e_map),
                      pl.BlockSpec((1, 1, D_MODEL), e_map),
                      pl.BlockSpec(memory_space=pl.ANY),
                      pl.BlockSpec(memory_space=pl.ANY)],
            out_specs=pl.BlockSpec((MOE_BLOCK, D_MODEL), row_map),
            scratch_shapes=[pltpu.VMEM((2, D_MODEL, 2 * D_MODEL), F32), pltpu.VMEM((2, D_MODEL, D_MODEL), F32),
                            pltpu.VMEM((D_MODEL, 2 * D_MODEL), BF16), pltpu.VMEM((D_MODEL, D_MODEL), BF16),
                            pltpu.SemaphoreType.DMA((2, 2))]),
        compiler_params=_params(("arbitrary",)),
        name="moe_experts",
    )(*block_tables, xs, bgu, bd, wgu, wd)


def _combine_kernel(nch_ref, lstart_ref, gstart_ref, tot_ref, route_ref, pick_ref, h_ref, g_ref, ys_hbm,
                    op_ref, os_ref, buf, scaled_scr, sem, *, n_prompt_tiles):
    i = pl.program_id(0)
    n = pl.num_programs(0)
    tm = h_ref.shape[0]
    b = i % 2

    @pl.when(i == 0)
    def _():
        buf[...] = jnp.zeros_like(buf)
        _chunk_copies(0, nch_ref, lstart_ref, gstart_ref, buf.at[0], ys_hbm, sem.at[0], False, _start)

    @pl.when(i + 1 < n)
    def _():
        _chunk_copies(i + 1, nch_ref, lstart_ref, gstart_ref, buf.at[1 - b], ys_hbm, sem.at[1 - b], False, _start)

    _wait_tile(i, tot_ref, buf.at[b], ys_hbm, sem.at[b], False)

    info = route_ref[...]
    info_hi = info.astype(BF16).astype(F32)
    gap = jnp.zeros((LANES - 2 * TOP_K, tm), F32)
    info2 = jnp.concatenate([info_hi, gap, info - info_hi, gap], axis=0).astype(BF16)
    lane = lax.broadcasted_iota(jnp.int32, (LOCAL_STEP, LANES), 1)

    def step(s, _):
        r0 = pl.multiple_of(s * LOCAL_STEP, LOCAL_STEP)
        pick = pick_ref[pl.ds(r0, LOCAL_STEP), :]
        own2 = _dot_nt(pick, info2)
        own = own2[:, :LANES] + own2[:, LANES:]
        rcol = (lax.broadcasted_iota(jnp.int32, (LOCAL_STEP, 1), 0) + r0).astype(F32)
        mine = jnp.where((own == rcol) & (lane >= TOP_K) & (lane < 2 * TOP_K), 1.0, 0.0)
        g_col = jnp.sum(own * pltpu.roll(mine, LANES - TOP_K, 1), axis=-1, keepdims=True)
        rows = buf[b, pl.ds(r0, LOCAL_STEP), :]
        scaled_scr[pl.ds(r0, LOCAL_STEP), :] = jnp.where(g_col != 0.0, rows * g_col, 0.0).astype(BF16)
        return 0

    lax.fori_loop(0, LOCAL_ROWS // LOCAL_STEP, step, 0, unroll=3)
    y = _dot_tn(pick_ref[...], scaled_scr[...])
    out = _rms(h_ref[...] + y, g_ref[...])

    @pl.when(i < n_prompt_tiles)
    def _():
        op_ref[...] = out

    @pl.when(i >= n_prompt_tiles)
    def _():
        os_ref[...] = out


def _combine(tables, route, pick, h2, g_final, ys, tp, ts):
    tm = TOKEN_TILE
    npt, nst = tp // tm, ts // tm
    tok = lambda w: pl.BlockSpec((tm, w), lambda i, *_: (i, 0))
    return pl.pallas_call(
        functools.partial(_combine_kernel, n_prompt_tiles=npt),
        out_shape=[jax.ShapeDtypeStruct((tp, D_MODEL), F32), jax.ShapeDtypeStruct((ts, D_MODEL), F32)],
        grid_spec=pltpu.PrefetchScalarGridSpec(
            num_scalar_prefetch=4,
            grid=(npt + nst,),
            in_specs=[pl.BlockSpec((2 * TOP_K, tm), lambda i, *_: (0, i)),
                      pl.BlockSpec((LOCAL_ROWS, tm), lambda i, *_: (i, 0)), tok(D_MODEL),
                      pl.BlockSpec((1, D_MODEL), lambda i, *_: (0, 0)),
                      pl.BlockSpec(memory_space=pl.ANY)],
            out_specs=[pl.BlockSpec((tm, D_MODEL), lambda i, *_: (jnp.minimum(i, npt - 1), 0)),
                       pl.BlockSpec((tm, D_MODEL), lambda i, *_: (jnp.maximum(i - npt, 0), 0))],
            scratch_shapes=[pltpu.VMEM((2, LOCAL_ROWS, D_MODEL), F32), pltpu.VMEM((LOCAL_ROWS, D_MODEL), BF16),
                            pltpu.SemaphoreType.DMA((2,))]),
        compiler_params=_params(("arbitrary",)),
        name="moe_combine_norm",
    )(*tables, route, pick, h2, g_final, ys)


def _rope_tables(seq_len):
    pos = jnp.arange(seq_len, dtype=F32)[:, None]
    lane = np.arange(LANES)

    def table(half, lane_freq, first, active):
        inv = ROPE_BASE ** (-jnp.arange(half, dtype=F32) / half)
        ang = pos * inv[None, :]
        cos = jnp.cos(ang)[:, lane_freq]
        sin = jnp.sin(ang)[:, lane_freq]
        c = jnp.where(active[None, :], cos, 1.0)
        s = jnp.where(active[None, :], jnp.where(first[None, :], -sin, sin), 0.0)
        return c, s

    half_r = RET_DK // 2
    cr, sr = table(half_r, lane % half_r, (lane % RET_DK) < half_r, np.ones(LANES, bool))
    half_m = MLA_ROPE // 2
    rel = lane - MLA_NOPE
    active = (rel >= 0) & (rel < MLA_ROPE)
    cm, sm = table(half_m, np.where(active, rel % half_m, 0), active & (rel < half_m), active)
    return cr, sr, cm, sm


def kernel(x_prompt, x_sample, mem_prompt, mem_sample, norm_mix, w_in, ret_decay_fwd, ret_decay_bwd, ret_gn, q_a_norm, w_uq, kv_a_norm, w_ukv, w_mix_out, norm_cross, norm_mem, w_xq, w_xkv, w_xo, norm_ffn, router_w, router_b, w_gu, b_gu, w_down, b_down, norm_final):
    assert norm_mix.shape[0] == 1, "single layer"
    bp, sp, d = x_prompt.shape
    bs, ss, _ = x_sample.shape
    tp, ts = bp * sp, bs * ss
    t = tp + ts
    assert d == D_MODEL and sp % TOKEN_TILE == 0 and ss % TOKEN_TILE == 0 and tp % ss == 0 and sp >= ss
    mem_len = mem_prompt.shape[1]

    w_in0 = w_in[0]
    w_in_pad = jnp.zeros((D_MODEL, D_IN_PAD), F32)
    w_in_pad = w_in_pad.at[:, :2816].set(w_in0[:, :2816])
    w_in_pad = w_in_pad.at[:, 2816 + MLA_NOPE:2816 + MLA_NOPE + MLA_ROPE].set(w_in0[:, 2816:]).astype(BF16)
    wuq_pad = jnp.pad(w_uq[0].reshape(Q_LORA, MLA_HEADS, MLA_NOPE + MLA_ROPE),
                      ((0, 0), (0, 0), (0, LANES - MLA_NOPE - MLA_ROPE))).reshape(Q_LORA, MLA_HEADS * LANES).astype(BF16)
    wukv = w_ukv[0].reshape(KV_LORA, MLA_HEADS, MLA_NOPE + MLA_V)
    wuk_pad = jnp.pad(wukv[:, :, :MLA_NOPE], ((0, 0), (0, 0), (0, LANES - MLA_NOPE))).reshape(KV_LORA, MLA_HEADS * LANES).astype(BF16)
    wuv = jnp.pad(wukv[:, :, MLA_NOPE:], ((0, 0), (0, 0), (0, LANES - MLA_V))).reshape(KV_LORA, MLA_HEADS * LANES).astype(BF16)
    rw_t = router_w[0].T.astype(BF16)
    rb_col = jnp.broadcast_to(router_b[0].astype(F32)[:, None], (N_EXPERTS, LANES))
    lgf = jnp.log1p(-jnp.exp2(ret_decay_fwd[0].astype(F32)))
    lgb = jnp.log1p(-jnp.exp2(ret_decay_bwd[0].astype(F32)))
    tabs = _rope_tables(sp)

    xp = x_prompt.reshape(tp, D_MODEL)
    xs = x_sample.reshape(ts, D_MODEL)
    mem = jnp.concatenate([mem_prompt.reshape(-1, D_MODEL), mem_sample.reshape(-1, D_MODEL)], axis=0)

    kvmem = _mem_kv(mem, norm_mem[0][None, :], w_xkv[0].astype(BF16)).reshape(bp + bs, mem_len, 2 * D_MODEL)

    rq, rk, rv, rg, qm, km, vm = _in_proj(xp, xs, norm_mix[0][None, :], w_in_pad, q_a_norm[0][None, :], wuq_pad,
                                          kv_a_norm[0][None, :], wuk_pad, wuv, tabs, sp, ss)

    gn = ret_gn[0][None, :]
    ret_p = _retention(lgf, lgb, rq, rk, rv, rg, gn, sp, bp, 0)
    ret_s = _retention(lgf, lgb, rq, rk, rv, rg, gn, ss, bs, tp // ss)
    mla_p = _mla_attn(qm, km, vm, sp, bp, 0)
    mla_s = _mla_attn(qm, km, vm, ss, bs, tp // ss)

    h2, xn, route, stats, counts = _mix_cross(
        xp, xs, ret_p, ret_s, mla_p, mla_s, w_mix_out[0].astype(BF16), norm_cross[0][None, :], w_xq[0].astype(BF16),
        kvmem, w_xo[0].astype(BF16), norm_ffn[0][None, :], rw_t, rb_col, sp, ss)

    used = counts[:, 0]
    padded = (used + MOE_BLOCK - 1) // MOE_BLOCK * MOE_BLOCK
    pend = jnp.cumsum(padded)
    pstart = pend - padded
    tile_cnt, tile_before, tile_lstart = (stats[:, :, r] for r in range(3))
    n_tiles = stats.shape[0]
    tile_chunks = (tile_cnt + ROW_CHUNK - 1) // ROW_CHUNK
    tables = (tile_chunks.reshape(-1),
              tile_lstart.reshape(-1),
              (pstart[None, :] + tile_before).reshape(-1),
              jnp.sum(tile_chunks, axis=-1))
    ztables = (pstart + used, (padded - used) // ROW_CHUNK)
    n_blocks = -(-(t * TOP_K + N_EXPERTS * (n_tiles * (ROW_CHUNK - 1) + MOE_BLOCK - 1)) // MOE_BLOCK)
    blk = jnp.arange(n_blocks, dtype=jnp.int32)
    n_active = (pend[-1] // MOE_BLOCK).astype(jnp.int32)
    block_i = jnp.minimum(blk, n_active - 1)
    block_e = jnp.minimum(jnp.sum((block_i[:, None] * MOE_BLOCK >= pend[None, :]).astype(jnp.int32), axis=-1),
                          N_EXPERTS - 1).astype(jnp.int32)
    block_first = jnp.concatenate([jnp.ones((1,), jnp.int32), (block_e[1:] != block_e[:-1]).astype(jnp.int32)])
    block_seg = jnp.cumsum(block_first) - 1
    next_blk = jnp.sum(jnp.where(block_e[:, None] == jnp.arange(N_EXPERTS)[None, :], pend[None, :], 0), axis=-1) // MOE_BLOCK
    block_next = jnp.where(next_blk < n_active, block_e[jnp.minimum(next_blk, n_blocks - 1)], -1).astype(jnp.int32)

    xs_sorted, pick = _dispatch(tables, ztables, xn, route, n_blocks * MOE_BLOCK)
    ys = _experts((block_e, block_i, n_active[None], block_first, block_seg.astype(jnp.int32), block_next),
                  xs_sorted, w_gu[0], b_gu[0][:, None, :], w_down[0], b_down[0][:, None, :])
    out_p, out_s = _combine(tables, route, pick, h2, norm_final[None, :], ys, tp, ts)
    return out_p.reshape(bp, sp, D_MODEL), out_s.reshape(bs, ss, D_MODEL)
```

```python
import functools
import math

import jax
import jax.numpy as jnp
import numpy as np
from jax import lax
from jax.experimental import pallas as pl
from jax.experimental.pallas import tpu as pltpu

D_MODEL = 1024
RET_HEADS = 8
RET_DK = 64
RET_WIDTH = 512
CHUNK = 128
MLA_HEADS = 8
MLA_NOPE = 64
MLA_ROPE = 32
MLA_V = 64
MLA_WIDTH = 512
Q_LORA = 512
KV_LORA = 256
ROPE_BASE = 10000.0
X_HEADS = 4
X_HEAD_DIM = 256
N_EXPERTS = 32
TOP_K = 4
SWIGLU_LIMIT = 7.0
SWIGLU_ALPHA = 1.702
MOE_BLOCK = 512
EPS = 1e-6

LANES = 128
VMEM_LIMIT = 56 * 1024 * 1024

TOKEN_TILE = 512
ROW_PARTS = 2
ATTN_TQ = 1024
ATTN_TK = 1024
ATTN_UNROLL = 4
RET_UNROLL = 16
ROW_CHUNK = 8
GROUP_BITS = (TOKEN_TILE // ROW_CHUNK).bit_length()
WAIT_CHUNKS = 16
LOCAL_STEP = 256
LOCAL_ROWS = -(-(TOKEN_TILE * 4 + 32 * (ROW_CHUNK - 1)) // LOCAL_STEP) * LOCAL_STEP
D_IN_PAD = 4 * 512 + Q_LORA + KV_LORA + LANES
NEG_BIG = -1e30

F32 = jnp.float32
BF16 = jnp.bfloat16


def _params(sem, vmem=VMEM_LIMIT):
    return pltpu.CompilerParams(dimension_semantics=sem, vmem_limit_bytes=vmem)


def _const_spec(shape):
    nd = len(shape)
    return pl.BlockSpec(shape, lambda *_: (0,) * nd, pipeline_mode=pl.Buffered(1))


def _rms(x, g):
    ms = jnp.mean(x * x, axis=-1, keepdims=True)
    return x * lax.rsqrt(ms + EPS) * g


def _dot(a, b):
    return jnp.dot(a, b, preferred_element_type=F32)


def _dot_nt(a, b):
    return lax.dot_general(a, b, (((1,), (1,)), ((), ())), preferred_element_type=F32)


def _dot_tn(a, b):
    return lax.dot_general(a, b, (((0,), (0,)), ((), ())), preferred_element_type=F32)


def _mem_kv_kernel(mem_ref, g_ref, w_ref, o_ref):
    mn = _rms(mem_ref[...], g_ref[...])
    o_ref[...] = _dot(mn.astype(BF16), w_ref[...]).astype(BF16)


def _mem_kv(mem, g, w):
    rows, mem_len = mem.shape[0], 256
    return pl.pallas_call(
        _mem_kv_kernel,
        out_shape=jax.ShapeDtypeStruct((rows, 2 * D_MODEL), BF16),
        grid=(rows // mem_len,),
        in_specs=[pl.BlockSpec((mem_len, D_MODEL), lambda i: (i, 0)),
                  _const_spec((1, D_MODEL)),
                  _const_spec((D_MODEL, 2 * D_MODEL))],
        out_specs=pl.BlockSpec((mem_len, 2 * D_MODEL), lambda i: (i, 0)),
        compiler_params=_params(("arbitrary",)),
        name="mem_kv",
    )(mem, g, w)


def _rope_slab(x, c, ss, first, shift_up, shift_down):
    swap = jnp.where(first, pltpu.roll(x, shift_up, 1), pltpu.roll(x, shift_down, 1))
    return x * c + swap * ss


def _in_proj_kernel(xp_ref, xs_ref, g_ref, win_ref, qan_ref, wuq_ref, kvan_ref, wuk_ref, wuv_ref,
                    cr_ref, sr_ref, cm_ref, sm_ref,
                    rq_ref, rk_ref, rv_ref, rg_ref, qm_ref, km_ref, vm_ref, *, n_prompt_tiles):
    i = pl.program_id(0)
    tm = rq_ref.shape[0]
    lane = lax.broadcasted_iota(jnp.int32, (1, LANES), 1)
    ret_first = (lane % RET_DK) < (RET_DK // 2)
    mla_first = (lane >= MLA_NOPE) & (lane < MLA_NOPE + MLA_ROPE // 2)
    half_r, half_m = RET_DK // 2, MLA_ROPE // 2
    q_scale = (MLA_NOPE + MLA_ROPE) ** -0.5 * math.log2(math.e)

    for part in range(ROW_PARTS):
        rows = slice(part * (tm // ROW_PARTS), (part + 1) * (tm // ROW_PARTS))
        x = jnp.where(i < n_prompt_tiles, xp_ref[rows, :], xs_ref[rows, :])
        xn = _rms(x, g_ref[...])
        proj = _dot(xn.astype(BF16), win_ref[...])
        cr, sr, cm, sm = cr_ref[rows, :], sr_ref[rows, :], cm_ref[rows, :], sm_ref[rows, :]

        for s in range(RET_WIDTH // LANES):
            lo = s * LANES
            q = _rope_slab(proj[:, lo:lo + LANES], cr, sr, ret_first, LANES - half_r, half_r)
            rq_ref[rows, lo:lo + LANES] = q.astype(BF16)
            k = _rope_slab(proj[:, 512 + lo:512 + lo + LANES], cr, sr, ret_first, LANES - half_r, half_r)
            rk_ref[rows, lo:lo + LANES] = (k * (RET_DK ** -0.5)).astype(BF16)
        rv_ref[rows, :] = proj[:, 1024:1536].astype(BF16)
        rg_ref[rows, :] = proj[:, 1536:2048].astype(BF16)

        cq = _rms(proj[:, 2048:2048 + Q_LORA], qan_ref[...])
        qm = _dot(cq.astype(BF16), wuq_ref[...])
        ckv = _rms(proj[:, 2560:2560 + KV_LORA], kvan_ref[...]).astype(BF16)
        kn = _dot(ckv, wuk_ref[...])
        vm = _dot(ckv, wuv_ref[...])
        for h in range(MLA_HEADS):
            lo = h * LANES
            vm_ref[rows, lo:lo + LANES] = jnp.where(lane == MLA_V, 1.0, vm[:, lo:lo + LANES]).astype(BF16)
        kr = proj[:, 2816:2816 + LANES]
        kpe = _rope_slab(kr, cm, sm, mla_first, LANES - half_m, half_m)
        for h in range(MLA_HEADS):
            lo = h * LANES
            qh = _rope_slab(qm[:, lo:lo + LANES], cm, sm, mla_first, LANES - half_m, half_m)
            qm_ref[rows, lo:lo + LANES] = (qh * q_scale).astype(BF16)
            km_ref[rows, lo:lo + LANES] = (kn[:, lo:lo + LANES] + kpe).astype(BF16)


def _in_proj(xp, xs, g, w_in_pad, qan, wuq_pad, kvan, wuk_pad, wuv, tabs, sp, ss):
    tm = TOKEN_TILE
    tp, ts = xp.shape[0], xs.shape[0]
    npt, nst = tp // tm, ts // tm
    t = tp + ts
    tiles_p, tiles_s = sp // tm, ss // tm

    def xp_map(i):
        return (jnp.minimum(i, npt - 1), 0)

    def xs_map(i):
        return (jnp.maximum(i - npt, 0), 0)

    def tab_map(i):
        return (jnp.where(i < npt, i % tiles_p, (i - npt) % tiles_s), 0)

    tok = lambda w: pl.BlockSpec((tm, w), lambda i: (i, 0))
    widths = (512, 512, 512, 512, 1024, 1024, 1024)
    return pl.pallas_call(
        functools.partial(_in_proj_kernel, n_prompt_tiles=npt),
        out_shape=[jax.ShapeDtypeStruct((t, w), BF16) for w in widths],
        grid=(npt + nst,),
        in_specs=[pl.BlockSpec((tm, D_MODEL), xp_map), pl.BlockSpec((tm, D_MODEL), xs_map),
                  _const_spec((1, D_MODEL)), _const_spec((D_MODEL, D_IN_PAD)),
                  _const_spec((1, Q_LORA)), _const_spec((Q_LORA, MLA_HEADS * LANES)),
                  _const_spec((1, KV_LORA)), _const_spec((KV_LORA, MLA_HEADS * LANES)),
                  _const_spec((KV_LORA, MLA_HEADS * LANES))]
                 + [pl.BlockSpec((tm, LANES), tab_map)] * 4,
        out_specs=[tok(w) for w in widths],
        compiler_params=_params(("arbitrary",)),
        name="in_proj",
    )(xp, xs, g, w_in_pad, qan, wuq_pad, kvan, wuk_pad, wuv, *tabs)


def _retention_kernel(lgf_ref, lgb_ref, q_ref, k_ref, v_ref, g_ref, gn_ref, o_ref, kvf_scr, kvb_scr, st_scr, *, n_chunks):
    hp = pl.program_id(1)
    c = CHUNK
    lane = lax.broadcasted_iota(jnp.int32, (1, c), 1)
    row = lax.broadcasted_iota(jnp.int32, (c, 1), 0)
    lane_h0 = lane < RET_DK
    row_h0 = row < RET_DK
    lgf0, lgf1 = lgf_ref[2 * hp], lgf_ref[2 * hp + 1]
    lgb0, lgb1 = lgb_ref[2 * hp], lgb_ref[2 * hp + 1]
    lgf_lane = jnp.where(lane_h0, lgf0, lgf1)
    lgb_lane = jnp.where(lane_h0, lgb0, lgb1)
    t = row.astype(F32)
    q_dec_f = jnp.exp((t + 1.0) * lgf_lane)
    q_dec_b = jnp.exp((c - t) * lgb_lane)
    k_dec_f = jnp.exp((c - 1.0 - t) * lgf_lane)
    k_dec_b = jnp.exp(t * lgb_lane)
    chunk_dec_f = jnp.exp(c * jnp.where(row_h0, lgf0, lgf1))
    chunk_dec_b = jnp.exp(c * jnp.where(row_h0, lgb0, lgb1))
    same_head = row_h0 == lane_h0
    diff = t - lane.astype(F32)
    d_intra = []
    for lgf, lgb in ((lgf0, lgb0), (lgf1, lgb1)):
        fwd = jnp.where(diff >= 0, jnp.exp(jnp.where(diff >= 0, diff, 0.0) * lgf), 0.0)
        bwd = jnp.where(diff < 0, jnp.exp(jnp.where(diff < 0, -diff, 0.0) * lgb), 0.0)
        d_intra.append(fwd + bwd)
    avg = jnp.where(same_head, 1.0 / RET_DK, 0.0).astype(BF16)
    gn = gn_ref[...]

    def chunk(n):
        return pl.ds(pl.multiple_of(n * c, c), c)

    unroll = min(RET_UNROLL, n_chunks)

    def chunk_kv(n, _):
        k = k_ref[chunk(n), :].astype(F32)
        kd = jnp.concatenate([(k * k_dec_f).astype(BF16), (k * k_dec_b).astype(BF16)], axis=-1)
        kv = _dot_tn(kd, v_ref[chunk(n), :])
        kvf_scr[n] = jnp.where(same_head, kv[0:c], 0.0)
        kvb_scr[n] = jnp.where(same_head, kv[c:2 * c], 0.0)
        return 0

    lax.fori_loop(0, n_chunks, chunk_kv, 0, unroll=unroll)

    def fwd_state(n, s_f):
        st_scr[n, 0:c, :] = s_f.astype(BF16)
        return chunk_dec_f * s_f + kvf_scr[n]

    lax.fori_loop(0, n_chunks, fwd_state, jnp.zeros((c, c), F32))

    def bwd_state(j, s_b):
        n = n_chunks - 1 - j
        st_scr[n, c:2 * c, :] = s_b.astype(BF16)
        return chunk_dec_b * s_b + kvb_scr[n]

    lax.fori_loop(0, n_chunks, bwd_state, jnp.zeros((c, c), F32))

    d_both = jnp.concatenate(d_intra, axis=0)
    zero_q = jnp.zeros((c, c), BF16)

    def chunk_out(n):
        q = q_ref[chunk(n), :]
        qf = q.astype(F32)
        q_both = jnp.concatenate([(qf * q_dec_f).astype(BF16), (qf * q_dec_b).astype(BF16)], axis=-1)
        q_heads = jnp.concatenate([jnp.where(lane_h0, q, zero_q), jnp.where(lane_h0, zero_q, q)], axis=0)
        p = (_dot_nt(q_heads, k_ref[chunk(n), :]) * d_both).astype(BF16)
        inner = _dot(p, v_ref[chunk(n), :])
        return _dot(q_both, st_scr[n]) + jnp.where(lane_h0, inner[0:c], inner[c:2 * c])

    def split_rows(a):
        hi = a.astype(BF16)
        return jnp.concatenate([hi, (a - hi.astype(F32)).astype(BF16)], axis=0)

    def out_group(gi, _):
        rows = pl.ds(pl.multiple_of(gi * (unroll * c), unroll * c), unroll * c)
        y = jnp.concatenate([chunk_out(gi * unroll + u) for u in range(unroll)], axis=0)
        m = y.shape[0]
        mu2 = _dot(split_rows(y), avg)
        d = y - (mu2[0:m] + mu2[m:2 * m])
        var2 = _dot(split_rows(d * d), avg)
        yn = d * lax.rsqrt(var2[0:m] + var2[m:2 * m] + EPS) * gn
        gate = g_ref[rows, :].astype(F32)
        o_ref[rows, :] = (yn * (gate * jax.nn.sigmoid(gate))).astype(BF16)
        return 0

    lax.fori_loop(0, n_chunks // unroll, out_group, 0)


def _retention(lgf, lgb, rq, rk, rv, rg, gn, seq_len, n_seq, row_block0):
    n_chunks = seq_len // CHUNK
    hp_count = RET_WIDTH // LANES
    blk = pl.BlockSpec((seq_len, LANES), lambda b, hp, *_: (row_block0 + b, hp))
    return pl.pallas_call(
        functools.partial(_retention_kernel, n_chunks=n_chunks),
        out_shape=jax.ShapeDtypeStruct((n_seq * seq_len, RET_WIDTH), BF16),
        grid_spec=pltpu.PrefetchScalarGridSpec(
            num_scalar_prefetch=2,
            grid=(n_seq, hp_count),
            in_specs=[blk, blk, blk, blk, pl.BlockSpec((1, LANES), lambda b, hp, *_: (0, hp))],
            out_specs=pl.BlockSpec((seq_len, LANES), lambda b, hp, *_: (b, hp)),
            scratch_shapes=[pltpu.VMEM((n_chunks, CHUNK, CHUNK), F32), pltpu.VMEM((n_chunks, CHUNK, CHUNK), F32),
                            pltpu.VMEM((n_chunks, 2 * CHUNK, CHUNK), BF16)]),
        compiler_params=_params(("arbitrary", "arbitrary")),
        name="retention",
    )(lgf, lgb, rq, rk, rv, rg, gn)


def _mla_attn_kernel(q_ref, k_ref, v_ref, o_ref, *, tk, n_kv):
    tq = q_ref.shape[0]
    lane = lax.broadcasted_iota(jnp.int32, (1, LANES), 1)
    qs = (q_ref[:, :LANES], q_ref[:, LANES:])

    def body(j, carry):
        rows = pl.ds(pl.multiple_of(j * tk, tk), tk)
        new = []
        for h in range(2):
            m, acc = carry[h]
            s = _dot_nt(qs[h], k_ref[rows, h * LANES:(h + 1) * LANES])
            m_new = jnp.maximum(m, jnp.max(s, axis=-1, keepdims=True))
            alpha = jnp.exp2(m - m_new)
            p = jnp.exp2((s - m_new).astype(BF16))
            acc = alpha * acc + _dot(p, v_ref[rows, h * LANES:(h + 1) * LANES])
            new.append((m_new, acc))
        return tuple(new)

    init = (jnp.full((tq, 1), NEG_BIG, F32), jnp.zeros((tq, LANES), F32))
    (_, acc0), (_, acc1) = lax.fori_loop(0, n_kv, body, (init, init), unroll=min(ATTN_UNROLL, n_kv))
    out0 = acc0 / acc0[:, MLA_V:MLA_V + 1]
    out1 = acc1 / acc1[:, MLA_V:MLA_V + 1]
    o_ref[...] = jnp.where(lane < MLA_V, out0, pltpu.roll(out1, MLA_V, 1)).astype(BF16)


def _mla_attn(qm, km, vm, seq_len, n_seq, row_block0):
    tq, tk = min(ATTN_TQ, seq_len), min(ATTN_TK, seq_len)
    hp_count = MLA_HEADS // 2
    nq = seq_len // tq
    return pl.pallas_call(
        functools.partial(_mla_attn_kernel, tk=tk, n_kv=seq_len // tk),
        out_shape=jax.ShapeDtypeStruct((n_seq * seq_len, MLA_WIDTH), BF16),
        grid=(n_seq, hp_count, nq),
        in_specs=[pl.BlockSpec((tq, 2 * LANES), lambda b, hp, i: ((row_block0 + b) * nq + i, hp)),
                  pl.BlockSpec((seq_len, 2 * LANES), lambda b, hp, i: (row_block0 + b, hp)),
                  pl.BlockSpec((seq_len, 2 * LANES), lambda b, hp, i: (row_block0 + b, hp))],
        out_specs=pl.BlockSpec((tq, LANES), lambda b, hp, i: (b * nq + i, hp)),
        compiler_params=_params(("arbitrary", "arbitrary", "arbitrary")),
        name="mla_attn",
    )(qm, km, vm)


def _mix_cross_kernel(xp_ref, xs_ref, rp_ref, rs_ref, mp_ref, ms_ref, wmix_ref, gx_ref, wxq_ref, kv_ref, wxo_ref,
                      gf_ref, rwt_ref, rbc_ref,
                      h_ref, xn_ref, route_ref, stat_ref, cnt_ref, carry_scr, *, n_prompt_tiles):
    i = pl.program_id(0)
    tm = h_ref.shape[0]
    is_p = i < n_prompt_tiles
    x = jnp.where(is_p, xp_ref[...], xs_ref[...])
    ret = jnp.where(is_p, rp_ref[...], rs_ref[...])
    mla = jnp.where(is_p, mp_ref[...], ms_ref[...])
    h1 = x + _dot(ret, wmix_ref[0:RET_WIDTH, :]) + _dot(mla, wmix_ref[RET_WIDTH:, :])

    hn = _rms(h1, gx_ref[...]).astype(BF16)
    q = (_dot(hn, wxq_ref[...]) * (X_HEAD_DIM ** -0.5)).astype(BF16)
    heads = []
    for h in range(X_HEADS):
        lo = h * X_HEAD_DIM
        s = _dot_nt(q[:, lo:lo + X_HEAD_DIM], kv_ref[0, :, lo:lo + X_HEAD_DIM])
        e = jnp.exp(s - jnp.max(s, axis=-1, keepdims=True))
        p = (e / jnp.sum(e, axis=-1, keepdims=True)).astype(BF16)
        heads.append(_dot(p, kv_ref[0, :, D_MODEL + lo:D_MODEL + lo + X_HEAD_DIM]).astype(BF16))
    h2 = h1 + _dot(jnp.concatenate(heads, axis=-1), wxo_ref[...])
    h_ref[...] = h2
    xn = _rms(h2, gf_ref[...]).astype(BF16)
    xn_ref[...] = xn

    ne = N_EXPERTS
    work = _dot_nt(rwt_ref[...], xn) + rbc_ref[:, 0:1]
    e_f = lax.broadcasted_iota(jnp.int32, (ne, tm), 0).astype(F32)
    vals, picks = [], []
    for _ in range(TOP_K):
        m = jnp.max(work, axis=0, keepdims=True)
        idx = jnp.min(jnp.where(work == m, e_f, float(ne)), axis=0, keepdims=True)
        sel = e_f == idx
        work = jnp.where(sel, -jnp.inf, work)
        vals.append(m)
        picks.append(jnp.where(sel, 1.0, 0.0))
    exps = [jnp.exp(v - vals[0]) for v in vals]
    denom = exps[0] + exps[1] + exps[2] + exps[3]
    onehot = picks[0] + picks[1] + picks[2] + picks[3]

    @pl.when(i == 0)
    def _():
        carry_scr[...] = jnp.zeros_like(carry_scr)

    t_row = lax.broadcasted_iota(jnp.int32, (tm, tm), 0)
    t_col = lax.broadcasted_iota(jnp.int32, (tm, tm), 1)
    earlier_tok = jnp.where(t_row < t_col, 1.0, 0.0).astype(BF16)
    before = _dot(onehot.astype(BF16), earlier_tok)
    tile_cnt = jnp.sum(onehot, axis=1, keepdims=True)
    group_rows = jnp.floor((tile_cnt + (ROW_CHUNK - 1)) * (1.0 / ROW_CHUNK)) * ROW_CHUNK
    carry_before = carry_scr[...]
    carry = carry_before + group_rows
    carry_scr[...] = carry
    cnt_ref[...] = carry.astype(jnp.int32)
    e_row = lax.broadcasted_iota(jnp.int32, (LANES, LANES), 0)
    e_col = lax.broadcasted_iota(jnp.int32, (LANES, LANES), 1)
    earlier_exp = jnp.where(e_col < e_row, 1.0, 0.0).astype(BF16)
    rows_pad = jnp.concatenate([jnp.broadcast_to(group_rows, (ne, LANES)), jnp.zeros((LANES - ne, LANES), F32)], axis=0)
    group_start = _dot(earlier_exp, rows_pad.astype(BF16))[0:ne]
    lane = lax.broadcasted_iota(jnp.int32, (ne, LANES), 1)
    stats = jnp.where(lane == 0, tile_cnt, jnp.where(lane == 1, carry_before, jnp.where(lane == 2, group_start, 0.0)))
    stat_ref[0] = stats.astype(jnp.int32)

    place = before + group_start[:, 0:1]
    row8 = lax.broadcasted_iota(jnp.int32, (2 * TOP_K, tm), 0)
    route = jnp.zeros((2 * TOP_K, tm), F32)
    for k in range(TOP_K):
        slot_k = jnp.sum(picks[k] * place, axis=0, keepdims=True)
        route = jnp.where(row8 == k, exps[k] / denom, jnp.where(row8 == TOP_K + k, slot_k, route))
    route_ref[...] = route


def _mix_cross(xp, xs, ret_p, ret_s, mla_p, mla_s, wmix, gx, wxq, kvmem, wxo, gf, rw_t, rb_col, sp, ss):
    tm = TOKEN_TILE
    tp, ts = xp.shape[0], xs.shape[0]
    npt, nst = tp // tm, ts // tm
    t = tp + ts
    n_seq_p = tp // sp
    mem_len = kvmem.shape[1]

    def p_map(i):
        return (jnp.minimum(i, npt - 1), 0)

    def s_map(i):
        return (jnp.maximum(i - npt, 0), 0)

    def kv_map(i):
        return (jnp.where(i < npt, i // (sp // tm), n_seq_p + (i - npt) // (ss // tm)), 0, 0)

    tok = lambda w: pl.BlockSpec((tm, w), lambda i: (i, 0))
    return pl.pallas_call(
        functools.partial(_mix_cross_kernel, n_prompt_tiles=npt),
        out_shape=[jax.ShapeDtypeStruct((t, D_MODEL), F32), jax.ShapeDtypeStruct((t, D_MODEL), BF16),
                   jax.ShapeDtypeStruct((2 * TOP_K, t), F32),
                   jax.ShapeDtypeStruct((npt + nst, N_EXPERTS, LANES), jnp.int32),
                   jax.ShapeDtypeStruct((N_EXPERTS, LANES), jnp.int32)],
        grid=(npt + nst,),
        in_specs=[pl.BlockSpec((tm, D_MODEL), p_map), pl.BlockSpec((tm, D_MODEL), s_map),
                  pl.BlockSpec((tm, RET_WIDTH), p_map), pl.BlockSpec((tm, RET_WIDTH), s_map),
                  pl.BlockSpec((tm, MLA_WIDTH), p_map), pl.BlockSpec((tm, MLA_WIDTH), s_map),
                  _const_spec((D_MODEL, D_MODEL)), _const_spec((1, D_MODEL)), _const_spec((D_MODEL, D_MODEL)),
                  pl.BlockSpec((1, mem_len, 2 * D_MODEL), kv_map),
                  _const_spec((D_MODEL, D_MODEL)), _const_spec((1, D_MODEL)),
                  _const_spec((N_EXPERTS, D_MODEL)), _const_spec((N_EXPERTS, LANES))],
        out_specs=[tok(D_MODEL), tok(D_MODEL),
                   pl.BlockSpec((2 * TOP_K, tm), lambda i: (0, i)),
                   pl.BlockSpec((1, N_EXPERTS, LANES), lambda i: (i, 0, 0)),
                   pl.BlockSpec((N_EXPERTS, LANES), lambda i: (0, 0))],
        scratch_shapes=[pltpu.VMEM((N_EXPERTS, LANES), F32)],
        compiler_params=_params(("arbitrary",)),
        name="mix_cross_router",
    )(xp, xs, ret_p, ret_s, mla_p, mla_s, wmix, gx, wxq, kvmem, wxo, gf, rw_t, rb_col)


def _chunk_copies(tile, nch_ref, lstart_ref, gstart_ref, local, hbm, sem, to_hbm, act):
    def per_expert(e, _):
        base = tile * N_EXPERTS + e
        n = nch_ref[base]
        ls = lstart_ref[base]
        gs = gstart_ref[base]
        def bit_copy(j):
            rows = ROW_CHUNK << j

            @pl.when(((n >> j) & 1) == 1)
            def _():
                off = (n & ((1 << j) - 1)) * ROW_CHUNK
                l_rows = local.at[pl.ds(pl.multiple_of(ls + off, ROW_CHUNK), rows)]
                g_rows = hbm.at[pl.ds(pl.multiple_of(gs + off, ROW_CHUNK), rows)]
                act(pltpu.make_async_copy(l_rows, g_rows, sem) if to_hbm else pltpu.make_async_copy(g_rows, l_rows, sem))

        for j in range(GROUP_BITS):
            bit_copy(j)
        return 0

    lax.fori_loop(0, N_EXPERTS, per_expert, 0)


def _wait_tile(tile, tot_ref, local, hbm, sem, to_hbm):
    total = tot_ref[tile]

    def wait_rows(rows):
        l_rows = local.at[pl.ds(0, rows)]
        g_rows = hbm.at[pl.ds(0, rows)]
        (pltpu.make_async_copy(l_rows, g_rows, sem) if to_hbm else pltpu.make_async_copy(g_rows, l_rows, sem)).wait()

    def big(_, c):
        wait_rows(ROW_CHUNK * WAIT_CHUNKS)
        return c

    def small(_, c):
        wait_rows(ROW_CHUNK)
        return c

    lax.fori_loop(0, total // WAIT_CHUNKS, big, 0)
    lax.fori_loop(0, total % WAIT_CHUNKS, small, 0)


def _start(cp):
    cp.start()


def _wait(cp):
    cp.wait()


def _dispatch_kernel(nch_ref, lstart_ref, gstart_ref, tot_ref, zstart_ref, znch_ref,
                     x_ref, route_ref, xs_hbm, pick_ref, buf, zero_scr, sem, zsem):
    i = pl.program_id(0)
    last = pl.num_programs(0) - 1
    tm = x_ref.shape[0]
    b = i % 2
    slot_rows = route_ref[TOP_K:2 * TOP_K, :]
    x = x_ref[...]

    def build(s, _):
        r0 = pl.multiple_of(s * LOCAL_STEP, LOCAL_STEP)
        rid = (lax.broadcasted_iota(jnp.int32, (LOCAL_STEP, tm), 0) + r0).astype(F32)
        hit = rid == slot_rows[0:1, :]
        for k in range(1, TOP_K):
            hit = hit | (rid == slot_rows[k:k + 1, :])
        pick = jnp.where(hit, 1.0, 0.0).astype(BF16)
        pick_ref[pl.ds(r0, LOCAL_STEP), :] = pick
        rows = _dot(pick, x)
        half = D_MODEL // 2
        left = lax.bitcast_convert_type(rows[:, :half], jnp.uint32)
        right = lax.bitcast_convert_type(rows[:, half:], jnp.uint32)
        buf[b, pl.ds(r0, LOCAL_STEP), :] = left | (right >> 16)
        return 0

    lax.fori_loop(0, LOCAL_ROWS // LOCAL_STEP, build, 0, unroll=3)

    _chunk_copies(i, nch_ref, lstart_ref, gstart_ref, buf.at[b], xs_hbm, sem.at[b], True, _start)

    @pl.when(i > 0)
    def _():
        _wait_tile(i - 1, tot_ref, buf.at[1 - b], xs_hbm, sem.at[1 - b], True)

    @pl.when(i == last)
    def _():
        _wait_tile(i, tot_ref, buf.at[b], xs_hbm, sem.at[b], True)
        zero_scr[...] = jnp.zeros_like(zero_scr)

        def fill(act):
            def per_expert(e, _):
                def per_chunk(c, _):
                    row = pl.multiple_of(zstart_ref[e] + c * ROW_CHUNK, ROW_CHUNK)
                    act(pltpu.make_async_copy(zero_scr, xs_hbm.at[pl.ds(row, ROW_CHUNK)], zsem))
                    return 0

                lax.fori_loop(0, znch_ref[e], per_chunk, 0)
                return 0

            lax.fori_loop(0, N_EXPERTS, per_expert, 0)

        fill(_start)
        fill(_wait)


def _dispatch(tables, ztables, xn, route, n_rows):
    tm = TOKEN_TILE
    t = xn.shape[0]
    return pl.pallas_call(
        _dispatch_kernel,
        out_shape=[jax.ShapeDtypeStruct((n_rows, D_MODEL // 2), jnp.uint32),
                   jax.ShapeDtypeStruct((t // tm * LOCAL_ROWS, tm), BF16)],
        grid_spec=pltpu.PrefetchScalarGridSpec(
            num_scalar_prefetch=6,
            grid=(t // tm,),
            in_specs=[pl.BlockSpec((tm, D_MODEL), lambda i, *_: (i, 0)),
                      pl.BlockSpec((2 * TOP_K, tm), lambda i, *_: (0, i))],
            out_specs=[pl.BlockSpec(memory_space=pl.ANY),
                       pl.BlockSpec((LOCAL_ROWS, tm), lambda i, *_: (i, 0))],
            scratch_shapes=[pltpu.VMEM((2, LOCAL_ROWS, D_MODEL // 2), jnp.uint32),
                            pltpu.VMEM((ROW_CHUNK, D_MODEL // 2), jnp.uint32),
                            pltpu.SemaphoreType.DMA((2,)), pltpu.SemaphoreType.DMA(())]),
        compiler_params=_params(("arbitrary",)),
        name="moe_dispatch",
    )(*tables, *ztables, xn, route)


def _experts_kernel(be_ref, bi_ref, na_ref, first_ref, seg_ref, nxt_ref, x_ref, bgu_ref, bd_ref, wgu_hbm, wd_hbm, o_ref,
                    wgu_f32, wd_f32, wgu_scr, wd_scr, sem):
    j = pl.program_id(0)
    active = j < na_ref[0]

    def weight_copies(e, slot):
        return (pltpu.make_async_copy(wgu_hbm.at[e], wgu_f32.at[slot], sem.at[slot, 0]),
                pltpu.make_async_copy(wd_hbm.at[e], wd_f32.at[slot], sem.at[slot, 1]))

    @pl.when(j == 0)
    def _():
        for cp in weight_copies(be_ref[0], 0):
            cp.start()

    @pl.when(active & (first_ref[j] == 1))
    def _():
        slot = seg_ref[j] % 2
        for cp in weight_copies(be_ref[j], slot):
            cp.wait()
        wgu_scr[...] = wgu_f32[slot].astype(BF16)
        wd_scr[...] = wd_f32[slot].astype(BF16)

        @pl.when(nxt_ref[j] >= 0)
        def _():
            for cp in weight_copies(nxt_ref[j], 1 - slot):
                cp.start()

    @pl.when(active)
    def _():
        words = x_ref[...]
        left = lax.bitcast_convert_type(words & jnp.uint32(0xFFFF0000), F32)
        right = lax.bitcast_convert_type(words << 16, F32)
        x = jnp.concatenate([left, right], axis=-1).astype(BF16)
        gu =_dot(x, wgu_scr[...]) + bgu_ref[0]
        gate = jnp.minimum(gu[:, :D_MODEL], SWIGLU_LIMIT)
        up = jnp.clip(gu[:, D_MODEL:], -SWIGLU_LIMIT, SWIGLU_LIMIT)
        hid = (up + 1.0) * (gate * jax.nn.sigmoid(SWIGLU_ALPHA * gate))
        o_ref[...] = _dot(hid.astype(BF16), wd_scr[...]) + bd_ref[0]


def _experts(block_tables, xs, wgu, bgu, wd, bd):
    n_blocks = xs.shape[0] // MOE_BLOCK
    row_map = lambda j, be, bi, *_: (bi[j], 0)
    e_map = lambda j, be, *_: (be[j], 0, 0)
    return pl.pallas_call(
        _experts_kernel,
        out_shape=jax.ShapeDtypeStruct((xs.shape[0], D_MODEL), F32),
        grid_spec=pltpu.PrefetchScalarGridSpec(
            num_scalar_prefetch=6,
            grid=(n_blocks,),
            in_specs=[pl.BlockSpec((MOE_BLOCK, D_MODEL // 2), row_map),
                      pl.BlockSpec((1, 1, 2 * D_MODEL), e_map),
                      pl.BlockSpec((1, 1, D_MODEL), e_map),
                      pl.BlockSpec(memory_space=pl.ANY),
                      pl.BlockSpec(memory_space=pl.ANY)],
            out_specs=pl.BlockSpec((MOE_BLOCK, D_MODEL), row_map),
            scratch_shapes=[pltpu.VMEM((2, D_MODEL, 2 * D_MODEL), F32), pltpu.VMEM((2, D_MODEL, D_MODEL), F32),
                            pltpu.VMEM((D_MODEL, 2 * D_MODEL), BF16), pltpu.VMEM((D_MODEL, D_MODEL), BF16),
                            pltpu.SemaphoreType.DMA((2, 2))]),
        compiler_params=_params(("arbitrary",)),
        name="moe_experts",
    )(*block_tables, xs, bgu, bd, wgu, wd)


def _combine_kernel(nch_ref, lstart_ref, gstart_ref, tot_ref, route_ref, pick_ref, h_ref, g_ref, ys_hbm,
                    op_ref, os_ref, buf, scaled_scr, sem, *, n_prompt_tiles):
    i = pl.program_id(0)
    n = pl.num_programs(0)
    tm = h_ref.shape[0]
    b = i % 2

    @pl.when(i == 0)
    def _():
        buf[...] = jnp.zeros_like(buf)
        _chunk_copies(0, nch_ref, lstart_ref, gstart_ref, buf.at[0], ys_hbm, sem.at[0], False, _start)

    @pl.when(i + 1 < n)
    def _():
        _chunk_copies(i + 1, nch_ref, lstart_ref, gstart_ref, buf.at[1 - b], ys_hbm, sem.at[1 - b], False, _start)

    _wait_tile(i, tot_ref, buf.at[b], ys_hbm, sem.at[b], False)

    info = route_ref[...]
    info_hi = info.astype(BF16).astype(F32)
    gap = jnp.zeros((LANES - 2 * TOP_K, tm), F32)
    info2 = jnp.concatenate([info_hi, gap, info - info_hi, gap], axis=0).astype(BF16)
    lane = lax.broadcasted_iota(jnp.int32, (LOCAL_STEP, LANES), 1)

    def step(s, _):
        r0 = pl.multiple_of(s * LOCAL_STEP, LOCAL_STEP)
        pick = pick_ref[pl.ds(r0, LOCAL_STEP), :]
        own2 = _dot_nt(pick, info2)
        own = own2[:, :LANES] + own2[:, LANES:]
        rcol = (lax.broadcasted_iota(jnp.int32, (LOCAL_STEP, 1), 0) + r0).astype(F32)
        mine = jnp.where((own == rcol) & (lane >= TOP_K) & (lane < 2 * TOP_K), 1.0, 0.0)
        g_col = jnp.sum(own * pltpu.roll(mine, LANES - TOP_K, 1), axis=-1, keepdims=True)
        rows = buf[b, pl.ds(r0, LOCAL_STEP), :]
        scaled_scr[pl.ds(r0, LOCAL_STEP), :] = jnp.where(g_col != 0.0, rows * g_col, 0.0).astype(BF16)
        return 0

    lax.fori_loop(0, LOCAL_ROWS // LOCAL_STEP, step, 0, unroll=3)
    y = _dot_tn(pick_ref[...], scaled_scr[...])
    out = _rms(h_ref[...] + y, g_ref[...])

    @pl.when(i < n_prompt_tiles)
    def _():
        op_ref[...] = out

    @pl.when(i >= n_prompt_tiles)
    def _():
        os_ref[...] = out


def _combine(tables, route, pick, h2, g_final, ys, tp, ts):
    tm = TOKEN_TILE
    npt, nst = tp // tm, ts // tm
    tok = lambda w: pl.BlockSpec((tm, w), lambda i, *_: (i, 0))
    return pl.pallas_call(
        functools.partial(_combine_kernel, n_prompt_tiles=npt),
        out_shape=[jax.ShapeDtypeStruct((tp, D_MODEL), F32), jax.ShapeDtypeStruct((ts, D_MODEL), F32)],
        grid_spec=pltpu.PrefetchScalarGridSpec(
            num_scalar_prefetch=4,
            grid=(npt + nst,),
            in_specs=[pl.BlockSpec((2 * TOP_K, tm), lambda i, *_: (0, i)),
                      pl.BlockSpec((LOCAL_ROWS, tm), lambda i, *_: (i, 0)), tok(D_MODEL),
                      pl.BlockSpec((1, D_MODEL), lambda i, *_: (0, 0)),
                      pl.BlockSpec(memory_space=pl.ANY)],
            out_specs=[pl.BlockSpec((tm, D_MODEL), lambda i, *_: (jnp.minimum(i, npt - 1), 0)),
                       pl.BlockSpec((tm, D_MODEL), lambda i, *_: (jnp.maximum(i - npt, 0), 0))],
            scratch_shapes=[pltpu.VMEM((2, LOCAL_ROWS, D_MODEL), F32), pltpu.VMEM((LOCAL_ROWS, D_MODEL), BF16),
                            pltpu.SemaphoreType.DMA((2,))]),
        compiler_params=_params(("arbitrary",)),
        name="moe_combine_norm",
    )(*tables, route, pick, h2, g_final, ys)


def _rope_tables(seq_len):
    pos = jnp.arange(seq_len, dtype=F32)[:, None]
    lane = np.arange(LANES)

    def table(half, lane_freq, first, active):
        inv = ROPE_BASE ** (-jnp.arange(half, dtype=F32) / half)
        ang = pos * inv[None, :]
        cos = jnp.cos(ang)[:, lane_freq]
        sin = jnp.sin(ang)[:, lane_freq]
        c = jnp.where(active[None, :], cos, 1.0)
        s = jnp.where(active[None, :], jnp.where(first[None, :], -sin, sin), 0.0)
        return c, s

    half_r = RET_DK // 2
    cr, sr = table(half_r, lane % half_r, (lane % RET_DK) < half_r, np.ones(LANES, bool))
    half_m = MLA_ROPE // 2
    rel = lane - MLA_NOPE
    active = (rel >= 0) & (rel < MLA_ROPE)
    cm, sm = table(half_m, np.where(active, rel % half_m, 0), active & (rel < half_m), active)
    return cr, sr, cm, sm


def kernel(x_prompt, x_sample, mem_prompt, mem_sample, norm_mix, w_in, ret_decay_fwd, ret_decay_bwd, ret_gn, q_a_norm, w_uq, kv_a_norm, w_ukv, w_mix_out, norm_cross, norm_mem, w_xq, w_xkv, w_xo, norm_ffn, router_w, router_b, w_gu, b_gu, w_down, b_down, norm_final):
    assert norm_mix.shape[0] == 1, "single layer"
    bp, sp, d = x_prompt.shape
    bs, ss, _ = x_sample.shape
    tp, ts = bp * sp, bs * ss
    t = tp + ts
    assert d == D_MODEL and sp % TOKEN_TILE == 0 and ss % TOKEN_TILE == 0 and tp % ss == 0 and sp >= ss
    mem_len = mem_prompt.shape[1]

    w_in0 = w_in[0]
    w_in_pad = jnp.zeros((D_MODEL, D_IN_PAD), F32)
    w_in_pad = w_in_pad.at[:, :2816].set(w_in0[:, :2816])
    w_in_pad = w_in_pad.at[:, 2816 + MLA_NOPE:2816 + MLA_NOPE + MLA_ROPE].set(w_in0[:, 2816:]).astype(BF16)
    wuq_pad = jnp.pad(w_uq[0].reshape(Q_LORA, MLA_HEADS, MLA_NOPE + MLA_ROPE),
                      ((0, 0), (0, 0), (0, LANES - MLA_NOPE - MLA_ROPE))).reshape(Q_LORA, MLA_HEADS * LANES).astype(BF16)
    wukv = w_ukv[0].reshape(KV_LORA, MLA_HEADS, MLA_NOPE + MLA_V)
    wuk_pad = jnp.pad(wukv[:, :, :MLA_NOPE], ((0, 0), (0, 0), (0, LANES - MLA_NOPE))).reshape(KV_LORA, MLA_HEADS * LANES).astype(BF16)
    wuv = jnp.pad(wukv[:, :, MLA_NOPE:], ((0, 0), (0, 0), (0, LANES - MLA_V))).reshape(KV_LORA, MLA_HEADS * LANES).astype(BF16)
    rw_t = router_w[0].T.astype(BF16)
    rb_col = jnp.broadcast_to(router_b[0].astype(F32)[:, None], (N_EXPERTS, LANES))
    lgf = jnp.log1p(-jnp.exp2(ret_decay_fwd[0].astype(F32)))
    lgb = jnp.log1p(-jnp.exp2(ret_decay_bwd[0].astype(F32)))
    tabs = _rope_tables(sp)

    xp = x_prompt.reshape(tp, D_MODEL)
    xs = x_sample.reshape(ts, D_MODEL)
    mem = jnp.concatenate([mem_prompt.reshape(-1, D_MODEL), mem_sample.reshape(-1, D_MODEL)], axis=0)

    kvmem = _mem_kv(mem, norm_mem[0][None, :], w_xkv[0].astype(BF16)).reshape(bp + bs, mem_len, 2 * D_MODEL)

    rq, rk, rv, rg, qm, km, vm = _in_proj(xp, xs, norm_mix[0][None, :], w_in_pad, q_a_norm[0][None, :], wuq_pad,
                                          kv_a_norm[0][None, :], wuk_pad, wuv, tabs, sp, ss)

    gn = ret_gn[0][None, :]
    ret_p = _retention(lgf, lgb, rq, rk, rv, rg, gn, sp, bp, 0)
    ret_s = _retention(lgf, lgb, rq, rk, rv, rg, gn, ss, bs, tp // ss)
    mla_p = _mla_attn(qm, km, vm, sp, bp, 0)
    mla_s = _mla_attn(qm, km, vm, ss, bs, tp // ss)

    h2, xn, route, stats, counts = _mix_cross(
        xp, xs, ret_p, ret_s, mla_p, mla_s, w_mix_out[0].astype(BF16), norm_cross[0][None, :], w_xq[0].astype(BF16),
        kvmem, w_xo[0].astype(BF16), norm_ffn[0][None, :], rw_t, rb_col, sp, ss)

    used = counts[:, 0]
    padded = (used + MOE_BLOCK - 1) // MOE_BLOCK * MOE_BLOCK
    pend = jnp.cumsum(padded)
    pstart = pend - padded
    tile_cnt, tile_before, tile_lstart = (stats[:, :, r] for r in range(3))
    n_tiles = stats.shape[0]
    tile_chunks = (tile_cnt + ROW_CHUNK - 1) // ROW_CHUNK
    tables = (tile_chunks.reshape(-1),
              tile_lstart.reshape(-1),
              (pstart[None, :] + tile_before).reshape(-1),
              jnp.sum(tile_chunks, axis=-1))
    ztables = (pstart + used, (padded - used) // ROW_CHUNK)
    n_blocks = -(-(t * TOP_K + N_EXPERTS * (n_tiles * (ROW_CHUNK - 1) + MOE_BLOCK - 1)) // MOE_BLOCK)
    blk = jnp.arange(n_blocks, dtype=jnp.int32)
    n_active = (pend[-1] // MOE_BLOCK).astype(jnp.int32)
    block_i = jnp.minimum(blk, n_active - 1)
    block_e = jnp.minimum(jnp.sum((block_i[:, None] * MOE_BLOCK >= pend[None, :]).astype(jnp.int32), axis=-1),
                          N_EXPERTS - 1).astype(jnp.int32)
    block_first = jnp.concatenate([jnp.ones((1,), jnp.int32), (block_e[1:] != block_e[:-1]).astype(jnp.int32)])
    block_seg = jnp.cumsum(block_first) - 1
    next_blk = jnp.sum(jnp.where(block_e[:, None] == jnp.arange(N_EXPERTS)[None, :], pend[None, :], 0), axis=-1) // MOE_BLOCK
    block_next = jnp.where(next_blk < n_active, block_e[jnp.minimum(next_blk, n_blocks - 1)], -1).astype(jnp.int32)

    xs_sorted, pick = _dispatch(tables, ztables, xn, route, n_blocks * MOE_BLOCK)
    ys = _experts((block_e, block_i, n_active[None], block_first, block_seg.astype(jnp.int32), block_next),
                  xs_sorted, w_gu[0], b_gu[0][:, None, :], w_down[0], b_down[0][:, None, :])
    out_p, out_s = _combine(tables, route, pick, h2, norm_final[None, :], ys, tp, ts)
    return out_p.reshape(bp, sp, D_MODEL), out_s.reshape(bs, ss, D_MODEL)
```

```python
import functools
import math

import jax
import jax.numpy as jnp
import numpy as np
from jax import lax
from jax.experimental import pallas as pl
from jax.experimental.pallas import tpu as pltpu

D_MODEL = 1024
RET_HEADS = 8
RET_DK = 64
RET_WIDTH = 512
CHUNK = 128
MLA_HEADS = 8
MLA_NOPE = 64
MLA_ROPE = 32
MLA_V = 64
MLA_WIDTH = 512
Q_LORA = 512
KV_LORA = 256
ROPE_BASE = 10000.0
X_HEADS = 4
X_HEAD_DIM = 256
N_EXPERTS = 32
TOP_K = 4
SWIGLU_LIMIT = 7.0
SWIGLU_ALPHA = 1.702
MOE_BLOCK = 512
EPS = 1e-6

LANES = 128
VMEM_LIMIT = 56 * 1024 * 1024

TOKEN_TILE = 512
ROW_PARTS = 2
ATTN_TQ = 1024
ATTN_TK = 1024
ATTN_UNROLL = 4
RET_UNROLL = 16
ROW_CHUNK = 8
GROUP_BITS = (TOKEN_TILE // ROW_CHUNK).bit_length()
WAIT_CHUNKS = 16
LOCAL_STEP = 256
LOCAL_ROWS = -(-(TOKEN_TILE * 4 + 32 * (ROW_CHUNK - 1)) // LOCAL_STEP) * LOCAL_STEP
D_IN_PAD = 4 * 512 + Q_LORA + KV_LORA + LANES
NEG_BIG = -1e30

F32 = jnp.float32
BF16 = jnp.bfloat16


def _params(sem, vmem=VMEM_LIMIT):
    return pltpu.CompilerParams(dimension_semantics=sem, vmem_limit_bytes=vmem)


def _const_spec(shape):
    nd = len(shape)
    return pl.BlockSpec(shape, lambda *_: (0,) * nd, pipeline_mode=pl.Buffered(1))


def _rms(x, g):
    ms = jnp.mean(x * x, axis=-1, keepdims=True)
    return x * lax.rsqrt(ms + EPS) * g


def _dot(a, b):
    return jnp.dot(a, b, preferred_element_type=F32)


def _dot_nt(a, b):
    return lax.dot_general(a, b, (((1,), (1,)), ((), ())), preferred_element_type=F32)


def _dot_tn(a, b):
    return lax.dot_general(a, b, (((0,), (0,)), ((), ())), preferred_element_type=F32)


def _mem_kv_kernel(mem_ref, g_ref, w_ref, o_ref):
    mn = _rms(mem_ref[...], g_ref[...])
    o_ref[...] = _dot(mn.astype(BF16), w_ref[...]).astype(BF16)


def _mem_kv(mem, g, w):
    rows, mem_len = mem.shape[0], 256
    return pl.pallas_call(
        _mem_kv_kernel,
        out_shape=jax.ShapeDtypeStruct((rows, 2 * D_MODEL), BF16),
        grid=(rows // mem_len,),
        in_specs=[pl.BlockSpec((mem_len, D_MODEL), lambda i: (i, 0)),
                  _const_spec((1, D_MODEL)),
                  _const_spec((D_MODEL, 2 * D_MODEL))],
        out_specs=pl.BlockSpec((mem_len, 2 * D_MODEL), lambda i: (i, 0)),
        compiler_params=_params(("arbitrary",)),
        name="mem_kv",
    )(mem, g, w)


def _rope_slab(x, c, ss, first, shift_up, shift_down):
    swap = jnp.where(first, pltpu.roll(x, shift_up, 1), pltpu.roll(x, shift_down, 1))
    return x * c + swap * ss


def _in_proj_kernel(xp_ref, xs_ref, g_ref, win_ref, qan_ref, wuq_ref, kvan_ref, wuk_ref, wuv_ref,
                    cr_ref, sr_ref, cm_ref, sm_ref,
                    rq_ref, rk_ref, rv_ref, rg_ref, qm_ref, km_ref, vm_ref, *, n_prompt_tiles):
    i = pl.program_id(0)
    tm = rq_ref.shape[0]
    lane = lax.broadcasted_iota(jnp.int32, (1, LANES), 1)
    ret_first = (lane % RET_DK) < (RET_DK // 2)
    mla_first = (lane >= MLA_NOPE) & (lane < MLA_NOPE + MLA_ROPE // 2)
    half_r, half_m = RET_DK // 2, MLA_ROPE // 2
    q_scale = (MLA_NOPE + MLA_ROPE) ** -0.5 * math.log2(math.e)

    for part in range(ROW_PARTS):
        rows = slice(part * (tm // ROW_PARTS), (part + 1) * (tm // ROW_PARTS))
        x = jnp.where(i < n_prompt_tiles, xp_ref[rows, :], xs_ref[rows, :])
        xn = _rms(x, g_ref[...])
        proj = _dot(xn.astype(BF16), win_ref[...])
        cr, sr, cm, sm = cr_ref[rows, :], sr_ref[rows, :], cm_ref[rows, :], sm_ref[rows, :]

        for s in range(RET_WIDTH // LANES):
            lo = s * LANES
            q = _rope_slab(proj[:, lo:lo + LANES], cr, sr, ret_first, LANES - half_r, half_r)
            rq_ref[rows, lo:lo + LANES] = q.astype(BF16)
            k = _rope_slab(proj[:, 512 + lo:512 + lo + LANES], cr, sr, ret_first, LANES - half_r, half_r)
            rk_ref[rows, lo:lo + LANES] = (k * (RET_DK ** -0.5)).astype(BF16)
        rv_ref[rows, :] = proj[:, 1024:1536].astype(BF16)
        rg_ref[rows, :] = proj[:, 1536:2048].astype(BF16)

        cq = _rms(proj[:, 2048:2048 + Q_LORA], qan_ref[...])
        qm = _dot(cq.astype(BF16), wuq_ref[...])
        ckv = _rms(proj[:, 2560:2560 + KV_LORA], kvan_ref[...]).astype(BF16)
        kn = _dot(ckv, wuk_ref[...])
        vm = _dot(ckv, wuv_ref[...])
        for h in range(MLA_HEADS):
            lo = h * LANES
            vm_ref[rows, lo:lo + LANES] = jnp.where(lane == MLA_V, 1.0, vm[:, lo:lo + LANES]).astype(BF16)
        kr = proj[:, 2816:2816 + LANES]
        kpe = _rope_slab(kr, cm, sm, mla_first, LANES - half_m, half_m)
        for h in range(MLA_HEADS):
            lo = h * LANES
            qh = _rope_slab(qm[:, lo:lo + LANES], cm, sm, mla_first, LANES - half_m, half_m)
            qm_ref[rows, lo:lo + LANES] = (qh * q_scale).astype(BF16)
            km_ref[rows, lo:lo + LANES] = (kn[:, lo:lo + LANES] + kpe).astype(BF16)


def _in_proj(xp, xs, g, w_in_pad, qan, wuq_pad, kvan, wuk_pad, wuv, tabs, sp, ss):
    tm = TOKEN_TILE
    tp, ts = xp.shape[0], xs.shape[0]
    npt, nst = tp // tm, ts // tm
    t = tp + ts
    tiles_p, tiles_s = sp // tm, ss // tm

    def xp_map(i):
        return (jnp.minimum(i, npt - 1), 0)

    def xs_map(i):
        return (jnp.maximum(i - npt, 0), 0)

    def tab_map(i):
        return (jnp.where(i < npt, i % tiles_p, (i - npt) % tiles_s), 0)

    tok = lambda w: pl.BlockSpec((tm, w), lambda i: (i, 0))
    widths = (512, 512, 512, 512, 1024, 1024, 1024)
    return pl.pallas_call(
        functools.partial(_in_proj_kernel, n_prompt_tiles=npt),
        out_shape=[jax.ShapeDtypeStruct((t, w), BF16) for w in widths],
        grid=(npt + nst,),
        in_specs=[pl.BlockSpec((tm, D_MODEL), xp_map), pl.BlockSpec((tm, D_MODEL), xs_map),
                  _const_spec((1, D_MODEL)), _const_spec((D_MODEL, D_IN_PAD)),
                  _const_spec((1, Q_LORA)), _const_spec((Q_LORA, MLA_HEADS * LANES)),
                  _const_spec((1, KV_LORA)), _const_spec((KV_LORA, MLA_HEADS * LANES)),
                  _const_spec((KV_LORA, MLA_HEADS * LANES))]
                 + [pl.BlockSpec((tm, LANES), tab_map)] * 4,
        out_specs=[tok(w) for w in widths],
        compiler_params=_params(("arbitrary",)),
        name="in_proj",
    )(xp, xs, g, w_in_pad, qan, wuq_pad, kvan, wuk_pad, wuv, *tabs)


def _retention_kernel(lgf_ref, lgb_ref, q_ref, k_ref, v_ref, g_ref, gn_ref, o_ref, kvf_scr, kvb_scr, st_scr, *, n_chunks):
    hp = pl.program_id(1)
    c = CHUNK
    lane = lax.broadcasted_iota(jnp.int32, (1, c), 1)
    row = lax.broadcasted_iota(jnp.int32, (c, 1), 0)
    lane_h0 = lane < RET_DK
    row_h0 = row < RET_DK
    lgf0, lgf1 = lgf_ref[2 * hp], lgf_ref[2 * hp + 1]
    lgb0, lgb1 = lgb_ref[2 * hp], lgb_ref[2 * hp + 1]
    lgf_lane = jnp.where(lane_h0, lgf0, lgf1)
    lgb_lane = jnp.where(lane_h0, lgb0, lgb1)
    t = row.astype(F32)
    q_dec_f = jnp.exp((t + 1.0) * lgf_lane)
    q_dec_b = jnp.exp((c - t) * lgb_lane)
    k_dec_f = jnp.exp((c - 1.0 - t) * lgf_lane)
    k_dec_b = jnp.exp(t * lgb_lane)
    chunk_dec_f = jnp.exp(c * jnp.where(row_h0, lgf0, lgf1))
    chunk_dec_b = jnp.exp(c * jnp.where(row_h0, lgb0, lgb1))
    same_head = row_h0 == lane_h0
    diff = t - lane.astype(F32)
    d_intra = []
    for lgf, lgb in ((lgf0, lgb0), (lgf1, lgb1)):
        fwd = jnp.where(diff >= 0, jnp.exp(jnp.where(diff >= 0, diff, 0.0) * lgf), 0.0)
        bwd = jnp.where(diff < 0, jnp.exp(jnp.where(diff < 0, -diff, 0.0) * lgb), 0.0)
        d_intra.append(fwd + bwd)
    avg = jnp.where(same_head, 1.0 / RET_DK, 0.0).astype(BF16)
    gn = gn_ref[...]

    def chunk(n):
        return pl.ds(pl.multiple_of(n * c, c), c)

    unroll = min(RET_UNROLL, n_chunks)

    def chunk_kv(n, _):
        k = k_ref[chunk(n), :].astype(F32)
        kd = jnp.concatenate([(k * k_dec_f).astype(BF16), (k * k_dec_b).astype(BF16)], axis=-1)
        kv = _dot_tn(kd, v_ref[chunk(n), :])
        kvf_scr[n] = jnp.where(same_head, kv[0:c], 0.0)
        kvb_scr[n] = jnp.where(same_head, kv[c:2 * c], 0.0)
        return 0

    lax.fori_loop(0, n_chunks, chunk_kv, 0, unroll=unroll)

    def fwd_state(n, s_f):
        st_scr[n, 0:c, :] = s_f.astype(BF16)
        return chunk_dec_f * s_f + kvf_scr[n]

    lax.fori_loop(0, n_chunks, fwd_state, jnp.zeros((c, c), F32))

    def bwd_state(j, s_b):
        n = n_chunks - 1 - j
        st_scr[n, c:2 * c, :] = s_b.astype(BF16)
        return chunk_dec_b * s_b + kvb_scr[n]

    lax.fori_loop(0, n_chunks, bwd_state, jnp.zeros((c, c), F32))

    d_both = jnp.concatenate(d_intra, axis=0)
    zero_q = jnp.zeros((c, c), BF16)

    def chunk_out(n):
        q = q_ref[chunk(n), :]
        qf = q.astype(F32)
        q_both = jnp.concatenate([(qf * q_dec_f).astype(BF16), (qf * q_dec_b).astype(BF16)], axis=-1)
        q_heads = jnp.concatenate([jnp.where(lane_h0, q, zero_q), jnp.where(lane_h0, zero_q, q)], axis=0)
        p = (_dot_nt(q_heads, k_ref[chunk(n), :]) * d_both).astype(BF16)
        inner = _dot(p, v_ref[chunk(n), :])
        return _dot(q_both, st_scr[n]) + jnp.where(lane_h0, inner[0:c], inner[c:2 * c])

    def split_rows(a):
        hi = a.astype(BF16)
        return jnp.concatenate([hi, (a - hi.astype(F32)).astype(BF16)], axis=0)

    def out_group(gi, _):
        rows = pl.ds(pl.multiple_of(gi * (unroll * c), unroll * c), unroll * c)
        y = jnp.concatenate([chunk_out(gi * unroll + u) for u in range(unroll)], axis=0)
        m = y.shape[0]
        mu2 = _dot(split_rows(y), avg)
        d = y - (mu2[0:m] + mu2[m:2 * m])
        var2 = _dot(split_rows(d * d), avg)
        yn = d * lax.rsqrt(var2[0:m] + var2[m:2 * m] + EPS) * gn
        gate = g_ref[rows, :].astype(F32)
        o_ref[rows, :] = (yn * (gate * jax.nn.sigmoid(gate))).astype(BF16)
        return 0

    lax.fori_loop(0, n_chunks // unroll, out_group, 0)


def _retention(lgf, lgb, rq, rk, rv, rg, gn, seq_len, n_seq, row_block0):
    n_chunks = seq_len // CHUNK
    hp_count = RET_WIDTH // LANES
    blk = pl.BlockSpec((seq_len, LANES), lambda b, hp, *_: (row_block0 + b, hp))
    return pl.pallas_call(
        functools.partial(_retention_kernel, n_chunks=n_chunks),
        out_shape=jax.ShapeDtypeStruct((n_seq * seq_len, RET_WIDTH), BF16),
        grid_spec=pltpu.PrefetchScalarGridSpec(
            num_scalar_prefetch=2,
            grid=(n_seq, hp_count),
            in_specs=[blk, blk, blk, blk, pl.BlockSpec((1, LANES), lambda b, hp, *_: (0, hp))],
            out_specs=pl.BlockSpec((seq_len, LANES), lambda b, hp, *_: (b, hp)),
            scratch_shapes=[pltpu.VMEM((n_chunks, CHUNK, CHUNK), F32), pltpu.VMEM((n_chunks, CHUNK, CHUNK), F32),
                            pltpu.VMEM((n_chunks, 2 * CHUNK, CHUNK), BF16)]),
        compiler_params=_params(("arbitrary", "arbitrary")),
        name="retention",
    )(lgf, lgb, rq, rk, rv, rg, gn)


def _mla_attn_kernel(q_ref, k_ref, v_ref, o_ref, *, tk, n_kv):
    tq = q_ref.shape[0]
    lane = lax.broadcasted_iota(jnp.int32, (1, LANES), 1)
    qs = (q_ref[:, :LANES], q_ref[:, LANES:])

    def body(j, carry):
        rows = pl.ds(pl.multiple_of(j * tk, tk), tk)
        new = []
        for h in range(2):
            m, acc = carry[h]
            s = _dot_nt(qs[h], k_ref[rows, h * LANES:(h + 1) * LANES])
            m_new = jnp.maximum(m, jnp.max(s, axis=-1, keepdims=True))
            alpha = jnp.exp2(m - m_new)
            p = jnp.exp2((s - m_new).astype(BF16))
            acc = alpha * acc + _dot(p, v_ref[rows, h * LANES:(h + 1) * LANES])
            new.append((m_new, acc))
        return tuple(new)

    init = (jnp.full((tq, 1), NEG_BIG, F32), jnp.zeros((tq, LANES), F32))
    (_, acc0), (_, acc1) = lax.fori_loop(0, n_kv, body, (init, init), unroll=min(ATTN_UNROLL, n_kv))
    out0 = acc0 / acc0[:, MLA_V:MLA_V + 1]
    out1 = acc1 / acc1[:, MLA_V:MLA_V + 1]
    o_ref[...] = jnp.where(lane < MLA_V, out0, pltpu.roll(out1, MLA_V, 1)).astype(BF16)


def _mla_attn(qm, km, vm, seq_len, n_seq, row_block0):
    tq, tk = min(ATTN_TQ, seq_len), min(ATTN_TK, seq_len)
    hp_count = MLA_HEADS // 2
    nq = seq_len // tq
    return pl.pallas_call(
        functools.partial(_mla_attn_kernel, tk=tk, n_kv=seq_len // tk),
        out_shape=jax.ShapeDtypeStruct((n_seq * seq_len, MLA_WIDTH), BF16),
        grid=(n_seq, hp_count, nq),
        in_specs=[pl.BlockSpec((tq, 2 * LANES), lambda b, hp, i: ((row_block0 + b) * nq + i, hp)),
                  pl.BlockSpec((seq_len, 2 * LANES), lambda b, hp, i: (row_block0 + b, hp)),
                  pl.BlockSpec((seq_len, 2 * LANES), lambda b, hp, i: (row_block0 + b, hp))],
        out_specs=pl.BlockSpec((tq, LANES), lambda b, hp, i: (b * nq + i, hp)),
        compiler_params=_params(("arbitrary", "arbitrary", "arbitrary")),
        name="mla_attn",
    )(qm, km, vm)


def _mix_cross_kernel(xp_ref, xs_ref, rp_ref, rs_ref, mp_ref, ms_ref, wmix_ref, gx_ref, wxq_ref, kv_ref, wxo_ref,
                      gf_ref, rwt_ref, rbc_ref,
                      h_ref, xn_ref, route_ref, stat_ref, cnt_ref, carry_scr, *, n_prompt_tiles):
    i = pl.program_id(0)
    tm = h_ref.shape[0]
    is_p = i < n_prompt_tiles
    x = jnp.where(is_p, xp_ref[...], xs_ref[...])
    ret = jnp.where(is_p, rp_ref[...], rs_ref[...])
    mla = jnp.where(is_p, mp_ref[...], ms_ref[...])
    h1 = x + _dot(ret, wmix_ref[0:RET_WIDTH, :]) + _dot(mla, wmix_ref[RET_WIDTH:, :])

    hn = _rms(h1, gx_ref[...]).astype(BF16)
    q = (_dot(hn, wxq_ref[...]) * (X_HEAD_DIM ** -0.5)).astype(BF16)
    heads = []
    for h in range(X_HEADS):
        lo = h * X_HEAD_DIM
        s = _dot_nt(q[:, lo:lo + X_HEAD_DIM], kv_ref[0, :, lo:lo + X_HEAD_DIM])
        e = jnp.exp(s - jnp.max(s, axis=-1, keepdims=True))
        p = (e / jnp.sum(e, axis=-1, keepdims=True)).astype(BF16)
        heads.append(_dot(p, kv_ref[0, :, D_MODEL + lo:D_MODEL + lo + X_HEAD_DIM]).astype(BF16))
    h2 = h1 + _dot(jnp.concatenate(heads, axis=-1), wxo_ref[...])
    h_ref[...] = h2
    xn = _rms(h2, gf_ref[...]).astype(BF16)
    xn_ref[...] = xn

    ne = N_EXPERTS
    work = _dot_nt(rwt_ref[...], xn) + rbc_ref[:, 0:1]
    e_f = lax.broadcasted_iota(jnp.int32, (ne, tm), 0).astype(F32)
    vals, picks = [], []
    for _ in range(TOP_K):
        m = jnp.max(work, axis=0, keepdims=True)
        idx = jnp.min(jnp.where(work == m, e_f, float(ne)), axis=0, keepdims=True)
        sel = e_f == idx
        work = jnp.where(sel, -jnp.inf, work)
        vals.append(m)
        picks.append(jnp.where(sel, 1.0, 0.0))
    exps = [jnp.exp(v - vals[0]) for v in vals]
    denom = exps[0] + exps[1] + exps[2] + exps[3]
    onehot = picks[0] + picks[1] + picks[2] + picks[3]

    @pl.when(i == 0)
    def _():
        carry_scr[...] = jnp.zeros_like(carry_scr)

    t_row = lax.broadcasted_iota(jnp.int32, (tm, tm), 0)
    t_col = lax.broadcasted_iota(jnp.int32, (tm, tm), 1)
    earlier_tok = jnp.where(t_row < t_col, 1.0, 0.0).astype(BF16)
    before = _dot(onehot.astype(BF16), earlier_tok)
    tile_cnt = jnp.sum(onehot, axis=1, keepdims=True)
    group_rows = jnp.floor((tile_cnt + (ROW_CHUNK - 1)) * (1.0 / ROW_CHUNK)) * ROW_CHUNK
    carry_before = carry_scr[...]
    carry = carry_before + group_rows
    carry_scr[...] = carry
    cnt_ref[...] = carry.astype(jnp.int32)
    e_row = lax.broadcasted_iota(jnp.int32, (LANES, LANES), 0)
    e_col = lax.broadcasted_iota(jnp.int32, (LANES, LANES), 1)
    earlier_exp = jnp.where(e_col < e_row, 1.0, 0.0).astype(BF16)
    rows_pad = jnp.concatenate([jnp.broadcast_to(group_rows, (ne, LANES)), jnp.zeros((LANES - ne, LANES), F32)], axis=0)
    group_start = _dot(earlier_exp, rows_pad.astype(BF16))[0:ne]
    lane = lax.broadcasted_iota(jnp.int32, (ne, LANES), 1)
    stats = jnp.where(lane == 0, tile_cnt, jnp.where(lane == 1, carry_before, jnp.where(lane == 2, group_start, 0.0)))
    stat_ref[0] = stats.astype(jnp.int32)

    place = before + group_start[:, 0:1]
    row8 = lax.broadcasted_iota(jnp.int32, (2 * TOP_K, tm), 0)
    route = jnp.zeros((2 * TOP_K, tm), F32)
    for k in range(TOP_K):
        slot_k = jnp.sum(picks[k] * place, axis=0, keepdims=True)
        route = jnp.where(row8 == k, exps[k] / denom, jnp.where(row8 == TOP_K + k, slot_k, route))
    route_ref[...] = route


def _mix_cross(xp, xs, ret_p, ret_s, mla_p, mla_s, wmix, gx, wxq, kvmem, wxo, gf, rw_t, rb_col, sp, ss):
    tm = TOKEN_TILE
    tp, ts = xp.shape[0], xs.shape[0]
    npt, nst = tp // tm, ts // tm
    t = tp + ts
    n_seq_p = tp // sp
    mem_len = kvmem.shape[1]

    def p_map(i):
        return (jnp.minimum(i, npt - 1), 0)

    def s_map(i):
        return (jnp.maximum(i - npt, 0), 0)

    def kv_map(i):
        return (jnp.where(i < npt, i // (sp // tm), n_seq_p + (i - npt) // (ss // tm)), 0, 0)

    tok = lambda w: pl.BlockSpec((tm, w), lambda i: (i, 0))
    return pl.pallas_call(
        functools.partial(_mix_cross_kernel, n_prompt_tiles=npt),
        out_shape=[jax.ShapeDtypeStruct((t, D_MODEL), F32), jax.ShapeDtypeStruct((t, D_MODEL), BF16),
                   jax.ShapeDtypeStruct((2 * TOP_K, t), F32),
                   jax.ShapeDtypeStruct((npt + nst, N_EXPERTS, LANES), jnp.int32),
                   jax.ShapeDtypeStruct((N_EXPERTS, LANES), jnp.int32)],
        grid=(npt + nst,),
        in_specs=[pl.BlockSpec((tm, D_MODEL), p_map), pl.BlockSpec((tm, D_MODEL), s_map),
                  pl.BlockSpec((tm, RET_WIDTH), p_map), pl.BlockSpec((tm, RET_WIDTH), s_map),
                  pl.BlockSpec((tm, MLA_WIDTH), p_map), pl.BlockSpec((tm, MLA_WIDTH), s_map),
                  _const_spec((D_MODEL, D_MODEL)), _const_spec((1, D_MODEL)), _const_spec((D_MODEL, D_MODEL)),
                  pl.BlockSpec((1, mem_len, 2 * D_MODEL), kv_map),
                  _const_spec((D_MODEL, D_MODEL)), _const_spec((1, D_MODEL)),
                  _const_spec((N_EXPERTS, D_MODEL)), _const_spec((N_EXPERTS, LANES))],
        out_specs=[tok(D_MODEL), tok(D_MODEL),
                   pl.BlockSpec((2 * TOP_K, tm), lambda i: (0, i)),
                   pl.BlockSpec((1, N_EXPERTS, LANES), lambda i: (i, 0, 0)),
                   pl.BlockSpec((N_EXPERTS, LANES), lambda i: (0, 0))],
        scratch_shapes=[pltpu.VMEM((N_EXPERTS, LANES), F32)],
        compiler_params=_params(("arbitrary",)),
        name="mix_cross_router",
    )(xp, xs, ret_p, ret_s, mla_p, mla_s, wmix, gx, wxq, kvmem, wxo, gf, rw_t, rb_col)


def _chunk_copies(tile, nch_ref, lstart_ref, gstart_ref, local, hbm, sem, to_hbm, act):
    def per_expert(e, _):
        base = tile * N_EXPERTS + e
        n = nch_ref[base]
        ls = lstart_ref[base]
        gs = gstart_ref[base]
        def bit_copy(j):
            rows = ROW_CHUNK << j

            @pl.when(((n >> j) & 1) == 1)
            def _():
                off = (n & ((1 << j) - 1)) * ROW_CHUNK
                l_rows = local.at[pl.ds(pl.multiple_of(ls + off, ROW_CHUNK), rows)]
                g_rows = hbm.at[pl.ds(pl.multiple_of(gs + off, ROW_CHUNK), rows)]
                act(pltpu.make_async_copy(l_rows, g_rows, sem) if to_hbm else pltpu.make_async_copy(g_rows, l_rows, sem))

        for j in range(GROUP_BITS):
            bit_copy(j)
        return 0

    lax.fori_loop(0, N_EXPERTS, per_expert, 0)


def _wait_tile(tile, tot_ref, local, hbm, sem, to_hbm):
    total = tot_ref[tile]

    def wait_rows(rows):
        l_rows = local.at[pl.ds(0, rows)]
        g_rows = hbm.at[pl.ds(0, rows)]
        (pltpu.make_async_copy(l_rows, g_rows, sem) if to_hbm else pltpu.make_async_copy(g_rows, l_rows, sem)).wait()

    def big(_, c):
        wait_rows(ROW_CHUNK * WAIT_CHUNKS)
        return c

    def small(_, c):
        wait_rows(ROW_CHUNK)
        return c

    lax.fori_loop(0, total // WAIT_CHUNKS, big, 0)
    lax.fori_loop(0, total % WAIT_CHUNKS, small, 0)


def _start(cp):
    cp.start()


def _wait(cp):
    cp.wait()


def _dispatch_kernel(nch_ref, lstart_ref, gstart_ref, tot_ref, zstart_ref, znch_ref,
                     x_ref, route_ref, xs_hbm, pick_ref, buf, zero_scr, sem, zsem):
    i = pl.program_id(0)
    last = pl.num_programs(0) - 1
    tm = x_ref.shape[0]
    b = i % 2
    slot_rows = route_ref[TOP_K:2 * TOP_K, :]
    x = x_ref[...]

    def build(s, _):
        r0 = pl.multiple_of(s * LOCAL_STEP, LOCAL_STEP)
        rid = (lax.broadcasted_iota(jnp.int32, (LOCAL_STEP, tm), 0) + r0).astype(F32)
        hit = rid == slot_rows[0:1, :]
        for k in range(1, TOP_K):
            hit = hit | (rid == slot_rows[k:k + 1, :])
        pick = jnp.where(hit, 1.0, 0.0).astype(BF16)
        pick_ref[pl.ds(r0, LOCAL_STEP), :] = pick
        buf[b, pl.ds(r0, LOCAL_STEP), :] = _dot(pick, x)
        return 0

    lax.fori_loop(0, LOCAL_ROWS // LOCAL_STEP, build, 0, unroll=3)

    _chunk_copies(i, nch_ref, lstart_ref, gstart_ref, buf.at[b], xs_hbm, sem.at[b], True, _start)

    @pl.when(i > 0)
    def _():
        _wait_tile(i - 1, tot_ref, buf.at[1 - b], xs_hbm, sem.at[1 - b], True)

    @pl.when(i == last)
    def _():
        _wait_tile(i, tot_ref, buf.at[b], xs_hbm, sem.at[b], True)
        zero_scr[...] = jnp.zeros_like(zero_scr)

        def fill(act):
            def per_expert(e, _):
                def per_chunk(c, _):
                    row = pl.multiple_of(zstart_ref[e] + c * ROW_CHUNK, ROW_CHUNK)
                    act(pltpu.make_async_copy(zero_scr, xs_hbm.at[pl.ds(row, ROW_CHUNK)], zsem))
                    return 0

                lax.fori_loop(0, znch_ref[e], per_chunk, 0)
                return 0

            lax.fori_loop(0, N_EXPERTS + 1, per_expert, 0)

        fill(_start)
        fill(_wait)


def _dispatch(tables, ztables, xn, route, n_rows):
    tm = TOKEN_TILE
    t = xn.shape[0]
    return pl.pallas_call(
        _dispatch_kernel,
        out_shape=[jax.ShapeDtypeStruct((n_rows, D_MODEL), F32),
                   jax.ShapeDtypeStruct((t // tm * LOCAL_ROWS, tm), BF16)],
        grid_spec=pltpu.PrefetchScalarGridSpec(
            num_scalar_prefetch=6,
            grid=(t // tm,),
            in_specs=[pl.BlockSpec((tm, D_MODEL), lambda i, *_: (i, 0)),
                      pl.BlockSpec((2 * TOP_K, tm), lambda i, *_: (0, i))],
            out_specs=[pl.BlockSpec(memory_space=pl.ANY),
                       pl.BlockSpec((LOCAL_ROWS, tm), lambda i, *_: (i, 0))],
            scratch_shapes=[pltpu.VMEM((2, LOCAL_ROWS, D_MODEL), F32), pltpu.VMEM((ROW_CHUNK, D_MODEL), F32),
                            pltpu.SemaphoreType.DMA((2,)), pltpu.SemaphoreType.DMA(())]),
        compiler_params=_params(("arbitrary",)),
        name="moe_dispatch",
    )(*tables, *ztables, xn, route)


def _experts_kernel(be_ref, bi_ref, na_ref, first_ref, seg_ref, nxt_ref, x_ref, bgu_ref, bd_ref, wgu_hbm, wd_hbm, o_ref,
                    wgu_f32, wd_f32, wgu_scr, wd_scr, sem):
    j = pl.program_id(0)
    active = j < na_ref[0]

    def weight_copies(e, slot):
        return (pltpu.make_async_copy(wgu_hbm.at[e], wgu_f32.at[slot], sem.at[slot, 0]),
                pltpu.make_async_copy(wd_hbm.at[e], wd_f32.at[slot], sem.at[slot, 1]))

    @pl.when(j == 0)
    def _():
        for cp in weight_copies(be_ref[0], 0):
            cp.start()

    @pl.when(active & (first_ref[j] == 1))
    def _():
        slot = seg_ref[j] % 2
        for cp in weight_copies(be_ref[j], slot):
            cp.wait()
        wgu_scr[...] = wgu_f32[slot].astype(BF16)
        wd_scr[...] = wd_f32[slot].astype(BF16)

        @pl.when(nxt_ref[j] >= 0)
        def _():
            for cp in weight_copies(nxt_ref[j], 1 - slot):
                cp.start()

    @pl.when(active)
    def _():
        x = x_ref[...].astype(BF16)
        gu = _dot(x, wgu_scr[...]) + bgu_ref[0]
        gate = jnp.minimum(gu[:, :D_MODEL], SWIGLU_LIMIT)
        up = jnp.clip(gu[:, D_MODEL:], -SWIGLU_LIMIT, SWIGLU_LIMIT)
        hid = (up + 1.0) * (gate * jax.nn.sigmoid(SWIGLU_ALPHA * gate))
        o_ref[...] = _dot(hid.astype(BF16), wd_scr[...]) + bd_ref[0]

    @pl.when(jnp.logical_not(active))
    def _():
        o_ref[...] = jnp.zeros_like(o_ref)


def _experts(block_tables, xs, wgu, bgu, wd, bd):
    n_blocks = xs.shape[0] // MOE_BLOCK
    row_map = lambda j, be, bi, *_: (bi[j], 0)
    e_map = lambda j, be, *_: (be[j], 0, 0)
    return pl.pallas_call(
        _experts_kernel,
        out_shape=jax.ShapeDtypeStruct(xs.shape, F32),
        grid_spec=pltpu.PrefetchScalarGridSpec(
            num_scalar_prefetch=6,
            grid=(n_blocks,),
            in_specs=[pl.BlockSpec((MOE_BLOCK, D_MODEL), row_map),
                      pl.BlockSpec((1, 1, 2 * D_MODEL), e_map),
                      pl.BlockSpec((1, 1, D_MODEL), e_map),
                      pl.BlockSpec(memory_space=pl.ANY),
                      pl.BlockSpec(memory_space=pl.ANY)],
            out_specs=pl.BlockSpec((MOE_BLOCK, D_MODEL), lambda j, *_: (j, 0)),
            scratch_shapes=[pltpu.VMEM((2, D_MODEL, 2 * D_MODEL), F32), pltpu.VMEM((2, D_MODEL, D_MODEL), F32),
                            pltpu.VMEM((D_MODEL, 2 * D_MODEL), BF16), pltpu.VMEM((D_MODEL, D_MODEL), BF16),
                            pltpu.SemaphoreType.DMA((2, 2))]),
        compiler_params=_params(("arbitrary",)),
        name="moe_experts",
    )(*block_tables, xs, bgu, bd, wgu, wd)


def _combine_kernel(nch_ref, lstart_ref, gstart_ref, tot_ref, route_ref, pick_ref, h_ref, g_ref, ys_hbm,
                    op_ref, os_ref, buf, scaled_scr, sem, *, n_prompt_tiles):
    i = pl.program_id(0)
    n = pl.num_programs(0)
    tm = h_ref.shape[0]
    b = i % 2

    @pl.when(i == 0)
    def _():
        buf[...] = jnp.zeros_like(buf)
        _chunk_copies(0, nch_ref, lstart_ref, gstart_ref, buf.at[0], ys_hbm, sem.at[0], False, _start)

    @pl.when(i + 1 < n)
    def _():
        _chunk_copies(i + 1, nch_ref, lstart_ref, gstart_ref, buf.at[1 - b], ys_hbm, sem.at[1 - b], False, _start)

    _wait_tile(i, tot_ref, buf.at[b], ys_hbm, sem.at[b], False)

    info = route_ref[...]
    info_hi = info.astype(BF16).astype(F32)
    gap = jnp.zeros((LANES - 2 * TOP_K, tm), F32)
    info2 = jnp.concatenate([info_hi, gap, info - info_hi, gap], axis=0).astype(BF16)
    lane = lax.broadcasted_iota(jnp.int32, (LOCAL_STEP, LANES), 1)

    def step(s, _):
        r0 = pl.multiple_of(s * LOCAL_STEP, LOCAL_STEP)
        pick = pick_ref[pl.ds(r0, LOCAL_STEP), :]
        own2 = _dot_nt(pick, info2)
        own = own2[:, :LANES] + own2[:, LANES:]
        rcol = (lax.broadcasted_iota(jnp.int32, (LOCAL_STEP, 1), 0) + r0).astype(F32)
        mine = jnp.where((own == rcol) & (lane >= TOP_K) & (lane < 2 * TOP_K), 1.0, 0.0)
        g_col = jnp.sum(own * pltpu.roll(mine, LANES - TOP_K, 1), axis=-1, keepdims=True)
        rows = buf[b, pl.ds(r0, LOCAL_STEP), :]
        scaled_scr[pl.ds(r0, LOCAL_STEP), :] = jnp.where(g_col != 0.0, rows * g_col, 0.0).astype(BF16)
        return 0

    lax.fori_loop(0, LOCAL_ROWS // LOCAL_STEP, step, 0, unroll=3)
    y = _dot_tn(pick_ref[...], scaled_scr[...])
    out = _rms(h_ref[...] + y, g_ref[...])

    @pl.when(i < n_prompt_tiles)
    def _():
        op_ref[...] = out

    @pl.when(i >= n_prompt_tiles)
    def _():
        os_ref[...] = out


def _combine(tables, route, pick, h2, g_final, ys, tp, ts):
    tm = TOKEN_TILE
    npt, nst = tp // tm, ts // tm
    tok = lambda w: pl.BlockSpec((tm, w), lambda i, *_: (i, 0))
    return pl.pallas_call(
        functools.partial(_combine_kernel, n_prompt_tiles=npt),
        out_shape=[jax.ShapeDtypeStruct((tp, D_MODEL), F32), jax.ShapeDtypeStruct((ts, D_MODEL), F32)],
        grid_spec=pltpu.PrefetchScalarGridSpec(
            num_scalar_prefetch=4,
            grid=(npt + nst,),
            in_specs=[pl.BlockSpec((2 * TOP_K, tm), lambda i, *_: (0, i)),
                      pl.BlockSpec((LOCAL_ROWS, tm), lambda i, *_: (i, 0)), tok(D_MODEL),
                      pl.BlockSpec((1, D_MODEL), lambda i, *_: (0, 0)),
                      pl.BlockSpec(memory_space=pl.ANY)],
            out_specs=[pl.BlockSpec((tm, D_MODEL), lambda i, *_: (jnp.minimum(i, npt - 1), 0)),
                       pl.BlockSpec((tm, D_MODEL), lambda i, *_: (jnp.maximum(i - npt, 0), 0))],
            scratch_shapes=[pltpu.VMEM((2, LOCAL_ROWS, D_MODEL), F32), pltpu.VMEM((LOCAL_ROWS, D_MODEL), BF16),
                            pltpu.SemaphoreType.DMA((2,))]),
        compiler_params=_params(("arbitrary",)),
        name="moe_combine_norm",
    )(*tables, route, pick, h2, g_final, ys)


def _rope_tables(seq_len):
    pos = jnp.arange(seq_len, dtype=F32)[:, None]
    lane = np.arange(LANES)

    def table(half, lane_freq, first, active):
        inv = ROPE_BASE ** (-jnp.arange(half, dtype=F32) / half)
        ang = pos * inv[None, :]
        cos = jnp.cos(ang)[:, lane_freq]
        sin = jnp.sin(ang)[:, lane_freq]
        c = jnp.where(active[None, :], cos, 1.0)
        s = jnp.where(active[None, :], jnp.where(first[None, :], -sin, sin), 0.0)
        return c, s

    half_r = RET_DK // 2
    cr, sr = table(half_r, lane % half_r, (lane % RET_DK) < half_r, np.ones(LANES, bool))
    half_m = MLA_ROPE // 2
    rel = lane - MLA_NOPE
    active = (rel >= 0) & (rel < MLA_ROPE)
    cm, sm = table(half_m, np.where(active, rel % half_m, 0), active & (rel < half_m), active)
    return cr, sr, cm, sm


def kernel(x_prompt, x_sample, mem_prompt, mem_sample, norm_mix, w_in, ret_decay_fwd, ret_decay_bwd, ret_gn, q_a_norm, w_uq, kv_a_norm, w_ukv, w_mix_out, norm_cross, norm_mem, w_xq, w_xkv, w_xo, norm_ffn, router_w, router_b, w_gu, b_gu, w_down, b_down, norm_final):
    assert norm_mix.shape[0] == 1, "single layer"
    bp, sp, d = x_prompt.shape
    bs, ss, _ = x_sample.shape
    tp, ts = bp * sp, bs * ss
    t = tp + ts
    assert d == D_MODEL and sp % TOKEN_TILE == 0 and ss % TOKEN_TILE == 0 and tp % ss == 0 and sp >= ss
    mem_len = mem_prompt.shape[1]

    w_in0 = w_in[0]
    w_in_pad = jnp.zeros((D_MODEL, D_IN_PAD), F32)
    w_in_pad = w_in_pad.at[:, :2816].set(w_in0[:, :2816])
    w_in_pad = w_in_pad.at[:, 2816 + MLA_NOPE:2816 + MLA_NOPE + MLA_ROPE].set(w_in0[:, 2816:]).astype(BF16)
    wuq_pad = jnp.pad(w_uq[0].reshape(Q_LORA, MLA_HEADS, MLA_NOPE + MLA_ROPE),
                      ((0, 0), (0, 0), (0, LANES - MLA_NOPE - MLA_ROPE))).reshape(Q_LORA, MLA_HEADS * LANES).astype(BF16)
    wukv = w_ukv[0].reshape(KV_LORA, MLA_HEADS, MLA_NOPE + MLA_V)
    wuk_pad = jnp.pad(wukv[:, :, :MLA_NOPE], ((0, 0), (0, 0), (0, LANES - MLA_NOPE))).reshape(KV_LORA, MLA_HEADS * LANES).astype(BF16)
    wuv = jnp.pad(wukv[:, :, MLA_NOPE:], ((0, 0), (0, 0), (0, LANES - MLA_V))).reshape(KV_LORA, MLA_HEADS * LANES).astype(BF16)
    rw_t = router_w[0].T.astype(BF16)
    rb_col = jnp.broadcast_to(router_b[0].astype(F32)[:, None], (N_EXPERTS, LANES))
    lgf = jnp.log1p(-jnp.exp2(ret_decay_fwd[0].astype(F32)))
    lgb = jnp.log1p(-jnp.exp2(ret_decay_bwd[0].astype(F32)))
    tabs = _rope_tables(sp)

    xp = x_prompt.reshape(tp, D_MODEL)
    xs = x_sample.reshape(ts, D_MODEL)
    mem = jnp.concatenate([mem_prompt.reshape(-1, D_MODEL), mem_sample.reshape(-1, D_MODEL)], axis=0)

    kvmem = _mem_kv(mem, norm_mem[0][None, :], w_xkv[0].astype(BF16)).reshape(bp + bs, mem_len, 2 * D_MODEL)

    rq, rk, rv, rg, qm, km, vm = _in_proj(xp, xs, norm_mix[0][None, :], w_in_pad, q_a_norm[0][None, :], wuq_pad,
                                          kv_a_norm[0][None, :], wuk_pad, wuv, tabs, sp, ss)

    gn = ret_gn[0][None, :]
    ret_p = _retention(lgf, lgb, rq, rk, rv, rg, gn, sp, bp, 0)
    ret_s = _retention(lgf, lgb, rq, rk, rv, rg, gn, ss, bs, tp // ss)
    mla_p = _mla_attn(qm, km, vm, sp, bp, 0)
    mla_s = _mla_attn(qm, km, vm, ss, bs, tp // ss)

    h2, xn, route, stats, counts = _mix_cross(
        xp, xs, ret_p, ret_s, mla_p, mla_s, w_mix_out[0].astype(BF16), norm_cross[0][None, :], w_xq[0].astype(BF16),
        kvmem, w_xo[0].astype(BF16), norm_ffn[0][None, :], rw_t, rb_col, sp, ss)

    used = counts[:, 0]
    padded = (used + MOE_BLOCK - 1) // MOE_BLOCK * MOE_BLOCK
    pend = jnp.cumsum(padded)
    pstart = pend - padded
    tile_cnt, tile_before, tile_lstart = (stats[:, :, r] for r in range(3))
    n_tiles = stats.shape[0]
    tile_chunks = (tile_cnt + ROW_CHUNK - 1) // ROW_CHUNK
    tables = (tile_chunks.reshape(-1),
              tile_lstart.reshape(-1),
              (pstart[None, :] + tile_before).reshape(-1),
              jnp.sum(tile_chunks, axis=-1))
    n_blocks = -(-(t * TOP_K + N_EXPERTS * (n_tiles * (ROW_CHUNK - 1) + MOE_BLOCK - 1)) // MOE_BLOCK)
    ztables = (jnp.concatenate([pstart + used, pend[-1:]]),
               jnp.concatenate([padded - used, n_blocks * MOE_BLOCK - pend[-1:]]) // ROW_CHUNK)
    blk = jnp.arange(n_blocks, dtype=jnp.int32)
    n_active = (pend[-1] // MOE_BLOCK).astype(jnp.int32)
    block_i = jnp.minimum(blk, n_active - 1)
    block_e = jnp.minimum(jnp.sum((block_i[:, None] * MOE_BLOCK >= pend[None, :]).astype(jnp.int32), axis=-1),
                          N_EXPERTS - 1).astype(jnp.int32)
    block_first = jnp.concatenate([jnp.ones((1,), jnp.int32), (block_e[1:] != block_e[:-1]).astype(jnp.int32)])
    block_seg = jnp.cumsum(block_first) - 1
    next_blk = jnp.sum(jnp.where(block_e[:, None] == jnp.arange(N_EXPERTS)[None, :], pend[None, :], 0), axis=-1) // MOE_BLOCK
    block_next = jnp.where(next_blk < n_active, block_e[jnp.minimum(next_blk, n_blocks - 1)], -1).astype(jnp.int32)

    xs_sorted, pick = _dispatch(tables, ztables, xn, route, n_blocks * MOE_BLOCK)
    ys = _experts((block_e, block_i, n_active[None], block_first, block_seg.astype(jnp.int32), block_next),
                  xs_sorted, w_gu[0], b_gu[0][:, None, :], w_down[0], b_down[0][:, None, :])
    out_p, out_s = _combine(tables, route, pick, h2, norm_final[None, :], ys, tp, ts)
    return out_p.reshape(bp, sp, D_MODEL), out_s.reshape(bs, ss, D_MODEL)
```

```python
import functools
import math

import jax
import jax.numpy as jnp
import numpy as np
from jax import lax
from jax.experimental import pallas as pl
from jax.experimental.pallas import tpu as pltpu

D_MODEL = 1024
RET_HEADS = 8
RET_DK = 64
RET_WIDTH = 512
CHUNK = 128
MLA_HEADS = 8
MLA_NOPE = 64
MLA_ROPE = 32
MLA_V = 64
MLA_WIDTH = 512
Q_LORA = 512
KV_LORA = 256
ROPE_BASE = 10000.0
X_HEADS = 4
X_HEAD_DIM = 256
N_EXPERTS = 32
TOP_K = 4
SWIGLU_LIMIT = 7.0
SWIGLU_ALPHA = 1.702
MOE_BLOCK = 512
EPS = 1e-6

LANES = 128
VMEM_LIMIT = 56 * 1024 * 1024

TOKEN_TILE = 512
ROW_PARTS = 2
ATTN_TQ = 1024
ATTN_TK = 1024
ATTN_UNROLL = 4
RET_UNROLL = 16
ROW_CHUNK = 8
GROUP_BITS = (TOKEN_TILE // ROW_CHUNK).bit_length()
WAIT_CHUNKS = 16
LOCAL_STEP = 256
LOCAL_ROWS = -(-(TOKEN_TILE * 4 + 32 * (ROW_CHUNK - 1)) // LOCAL_STEP) * LOCAL_STEP
D_IN_PAD = 4 * 512 + Q_LORA + KV_LORA + LANES
NEG_BIG = -1e30

F32 = jnp.float32
BF16 = jnp.bfloat16


def _params(sem, vmem=VMEM_LIMIT):
    return pltpu.CompilerParams(dimension_semantics=sem, vmem_limit_bytes=vmem)


def _const_spec(shape):
    nd = len(shape)
    return pl.BlockSpec(shape, lambda *_: (0,) * nd, pipeline_mode=pl.Buffered(1))


def _rms(x, g):
    ms = jnp.mean(x * x, axis=-1, keepdims=True)
    return x * lax.rsqrt(ms + EPS) * g


def _dot(a, b):
    return jnp.dot(a, b, preferred_element_type=F32)


def _dot_nt(a, b):
    return lax.dot_general(a, b, (((1,), (1,)), ((), ())), preferred_element_type=F32)


def _dot_tn(a, b):
    return lax.dot_general(a, b, (((0,), (0,)), ((), ())), preferred_element_type=F32)


def _mem_kv_kernel(mem_ref, g_ref, w_ref, o_ref):
    mn = _rms(mem_ref[...], g_ref[...])
    o_ref[...] = _dot(mn.astype(BF16), w_ref[...]).astype(BF16)


def _mem_kv(mem, g, w):
    rows, mem_len = mem.shape[0], 256
    return pl.pallas_call(
        _mem_kv_kernel,
        out_shape=jax.ShapeDtypeStruct((rows, 2 * D_MODEL), BF16),
        grid=(rows // mem_len,),
        in_specs=[pl.BlockSpec((mem_len, D_MODEL), lambda i: (i, 0)),
                  _const_spec((1, D_MODEL)),
                  _const_spec((D_MODEL, 2 * D_MODEL))],
        out_specs=pl.BlockSpec((mem_len, 2 * D_MODEL), lambda i: (i, 0)),
        compiler_params=_params(("arbitrary",)),
        name="mem_kv",
    )(mem, g, w)


def _rope_slab(x, c, ss, first, shift_up, shift_down):
    swap = jnp.where(first, pltpu.roll(x, shift_up, 1), pltpu.roll(x, shift_down, 1))
    return x * c + swap * ss


def _in_proj_kernel(xp_ref, xs_ref, g_ref, win_ref, qan_ref, wuq_ref, kvan_ref, wuk_ref, wuv_ref,
                    cr_ref, sr_ref, cm_ref, sm_ref,
                    rq_ref, rk_ref, rv_ref, rg_ref, qm_ref, km_ref, vm_ref, *, n_prompt_tiles):
    i = pl.program_id(0)
    tm = rq_ref.shape[0]
    lane = lax.broadcasted_iota(jnp.int32, (1, LANES), 1)
    ret_first = (lane % RET_DK) < (RET_DK // 2)
    mla_first = (lane >= MLA_NOPE) & (lane < MLA_NOPE + MLA_ROPE // 2)
    half_r, half_m = RET_DK // 2, MLA_ROPE // 2
    q_scale = (MLA_NOPE + MLA_ROPE) ** -0.5 * math.log2(math.e)

    for part in range(ROW_PARTS):
        rows = slice(part * (tm // ROW_PARTS), (part + 1) * (tm // ROW_PARTS))
        x = jnp.where(i < n_prompt_tiles, xp_ref[rows, :], xs_ref[rows, :])
        xn = _rms(x, g_ref[...])
        proj = _dot(xn.astype(BF16), win_ref[...])
        cr, sr, cm, sm = cr_ref[rows, :], sr_ref[rows, :], cm_ref[rows, :], sm_ref[rows, :]

        for s in range(RET_WIDTH // LANES):
            lo = s * LANES
            q = _rope_slab(proj[:, lo:lo + LANES], cr, sr, ret_first, LANES - half_r, half_r)
            rq_ref[rows, lo:lo + LANES] = q.astype(BF16)
            k = _rope_slab(proj[:, 512 + lo:512 + lo + LANES], cr, sr, ret_first, LANES - half_r, half_r)
            rk_ref[rows, lo:lo + LANES] = (k * (RET_DK ** -0.5)).astype(BF16)
        rv_ref[rows, :] = proj[:, 1024:1536].astype(BF16)
        rg_ref[rows, :] = proj[:, 1536:2048].astype(BF16)

        cq = _rms(proj[:, 2048:2048 + Q_LORA], qan_ref[...])
        qm = _dot(cq.astype(BF16), wuq_ref[...])
        ckv = _rms(proj[:, 2560:2560 + KV_LORA], kvan_ref[...]).astype(BF16)
        kn = _dot(ckv, wuk_ref[...])
        vm = _dot(ckv, wuv_ref[...])
        for h in range(MLA_HEADS):
            lo = h * LANES
            vm_ref[rows, lo:lo + LANES] = jnp.where(lane == MLA_V, 1.0, vm[:, lo:lo + LANES]).astype(BF16)
        kr = proj[:, 2816:2816 + LANES]
        kpe = _rope_slab(kr, cm, sm, mla_first, LANES - half_m, half_m)
        for h in range(MLA_HEADS):
            lo = h * LANES
            qh = _rope_slab(qm[:, lo:lo + LANES], cm, sm, mla_first, LANES - half_m, half_m)
            qm_ref[rows, lo:lo + LANES] = (qh * q_scale).astype(BF16)
            km_ref[rows, lo:lo + LANES] = (kn[:, lo:lo + LANES] + kpe).astype(BF16)


def _in_proj(xp, xs, g, w_in_pad, qan, wuq_pad, kvan, wuk_pad, wuv, tabs, sp, ss):
    tm = TOKEN_TILE
    tp, ts = xp.shape[0], xs.shape[0]
    npt, nst = tp // tm, ts // tm
    t = tp + ts
    tiles_p, tiles_s = sp // tm, ss // tm

    def xp_map(i):
        return (jnp.minimum(i, npt - 1), 0)

    def xs_map(i):
        return (jnp.maximum(i - npt, 0), 0)

    def tab_map(i):
        return (jnp.where(i < npt, i % tiles_p, (i - npt) % tiles_s), 0)

    tok = lambda w: pl.BlockSpec((tm, w), lambda i: (i, 0))
    widths = (512, 512, 512, 512, 1024, 1024, 1024)
    return pl.pallas_call(
        functools.partial(_in_proj_kernel, n_prompt_tiles=npt),
        out_shape=[jax.ShapeDtypeStruct((t, w), BF16) for w in widths],
        grid=(npt + nst,),
        in_specs=[pl.BlockSpec((tm, D_MODEL), xp_map), pl.BlockSpec((tm, D_MODEL), xs_map),
                  _const_spec((1, D_MODEL)), _const_spec((D_MODEL, D_IN_PAD)),
                  _const_spec((1, Q_LORA)), _const_spec((Q_LORA, MLA_HEADS * LANES)),
                  _const_spec((1, KV_LORA)), _const_spec((KV_LORA, MLA_HEADS * LANES)),
                  _const_spec((KV_LORA, MLA_HEADS * LANES))]
                 + [pl.BlockSpec((tm, LANES), tab_map)] * 4,
        out_specs=[tok(w) for w in widths],
        compiler_params=_params(("arbitrary",)),
        name="in_proj",
    )(xp, xs, g, w_in_pad, qan, wuq_pad, kvan, wuk_pad, wuv, *tabs)


def _retention_kernel(lgf_ref, lgb_ref, q_ref, k_ref, v_ref, g_ref, gn_ref, o_ref, kvf_scr, kvb_scr, st_scr, *, n_chunks):
    hp = pl.program_id(1)
    c = CHUNK
    lane = lax.broadcasted_iota(jnp.int32, (1, c), 1)
    row = lax.broadcasted_iota(jnp.int32, (c, 1), 0)
    lane_h0 = lane < RET_DK
    row_h0 = row < RET_DK
    lgf0, lgf1 = lgf_ref[2 * hp], lgf_ref[2 * hp + 1]
    lgb0, lgb1 = lgb_ref[2 * hp], lgb_ref[2 * hp + 1]
    lgf_lane = jnp.where(lane_h0, lgf0, lgf1)
    lgb_lane = jnp.where(lane_h0, lgb0, lgb1)
    t = row.astype(F32)
    q_dec_f = jnp.exp((t + 1.0) * lgf_lane)
    q_dec_b = jnp.exp((c - t) * lgb_lane)
    k_dec_f = jnp.exp((c - 1.0 - t) * lgf_lane)
    k_dec_b = jnp.exp(t * lgb_lane)
    chunk_dec_f = jnp.exp(c * jnp.where(row_h0, lgf0, lgf1))
    chunk_dec_b = jnp.exp(c * jnp.where(row_h0, lgb0, lgb1))
    same_head = row_h0 == lane_h0
    diff = t - lane.astype(F32)
    d_intra = []
    for lgf, lgb in ((lgf0, lgb0), (lgf1, lgb1)):
        fwd = jnp.where(diff >= 0, jnp.exp(jnp.where(diff >= 0, diff, 0.0) * lgf), 0.0)
        bwd = jnp.where(diff < 0, jnp.exp(jnp.where(diff < 0, -diff, 0.0) * lgb), 0.0)
        d_intra.append(fwd + bwd)
    avg = jnp.where(same_head, 1.0 / RET_DK, 0.0).astype(BF16)
    gn = gn_ref[...]

    def chunk(n):
        return pl.ds(pl.multiple_of(n * c, c), c)

    unroll = min(RET_UNROLL, n_chunks)

    def chunk_kv(n, _):
        k = k_ref[chunk(n), :].astype(F32)
        kd = jnp.concatenate([(k * k_dec_f).astype(BF16), (k * k_dec_b).astype(BF16)], axis=-1)
        kv = _dot_tn(kd, v_ref[chunk(n), :])
        kvf_scr[n] = jnp.where(same_head, kv[0:c], 0.0)
        kvb_scr[n] = jnp.where(same_head, kv[c:2 * c], 0.0)
        return 0

    lax.fori_loop(0, n_chunks, chunk_kv, 0, unroll=unroll)

    def fwd_state(n, s_f):
        st_scr[n, 0:c, :] = s_f.astype(BF16)
        return chunk_dec_f * s_f + kvf_scr[n]

    lax.fori_loop(0, n_chunks, fwd_state, jnp.zeros((c, c), F32))

    def bwd_state(j, s_b):
        n = n_chunks - 1 - j
        st_scr[n, c:2 * c, :] = s_b.astype(BF16)
        return chunk_dec_b * s_b + kvb_scr[n]

    lax.fori_loop(0, n_chunks, bwd_state, jnp.zeros((c, c), F32))

    d_both = jnp.concatenate(d_intra, axis=0)
    zero_q = jnp.zeros((c, c), BF16)

    def chunk_out(n):
        q = q_ref[chunk(n), :]
        qf = q.astype(F32)
        q_both = jnp.concatenate([(qf * q_dec_f).astype(BF16), (qf * q_dec_b).astype(BF16)], axis=-1)
        q_heads = jnp.concatenate([jnp.where(lane_h0, q, zero_q), jnp.where(lane_h0, zero_q, q)], axis=0)
        p = (_dot_nt(q_heads, k_ref[chunk(n), :]) * d_both).astype(BF16)
        inner = _dot(p, v_ref[chunk(n), :])
        return _dot(q_both, st_scr[n]) + jnp.where(lane_h0, inner[0:c], inner[c:2 * c])

    def split_rows(a):
        hi = a.astype(BF16)
        return jnp.concatenate([hi, (a - hi.astype(F32)).astype(BF16)], axis=0)

    def out_group(gi, _):
        rows = pl.ds(pl.multiple_of(gi * (unroll * c), unroll * c), unroll * c)
        y = jnp.concatenate([chunk_out(gi * unroll + u) for u in range(unroll)], axis=0)
        m = y.shape[0]
        mu2 = _dot(split_rows(y), avg)
        d = y - (mu2[0:m] + mu2[m:2 * m])
        var2 = _dot(split_rows(d * d), avg)
        yn = d * lax.rsqrt(var2[0:m] + var2[m:2 * m] + EPS) * gn
        gate = g_ref[rows, :].astype(F32)
        o_ref[rows, :] = (yn * (gate * jax.nn.sigmoid(gate))).astype(BF16)
        return 0

    lax.fori_loop(0, n_chunks // unroll, out_group, 0)


def _retention(lgf, lgb, rq, rk, rv, rg, gn, seq_len, n_seq, row_block0):
    n_chunks = seq_len // CHUNK
    hp_count = RET_WIDTH // LANES
    blk = pl.BlockSpec((seq_len, LANES), lambda b, hp, *_: (row_block0 + b, hp))
    return pl.pallas_call(
        functools.partial(_retention_kernel, n_chunks=n_chunks),
        out_shape=jax.ShapeDtypeStruct((n_seq * seq_len, RET_WIDTH), BF16),
        grid_spec=pltpu.PrefetchScalarGridSpec(
            num_scalar_prefetch=2,
            grid=(n_seq, hp_count),
            in_specs=[blk, blk, blk, blk, pl.BlockSpec((1, LANES), lambda b, hp, *_: (0, hp))],
            out_specs=pl.BlockSpec((seq_len, LANES), lambda b, hp, *_: (b, hp)),
            scratch_shapes=[pltpu.VMEM((n_chunks, CHUNK, CHUNK), F32), pltpu.VMEM((n_chunks, CHUNK, CHUNK), F32),
                            pltpu.VMEM((n_chunks, 2 * CHUNK, CHUNK), BF16)]),
        compiler_params=_params(("arbitrary", "arbitrary")),
        name="retention",
    )(lgf, lgb, rq, rk, rv, rg, gn)


def _mla_attn_kernel(q_ref, k_ref, v_ref, o_ref, *, tk, n_kv):
    tq = q_ref.shape[0]
    lane = lax.broadcasted_iota(jnp.int32, (1, LANES), 1)
    qs = (q_ref[:, :LANES], q_ref[:, LANES:])

    def body(j, carry):
        rows = pl.ds(pl.multiple_of(j * tk, tk), tk)
        new = []
        for h in range(2):
            m, acc = carry[h]
            s = _dot_nt(qs[h], k_ref[rows, h * LANES:(h + 1) * LANES])
            m_new = jnp.maximum(m, jnp.max(s, axis=-1, keepdims=True))
            alpha = jnp.exp2(m - m_new)
            p = jnp.exp2((s - m_new).astype(BF16))
            acc = alpha * acc + _dot(p, v_ref[rows, h * LANES:(h + 1) * LANES])
            new.append((m_new, acc))
        return tuple(new)

    init = (jnp.full((tq, 1), NEG_BIG, F32), jnp.zeros((tq, LANES), F32))
    (_, acc0), (_, acc1) = lax.fori_loop(0, n_kv, body, (init, init), unroll=min(ATTN_UNROLL, n_kv))
    out0 = acc0 / acc0[:, MLA_V:MLA_V + 1]
    out1 = acc1 / acc1[:, MLA_V:MLA_V + 1]
    o_ref[...] = jnp.where(lane < MLA_V, out0, pltpu.roll(out1, MLA_V, 1)).astype(BF16)


def _mla_attn(qm, km, vm, seq_len, n_seq, row_block0):
    tq, tk = min(ATTN_TQ, seq_len), min(ATTN_TK, seq_len)
    hp_count = MLA_HEADS // 2
    nq = seq_len // tq
    return pl.pallas_call(
        functools.partial(_mla_attn_kernel, tk=tk, n_kv=seq_len // tk),
        out_shape=jax.ShapeDtypeStruct((n_seq * seq_len, MLA_WIDTH), BF16),
        grid=(n_seq, hp_count, nq),
        in_specs=[pl.BlockSpec((tq, 2 * LANES), lambda b, hp, i: ((row_block0 + b) * nq + i, hp)),
                  pl.BlockSpec((seq_len, 2 * LANES), lambda b, hp, i: (row_block0 + b, hp)),
                  pl.BlockSpec((seq_len, 2 * LANES), lambda b, hp, i: (row_block0 + b, hp))],
        out_specs=pl.BlockSpec((tq, LANES), lambda b, hp, i: (b * nq + i, hp)),
        compiler_params=_params(("arbitrary", "arbitrary", "arbitrary")),
        name="mla_attn",
    )(qm, km, vm)


def _mix_cross_kernel(xp_ref, xs_ref, rp_ref, rs_ref, mp_ref, ms_ref, wmix_ref, gx_ref, wxq_ref, kv_ref, wxo_ref,
                      gf_ref, rwt_ref, rbc_ref,
                      h_ref, xn_ref, route_ref, stat_ref, cnt_ref, carry_scr, *, n_prompt_tiles):
    i = pl.program_id(0)
    tm = h_ref.shape[0]
    is_p = i < n_prompt_tiles
    x = jnp.where(is_p, xp_ref[...], xs_ref[...])
    ret = jnp.where(is_p, rp_ref[...], rs_ref[...])
    mla = jnp.where(is_p, mp_ref[...], ms_ref[...])
    h1 = x + _dot(ret, wmix_ref[0:RET_WIDTH, :]) + _dot(mla, wmix_ref[RET_WIDTH:, :])

    hn = _rms(h1, gx_ref[...]).astype(BF16)
    q = (_dot(hn, wxq_ref[...]) * (X_HEAD_DIM ** -0.5)).astype(BF16)
    heads = []
    for h in range(X_HEADS):
        lo = h * X_HEAD_DIM
        s = _dot_nt(q[:, lo:lo + X_HEAD_DIM], kv_ref[0, :, lo:lo + X_HEAD_DIM])
        e = jnp.exp(s - jnp.max(s, axis=-1, keepdims=True))
        p = (e / jnp.sum(e, axis=-1, keepdims=True)).astype(BF16)
        heads.append(_dot(p, kv_ref[0, :, D_MODEL + lo:D_MODEL + lo + X_HEAD_DIM]).astype(BF16))
    h2 = h1 + _dot(jnp.concatenate(heads, axis=-1), wxo_ref[...])
    h_ref[...] = h2
    xn = _rms(h2, gf_ref[...]).astype(BF16)
    xn_ref[...] = xn

    ne = N_EXPERTS
    work = _dot_nt(rwt_ref[...], xn) + rbc_ref[:, 0:1]
    e_f = lax.broadcasted_iota(jnp.int32, (ne, tm), 0).astype(F32)
    vals, picks = [], []
    for _ in range(TOP_K):
        m = jnp.max(work, axis=0, keepdims=True)
        idx = jnp.min(jnp.where(work == m, e_f, float(ne)), axis=0, keepdims=True)
        sel = e_f == idx
        work = jnp.where(sel, -jnp.inf, work)
        vals.append(m)
        picks.append(jnp.where(sel, 1.0, 0.0))
    exps = [jnp.exp(v - vals[0]) for v in vals]
    denom = exps[0] + exps[1] + exps[2] + exps[3]
    onehot = picks[0] + picks[1] + picks[2] + picks[3]

    @pl.when(i == 0)
    def _():
        carry_scr[...] = jnp.zeros_like(carry_scr)

    t_row = lax.broadcasted_iota(jnp.int32, (tm, tm), 0)
    t_col = lax.broadcasted_iota(jnp.int32, (tm, tm), 1)
    earlier_tok = jnp.where(t_row < t_col, 1.0, 0.0).astype(BF16)
    before = _dot(onehot.astype(BF16), earlier_tok)
    tile_cnt = jnp.sum(onehot, axis=1, keepdims=True)
    group_rows = jnp.floor((tile_cnt + (ROW_CHUNK - 1)) * (1.0 / ROW_CHUNK)) * ROW_CHUNK
    carry_before = carry_scr[...]
    carry = carry_before + group_rows
    carry_scr[...] = carry
    cnt_ref[...] = carry.astype(jnp.int32)
    e_row = lax.broadcasted_iota(jnp.int32, (LANES, LANES), 0)
    e_col = lax.broadcasted_iota(jnp.int32, (LANES, LANES), 1)
    earlier_exp = jnp.where(e_col < e_row, 1.0, 0.0).astype(BF16)
    rows_pad = jnp.concatenate([jnp.broadcast_to(group_rows, (ne, LANES)), jnp.zeros((LANES - ne, LANES), F32)], axis=0)
    group_start = _dot(earlier_exp, rows_pad.astype(BF16))[0:ne]
    lane = lax.broadcasted_iota(jnp.int32, (ne, LANES), 1)
    stats = jnp.where(lane == 0, tile_cnt, jnp.where(lane == 1, carry_before, jnp.where(lane == 2, group_start, 0.0)))
    stat_ref[0] = stats.astype(jnp.int32)

    place = before + group_start[:, 0:1]
    row8 = lax.broadcasted_iota(jnp.int32, (2 * TOP_K, tm), 0)
    route = jnp.zeros((2 * TOP_K, tm), F32)
    for k in range(TOP_K):
        slot_k = jnp.sum(picks[k] * place, axis=0, keepdims=True)
        route = jnp.where(row8 == k, exps[k] / denom, jnp.where(row8 == TOP_K + k, slot_k, route))
    route_ref[...] = route


def _mix_cross(xp, xs, ret_p, ret_s, mla_p, mla_s, wmix, gx, wxq, kvmem, wxo, gf, rw_t, rb_col, sp, ss):
    tm = TOKEN_TILE
    tp, ts = xp.shape[0], xs.shape[0]
    npt, nst = tp // tm, ts // tm
    t = tp + ts
    n_seq_p = tp // sp
    mem_len = kvmem.shape[1]

    def p_map(i):
        return (jnp.minimum(i, npt - 1), 0)

    def s_map(i):
        return (jnp.maximum(i - npt, 0), 0)

    def kv_map(i):
        return (jnp.where(i < npt, i // (sp // tm), n_seq_p + (i - npt) // (ss // tm)), 0, 0)

    tok = lambda w: pl.BlockSpec((tm, w), lambda i: (i, 0))
    return pl.pallas_call(
        functools.partial(_mix_cross_kernel, n_prompt_tiles=npt),
        out_shape=[jax.ShapeDtypeStruct((t, D_MODEL), F32), jax.ShapeDtypeStruct((t, D_MODEL), BF16),
                   jax.ShapeDtypeStruct((2 * TOP_K, t), F32),
                   jax.ShapeDtypeStruct((npt + nst, N_EXPERTS, LANES), jnp.int32),
                   jax.ShapeDtypeStruct((N_EXPERTS, LANES), jnp.int32)],
        grid=(npt + nst,),
        in_specs=[pl.BlockSpec((tm, D_MODEL), p_map), pl.BlockSpec((tm, D_MODEL), s_map),
                  pl.BlockSpec((tm, RET_WIDTH), p_map), pl.BlockSpec((tm, RET_WIDTH), s_map),
                  pl.BlockSpec((tm, MLA_WIDTH), p_map), pl.BlockSpec((tm, MLA_WIDTH), s_map),
                  _const_spec((D_MODEL, D_MODEL)), _const_spec((1, D_MODEL)), _const_spec((D_MODEL, D_MODEL)),
                  pl.BlockSpec((1, mem_len, 2 * D_MODEL), kv_map),
                  _const_spec((D_MODEL, D_MODEL)), _const_spec((1, D_MODEL)),
                  _const_spec((N_EXPERTS, D_MODEL)), _const_spec((N_EXPERTS, LANES))],
        out_specs=[tok(D_MODEL), tok(D_MODEL),
                   pl.BlockSpec((2 * TOP_K, tm), lambda i: (0, i)),
                   pl.BlockSpec((1, N_EXPERTS, LANES), lambda i: (i, 0, 0)),
                   pl.BlockSpec((N_EXPERTS, LANES), lambda i: (0, 0))],
        scratch_shapes=[pltpu.VMEM((N_EXPERTS, LANES), F32)],
        compiler_params=_params(("arbitrary",)),
        name="mix_cross_router",
    )(xp, xs, ret_p, ret_s, mla_p, mla_s, wmix, gx, wxq, kvmem, wxo, gf, rw_t, rb_col)


def _chunk_copies(tile, nch_ref, lstart_ref, gstart_ref, local, hbm, sem, to_hbm, act):
    def per_expert(e, _):
        base = tile * N_EXPERTS + e
        n = nch_ref[base]
        ls = lstart_ref[base]
        gs = gstart_ref[base]
        def bit_copy(j):
            rows = ROW_CHUNK << j

            @pl.when(((n >> j) & 1) == 1)
            def _():
                off = (n & ((1 << j) - 1)) * ROW_CHUNK
                l_rows = local.at[pl.ds(pl.multiple_of(ls + off, ROW_CHUNK), rows)]
                g_rows = hbm.at[pl.ds(pl.multiple_of(gs + off, ROW_CHUNK), rows)]
                act(pltpu.make_async_copy(l_rows, g_rows, sem) if to_hbm else pltpu.make_async_copy(g_rows, l_rows, sem))

        for j in range(GROUP_BITS):
            bit_copy(j)
        return 0

    lax.fori_loop(0, N_EXPERTS, per_expert, 0)


def _wait_tile(tile, tot_ref, local, hbm, sem, to_hbm):
    total = tot_ref[tile]

    def wait_rows(rows):
        l_rows = local.at[pl.ds(0, rows)]
        g_rows = hbm.at[pl.ds(0, rows)]
        (pltpu.make_async_copy(l_rows, g_rows, sem) if to_hbm else pltpu.make_async_copy(g_rows, l_rows, sem)).wait()

    def big(_, c):
        wait_rows(ROW_CHUNK * WAIT_CHUNKS)
        return c

    def small(_, c):
        wait_rows(ROW_CHUNK)
        return c

    lax.fori_loop(0, total // WAIT_CHUNKS, big, 0)
    lax.fori_loop(0, total % WAIT_CHUNKS, small, 0)


def _start(cp):
    cp.start()


def _wait(cp):
    cp.wait()


def _dispatch_kernel(nch_ref, lstart_ref, gstart_ref, tot_ref, zstart_ref, znch_ref,
                     x_ref, route_ref, xs_hbm, pick_ref, buf, zero_scr, sem, zsem):
    i = pl.program_id(0)
    last = pl.num_programs(0) - 1
    tm = x_ref.shape[0]
    b = i % 2
    slot_rows = route_ref[TOP_K:2 * TOP_K, :]
    x = x_ref[...]

    def build(s, _):
        r0 = pl.multiple_of(s * LOCAL_STEP, LOCAL_STEP)
        rid = (lax.broadcasted_iota(jnp.int32, (LOCAL_STEP, tm), 0) + r0).astype(F32)
        hit = rid == slot_rows[0:1, :]
        for k in range(1, TOP_K):
            hit = hit | (rid == slot_rows[k:k + 1, :])
        pick = jnp.where(hit, 1.0, 0.0).astype(BF16)
        pick_ref[pl.ds(r0, LOCAL_STEP), :] = pick
        buf[b, pl.ds(r0, LOCAL_STEP), :] = _dot(pick, x)
        return 0

    lax.fori_loop(0, LOCAL_ROWS // LOCAL_STEP, build, 0, unroll=3)

    _chunk_copies(i, nch_ref, lstart_ref, gstart_ref, buf.at[b], xs_hbm, sem.at[b], True, _start)

    @pl.when(i > 0)
    def _():
        _wait_tile(i - 1, tot_ref, buf.at[1 - b], xs_hbm, sem.at[1 - b], True)

    @pl.when(i == last)
    def _():
        _wait_tile(i, tot_ref, buf.at[b], xs_hbm, sem.at[b], True)
        zero_scr[...] = jnp.zeros_like(zero_scr)

        def fill(act):
            def per_expert(e, _):
                def per_chunk(c, _):
                    row = pl.multiple_of(zstart_ref[e] + c * ROW_CHUNK, ROW_CHUNK)
                    act(pltpu.make_async_copy(zero_scr.at[pl.ds(0, ROW_CHUNK)], xs_hbm.at[pl.ds(row, ROW_CHUNK)], zsem))
                    return 0

                lax.fori_loop(0, znch_ref[e], per_chunk, 0)
                return 0

            lax.fori_loop(0, N_EXPERTS, per_expert, 0)

            def per_block(c, _):
                row = pl.multiple_of(zstart_ref[N_EXPERTS] + c * MOE_BLOCK, MOE_BLOCK)
                act(pltpu.make_async_copy(zero_scr, xs_hbm.at[pl.ds(row, MOE_BLOCK)], zsem))
                return 0

            lax.fori_loop(0, znch_ref[N_EXPERTS], per_block, 0)

        fill(_start)
        fill(_wait)


def _dispatch(tables, ztables, xn, route, n_rows):
    tm = TOKEN_TILE
    t = xn.shape[0]
    return pl.pallas_call(
        _dispatch_kernel,
        out_shape=[jax.ShapeDtypeStruct((n_rows, D_MODEL), F32),
                   jax.ShapeDtypeStruct((t // tm * LOCAL_ROWS, tm), BF16)],
        grid_spec=pltpu.PrefetchScalarGridSpec(
            num_scalar_prefetch=6,
            grid=(t // tm,),
            in_specs=[pl.BlockSpec((tm, D_MODEL), lambda i, *_: (i, 0)),
                      pl.BlockSpec((2 * TOP_K, tm), lambda i, *_: (0, i))],
            out_specs=[pl.BlockSpec(memory_space=pl.ANY),
                       pl.BlockSpec((LOCAL_ROWS, tm), lambda i, *_: (i, 0))],
            scratch_shapes=[pltpu.VMEM((2, LOCAL_ROWS, D_MODEL), F32), pltpu.VMEM((MOE_BLOCK, D_MODEL), F32),
                            pltpu.SemaphoreType.DMA((2,)), pltpu.SemaphoreType.DMA(())]),
        compiler_params=_params(("arbitrary",)),
        name="moe_dispatch",
    )(*tables, *ztables, xn, route)


def _experts_kernel(be_ref, bi_ref, na_ref, first_ref, seg_ref, nxt_ref, x_ref, bgu_ref, bd_ref, wgu_hbm, wd_hbm, o_ref,
                    wgu_f32, wd_f32, wgu_scr, wd_scr, sem):
    j = pl.program_id(0)
    active = j < na_ref[0]

    def weight_copies(e, slot):
        return (pltpu.make_async_copy(wgu_hbm.at[e], wgu_f32.at[slot], sem.at[slot, 0]),
                pltpu.make_async_copy(wd_hbm.at[e], wd_f32.at[slot], sem.at[slot, 1]))

    @pl.when(j == 0)
    def _():
        for cp in weight_copies(be_ref[0], 0):
            cp.start()

    @pl.when(active & (first_ref[j] == 1))
    def _():
        slot = seg_ref[j] % 2
        for cp in weight_copies(be_ref[j], slot):
            cp.wait()
        wgu_scr[...] = wgu_f32[slot].astype(BF16)
        wd_scr[...] = wd_f32[slot].astype(BF16)

        @pl.when(nxt_ref[j] >= 0)
        def _():
            for cp in weight_copies(nxt_ref[j], 1 - slot):
                cp.start()

    @pl.when(active)
    def _():
        x = x_ref[...].astype(BF16)
        gu = _dot(x, wgu_scr[...]) + bgu_ref[0]
        gate = jnp.minimum(gu[:, :D_MODEL], SWIGLU_LIMIT)
        up = jnp.clip(gu[:, D_MODEL:], -SWIGLU_LIMIT, SWIGLU_LIMIT)
        hid = (up + 1.0) * (gate * jax.nn.sigmoid(SWIGLU_ALPHA * gate))
        o_ref[...] = _dot(hid.astype(BF16), wd_scr[...]) + bd_ref[0]


def _experts(block_tables, xs, wgu, bgu, wd, bd):
    n_blocks = xs.shape[0] // MOE_BLOCK
    row_map = lambda j, be, bi, *_: (bi[j], 0)
    e_map = lambda j, be, *_: (be[j], 0, 0)
    return pl.pallas_call(
        _experts_kernel,
        out_shape=jax.ShapeDtypeStruct(xs.shape, F32),
        grid_spec=pltpu.PrefetchScalarGridSpec(
            num_scalar_prefetch=6,
            grid=(n_blocks,),
            in_specs=[pl.BlockSpec((MOE_BLOCK, D_MODEL), row_map),
                      pl.BlockSpec((1, 1, 2 * D_MODEL), e_map),
                      pl.BlockSpec((1, 1, D_MODEL), e_map),
                      pl.BlockSpec(memory_space=pl.ANY),
                      pl.BlockSpec(memory_space=pl.ANY)],
            out_specs=pl.BlockSpec((MOE_BLOCK, D_MODEL), row_map),
            scratch_shapes=[pltpu.VMEM((2, D_MODEL, 2 * D_MODEL), F32), pltpu.VMEM((2, D_MODEL, D_MODEL), F32),
                            pltpu.VMEM((D_MODEL, 2 * D_MODEL), BF16), pltpu.VMEM((D_MODEL, D_MODEL), BF16),
                            pltpu.SemaphoreType.DMA((2, 2))]),
        input_output_aliases={len(block_tables): 0},
        compiler_params=_params(("arbitrary",)),
        name="moe_experts",
    )(*block_tables, xs, bgu, bd, wgu, wd)


def _combine_kernel(nch_ref, lstart_ref, gstart_ref, tot_ref, route_ref, pick_ref, h_ref, g_ref, ys_hbm,
                    op_ref, os_ref, buf, scaled_scr, sem, *, n_prompt_tiles):
    i = pl.program_id(0)
    n = pl.num_programs(0)
    tm = h_ref.shape[0]
    b = i % 2

    @pl.when(i == 0)
    def _():
        buf[...] = jnp.zeros_like(buf)
        _chunk_copies(0, nch_ref, lstart_ref, gstart_ref, buf.at[0], ys_hbm, sem.at[0], False, _start)

    @pl.when(i + 1 < n)
    def _():
        _chunk_copies(i + 1, nch_ref, lstart_ref, gstart_ref, buf.at[1 - b], ys_hbm, sem.at[1 - b], False, _start)

    _wait_tile(i, tot_ref, buf.at[b], ys_hbm, sem.at[b], False)

    info = route_ref[...]
    info_hi = info.astype(BF16).astype(F32)
    gap = jnp.zeros((LANES - 2 * TOP_K, tm), F32)
    info2 = jnp.concatenate([info_hi, gap, info - info_hi, gap], axis=0).astype(BF16)
    lane = lax.broadcasted_iota(jnp.int32, (LOCAL_STEP, LANES), 1)

    def step(s, _):
        r0 = pl.multiple_of(s * LOCAL_STEP, LOCAL_STEP)
        pick = pick_ref[pl.ds(r0, LOCAL_STEP), :]
        own2 = _dot_nt(pick, info2)
        own = own2[:, :LANES] + own2[:, LANES:]
        rcol = (lax.broadcasted_iota(jnp.int32, (LOCAL_STEP, 1), 0) + r0).astype(F32)
        mine = jnp.where((own == rcol) & (lane >= TOP_K) & (lane < 2 * TOP_K), 1.0, 0.0)
        g_col = jnp.sum(own * pltpu.roll(mine, LANES - TOP_K, 1), axis=-1, keepdims=True)
        rows = buf[b, pl.ds(r0, LOCAL_STEP), :]
        scaled_scr[pl.ds(r0, LOCAL_STEP), :] = jnp.where(g_col != 0.0, rows * g_col, 0.0).astype(BF16)
        return 0

    lax.fori_loop(0, LOCAL_ROWS // LOCAL_STEP, step, 0, unroll=3)
    y = _dot_tn(pick_ref[...], scaled_scr[...])
    out = _rms(h_ref[...] + y, g_ref[...])

    @pl.when(i < n_prompt_tiles)
    def _():
        op_ref[...] = out

    @pl.when(i >= n_prompt_tiles)
    def _():
        os_ref[...] = out


def _combine(tables, route, pick, h2, g_final, ys, tp, ts):
    tm = TOKEN_TILE
    npt, nst = tp // tm, ts // tm
    tok = lambda w: pl.BlockSpec((tm, w), lambda i, *_: (i, 0))
    return pl.pallas_call(
        functools.partial(_combine_kernel, n_prompt_tiles=npt),
        out_shape=[jax.ShapeDtypeStruct((tp, D_MODEL), F32), jax.ShapeDtypeStruct((ts, D_MODEL), F32)],
        grid_spec=pltpu.PrefetchScalarGridSpec(
            num_scalar_prefetch=4,
            grid=(npt + nst,),
            in_specs=[pl.BlockSpec((2 * TOP_K, tm), lambda i, *_: (0, i)),
                      pl.BlockSpec((LOCAL_ROWS, tm), lambda i, *_: (i, 0)), tok(D_MODEL),
                      pl.BlockSpec((1, D_MODEL), lambda i, *_: (0, 0)),
                      pl.BlockSpec(memory_space=pl.ANY)],
            out_specs=[pl.BlockSpec((tm, D_MODEL), lambda i, *_: (jnp.minimum(i, npt - 1), 0)),
                       pl.BlockSpec((tm, D_MODEL), lambda i, *_: (jnp.maximum(i - npt, 0), 0))],
            scratch_shapes=[pltpu.VMEM((2, LOCAL_ROWS, D_MODEL), F32), pltpu.VMEM((LOCAL_ROWS, D_MODEL), BF16),
                            pltpu.SemaphoreType.DMA((2,))]),
        compiler_params=_params(("arbitrary",)),
        name="moe_combine_norm",
    )(*tables, route, pick, h2, g_final, ys)


def _rope_tables(seq_len):
    pos = jnp.arange(seq_len, dtype=F32)[:, None]
    lane = np.arange(LANES)

    def table(half, lane_freq, first, active):
        inv = ROPE_BASE ** (-jnp.arange(half, dtype=F32) / half)
        ang = pos * inv[None, :]
        cos = jnp.cos(ang)[:, lane_freq]
        sin = jnp.sin(ang)[:, lane_freq]
        c = jnp.where(active[None, :], cos, 1.0)
        s = jnp.where(active[None, :], jnp.where(first[None, :], -sin, sin), 0.0)
        return c, s

    half_r = RET_DK // 2
    cr, sr = table(half_r, lane % half_r, (lane % RET_DK) < half_r, np.ones(LANES, bool))
    half_m = MLA_ROPE // 2
    rel = lane - MLA_NOPE
    active = (rel >= 0) & (rel < MLA_ROPE)
    cm, sm = table(half_m, np.where(active, rel % half_m, 0), active & (rel < half_m), active)
    return cr, sr, cm, sm


def kernel(x_prompt, x_sample, mem_prompt, mem_sample, norm_mix, w_in, ret_decay_fwd, ret_decay_bwd, ret_gn, q_a_norm, w_uq, kv_a_norm, w_ukv, w_mix_out, norm_cross, norm_mem, w_xq, w_xkv, w_xo, norm_ffn, router_w, router_b, w_gu, b_gu, w_down, b_down, norm_final):
    assert norm_mix.shape[0] == 1, "single layer"
    bp, sp, d = x_prompt.shape
    bs, ss, _ = x_sample.shape
    tp, ts = bp * sp, bs * ss
    t = tp + ts
    assert d == D_MODEL and sp % TOKEN_TILE == 0 and ss % TOKEN_TILE == 0 and tp % ss == 0 and sp >= ss
    mem_len = mem_prompt.shape[1]

    w_in0 = w_in[0]
    w_in_pad = jnp.zeros((D_MODEL, D_IN_PAD), F32)
    w_in_pad = w_in_pad.at[:, :2816].set(w_in0[:, :2816])
    w_in_pad = w_in_pad.at[:, 2816 + MLA_NOPE:2816 + MLA_NOPE + MLA_ROPE].set(w_in0[:, 2816:]).astype(BF16)
    wuq_pad = jnp.pad(w_uq[0].reshape(Q_LORA, MLA_HEADS, MLA_NOPE + MLA_ROPE),
                      ((0, 0), (0, 0), (0, LANES - MLA_NOPE - MLA_ROPE))).reshape(Q_LORA, MLA_HEADS * LANES).astype(BF16)
    wukv = w_ukv[0].reshape(KV_LORA, MLA_HEADS, MLA_NOPE + MLA_V)
    wuk_pad = jnp.pad(wukv[:, :, :MLA_NOPE], ((0, 0), (0, 0), (0, LANES - MLA_NOPE))).reshape(KV_LORA, MLA_HEADS * LANES).astype(BF16)
    wuv = jnp.pad(wukv[:, :, MLA_NOPE:], ((0, 0), (0, 0), (0, LANES - MLA_V))).reshape(KV_LORA, MLA_HEADS * LANES).astype(BF16)
    rw_t = router_w[0].T.astype(BF16)
    rb_col = jnp.broadcast_to(router_b[0].astype(F32)[:, None], (N_EXPERTS, LANES))
    lgf = jnp.log1p(-jnp.exp2(ret_decay_fwd[0].astype(F32)))
    lgb = jnp.log1p(-jnp.exp2(ret_decay_bwd[0].astype(F32)))
    tabs = _rope_tables(sp)

    xp = x_prompt.reshape(tp, D_MODEL)
    xs = x_sample.reshape(ts, D_MODEL)
    mem = jnp.concatenate([mem_prompt.reshape(-1, D_MODEL), mem_sample.reshape(-1, D_MODEL)], axis=0)

    kvmem = _mem_kv(mem, norm_mem[0][None, :], w_xkv[0].astype(BF16)).reshape(bp + bs, mem_len, 2 * D_MODEL)

    rq, rk, rv, rg, qm, km, vm = _in_proj(xp, xs, norm_mix[0][None, :], w_in_pad, q_a_norm[0][None, :], wuq_pad,
                                          kv_a_norm[0][None, :], wuk_pad, wuv, tabs, sp, ss)

    gn = ret_gn[0][None, :]
    ret_p = _retention(lgf, lgb, rq, rk, rv, rg, gn, sp, bp, 0)
    ret_s = _retention(lgf, lgb, rq, rk, rv, rg, gn, ss, bs, tp // ss)
    mla_p = _mla_attn(qm, km, vm, sp, bp, 0)
    mla_s = _mla_attn(qm, km, vm, ss, bs, tp // ss)

    h2, xn, route, stats, counts = _mix_cross(
        xp, xs, ret_p, ret_s, mla_p, mla_s, w_mix_out[0].astype(BF16), norm_cross[0][None, :], w_xq[0].astype(BF16),
        kvmem, w_xo[0].astype(BF16), norm_ffn[0][None, :], rw_t, rb_col, sp, ss)

    used = counts[:, 0]
    padded = (used + MOE_BLOCK - 1) // MOE_BLOCK * MOE_BLOCK
    pend = jnp.cumsum(padded)
    pstart = pend - padded
    tile_cnt, tile_before, tile_lstart = (stats[:, :, r] for r in range(3))
    n_tiles = stats.shape[0]
    tile_chunks = (tile_cnt + ROW_CHUNK - 1) // ROW_CHUNK
    tables = (tile_chunks.reshape(-1),
              tile_lstart.reshape(-1),
              (pstart[None, :] + tile_before).reshape(-1),
              jnp.sum(tile_chunks, axis=-1))
    n_blocks = -(-(t * TOP_K + N_EXPERTS * (n_tiles * (ROW_CHUNK - 1) + MOE_BLOCK - 1)) // MOE_BLOCK)
    ztables = (jnp.concatenate([pstart + used, pend[-1:]]),
               jnp.concatenate([(padded - used) // ROW_CHUNK, n_blocks - pend[-1:] // MOE_BLOCK]))
    blk = jnp.arange(n_blocks, dtype=jnp.int32)
    n_active = (pend[-1] // MOE_BLOCK).astype(jnp.int32)
    block_i = jnp.minimum(blk, n_active - 1)
    block_e = jnp.minimum(jnp.sum((block_i[:, None] * MOE_BLOCK >= pend[None, :]).astype(jnp.int32), axis=-1),
                          N_EXPERTS - 1).astype(jnp.int32)
    block_first = jnp.concatenate([jnp.ones((1,), jnp.int32), (block_e[1:] != block_e[:-1]).astype(jnp.int32)])
    block_seg = jnp.cumsum(block_first) - 1
    next_blk = jnp.sum(jnp.where(block_e[:, None] == jnp.arange(N_EXPERTS)[None, :], pend[None, :], 0), axis=-1) // MOE_BLOCK
    block_next = jnp.where(next_blk < n_active, block_e[jnp.minimum(next_blk, n_blocks - 1)], -1).astype(jnp.int32)

    xs_sorted, pick = _dispatch(tables, ztables, xn, route, n_blocks * MOE_BLOCK)
    ys = _experts((block_e, block_i, n_active[None], block_first, block_seg.astype(jnp.int32), block_next),
                  xs_sorted, w_gu[0], b_gu[0][:, None, :], w_down[0], b_down[0][:, None, :])
    out_p, out_s = _combine(tables, route, pick, h2, norm_final[None, :], ys, tp, ts)
    return out_p.reshape(bp, sp, D_MODEL), out_s.reshape(bs, ss, D_MODEL)
```

```python
import functools
import math

import jax
import jax.numpy as jnp
import numpy as np
from jax import lax
from jax.experimental import pallas as pl
from jax.experimental.pallas import tpu as pltpu

D_MODEL = 1024
RET_HEADS = 8
RET_DK = 64
RET_WIDTH = 512
CHUNK = 128
MLA_HEADS = 8
MLA_NOPE = 64
MLA_ROPE = 32
MLA_V = 64
MLA_WIDTH = 512
Q_LORA = 512
KV_LORA = 256
ROPE_BASE = 10000.0
X_HEADS = 4
X_HEAD_DIM = 256
N_EXPERTS = 32
TOP_K = 4
SWIGLU_LIMIT = 7.0
SWIGLU_ALPHA = 1.702
MOE_BLOCK = 512
EPS = 1e-6

LANES = 128
VMEM_LIMIT = 56 * 1024 * 1024

TOKEN_TILE = 512
ROW_PARTS = 2
ATTN_TQ = 1024
ATTN_TK = 1024
ATTN_UNROLL = 4
RET_UNROLL = 16
ROW_CHUNK = 8
GROUP_BITS = (TOKEN_TILE // ROW_CHUNK).bit_length()
WAIT_CHUNKS = 16
LOCAL_STEP = 256
LOCAL_ROWS = -(-(TOKEN_TILE * 4 + 32 * (ROW_CHUNK - 1)) // LOCAL_STEP) * LOCAL_STEP
D_IN_PAD = 4 * 512 + Q_LORA + KV_LORA + LANES
NEG_BIG = -1e30

F32 = jnp.float32
BF16 = jnp.bfloat16


def _params(sem, vmem=VMEM_LIMIT):
    return pltpu.CompilerParams(dimension_semantics=sem, vmem_limit_bytes=vmem)


def _const_spec(shape):
    nd = len(shape)
    return pl.BlockSpec(shape, lambda *_: (0,) * nd, pipeline_mode=pl.Buffered(1))


def _rms(x, g):
    ms = jnp.mean(x * x, axis=-1, keepdims=True)
    return x * lax.rsqrt(ms + EPS) * g


def _dot(a, b):
    return jnp.dot(a, b, preferred_element_type=F32)


def _dot_nt(a, b):
    return lax.dot_general(a, b, (((1,), (1,)), ((), ())), preferred_element_type=F32)


def _dot_tn(a, b):
    return lax.dot_general(a, b, (((0,), (0,)), ((), ())), preferred_element_type=F32)


def _mem_kv_kernel(mem_ref, g_ref, w_ref, o_ref):
    mn = _rms(mem_ref[...], g_ref[...])
    o_ref[...] = _dot(mn.astype(BF16), w_ref[...]).astype(BF16)


def _mem_kv(mem, g, w):
    rows, mem_len = mem.shape[0], 256
    return pl.pallas_call(
        _mem_kv_kernel,
        out_shape=jax.ShapeDtypeStruct((rows, 2 * D_MODEL), BF16),
        grid=(rows // mem_len,),
        in_specs=[pl.BlockSpec((mem_len, D_MODEL), lambda i: (i, 0)),
                  _const_spec((1, D_MODEL)),
                  _const_spec((D_MODEL, 2 * D_MODEL))],
        out_specs=pl.BlockSpec((mem_len, 2 * D_MODEL), lambda i: (i, 0)),
        compiler_params=_params(("arbitrary",)),
        name="mem_kv",
    )(mem, g, w)


def _rope_slab(x, c, ss, first, shift_up, shift_down):
    swap = jnp.where(first, pltpu.roll(x, shift_up, 1), pltpu.roll(x, shift_down, 1))
    return x * c + swap * ss


def _in_proj_kernel(xp_ref, xs_ref, g_ref, win_ref, qan_ref, wuq_ref, kvan_ref, wuk_ref, wuv_ref,
                    cr_ref, sr_ref, cm_ref, sm_ref,
                    rq_ref, rk_ref, rv_ref, rg_ref, qm_ref, km_ref, vm_ref, *, n_prompt_tiles):
    i = pl.program_id(0)
    tm = rq_ref.shape[0]
    lane = lax.broadcasted_iota(jnp.int32, (1, LANES), 1)
    ret_first = (lane % RET_DK) < (RET_DK // 2)
    mla_first = (lane >= MLA_NOPE) & (lane < MLA_NOPE + MLA_ROPE // 2)
    half_r, half_m = RET_DK // 2, MLA_ROPE // 2
    q_scale = (MLA_NOPE + MLA_ROPE) ** -0.5 * math.log2(math.e)

    for part in range(ROW_PARTS):
        rows = slice(part * (tm // ROW_PARTS), (part + 1) * (tm // ROW_PARTS))
        x = jnp.where(i < n_prompt_tiles, xp_ref[rows, :], xs_ref[rows, :])
        xn = _rms(x, g_ref[...])
        proj = _dot(xn.astype(BF16), win_ref[...])
        cr, sr, cm, sm = cr_ref[rows, :], sr_ref[rows, :], cm_ref[rows, :], sm_ref[rows, :]

        for s in range(RET_WIDTH // LANES):
            lo = s * LANES
            q = _rope_slab(proj[:, lo:lo + LANES], cr, sr, ret_first, LANES - half_r, half_r)
            rq_ref[rows, lo:lo + LANES] = q.astype(BF16)
            k = _rope_slab(proj[:, 512 + lo:512 + lo + LANES], cr, sr, ret_first, LANES - half_r, half_r)
            rk_ref[rows, lo:lo + LANES] = (k * (RET_DK ** -0.5)).astype(BF16)
        rv_ref[rows, :] = proj[:, 1024:1536].astype(BF16)
        rg_ref[rows, :] = proj[:, 1536:2048].astype(BF16)

        cq = _rms(proj[:, 2048:2048 + Q_LORA], qan_ref[...])
        qm = _dot(cq.astype(BF16), wuq_ref[...])
        ckv = _rms(proj[:, 2560:2560 + KV_LORA], kvan_ref[...]).astype(BF16)
        kn = _dot(ckv, wuk_ref[...])
        vm = _dot(ckv, wuv_ref[...])
        for h in range(MLA_HEADS):
            lo = h * LANES
            vm_ref[rows, lo:lo + LANES] = jnp.where(lane == MLA_V, 1.0, vm[:, lo:lo + LANES]).astype(BF16)
        kr = proj[:, 2816:2816 + LANES]
        kpe = _rope_slab(kr, cm, sm, mla_first, LANES - half_m, half_m)
        for h in range(MLA_HEADS):
            lo = h * LANES
            qh = _rope_slab(qm[:, lo:lo + LANES], cm, sm, mla_first, LANES - half_m, half_m)
            qm_ref[rows, lo:lo + LANES] = (qh * q_scale).astype(BF16)
            km_ref[rows, lo:lo + LANES] = (kn[:, lo:lo + LANES] + kpe).astype(BF16)


def _in_proj(xp, xs, g, w_in_pad, qan, wuq_pad, kvan, wuk_pad, wuv, tabs, sp, ss):
    tm = TOKEN_TILE
    tp, ts = xp.shape[0], xs.shape[0]
    npt, nst = tp // tm, ts // tm
    t = tp + ts
    tiles_p, tiles_s = sp // tm, ss // tm

    def xp_map(i):
        return (jnp.minimum(i, npt - 1), 0)

    def xs_map(i):
        return (jnp.maximum(i - npt, 0), 0)

    def tab_map(i):
        return (jnp.where(i < npt, i % tiles_p, (i - npt) % tiles_s), 0)

    tok = lambda w: pl.BlockSpec((tm, w), lambda i: (i, 0))
    widths = (512, 512, 512, 512, 1024, 1024, 1024)
    return pl.pallas_call(
        functools.partial(_in_proj_kernel, n_prompt_tiles=npt),
        out_shape=[jax.ShapeDtypeStruct((t, w), BF16) for w in widths],
        grid=(npt + nst,),
        in_specs=[pl.BlockSpec((tm, D_MODEL), xp_map), pl.BlockSpec((tm, D_MODEL), xs_map),
                  _const_spec((1, D_MODEL)), _const_spec((D_MODEL, D_IN_PAD)),
                  _const_spec((1, Q_LORA)), _const_spec((Q_LORA, MLA_HEADS * LANES)),
                  _const_spec((1, KV_LORA)), _const_spec((KV_LORA, MLA_HEADS * LANES)),
                  _const_spec((KV_LORA, MLA_HEADS * LANES))]
                 + [pl.BlockSpec((tm, LANES), tab_map)] * 4,
        out_specs=[tok(w) for w in widths],
        compiler_params=_params(("arbitrary",)),
        name="in_proj",
    )(xp, xs, g, w_in_pad, qan, wuq_pad, kvan, wuk_pad, wuv, *tabs)


def _retention_kernel(lgf_ref, lgb_ref, q_ref, k_ref, v_ref, g_ref, gn_ref, o_ref, kvf_scr, kvb_scr, st_scr, *, n_chunks):
    hp = pl.program_id(1)
    c = CHUNK
    lane = lax.broadcasted_iota(jnp.int32, (1, c), 1)
    row = lax.broadcasted_iota(jnp.int32, (c, 1), 0)
    lane_h0 = lane < RET_DK
    row_h0 = row < RET_DK
    lgf0, lgf1 = lgf_ref[2 * hp], lgf_ref[2 * hp + 1]
    lgb0, lgb1 = lgb_ref[2 * hp], lgb_ref[2 * hp + 1]
    lgf_lane = jnp.where(lane_h0, lgf0, lgf1)
    lgb_lane = jnp.where(lane_h0, lgb0, lgb1)
    t = row.astype(F32)
    q_dec_f = jnp.exp((t + 1.0) * lgf_lane)
    q_dec_b = jnp.exp((c - t) * lgb_lane)
    k_dec_f = jnp.exp((c - 1.0 - t) * lgf_lane)
    k_dec_b = jnp.exp(t * lgb_lane)
    chunk_dec_f = jnp.exp(c * jnp.where(row_h0, lgf0, lgf1))
    chunk_dec_b = jnp.exp(c * jnp.where(row_h0, lgb0, lgb1))
    same_head = row_h0 == lane_h0
    diff = t - lane.astype(F32)
    d_intra = []
    for lgf, lgb in ((lgf0, lgb0), (lgf1, lgb1)):
        fwd = jnp.where(diff >= 0, jnp.exp(jnp.where(diff >= 0, diff, 0.0) * lgf), 0.0)
        bwd = jnp.where(diff < 0, jnp.exp(jnp.where(diff < 0, -diff, 0.0) * lgb), 0.0)
        d_intra.append(fwd + bwd)
    avg = jnp.where(same_head, 1.0 / RET_DK, 0.0).astype(BF16)
    gn = gn_ref[...]

    def chunk(n):
        return pl.ds(pl.multiple_of(n * c, c), c)

    unroll = min(RET_UNROLL, n_chunks)

    def chunk_kv(n, _):
        k = k_ref[chunk(n), :].astype(F32)
        kd = jnp.concatenate([(k * k_dec_f).astype(BF16), (k * k_dec_b).astype(BF16)], axis=-1)
        kv = _dot_tn(kd, v_ref[chunk(n), :])
        kvf_scr[n] = jnp.where(same_head, kv[0:c], 0.0)
        kvb_scr[n] = jnp.where(same_head, kv[c:2 * c], 0.0)
        return 0

    lax.fori_loop(0, n_chunks, chunk_kv, 0, unroll=unroll)

    def fwd_state(n, s_f):
        st_scr[n, 0:c, :] = s_f.astype(BF16)
        return chunk_dec_f * s_f + kvf_scr[n]

    lax.fori_loop(0, n_chunks, fwd_state, jnp.zeros((c, c), F32))

    def bwd_state(j, s_b):
        n = n_chunks - 1 - j
        st_scr[n, c:2 * c, :] = s_b.astype(BF16)
        return chunk_dec_b * s_b + kvb_scr[n]

    lax.fori_loop(0, n_chunks, bwd_state, jnp.zeros((c, c), F32))

    d_both = jnp.concatenate(d_intra, axis=0)
    zero_q = jnp.zeros((c, c), BF16)

    def chunk_out(n):
        q = q_ref[chunk(n), :]
        qf = q.astype(F32)
        q_both = jnp.concatenate([(qf * q_dec_f).astype(BF16), (qf * q_dec_b).astype(BF16)], axis=-1)
        q_heads = jnp.concatenate([jnp.where(lane_h0, q, zero_q), jnp.where(lane_h0, zero_q, q)], axis=0)
        p = (_dot_nt(q_heads, k_ref[chunk(n), :]) * d_both).astype(BF16)
        inner = _dot(p, v_ref[chunk(n), :])
        return _dot(q_both, st_scr[n]) + jnp.where(lane_h0, inner[0:c], inner[c:2 * c])

    def split_rows(a):
        hi = a.astype(BF16)
        return jnp.concatenate([hi, (a - hi.astype(F32)).astype(BF16)], axis=0)

    def out_group(gi, _):
        rows = pl.ds(pl.multiple_of(gi * (unroll * c), unroll * c), unroll * c)
        y = jnp.concatenate([chunk_out(gi * unroll + u) for u in range(unroll)], axis=0)
        m = y.shape[0]
        mu2 = _dot(split_rows(y), avg)
        d = y - (mu2[0:m] + mu2[m:2 * m])
        var2 = _dot(split_rows(d * d), avg)
        yn = d * lax.rsqrt(var2[0:m] + var2[m:2 * m] + EPS) * gn
        gate = g_ref[rows, :].astype(F32)
        o_ref[rows, :] = (yn * (gate * jax.nn.sigmoid(gate))).astype(BF16)
        return 0

    lax.fori_loop(0, n_chunks // unroll, out_group, 0)


def _retention(lgf, lgb, rq, rk, rv, rg, gn, seq_len, n_seq, row_block0):
    n_chunks = seq_len // CHUNK
    hp_count = RET_WIDTH // LANES
    blk = pl.BlockSpec((seq_len, LANES), lambda b, hp, *_: (row_block0 + b, hp))
    return pl.pallas_call(
        functools.partial(_retention_kernel, n_chunks=n_chunks),
        out_shape=jax.ShapeDtypeStruct((n_seq * seq_len, RET_WIDTH), BF16),
        grid_spec=pltpu.PrefetchScalarGridSpec(
            num_scalar_prefetch=2,
            grid=(n_seq, hp_count),
            in_specs=[blk, blk, blk, blk, pl.BlockSpec((1, LANES), lambda b, hp, *_: (0, hp))],
            out_specs=pl.BlockSpec((seq_len, LANES), lambda b, hp, *_: (b, hp)),
            scratch_shapes=[pltpu.VMEM((n_chunks, CHUNK, CHUNK), F32), pltpu.VMEM((n_chunks, CHUNK, CHUNK), F32),
                            pltpu.VMEM((n_chunks, 2 * CHUNK, CHUNK), BF16)]),
        compiler_params=_params(("arbitrary", "arbitrary")),
        name="retention",
    )(lgf, lgb, rq, rk, rv, rg, gn)


def _mla_attn_kernel(q_ref, k_ref, v_ref, o_ref, *, tk, n_kv):
    tq = q_ref.shape[0]
    lane = lax.broadcasted_iota(jnp.int32, (1, LANES), 1)
    qs = (q_ref[:, :LANES], q_ref[:, LANES:])

    def body(j, carry):
        rows = pl.ds(pl.multiple_of(j * tk, tk), tk)
        new = []
        for h in range(2):
            m, acc = carry[h]
            s = _dot_nt(qs[h], k_ref[rows, h * LANES:(h + 1) * LANES])
            m_new = jnp.maximum(m, jnp.max(s, axis=-1, keepdims=True))
            alpha = jnp.exp2(m - m_new)
            p = jnp.exp2((s - m_new).astype(BF16))
            acc = alpha * acc + _dot(p, v_ref[rows, h * LANES:(h + 1) * LANES])
            new.append((m_new, acc))
        return tuple(new)

    init = (jnp.full((tq, 1), NEG_BIG, F32), jnp.zeros((tq, LANES), F32))
    (_, acc0), (_, acc1) = lax.fori_loop(0, n_kv, body, (init, init), unroll=min(ATTN_UNROLL, n_kv))
    out0 = acc0 / acc0[:, MLA_V:MLA_V + 1]
    out1 = acc1 / acc1[:, MLA_V:MLA_V + 1]
    o_ref[...] = jnp.where(lane < MLA_V, out0, pltpu.roll(out1, MLA_V, 1)).astype(BF16)


def _mla_attn(qm, km, vm, seq_len, n_seq, row_block0):
    tq, tk = min(ATTN_TQ, seq_len), min(ATTN_TK, seq_len)
    hp_count = MLA_HEADS // 2
    nq = seq_len // tq
    return pl.pallas_call(
        functools.partial(_mla_attn_kernel, tk=tk, n_kv=seq_len // tk),
        out_shape=jax.ShapeDtypeStruct((n_seq * seq_len, MLA_WIDTH), BF16),
        grid=(n_seq, hp_count, nq),
        in_specs=[pl.BlockSpec((tq, 2 * LANES), lambda b, hp, i: ((row_block0 + b) * nq + i, hp)),
                  pl.BlockSpec((seq_len, 2 * LANES), lambda b, hp, i: (row_block0 + b, hp)),
                  pl.BlockSpec((seq_len, 2 * LANES), lambda b, hp, i: (row_block0 + b, hp))],
        out_specs=pl.BlockSpec((tq, LANES), lambda b, hp, i: (b * nq + i, hp)),
        compiler_params=_params(("arbitrary", "arbitrary", "arbitrary")),
        name="mla_attn",
    )(qm, km, vm)


def _mix_cross_kernel(xp_ref, xs_ref, rp_ref, rs_ref, mp_ref, ms_ref, wmix_ref, gx_ref, wxq_ref, kv_ref, wxo_ref,
                      gf_ref, rwt_ref, rbc_ref,
                      h_ref, xn_ref, route_ref, stat_ref, cnt_ref, carry_scr, *, n_prompt_tiles):
    i = pl.program_id(0)
    tm = h_ref.shape[0]
    is_p = i < n_prompt_tiles
    x = jnp.where(is_p, xp_ref[...], xs_ref[...])
    ret = jnp.where(is_p, rp_ref[...], rs_ref[...])
    mla = jnp.where(is_p, mp_ref[...], ms_ref[...])
    h1 = x + _dot(ret, wmix_ref[0:RET_WIDTH, :]) + _dot(mla, wmix_ref[RET_WIDTH:, :])

    hn = _rms(h1, gx_ref[...]).astype(BF16)
    q = (_dot(hn, wxq_ref[...]) * (X_HEAD_DIM ** -0.5)).astype(BF16)
    heads = []
    for h in range(X_HEADS):
        lo = h * X_HEAD_DIM
        s = _dot_nt(q[:, lo:lo + X_HEAD_DIM], kv_ref[0, :, lo:lo + X_HEAD_DIM])
        e = jnp.exp(s - jnp.max(s, axis=-1, keepdims=True))
        p = (e / jnp.sum(e, axis=-1, keepdims=True)).astype(BF16)
        heads.append(_dot(p, kv_ref[0, :, D_MODEL + lo:D_MODEL + lo + X_HEAD_DIM]).astype(BF16))
    h2 = h1 + _dot(jnp.concatenate(heads, axis=-1), wxo_ref[...])
    h_ref[...] = h2
    xn = _rms(h2, gf_ref[...]).astype(BF16)
    xn_ref[...] = xn

    ne = N_EXPERTS
    work = _dot_nt(rwt_ref[...], xn) + rbc_ref[:, 0:1]
    e_f = lax.broadcasted_iota(jnp.int32, (ne, tm), 0).astype(F32)
    vals, picks = [], []
    for _ in range(TOP_K):
        m = jnp.max(work, axis=0, keepdims=True)
        idx = jnp.min(jnp.where(work == m, e_f, float(ne)), axis=0, keepdims=True)
        sel = e_f == idx
        work = jnp.where(sel, -jnp.inf, work)
        vals.append(m)
        picks.append(jnp.where(sel, 1.0, 0.0))
    exps = [jnp.exp(v - vals[0]) for v in vals]
    denom = exps[0] + exps[1] + exps[2] + exps[3]
    onehot = picks[0] + picks[1] + picks[2] + picks[3]

    @pl.when(i == 0)
    def _():
        carry_scr[...] = jnp.zeros_like(carry_scr)

    t_row = lax.broadcasted_iota(jnp.int32, (tm, tm), 0)
    t_col = lax.broadcasted_iota(jnp.int32, (tm, tm), 1)
    earlier_tok = jnp.where(t_row < t_col, 1.0, 0.0).astype(BF16)
    before = _dot(onehot.astype(BF16), earlier_tok)
    tile_cnt = jnp.sum(onehot, axis=1, keepdims=True)
    group_rows = jnp.floor((tile_cnt + (ROW_CHUNK - 1)) * (1.0 / ROW_CHUNK)) * ROW_CHUNK
    carry_before = carry_scr[...]
    carry = carry_before + group_rows
    carry_scr[...] = carry
    cnt_ref[...] = carry.astype(jnp.int32)
    e_row = lax.broadcasted_iota(jnp.int32, (LANES, LANES), 0)
    e_col = lax.broadcasted_iota(jnp.int32, (LANES, LANES), 1)
    earlier_exp = jnp.where(e_col < e_row, 1.0, 0.0).astype(BF16)
    rows_pad = jnp.concatenate([jnp.broadcast_to(group_rows, (ne, LANES)), jnp.zeros((LANES - ne, LANES), F32)], axis=0)
    group_start = _dot(earlier_exp, rows_pad.astype(BF16))[0:ne]
    lane = lax.broadcasted_iota(jnp.int32, (ne, LANES), 1)
    stats = jnp.where(lane == 0, tile_cnt, jnp.where(lane == 1, carry_before, jnp.where(lane == 2, group_start, 0.0)))
    stat_ref[0] = stats.astype(jnp.int32)

    place = before + group_start[:, 0:1]
    row8 = lax.broadcasted_iota(jnp.int32, (2 * TOP_K, tm), 0)
    route = jnp.zeros((2 * TOP_K, tm), F32)
    for k in range(TOP_K):
        slot_k = jnp.sum(picks[k] * place, axis=0, keepdims=True)
        route = jnp.where(row8 == k, exps[k] / denom, jnp.where(row8 == TOP_K + k, slot_k, route))
    route_ref[...] = route


def _mix_cross(xp, xs, ret_p, ret_s, mla_p, mla_s, wmix, gx, wxq, kvmem, wxo, gf, rw_t, rb_col, sp, ss):
    tm = TOKEN_TILE
    tp, ts = xp.shape[0], xs.shape[0]
    npt, nst = tp // tm, ts // tm
    t = tp + ts
    n_seq_p = tp // sp
    mem_len = kvmem.shape[1]

    def p_map(i):
        return (jnp.minimum(i, npt - 1), 0)

    def s_map(i):
        return (jnp.maximum(i - npt, 0), 0)

    def kv_map(i):
        return (jnp.where(i < npt, i // (sp // tm), n_seq_p + (i - npt) // (ss // tm)), 0, 0)

    tok = lambda w: pl.BlockSpec((tm, w), lambda i: (i, 0))
    return pl.pallas_call(
        functools.partial(_mix_cross_kernel, n_prompt_tiles=npt),
        out_shape=[jax.ShapeDtypeStruct((t, D_MODEL), F32), jax.ShapeDtypeStruct((t, D_MODEL), BF16),
                   jax.ShapeDtypeStruct((2 * TOP_K, t), F32),
                   jax.ShapeDtypeStruct((npt + nst, N_EXPERTS, LANES), jnp.int32),
                   jax.ShapeDtypeStruct((N_EXPERTS, LANES), jnp.int32)],
        grid=(npt + nst,),
        in_specs=[pl.BlockSpec((tm, D_MODEL), p_map), pl.BlockSpec((tm, D_MODEL), s_map),
                  pl.BlockSpec((tm, RET_WIDTH), p_map), pl.BlockSpec((tm, RET_WIDTH), s_map),
                  pl.BlockSpec((tm, MLA_WIDTH), p_map), pl.BlockSpec((tm, MLA_WIDTH), s_map),
                  _const_spec((D_MODEL, D_MODEL)), _const_spec((1, D_MODEL)), _const_spec((D_MODEL, D_MODEL)),
                  pl.BlockSpec((1, mem_len, 2 * D_MODEL), kv_map),
                  _const_spec((D_MODEL, D_MODEL)), _const_spec((1, D_MODEL)),
                  _const_spec((N_EXPERTS, D_MODEL)), _const_spec((N_EXPERTS, LANES))],
        out_specs=[tok(D_MODEL), tok(D_MODEL),
                   pl.BlockSpec((2 * TOP_K, tm), lambda i: (0, i)),
                   pl.BlockSpec((1, N_EXPERTS, LANES), lambda i: (i, 0, 0)),
                   pl.BlockSpec((N_EXPERTS, LANES), lambda i: (0, 0))],
        scratch_shapes=[pltpu.VMEM((N_EXPERTS, LANES), F32)],
        compiler_params=_params(("arbitrary",)),
        name="mix_cross_router",
    )(xp, xs, ret_p, ret_s, mla_p, mla_s, wmix, gx, wxq, kvmem, wxo, gf, rw_t, rb_col)


def _chunk_copies(tile, nch_ref, lstart_ref, gstart_ref, local, hbm, sem, to_hbm, act):
    def per_expert(e, _):
        base = tile * N_EXPERTS + e
        n = nch_ref[base]
        ls = lstart_ref[base]
        gs = gstart_ref[base]
        def bit_copy(j):
            rows = ROW_CHUNK << j

            @pl.when(((n >> j) & 1) == 1)
            def _():
                off = (n & ((1 << j) - 1)) * ROW_CHUNK
                l_rows = local.at[pl.ds(pl.multiple_of(ls + off, ROW_CHUNK), rows)]
                g_rows = hbm.at[pl.ds(pl.multiple_of(gs + off, ROW_CHUNK), rows)]
                act(pltpu.make_async_copy(l_rows, g_rows, sem) if to_hbm else pltpu.make_async_copy(g_rows, l_rows, sem),
                    j % 2)

        for j in range(GROUP_BITS):
            bit_copy(j)
        return 0

    lax.fori_loop(0, N_EXPERTS, per_expert, 0)


def _wait_tile(tile, tot_ref, local, hbm, sem, to_hbm):
    total = tot_ref[tile]

    def wait_rows(rows):
        l_rows = local.at[pl.ds(0, rows)]
        g_rows = hbm.at[pl.ds(0, rows)]
        (pltpu.make_async_copy(l_rows, g_rows, sem) if to_hbm else pltpu.make_async_copy(g_rows, l_rows, sem)).wait()

    def big(_, c):
        wait_rows(ROW_CHUNK * WAIT_CHUNKS)
        return c

    def small(_, c):
        wait_rows(ROW_CHUNK)
        return c

    lax.fori_loop(0, total // WAIT_CHUNKS, big, 0)
    lax.fori_loop(0, total % WAIT_CHUNKS, small, 0)


def _start(cp, priority=0):
    cp.start(priority=priority)


def _wait(cp, priority=0):
    cp.wait()


def _dispatch_kernel(nch_ref, lstart_ref, gstart_ref, tot_ref, zstart_ref, znch_ref,
                     x_ref, route_ref, xs_hbm, pick_ref, buf, zero_scr, sem, zsem):
    i = pl.program_id(0)
    last = pl.num_programs(0) - 1
    tm = x_ref.shape[0]
    b = i % 2
    slot_rows = route_ref[TOP_K:2 * TOP_K, :]
    x = x_ref[...]

    def build(s, _):
        r0 = pl.multiple_of(s * LOCAL_STEP, LOCAL_STEP)
        rid = (lax.broadcasted_iota(jnp.int32, (LOCAL_STEP, tm), 0) + r0).astype(F32)
        hit = rid == slot_rows[0:1, :]
        for k in range(1, TOP_K):
            hit = hit | (rid == slot_rows[k:k + 1, :])
        pick = jnp.where(hit, 1.0, 0.0).astype(BF16)
        pick_ref[pl.ds(r0, LOCAL_STEP), :] = pick
        buf[b, pl.ds(r0, LOCAL_STEP), :] = _dot(pick, x)
        return 0

    lax.fori_loop(0, LOCAL_ROWS // LOCAL_STEP, build, 0, unroll=3)

    _chunk_copies(i, nch_ref, lstart_ref, gstart_ref, buf.at[b], xs_hbm, sem.at[b], True, _start)

    @pl.when(i > 0)
    def _():
        _wait_tile(i - 1, tot_ref, buf.at[1 - b], xs_hbm, sem.at[1 - b], True)

    @pl.when(i == last)
    def _():
        _wait_tile(i, tot_ref, buf.at[b], xs_hbm, sem.at[b], True)
        zero_scr[...] = jnp.zeros_like(zero_scr)

        def fill(act):
            def per_expert(e, _):
                def per_chunk(c, _):
                    row = pl.multiple_of(zstart_ref[e] + c * ROW_CHUNK, ROW_CHUNK)
                    act(pltpu.make_async_copy(zero_scr.at[pl.ds(0, ROW_CHUNK)], xs_hbm.at[pl.ds(row, ROW_CHUNK)], zsem))
                    return 0

                lax.fori_loop(0, znch_ref[e], per_chunk, 0)
                return 0

            lax.fori_loop(0, N_EXPERTS, per_expert, 0)

            def per_block(c, _):
                row = pl.multiple_of(zstart_ref[N_EXPERTS] + c * MOE_BLOCK, MOE_BLOCK)
                act(pltpu.make_async_copy(zero_scr, xs_hbm.at[pl.ds(row, MOE_BLOCK)], zsem))
                return 0

            lax.fori_loop(0, znch_ref[N_EXPERTS], per_block, 0)

        fill(_start)
        fill(_wait)


def _dispatch(tables, ztables, xn, route, n_rows):
    tm = TOKEN_TILE
    t = xn.shape[0]
    return pl.pallas_call(
        _dispatch_kernel,
        out_shape=[jax.ShapeDtypeStruct((n_rows, D_MODEL), F32),
                   jax.ShapeDtypeStruct((t // tm * LOCAL_ROWS, tm), BF16)],
        grid_spec=pltpu.PrefetchScalarGridSpec(
            num_scalar_prefetch=6,
            grid=(t // tm,),
            in_specs=[pl.BlockSpec((tm, D_MODEL), lambda i, *_: (i, 0)),
                      pl.BlockSpec((2 * TOP_K, tm), lambda i, *_: (0, i))],
            out_specs=[pl.BlockSpec(memory_space=pl.ANY),
                       pl.BlockSpec((LOCAL_ROWS, tm), lambda i, *_: (i, 0))],
            scratch_shapes=[pltpu.VMEM((2, LOCAL_ROWS, D_MODEL), F32), pltpu.VMEM((MOE_BLOCK, D_MODEL), F32),
                            pltpu.SemaphoreType.DMA((2,)), pltpu.SemaphoreType.DMA(())]),
        compiler_params=_params(("arbitrary",)),
        name="moe_dispatch",
    )(*tables, *ztables, xn, route)


def _experts_kernel(be_ref, bi_ref, na_ref, first_ref, seg_ref, nxt_ref, x_ref, bgu_ref, bd_ref, wgu_hbm, wd_hbm, o_ref,
                    wgu_f32, wd_f32, wgu_scr, wd_scr, sem):
    j = pl.program_id(0)
    active = j < na_ref[0]

    def weight_copies(e, slot):
        return (pltpu.make_async_copy(wgu_hbm.at[e], wgu_f32.at[slot], sem.at[slot, 0]),
                pltpu.make_async_copy(wd_hbm.at[e], wd_f32.at[slot], sem.at[slot, 1]))

    @pl.when(j == 0)
    def _():
        for cp in weight_copies(be_ref[0], 0):
            cp.start()

    @pl.when(active & (first_ref[j] == 1))
    def _():
        slot = seg_ref[j] % 2
        for cp in weight_copies(be_ref[j], slot):
            cp.wait()
        wgu_scr[...] = wgu_f32[slot].astype(BF16)
        wd_scr[...] = wd_f32[slot].astype(BF16)

        @pl.when(nxt_ref[j] >= 0)
        def _():
            for cp in weight_copies(nxt_ref[j], 1 - slot):
                cp.start()

    @pl.when(active)
    def _():
        x = x_ref[...].astype(BF16)
        gu = _dot(x, wgu_scr[...]) + bgu_ref[0]
        gate = jnp.minimum(gu[:, :D_MODEL], SWIGLU_LIMIT)
        up = jnp.clip(gu[:, D_MODEL:], -SWIGLU_LIMIT, SWIGLU_LIMIT)
        hid = (up + 1.0) * (gate * jax.nn.sigmoid(SWIGLU_ALPHA * gate))
        o_ref[...] = _dot(hid.astype(BF16), wd_scr[...]) + bd_ref[0]


def _experts(block_tables, xs, wgu, bgu, wd, bd):
    n_blocks = xs.shape[0] // MOE_BLOCK
    row_map = lambda j, be, bi, *_: (bi[j], 0)
    e_map = lambda j, be, *_: (be[j], 0, 0)
    return pl.pallas_call(
        _experts_kernel,
        out_shape=jax.ShapeDtypeStruct(xs.shape, F32),
        grid_spec=pltpu.PrefetchScalarGridSpec(
            num_scalar_prefetch=6,
            grid=(n_blocks,),
            in_specs=[pl.BlockSpec((MOE_BLOCK, D_MODEL), row_map),
                      pl.BlockSpec((1, 1, 2 * D_MODEL), e_map),
                      pl.BlockSpec((1, 1, D_MODEL), e_map),
                      pl.BlockSpec(memory_space=pl.ANY),
                      pl.BlockSpec(memory_space=pl.ANY)],
            out_specs=pl.BlockSpec((MOE_BLOCK, D_MODEL), row_map),
            scratch_shapes=[pltpu.VMEM((2, D_MODEL, 2 * D_MODEL), F32), pltpu.VMEM((2, D_MODEL, D_MODEL), F32),
                            pltpu.VMEM((D_MODEL, 2 * D_MODEL), BF16), pltpu.VMEM((D_MODEL, D_MODEL), BF16),
                            pltpu.SemaphoreType.DMA((2, 2))]),
        input_output_aliases={len(block_tables): 0},
        compiler_params=_params(("arbitrary",)),
        name="moe_experts",
    )(*block_tables, xs, bgu, bd, wgu, wd)


def _combine_kernel(nch_ref, lstart_ref, gstart_ref, tot_ref, route_ref, pick_ref, h_ref, g_ref, ys_hbm,
                    op_ref, os_ref, buf, scaled_scr, sem, *, n_prompt_tiles):
    i = pl.program_id(0)
    n = pl.num_programs(0)
    tm = h_ref.shape[0]
    b = i % 2

    @pl.when(i == 0)
    def _():
        buf[...] = jnp.zeros_like(buf)
        _chunk_copies(0, nch_ref, lstart_ref, gstart_ref, buf.at[0], ys_hbm, sem.at[0], False, _start)

    @pl.when(i + 1 < n)
    def _():
        _chunk_copies(i + 1, nch_ref, lstart_ref, gstart_ref, buf.at[1 - b], ys_hbm, sem.at[1 - b], False, _start)

    _wait_tile(i, tot_ref, buf.at[b], ys_hbm, sem.at[b], False)

    info = route_ref[...]
    info_hi = info.astype(BF16).astype(F32)
    gap = jnp.zeros((LANES - 2 * TOP_K, tm), F32)
    info2 = jnp.concatenate([info_hi, gap, info - info_hi, gap], axis=0).astype(BF16)
    lane = lax.broadcasted_iota(jnp.int32, (LOCAL_STEP, LANES), 1)

    def step(s, _):
        r0 = pl.multiple_of(s * LOCAL_STEP, LOCAL_STEP)
        pick = pick_ref[pl.ds(r0, LOCAL_STEP), :]
        own2 = _dot_nt(pick, info2)
        own = own2[:, :LANES] + own2[:, LANES:]
        rcol = (lax.broadcasted_iota(jnp.int32, (LOCAL_STEP, 1), 0) + r0).astype(F32)
        mine = jnp.where((own == rcol) & (lane >= TOP_K) & (lane < 2 * TOP_K), 1.0, 0.0)
        g_col = jnp.sum(own * pltpu.roll(mine, LANES - TOP_K, 1), axis=-1, keepdims=True)
        rows = buf[b, pl.ds(r0, LOCAL_STEP), :]
        scaled_scr[pl.ds(r0, LOCAL_STEP), :] = jnp.where(g_col != 0.0, rows * g_col, 0.0).astype(BF16)
        return 0

    lax.fori_loop(0, LOCAL_ROWS // LOCAL_STEP, step, 0, unroll=3)
    y = _dot_tn(pick_ref[...], scaled_scr[...])
    out = _rms(h_ref[...] + y, g_ref[...])

    @pl.when(i < n_prompt_tiles)
    def _():
        op_ref[...] = out

    @pl.when(i >= n_prompt_tiles)
    def _():
        os_ref[...] = out


def _combine(tables, route, pick, h2, g_final, ys, tp, ts):
    tm = TOKEN_TILE
    npt, nst = tp // tm, ts // tm
    tok = lambda w: pl.BlockSpec((tm, w), lambda i, *_: (i, 0))
    return pl.pallas_call(
        functools.partial(_combine_kernel, n_prompt_tiles=npt),
        out_shape=[jax.ShapeDtypeStruct((tp, D_MODEL), F32), jax.ShapeDtypeStruct((ts, D_MODEL), F32)],
        grid_spec=pltpu.PrefetchScalarGridSpec(
            num_scalar_prefetch=4,
            grid=(npt + nst,),
            in_specs=[pl.BlockSpec((2 * TOP_K, tm), lambda i, *_: (0, i)),
                      pl.BlockSpec((LOCAL_ROWS, tm), lambda i, *_: (i, 0)), tok(D_MODEL),
                      pl.BlockSpec((1, D_MODEL), lambda i, *_: (0, 0)),
                      pl.BlockSpec(memory_space=pl.ANY)],
            out_specs=[pl.BlockSpec((tm, D_MODEL), lambda i, *_: (jnp.minimum(i, npt - 1), 0)),
                       pl.BlockSpec((tm, D_MODEL), lambda i, *_: (jnp.maximum(i - npt, 0), 0))],
            scratch_shapes=[pltpu.VMEM((2, LOCAL_ROWS, D_MODEL), F32), pltpu.VMEM((LOCAL_ROWS, D_MODEL), BF16),
                            pltpu.SemaphoreType.DMA((2,))]),
        compiler_params=_params(("arbitrary",)),
        name="moe_combine_norm",
    )(*tables, route, pick, h2, g_final, ys)


def _rope_tables(seq_len):
    pos = jnp.arange(seq_len, dtype=F32)[:, None]
    lane = np.arange(LANES)

    def table(half, lane_freq, first, active):
        inv = ROPE_BASE ** (-jnp.arange(half, dtype=F32) / half)
        ang = pos * inv[None, :]
        cos = jnp.cos(ang)[:, lane_freq]
        sin = jnp.sin(ang)[:, lane_freq]
        c = jnp.where(active[None, :], cos, 1.0)
        s = jnp.where(active[None, :], jnp.where(first[None, :], -sin, sin), 0.0)
        return c, s

    half_r = RET_DK // 2
    cr, sr = table(half_r, lane % half_r, (lane % RET_DK) < half_r, np.ones(LANES, bool))
    half_m = MLA_ROPE // 2
    rel = lane - MLA_NOPE
    active = (rel >= 0) & (rel < MLA_ROPE)
    cm, sm = table(half_m, np.where(active, rel % half_m, 0), active & (rel < half_m), active)
    return cr, sr, cm, sm


def kernel(x_prompt, x_sample, mem_prompt, mem_sample, norm_mix, w_in, ret_decay_fwd, ret_decay_bwd, ret_gn, q_a_norm, w_uq, kv_a_norm, w_ukv, w_mix_out, norm_cross, norm_mem, w_xq, w_xkv, w_xo, norm_ffn, router_w, router_b, w_gu, b_gu, w_down, b_down, norm_final):
    assert norm_mix.shape[0] == 1, "single layer"
    bp, sp, d = x_prompt.shape
    bs, ss, _ = x_sample.shape
    tp, ts = bp * sp, bs * ss
    t = tp + ts
    assert d == D_MODEL and sp % TOKEN_TILE == 0 and ss % TOKEN_TILE == 0 and tp % ss == 0 and sp >= ss
    mem_len = mem_prompt.shape[1]

    w_in0 = w_in[0]
    w_in_pad = jnp.zeros((D_MODEL, D_IN_PAD), F32)
    w_in_pad = w_in_pad.at[:, :2816].set(w_in0[:, :2816])
    w_in_pad = w_in_pad.at[:, 2816 + MLA_NOPE:2816 + MLA_NOPE + MLA_ROPE].set(w_in0[:, 2816:]).astype(BF16)
    wuq_pad = jnp.pad(w_uq[0].reshape(Q_LORA, MLA_HEADS, MLA_NOPE + MLA_ROPE),
                      ((0, 0), (0, 0), (0, LANES - MLA_NOPE - MLA_ROPE))).reshape(Q_LORA, MLA_HEADS * LANES).astype(BF16)
    wukv = w_ukv[0].reshape(KV_LORA, MLA_HEADS, MLA_NOPE + MLA_V)
    wuk_pad = jnp.pad(wukv[:, :, :MLA_NOPE], ((0, 0), (0, 0), (0, LANES - MLA_NOPE))).reshape(KV_LORA, MLA_HEADS * LANES).astype(BF16)
    wuv = jnp.pad(wukv[:, :, MLA_NOPE:], ((0, 0), (0, 0), (0, LANES - MLA_V))).reshape(KV_LORA, MLA_HEADS * LANES).astype(BF16)
    rw_t = router_w[0].T.astype(BF16)
    rb_col = jnp.broadcast_to(router_b[0].astype(F32)[:, None], (N_EXPERTS, LANES))
    lgf = jnp.log1p(-jnp.exp2(ret_decay_fwd[0].astype(F32)))
    lgb = jnp.log1p(-jnp.exp2(ret_decay_bwd[0].astype(F32)))
    tabs = _rope_tables(sp)

    xp = x_prompt.reshape(tp, D_MODEL)
    xs = x_sample.reshape(ts, D_MODEL)
    mem = jnp.concatenate([mem_prompt.reshape(-1, D_MODEL), mem_sample.reshape(-1, D_MODEL)], axis=0)

    kvmem = _mem_kv(mem, norm_mem[0][None, :], w_xkv[0].astype(BF16)).reshape(bp + bs, mem_len, 2 * D_MODEL)

    rq, rk, rv, rg, qm, km, vm = _in_proj(xp, xs, norm_mix[0][None, :], w_in_pad, q_a_norm[0][None, :], wuq_pad,
                                          kv_a_norm[0][None, :], wuk_pad, wuv, tabs, sp, ss)

    gn = ret_gn[0][None, :]
    ret_p = _retention(lgf, lgb, rq, rk, rv, rg, gn, sp, bp, 0)
    ret_s = _retention(lgf, lgb, rq, rk, rv, rg, gn, ss, bs, tp // ss)
    mla_p = _mla_attn(qm, km, vm, sp, bp, 0)
    mla_s = _mla_attn(qm, km, vm, ss, bs, tp // ss)

    h2, xn, route, stats, counts = _mix_cross(
        xp, xs, ret_p, ret_s, mla_p, mla_s, w_mix_out[0].astype(BF16), norm_cross[0][None, :], w_xq[0].astype(BF16),
        kvmem, w_xo[0].astype(BF16), norm_ffn[0][None, :], rw_t, rb_col, sp, ss)

    used = counts[:, 0]
    padded = (used + MOE_BLOCK - 1) // MOE_BLOCK * MOE_BLOCK
    pend = jnp.cumsum(padded)
    pstart = pend - padded
    tile_cnt, tile_before, tile_lstart = (stats[:, :, r] for r in range(3))
    n_tiles = stats.shape[0]
    tile_chunks = (tile_cnt + ROW_CHUNK - 1) // ROW_CHUNK
    tables = (tile_chunks.reshape(-1),
              tile_lstart.reshape(-1),
              (pstart[None, :] + tile_before).reshape(-1),
              jnp.sum(tile_chunks, axis=-1))
    n_blocks = -(-(t * TOP_K + N_EXPERTS * (n_tiles * (ROW_CHUNK - 1) + MOE_BLOCK - 1)) // MOE_BLOCK)
    ztables = (jnp.concatenate([pstart + used, pend[-1:]]),
               jnp.concatenate([(padded - used) // ROW_CHUNK, n_blocks - pend[-1:] // MOE_BLOCK]))
    blk = jnp.arange(n_blocks, dtype=jnp.int32)
    n_active = (pend[-1] // MOE_BLOCK).astype(jnp.int32)
    block_i = jnp.minimum(blk, n_active - 1)
    block_e = jnp.minimum(jnp.sum((block_i[:, None] * MOE_BLOCK >= pend[None, :]).astype(jnp.int32), axis=-1),
                          N_EXPERTS - 1).astype(jnp.int32)
    block_first = jnp.concatenate([jnp.ones((1,), jnp.int32), (block_e[1:] != block_e[:-1]).astype(jnp.int32)])
    block_seg = jnp.cumsum(block_first) - 1
    next_blk = jnp.sum(jnp.where(block_e[:, None] == jnp.arange(N_EXPERTS)[None, :], pend[None, :], 0), axis=-1) // MOE_BLOCK
    block_next = jnp.where(next_blk < n_active, block_e[jnp.minimum(next_blk, n_blocks - 1)], -1).astype(jnp.int32)

    xs_sorted, pick = _dispatch(tables, ztables, xn, route, n_blocks * MOE_BLOCK)
    ys = _experts((block_e, block_i, n_active[None], block_first, block_seg.astype(jnp.int32), block_next),
                  xs_sorted, w_gu[0], b_gu[0][:, None, :], w_down[0], b_down[0][:, None, :])
    out_p, out_s = _combine(tables, route, pick, h2, norm_final[None, :], ys, tp, ts)
    return out_p.reshape(bp, sp, D_MODEL), out_s.reshape(bs, ss, D_MODEL)
```
